```python
import jax
import jax.numpy as jnp
from jax import lax
import numpy as np

D_MODEL = 1024
BATCH = 8
SEQ = 2048
DEPTH = 1

CHUNK = 64
ATT_HEADS = 8
ATT_HEAD_DIM = 64
ATT_WIDTH = ATT_HEADS * ATT_HEAD_DIM
LEFT_CHUNKS = 8
BAND_CHUNKS = LEFT_CHUNKS + 1
BAND = BAND_CHUNKS * CHUNK
REL_CLIP = 64
N_REL = 2 * REL_CLIP + 1
RWKV_HEADS = 8
RWKV_HEAD_DIM = 64
RWKV_WIDTH = RWKV_HEADS * RWKV_HEAD_DIM
DECAY_LORA = 64
AAA_LORA = 64
GATE_LORA = 128
GN_EPS = 64e-5
ATT_PROJ = 3 * ATT_WIDTH
RWKV_PROJ = 3 * RWKV_WIDTH + DECAY_LORA + AAA_LORA + GATE_LORA
D_IN = ATT_PROJ + RWKV_PROJ + 2 * D_MODEL
N_GROUPS = 4
EXPERTS_PER_GROUP = 8
N_EXPERTS = N_GROUPS * EXPERTS_PER_GROUP
TOP_K = 2
D_EXPERT = 256

RMS_EPS = 1e-6

kernel_name = 'hybrid_chunk_attn_rwkv7_hier_moe'


def rmsnorm(x, g):
    xf = x.astype(jnp.float32)
    y = xf * lax.rsqrt(jnp.mean(xf * xf, axis=-1, keepdims=True) + RMS_EPS)
    return (y * g.astype(jnp.float32)).astype(x.dtype)


def token_shift(p, mu):
    prev = jnp.pad(p[:, :-1], ((0, 0), (1, 0), (0, 0)))
    return p + (prev - p) * mu


def chunk_band_attention(q, k, v, rel_table):
    bsz, seq, _ = q.shape
    nc = seq // CHUNK
    shp = (bsz, nc, CHUNK, ATT_HEADS, ATT_HEAD_DIM)
    qc = q.reshape(shp) * (ATT_HEAD_DIM ** -0.5)
    pad = ((0, 0), (LEFT_CHUNKS, 0), (0, 0), (0, 0), (0, 0))
    kp = jnp.pad(k.reshape(shp), pad)
    vp = jnp.pad(v.reshape(shp), pad)
    band_idx = jnp.arange(nc)[:, None] + jnp.arange(BAND_CHUNKS)[None, :]
    kb = kp[:, band_idx].reshape(bsz, nc, BAND, ATT_HEADS, ATT_HEAD_DIM)
    vb = vp[:, band_idx].reshape(bsz, nc, BAND, ATT_HEADS, ATT_HEAD_DIM)
    scores = jnp.einsum('bcqhd,bckhd->bhcqk', qc, kb).astype(jnp.float32)
    qpos = jnp.arange(CHUNK)
    kpos = jnp.arange(BAND)
    rel = LEFT_CHUNKS * CHUNK + qpos[:, None] - kpos[None, :]
    rel_idx = jnp.clip(rel, -REL_CLIP, REL_CLIP) + REL_CLIP
    bias = rel_table.astype(jnp.float32)[:, rel_idx]
    valid = (jnp.arange(nc)[:, None] - LEFT_CHUNKS + kpos[None, :] // CHUNK) >= 0
    scores = jnp.where(valid[None, None, :, None, :], scores + bias[None, :, None],
                       jnp.finfo(jnp.float32).min)
    probs = jax.nn.softmax(scores, axis=-1).astype(v.dtype)
    out = jnp.einsum('bhcqk,bckhd->bcqhd', probs, vb)
    return out.reshape(bsz, seq, ATT_WIDTH)


def rwkv7_step(state, inp):
    r_t, w_t, k_t, v_t, kk_t, a_t = inp
    sa = jnp.einsum('bhvk,bhk->bhv', state, -kk_t)
    state = (state * w_t[:, :, None, :]
             + sa[..., None] * (kk_t * a_t)[:, :, None, :]
             + v_t[..., None] * k_t[:, :, None, :])
    y_t = jnp.einsum('bhvk,bhk->bhv', state, r_t)
    return state, y_t


def rwkv7_time_mix(r, k, v, w_lo, a_lo, g_lo, w0, w2, a0, a2, g2, k_k, k_a, r_k, gn_g, gn_b):
    out_dtype = r.dtype
    f32 = jnp.float32
    r, k, v, w_lo, a_lo, g_lo = [t.astype(f32) for t in (r, k, v, w_lo, a_lo, g_lo)]
    bsz, seq, _ = r.shape
    log_w = -jax.nn.softplus(-(w0 + jnp.tanh(w_lo) @ w2)) - 0.5
    decay = jnp.exp(-jnp.exp(log_w))
    a = jax.nn.sigmoid(a0 + a_lo @ a2)
    g = jax.nn.sigmoid(g_lo) @ g2
    heads = lambda t: t.reshape(bsz, seq, RWKV_HEADS, RWKV_HEAD_DIM)
    kk = heads(k * k_k)
    kk = kk / jnp.maximum(jnp.sqrt(jnp.sum(kk * kk, axis=-1, keepdims=True)), 1e-12)
    k = k * (1.0 + (a - 1.0) * k_a)
    rh, wh, kh, vh, ah = [heads(t) for t in (r, decay, k, v, a)]
    xs = tuple(jnp.moveaxis(t, 1, 0) for t in (rh, wh, kh, vh, kk, ah))
    state0 = jnp.zeros((bsz, RWKV_HEADS, RWKV_HEAD_DIM, RWKV_HEAD_DIM), f32)
    _, y = lax.scan(rwkv7_step, state0, xs)
    y = jnp.moveaxis(y, 0, 1)
    mean = jnp.mean(y, axis=-1, keepdims=True)
    var = jnp.mean(jnp.square(y - mean), axis=-1, keepdims=True)
    y = ((y - mean) * lax.rsqrt(var + GN_EPS)).reshape(bsz, seq, RWKV_WIDTH) * gn_g + gn_b
    bonus = jnp.sum(rh * kh * r_k, axis=-1, keepdims=True) * vh
    y = (y + bonus.reshape(bsz, seq, RWKV_WIDTH)) * g
    return y.astype(out_dtype)


def hierarchical_moe(h, rg_w, rg_b, re_w, re_b, w_gate, w_up, w_down):
    bsz, seq, d = h.shape
    f32 = jnp.float32
    hf = h.reshape(bsz * seq, d)
    group_logits = (hf @ rg_w).astype(f32) + rg_b.astype(f32)
    group_prob = jax.nn.softmax(group_logits, axis=-1)
    g_prob, g_idx = lax.top_k(group_prob, 1)
    expert_logits = ((hf @ re_w).astype(f32) + re_b.astype(f32)).reshape(-1, N_GROUPS, EXPERTS_PER_GROUP)
    in_group = jnp.take_along_axis(expert_logits, g_idx[:, :, None], axis=1)[:, 0]
    e_logit, e_idx = lax.top_k(in_group, TOP_K)
    gate = jax.nn.softmax(e_logit, axis=-1) * g_prob
    flat_idx = g_idx * EXPERTS_PER_GROUP + e_idx
    combine = jnp.einsum('nk,nke->ne', gate, jax.nn.one_hot(flat_idx, N_EXPERTS, dtype=f32)).astype(h.dtype)
    y = jnp.zeros_like(hf)
    for grp in range(N_GROUPS):
        sl = slice(grp * EXPERTS_PER_GROUP, (grp + 1) * EXPERTS_PER_GROUP)
        hg = jnp.einsum('nd,edf->nef', hf, w_gate[sl])
        hu = jnp.einsum('nd,edf->nef', hf, w_up[sl])
        act = jax.nn.silu(hg) * hu * combine[:, sl, None]
        y = y + jnp.einsum('nef,efd->nd', act, w_down[sl])
    return y.reshape(bsz, seq, d)


def setup_inputs(seed: int = 0) -> dict:
    key = jax.random.key(seed)
    ks = jax.random.split(key, 27)
    f32 = jnp.float32
    nrm = lambda k, shape, scale: scale * jax.random.normal(k, shape, f32)
    return {
        'x': nrm(ks[0], (BATCH, SEQ, D_MODEL), 1.0),
        'ln_mix_g': 1.0 + nrm(ks[1], (DEPTH, D_MODEL), 0.02),
        'w_in': nrm(ks[2], (DEPTH, D_MODEL, D_IN), D_MODEL ** -0.5),
        'att_rel_bias': nrm(ks[3], (DEPTH, ATT_HEADS, N_REL), 0.1),
        'rwkv_mu': jax.random.uniform(ks[4], (DEPTH, RWKV_PROJ), f32),
        'rwkv_w0': jax.random.uniform(ks[5], (DEPTH, RWKV_WIDTH), f32, -6.5, -1.5),
        'rwkv_w2': nrm(ks[6], (DEPTH, DECAY_LORA, RWKV_WIDTH), 0.1 * DECAY_LORA ** -0.5),
        'rwkv_a0': nrm(ks[7], (DEPTH, RWKV_WIDTH), 0.1),
        'rwkv_a2': nrm(ks[8], (DEPTH, AAA_LORA, RWKV_WIDTH), 0.1 * AAA_LORA ** -0.5),
        'rwkv_g2': nrm(ks[9], (DEPTH, GATE_LORA, RWKV_WIDTH), GATE_LORA ** -0.5),
        'rwkv_k_k': 0.85 + nrm(ks[10], (DEPTH, RWKV_WIDTH), 0.05),
        'rwkv_k_a': 1.0 + nrm(ks[11], (DEPTH, RWKV_WIDTH), 0.05),
        'rwkv_r_k': nrm(ks[12], (DEPTH, RWKV_HEADS, RWKV_HEAD_DIM), 0.1),
        'rwkv_gn_g': 1.0 + nrm(ks[13], (DEPTH, RWKV_WIDTH), 0.02),
        'rwkv_gn_b': nrm(ks[14], (DEPTH, RWKV_WIDTH), 0.02),
        'w_branch_att': nrm(ks[15], (DEPTH, ATT_WIDTH, D_MODEL), ATT_WIDTH ** -0.5),
        'w_branch_rwkv': nrm(ks[16], (DEPTH, RWKV_WIDTH, D_MODEL), RWKV_WIDTH ** -0.5),
        'w_out': nrm(ks[17], (DEPTH, D_MODEL, D_MODEL), D_MODEL ** -0.5),
        'ln_ffn_g': 1.0 + nrm(ks[18], (DEPTH, D_MODEL), 0.02),
        'router_group_w': nrm(ks[19], (DEPTH, D_MODEL, N_GROUPS), D_MODEL ** -0.5),
        'router_group_b': nrm(ks[20], (DEPTH, N_GROUPS), 0.01),
        'router_expert_w': nrm(ks[21], (DEPTH, D_MODEL, N_EXPERTS), D_MODEL ** -0.5),
        'router_expert_b': nrm(ks[22], (DEPTH, N_EXPERTS), 0.01),
        'expert_w_gate': nrm(ks[23], (DEPTH, N_EXPERTS, D_MODEL, D_EXPERT), D_MODEL ** -0.5),
        'expert_w_up': nrm(ks[24], (DEPTH, N_EXPERTS, D_MODEL, D_EXPERT), D_MODEL ** -0.5),
        'expert_w_down': nrm(ks[25], (DEPTH, N_EXPERTS, D_EXPERT, D_MODEL), D_EXPERT ** -0.5),
        'ln_final_g': 1.0 + nrm(ks[26], (D_MODEL,), 0.02),
    }


def reference(x, ln_mix_g, w_in, att_rel_bias, rwkv_mu, rwkv_w0, rwkv_w2, rwkv_a0, rwkv_a2,
              rwkv_g2, rwkv_k_k, rwkv_k_a, rwkv_r_k, rwkv_gn_g, rwkv_gn_b, w_branch_att,
              w_branch_rwkv, w_out, ln_ffn_g, router_group_w, router_group_b, router_expert_w,
              router_expert_b, expert_w_gate, expert_w_up, expert_w_down, ln_final_g):
    W = RWKV_WIDTH
    o_w = 3 * W
    o_a = o_w + DECAY_LORA
    o_g = o_a + AAA_LORA
    for l in range(DEPTH):
        h = rmsnorm(x, ln_mix_g[l])
        proj = h @ w_in[l]
        q = proj[..., :ATT_WIDTH]
        k = proj[..., ATT_WIDTH:2 * ATT_WIDTH]
        v = proj[..., 2 * ATT_WIDTH:ATT_PROJ]
        rw = token_shift(proj[..., ATT_PROJ:ATT_PROJ + RWKV_PROJ], rwkv_mu[l])
        gates = jax.nn.sigmoid(proj[..., ATT_PROJ + RWKV_PROJ:])
        att = chunk_band_attention(q, k, v, att_rel_bias[l])
        rwkv = rwkv7_time_mix(rw[..., :W], rw[..., W:2 * W], rw[..., 2 * W:3 * W],
                              rw[..., o_w:o_a], rw[..., o_a:o_g], rw[..., o_g:],
                              rwkv_w0[l], rwkv_w2[l], rwkv_a0[l], rwkv_a2[l], rwkv_g2[l],
                              rwkv_k_k[l], rwkv_k_a[l], rwkv_r_k[l], rwkv_gn_g[l], rwkv_gn_b[l])
        merged = (gates[..., :D_MODEL] * (att @ w_branch_att[l])
                  + gates[..., D_MODEL:] * (rwkv @ w_branch_rwkv[l]))
        x = x + merged @ w_out[l]
        h = rmsnorm(x, ln_ffn_g[l])
        x = x + hierarchical_moe(h, router_group_w[l], router_group_b[l], router_expert_w[l],
                                 router_expert_b[l], expert_w_gate[l], expert_w_up[l],
                                 expert_w_down[l])
    return rmsnorm(x, ln_final_g)
```

```python
import functools
import math

import jax
import jax.numpy as jnp
from jax import lax
from jax.experimental import pallas as pl
from jax.experimental.pallas import tpu as pltpu

F32 = jnp.float32
BF16 = jnp.bfloat16
HIGHEST = lax.Precision.HIGHEST

D_MODEL = 1024
CHUNK = 64
HEADS = 8
HEAD_DIM = 64
WIDTH = HEADS * HEAD_DIM
LEFT_CHUNKS = 8
BAND = (LEFT_CHUNKS + 1) * CHUNK
REL_CLIP = 64
N_REL = 2 * REL_CLIP + 1
DECAY_LORA = 64
AAA_LORA = 64
GATE_LORA = 128
GN_EPS = 64e-5
RMS_EPS = 1e-6
ATT_PROJ = 3 * WIDTH
RWKV_PROJ = 3 * WIDTH + DECAY_LORA + AAA_LORA + GATE_LORA
D_IN = ATT_PROJ + RWKV_PROJ + 2 * D_MODEL
N_GROUPS = 4
EXPERTS_PER_GROUP = 8
N_EXPERTS = N_GROUPS * EXPERTS_PER_GROUP
D_EXPERT = 256
GROUP_FF = EXPERTS_PER_GROUP * D_EXPERT
ROUTER_LANES = 128
GROUP_LANE0 = N_EXPERTS

VMEM_LIMIT = 52 * 1024 * 1024


def _dot(a, b):
    return jnp.dot(a, b, preferred_element_type=F32)


def _dot_hi(a, b):
    return jnp.dot(a, b, preferred_element_type=F32, precision=HIGHEST)


def _dot_nt(a, b, precision=None):
    return lax.dot_general(a, b, (((1,), (1,)), ((), ())),
                           preferred_element_type=F32, precision=precision)


def _dot_tn(a, b, precision=None):
    return lax.dot_general(a, b, (((0,), (0,)), ((), ())),
                           preferred_element_type=F32, precision=precision)


def _sigmoid(x):
    return 1.0 / (1.0 + jnp.exp(-x))


def _rel_bias_kernel(tab_ref, out_ref):
    rows = tab_ref.shape[1]
    n = lax.broadcasted_iota(jnp.int32, (rows, CHUNK * 128), 1)
    r = lax.broadcasted_iota(jnp.int32, (rows, CHUNK * 128), 0)
    q = n >> 7
    kk = n & 127
    idx = jnp.clip(CHUNK + q - kk, -REL_CLIP, REL_CLIP) + REL_CLIP
    onehot = jnp.where(r == idx, 1.0, 0.0).astype(F32)
    out_ref[...] = _dot_hi(tab_ref[...], onehot)


def _rel_bias(rel_table):
    rows = 136
    tab = jnp.pad(rel_table.astype(F32), ((0, 0), (0, rows - N_REL)))
    tail = pl.pallas_call(
        _rel_bias_kernel,
        out_shape=jax.ShapeDtypeStruct((HEADS, CHUNK * 128), F32),
        name="rel_bias",
    )(tab)
    tail = tail.reshape(HEADS, CHUNK, 128)
    head = jnp.broadcast_to(rel_table.astype(F32)[:, N_REL - 1][:, None, None],
                            (HEADS, CHUNK, BAND - 128))
    return jnp.concatenate([head, tail], axis=-1)


def _in_proj_kernel(x_ref, g_ref, w_ref, mu_ref, q_ref, k_ref, v_ref, rw_ref, gate_ref,
                    carry_ref, *, tiles_per_seq):
    i = pl.program_id(0)

    @pl.when(i == 0)
    def _():
        carry_ref[...] = jnp.zeros(carry_ref.shape, F32)

    x = x_ref[...]
    h = x * lax.rsqrt(jnp.mean(x * x, axis=-1, keepdims=True) + RMS_EPS) * g_ref[...]
    hb = h.astype(BF16)
    q_ref[...] = _dot(hb, w_ref[:, 0:WIDTH]).astype(BF16)
    k_ref[...] = _dot(hb, w_ref[:, WIDTH:2 * WIDTH]).astype(BF16)
    v_ref[...] = _dot(hb, w_ref[:, 2 * WIDTH:ATT_PROJ]).astype(BF16)
    rw = _dot(hb, w_ref[:, ATT_PROJ:ATT_PROJ + RWKV_PROJ])
    tm = rw.shape[0]
    first_prev = jnp.where(i % tiles_per_seq == 0, 0.0, carry_ref[0:1, :])
    rolled = pltpu.roll(rw, 1, axis=0)
    row = lax.broadcasted_iota(jnp.int32, rw.shape, 0)
    prev = jnp.where(row == 0, first_prev, rolled)
    carry_ref[0:1, :] = rw[tm - 1:tm, :]
    rw_ref[...] = rw + (prev - rw) * mu_ref[...]
    gate_ref[...] = _sigmoid(_dot(hb, w_ref[:, ATT_PROJ + RWKV_PROJ:D_IN])).astype(BF16)


def _in_proj(x2, ln_g, w_in_b, mu, seq):
    n = x2.shape[0]
    tm = 256
    row = lambda i: (i, 0)
    const = lambda i: (0, 0)
    return pl.pallas_call(
        functools.partial(_in_proj_kernel, tiles_per_seq=seq // tm),
        grid=(n // tm,),
        in_specs=[
            pl.BlockSpec((tm, D_MODEL), row),
            pl.BlockSpec((1, D_MODEL), const),
            pl.BlockSpec((D_MODEL, D_IN), const),
            pl.BlockSpec((1, RWKV_PROJ), const),
        ],
        out_specs=[
            pl.BlockSpec((tm, WIDTH), row),
            pl.BlockSpec((tm, WIDTH), row),
            pl.BlockSpec((tm, WIDTH), row),
            pl.BlockSpec((tm, RWKV_PROJ), row),
            pl.BlockSpec((tm, 2 * D_MODEL), row),
        ],
        out_shape=[
            jax.ShapeDtypeStruct((n, WIDTH), BF16),
            jax.ShapeDtypeStruct((n, WIDTH), BF16),
            jax.ShapeDtypeStruct((n, WIDTH), BF16),
            jax.ShapeDtypeStruct((n, RWKV_PROJ), F32),
            jax.ShapeDtypeStruct((n, 2 * D_MODEL), BF16),
        ],
        scratch_shapes=[pltpu.VMEM((8, RWKV_PROJ), F32)],
        compiler_params=pltpu.CompilerParams(
            dimension_semantics=("arbitrary",), vmem_limit_bytes=VMEM_LIMIT),
        name="in_proj",
    )(x2, ln_g, w_in_b, mu)


def _band_attn_kernel(q_ref, k_ref, v_ref, bias_ref, o_ref, kpad_ref, vpad_ref):
    c = pl.program_id(1)
    seq = k_ref.shape[1]
    pad = LEFT_CHUNKS * CHUNK

    @pl.when(c == 0)
    def _():
        kpad_ref[0:pad, :] = jnp.zeros((pad, WIDTH), BF16)
        vpad_ref[0:pad, :] = jnp.zeros((pad, WIDTH), BF16)
        kpad_ref[pad:pad + seq, :] = k_ref[0]
        vpad_ref[pad:pad + seq, :] = v_ref[0]

    start = pl.multiple_of(c * CHUNK, CHUNK)
    kb = kpad_ref[pl.ds(start, BAND), :]
    vb = vpad_ref[pl.ds(start, BAND), :]
    q = q_ref[0]
    lane = lax.broadcasted_iota(jnp.int32, (CHUNK, WIDTH), 1)
    kpos = lax.broadcasted_iota(jnp.int32, (CHUNK, BAND), 1)
    valid = kpos >= (LEFT_CHUNKS - c) * CHUNK
    neg = jnp.finfo(F32).min
    acc = jnp.zeros((CHUNK, WIDTH), F32)
    for h in range(HEADS):
        in_head = (lane >= h * HEAD_DIM) & (lane < (h + 1) * HEAD_DIM)
        qh = jnp.where(in_head, q, jnp.zeros_like(q))
        s = _dot_nt(qh, kb) * (HEAD_DIM ** -0.5) + bias_ref[h]
        s = jnp.where(valid, s, neg)
        m = jnp.max(s, axis=-1, keepdims=True)
        p = jnp.exp(s - m)
        l = jnp.sum(p, axis=-1, keepdims=True)
        oh = _dot(p.astype(BF16), vb) / l
        acc = jnp.where(in_head, oh, acc)
    o_ref[0] = acc.astype(BF16)


def _band_attn(q, k, v, bias):
    b, seq, _ = q.shape
    nc = seq // CHUNK
    return pl.pallas_call(
        _band_attn_kernel,
        grid=(b, nc),
        in_specs=[
            pl.BlockSpec((1, CHUNK, WIDTH), lambda i, c: (i, c, 0)),
            pl.BlockSpec((1, seq, WIDTH), lambda i, c: (i, 0, 0)),
            pl.BlockSpec((1, seq, WIDTH), lambda i, c: (i, 0, 0)),
            pl.BlockSpec((HEADS, CHUNK, BAND), lambda i, c: (0, 0, 0)),
        ],
        out_specs=pl.BlockSpec((1, CHUNK, WIDTH), lambda i, c: (i, c, 0)),
        out_shape=jax.ShapeDtypeStruct((b, seq, WIDTH), BF16),
        scratch_shapes=[pltpu.VMEM((seq + LEFT_CHUNKS * CHUNK, WIDTH), BF16),
                        pltpu.VMEM((seq + LEFT_CHUNKS * CHUNK, WIDTH), BF16)],
        compiler_params=pltpu.CompilerParams(
            dimension_semantics=("arbitrary", "arbitrary"), vmem_limit_bytes=VMEM_LIMIT),
        name="band_attn",
    )(q, k, v, bias)


def _rwkv_kernel(rw_ref, w2a2_ref, g2_ref, w0_ref, a0_ref, kk_ref, ka_ref, rk_ref,
                 gng_ref, gnb_ref, y_ref, state_ref):
    c = pl.program_id(1)
    t = CHUNK

    @pl.when(c == 0)
    def _():
        state_ref[...] = jnp.zeros(state_ref.shape, F32)

    rw = rw_ref[0]
    r = rw[:, 0:WIDTH]
    k = rw[:, WIDTH:2 * WIDTH]
    v = rw[:, 2 * WIDTH:3 * WIDTH]
    lora = rw[:, 3 * WIDTH:3 * WIDTH + DECAY_LORA + AAA_LORA]
    g_lo = rw[:, 3 * WIDTH + DECAY_LORA + AAA_LORA:RWKV_PROJ]

    lane128 = lax.broadcasted_iota(jnp.int32, lora.shape, 1)
    lora = jnp.where(lane128 < DECAY_LORA, jnp.tanh(lora), lora)
    wa = _dot_hi(lora, w2a2_ref[...])
    log_decay = -math.exp(-0.5) * _sigmoid(w0_ref[...] + wa[:, 0:WIDTH])
    lr = _sigmoid(a0_ref[...] + wa[:, WIDTH:2 * WIDTH])
    gate = _dot_hi(_sigmoid(g_lo), g2_ref[...])

    kk_raw = k * kk_ref[...]
    k_mod = k * (1.0 + (lr - 1.0) * ka_ref[...])

    row = lax.broadcasted_iota(jnp.int32, (t, t), 0)
    col = lax.broadcasted_iota(jnp.int32, (t, t), 1)
    tri_incl = jnp.where(row >= col, 1.0, 0.0).astype(F32)
    strict = row > col
    incl = row >= col
    eye = jnp.where(row == col, 1.0, 0.0).astype(F32)

    logp = _dot_hi(tri_incl, log_decay)
    p_in = jnp.exp(logp)
    p_ex = jnp.exp(logp - log_decay)
    p_inv = jnp.exp(-logp)
    p_end = p_in[t - 1:t, :]

    for h in range(HEADS):
        sl = slice(h * HEAD_DIM, (h + 1) * HEAD_DIM)
        kk_h = kk_raw[:, sl]
        nrm = jnp.sqrt(jnp.sum(kk_h * kk_h, axis=-1, keepdims=True))
        kk_h = kk_h / jnp.maximum(nrm, 1e-12)
        lr_h = lr[:, sl]
        r_h = r[:, sl]
        v_h = v[:, sl]
        km_h = k_mod[:, sl]
        a_hat = -kk_h * p_ex[:, sl]
        r_hat = r_h * p_in[:, sl]
        b_hat = kk_h * lr_h * p_inv[:, sl]
        k_hat = km_h * p_inv[:, sl]
        pe_h = p_end[:, sl]

        l_ab = jnp.where(strict, _dot_nt(a_hat, b_hat, HIGHEST), 0.0)
        l_ak = jnp.where(strict, _dot_nt(a_hat, k_hat, HIGHEST), 0.0)
        m_rb = jnp.where(incl, _dot_nt(r_hat, b_hat, HIGHEST), 0.0)
        m_rk = jnp.where(incl, _dot_nt(r_hat, k_hat, HIGHEST), 0.0)

        w_inv = eye + l_ab
        l_pow = l_ab
        for _ in range(int(math.log2(t)) - 1):
            l_pow = _dot_hi(l_pow, l_pow)
            w_inv = w_inv + _dot_hi(w_inv, l_pow)

        s0 = state_ref[h]
        z = _dot_hi(w_inv, _dot_hi(a_hat, s0) + _dot_hi(l_ak, v_h))
        y = _dot_hi(r_hat, s0) + _dot_hi(m_rb, z) + _dot_hi(m_rk, v_h)
        b_til = b_hat * pe_h
        k_til = k_hat * pe_h
        state_ref[h] = (_dot_hi(eye * pe_h, s0) + _dot_tn(b_til, z, HIGHEST)
                        + _dot_tn(k_til, v_h, HIGHEST))

        mean = jnp.mean(y, axis=-1, keepdims=True)
        var = jnp.mean(jnp.square(y - mean), axis=-1, keepdims=True)
        yn = (y - mean) * lax.rsqrt(var + GN_EPS) * gng_ref[:, sl] + gnb_ref[:, sl]
        bonus = jnp.sum(r_h * km_h * rk_ref[:, sl], axis=-1, keepdims=True) * v_h
        y_ref[0, :, sl] = ((yn + bonus) * gate[:, sl]).astype(BF16)


def _rwkv(rw, w2a2, g2, w0, a0, k_k, k_a, r_k, gn_g, gn_b):
    b, seq, _ = rw.shape
    nc = seq // CHUNK
    const = lambda i, c: (0, 0)
    vec = pl.BlockSpec((1, WIDTH), const)
    return pl.pallas_call(
        _rwkv_kernel,
        grid=(b, nc),
        in_specs=[
            pl.BlockSpec((1, CHUNK, RWKV_PROJ), lambda i, c: (i, c, 0)),
            pl.BlockSpec((DECAY_LORA + AAA_LORA, 2 * WIDTH), const),
            pl.BlockSpec((GATE_LORA, WIDTH), const),
            vec, vec, vec, vec, vec, vec, vec,
        ],
        out_specs=pl.BlockSpec((1, CHUNK, WIDTH), lambda i, c: (i, c, 0)),
        out_shape=jax.ShapeDtypeStruct((b, seq, WIDTH), BF16),
        scratch_shapes=[pltpu.VMEM((HEADS, HEAD_DIM, HEAD_DIM), F32)],
        compiler_params=pltpu.CompilerParams(
            dimension_semantics=("arbitrary", "arbitrary"), vmem_limit_bytes=VMEM_LIMIT),
        name="rwkv7",
    )(rw, w2a2, g2, w0, a0, k_k, k_a, r_k, gn_g, gn_b)


def _merge_kernel(x_ref, att_ref, rwkv_ref, gate_ref, wa_ref, wb_ref, wo_ref, g_ref,
                  wr_ref, br_ref, x1_ref, h2_ref, comb_ref):
    ga = gate_ref[:, 0:D_MODEL].astype(F32)
    gb = gate_ref[:, D_MODEL:2 * D_MODEL].astype(F32)
    merged = ga * _dot(att_ref[...], wa_ref[...]) + gb * _dot(rwkv_ref[...], wb_ref[...])
    x1 = x_ref[...] + _dot(merged.astype(BF16), wo_ref[...])
    x1_ref[...] = x1
    h2 = x1 * lax.rsqrt(jnp.mean(x1 * x1, axis=-1, keepdims=True) + RMS_EPS) * g_ref[...]
    h2_ref[...] = h2.astype(BF16)

    logits = _dot_hi(h2, wr_ref[...]) + br_ref[...]
    lane = lax.broadcasted_iota(jnp.int32, logits.shape, 1)
    neg = jnp.finfo(F32).min
    big = jnp.int32(ROUTER_LANES)

    def first_argmax(vals, mask):
        vm = jnp.where(mask, vals, neg)
        mx = jnp.max(vm, axis=-1, keepdims=True)
        idx = jnp.min(jnp.where(mask & (vm == mx), lane, big), axis=-1, keepdims=True)
        return mx, idx

    is_group = (lane >= GROUP_LANE0) & (lane < GROUP_LANE0 + N_GROUPS)
    g_max, g_lane = first_argmax(logits, is_group)
    g_prob = 1.0 / jnp.sum(jnp.where(is_group, jnp.exp(logits - g_max), 0.0),
                           axis=-1, keepdims=True)
    g_idx = g_lane - GROUP_LANE0
    in_group = (lane >> 3) == g_idx
    e1, i1 = first_argmax(logits, in_group)
    e2, i2 = first_argmax(logits, in_group & (lane != i1))
    w2 = jnp.exp(e2 - e1)
    p1 = 1.0 / (1.0 + w2)
    p2 = w2 / (1.0 + w2)
    comb_ref[...] = jnp.where(lane == i1, p1 * g_prob, jnp.where(lane == i2, p2 * g_prob, 0.0))


def _merge(x2, att, rwkv, gates, wa, wb, wo, ln_g, wr, br):
    n = x2.shape[0]
    tm = 512
    row = lambda i: (i, 0)
    const = lambda i: (0, 0)
    return pl.pallas_call(
        _merge_kernel,
        grid=(n // tm,),
        in_specs=[
            pl.BlockSpec((tm, D_MODEL), row),
            pl.BlockSpec((tm, WIDTH), row),
            pl.BlockSpec((tm, WIDTH), row),
            pl.BlockSpec((tm, 2 * D_MODEL), row),
            pl.BlockSpec((WIDTH, D_MODEL), const),
            pl.BlockSpec((WIDTH, D_MODEL), const),
            pl.BlockSpec((D_MODEL, D_MODEL), const),
            pl.BlockSpec((1, D_MODEL), const),
            pl.BlockSpec((D_MODEL, ROUTER_LANES), const),
            pl.BlockSpec((1, ROUTER_LANES), const),
        ],
        out_specs=[
            pl.BlockSpec((tm, D_MODEL), row),
            pl.BlockSpec((tm, D_MODEL), row),
            pl.BlockSpec((tm, ROUTER_LANES), row),
        ],
        out_shape=[
            jax.ShapeDtypeStruct((n, D_MODEL), F32),
            jax.ShapeDtypeStruct((n, D_MODEL), BF16),
            jax.ShapeDtypeStruct((n, ROUTER_LANES), F32),
        ],
        compiler_params=pltpu.CompilerParams(
            dimension_semantics=("arbitrary",), vmem_limit_bytes=VMEM_LIMIT),
        name="merge",
    )(x2, att, rwkv, gates, wa, wb, wo, ln_g, wr, br)


def _moe_kernel(x1_ref, h2_ref, comb_ref, wg_ref, wu_ref, wd_ref, lnf_ref, out_ref, acc_ref, *,
                final_norm):
    g = pl.program_id(1)

    @pl.when(g == 0)
    def _():
        acc_ref[...] = x1_ref[...]

    h2 = h2_ref[...]
    hg = _dot(h2, wg_ref[0])
    hu = _dot(h2, wu_ref[0])
    comb = comb_ref[...]
    lane = lax.broadcasted_iota(jnp.int32, (ROUTER_LANES, GROUP_FF), 0)
    colexp = lax.broadcasted_iota(jnp.int32, (ROUTER_LANES, GROUP_FF), 1) // D_EXPERT
    expand = jnp.where(lane == g * EXPERTS_PER_GROUP + colexp, 1.0, 0.0).astype(F32)
    cw = _dot_hi(comb, expand)
    act = hg * _sigmoid(hg) * hu * cw
    acc_ref[...] += _dot(act.astype(BF16), wd_ref[0])

    @pl.when(g == pl.num_programs(1) - 1)
    def _():
        y = acc_ref[...]
        if final_norm:
            y = y * lax.rsqrt(jnp.mean(y * y, axis=-1, keepdims=True) + RMS_EPS) * lnf_ref[...]
        out_ref[...] = y


def _moe(x1, h2, comb, wg, wu, wd, lnf, final_norm):
    n = x1.shape[0]
    tm = 512
    row = lambda i, g: (i, 0)
    return pl.pallas_call(
        functools.partial(_moe_kernel, final_norm=final_norm),
        grid=(n // tm, N_GROUPS),
        in_specs=[
            pl.BlockSpec((tm, D_MODEL), row),
            pl.BlockSpec((tm, D_MODEL), row),
            pl.BlockSpec((tm, ROUTER_LANES), row),
            pl.BlockSpec((1, D_MODEL, GROUP_FF), lambda i, g: (g, 0, 0)),
            pl.BlockSpec((1, D_MODEL, GROUP_FF), lambda i, g: (g, 0, 0)),
            pl.BlockSpec((1, GROUP_FF, D_MODEL), lambda i, g: (g, 0, 0)),
            pl.BlockSpec((1, D_MODEL), lambda i, g: (0, 0)),
        ],
        out_specs=pl.BlockSpec((tm, D_MODEL), row),
        out_shape=jax.ShapeDtypeStruct((n, D_MODEL), F32),
        scratch_shapes=[pltpu.VMEM((tm, D_MODEL), F32)],
        compiler_params=pltpu.CompilerParams(
            dimension_semantics=("arbitrary", "arbitrary"), vmem_limit_bytes=VMEM_LIMIT),
        name="moe",
    )(x1, h2, comb, wg, wu, wd, lnf)


def _group_major(w, transpose_in):
    if transpose_in:
        w = w.reshape(N_GROUPS, EXPERTS_PER_GROUP, D_MODEL, D_EXPERT)
        return jnp.transpose(w, (0, 2, 1, 3)).reshape(N_GROUPS, D_MODEL, GROUP_FF).astype(BF16)
    return w.reshape(N_GROUPS, GROUP_FF, D_MODEL).astype(BF16)


def kernel(x, ln_mix_g, w_in, att_rel_bias, rwkv_mu, rwkv_w0, rwkv_w2, rwkv_a0, rwkv_a2, rwkv_g2,
           rwkv_k_k, rwkv_k_a, rwkv_r_k, rwkv_gn_g, rwkv_gn_b, w_branch_att, w_branch_rwkv, w_out,
           ln_ffn_g, router_group_w, router_group_b, router_expert_w, router_expert_b,
           expert_w_gate, expert_w_up, expert_w_down, ln_final_g):
    bsz, seq, d = x.shape
    depth = w_in.shape[0]
    n = bsz * seq
    x2 = x.reshape(n, d)
    for l in range(depth):
        q, k, v, rw, gates = _in_proj(x2, ln_mix_g[l][None, :], w_in[l].astype(BF16),
                                      rwkv_mu[l][None, :], seq)
        bias = _rel_bias(att_rel_bias[l])
        att = _band_attn(q.reshape(bsz, seq, WIDTH), k.reshape(bsz, seq, WIDTH),
                         v.reshape(bsz, seq, WIDTH), bias)
        zeros = jnp.zeros((DECAY_LORA, WIDTH), F32)
        w2a2 = jnp.concatenate(
            [jnp.concatenate([rwkv_w2[l], zeros], axis=1),
             jnp.concatenate([zeros, rwkv_a2[l]], axis=1)], axis=0)
        rwkv = _rwkv(rw.reshape(bsz, seq, RWKV_PROJ), w2a2, rwkv_g2[l],
                     rwkv_w0[l][None, :], rwkv_a0[l][None, :], rwkv_k_k[l][None, :],
                     rwkv_k_a[l][None, :], rwkv_r_k[l].reshape(1, WIDTH),
                     rwkv_gn_g[l][None, :], rwkv_gn_b[l][None, :])
        wr = jnp.concatenate([router_expert_w[l], router_group_w[l]], axis=1)
        wr = jnp.pad(wr, ((0, 0), (0, ROUTER_LANES - wr.shape[1])))
        br = jnp.concatenate([router_expert_b[l], router_group_b[l]])
        br = jnp.pad(br, (0, ROUTER_LANES - br.shape[0]))[None, :]
        x1, h2, comb = _merge(x2, att.reshape(n, WIDTH), rwkv.reshape(n, WIDTH), gates,
                              w_branch_att[l].astype(BF16), w_branch_rwkv[l].astype(BF16),
                              w_out[l].astype(BF16), ln_ffn_g[l][None, :], wr, br)
        x2 = _moe(x1, h2, comb, _group_major(expert_w_gate[l], True),
                  _group_major(expert_w_up[l], True), _group_major(expert_w_down[l], False),
                  ln_final_g[None, :], final_norm=(l == depth - 1))
    return x2.reshape(bsz, seq, d)
```

```python
import functools
import math

import jax
import jax.numpy as jnp
from jax import lax
from jax.experimental import pallas as pl
from jax.experimental.pallas import tpu as pltpu

F32 = jnp.float32
BF16 = jnp.bfloat16
HIGHEST = lax.Precision.HIGHEST

D_MODEL = 1024
CHUNK = 64
HEADS = 8
HEAD_DIM = 64
WIDTH = HEADS * HEAD_DIM
LEFT_CHUNKS = 8
BAND = (LEFT_CHUNKS + 1) * CHUNK
REL_CLIP = 64
N_REL = 2 * REL_CLIP + 1
DECAY_LORA = 64
AAA_LORA = 64
GATE_LORA = 128
GN_EPS = 64e-5
RMS_EPS = 1e-6
ATT_PROJ = 3 * WIDTH
RWKV_PROJ = 3 * WIDTH + DECAY_LORA + AAA_LORA + GATE_LORA
D_IN = ATT_PROJ + RWKV_PROJ + 2 * D_MODEL
N_GROUPS = 4
EXPERTS_PER_GROUP = 8
N_EXPERTS = N_GROUPS * EXPERTS_PER_GROUP
D_EXPERT = 256
GROUP_FF = EXPERTS_PER_GROUP * D_EXPERT
ATT_GROUP = 4
ROUTER_LANES = 128
GROUP_LANE0 = N_EXPERTS
GROUP_ID_LANE = ROUTER_LANES - 1
MERGE_TILE = 512
MOE_TILE = 1024
MOE_BLOCK = 256
MERGE_PER_MOE = MOE_TILE // MERGE_TILE

VMEM_LIMIT = 52 * 1024 * 1024
MOE_VMEM_LIMIT = 58 * 1024 * 1024


def _dot(a, b):
    return jnp.dot(a, b, preferred_element_type=F32)


def _dot_hi(a, b):
    return jnp.dot(a, b, preferred_element_type=F32, precision=HIGHEST)


def _dot_nt(a, b, precision=None):
    return lax.dot_general(a, b, (((1,), (1,)), ((), ())),
                           preferred_element_type=F32, precision=precision)


def _dot_tn(a, b, precision=None):
    return lax.dot_general(a, b, (((0,), (0,)), ((), ())),
                           preferred_element_type=F32, precision=precision)


def _sigmoid(x):
    return 1.0 / (1.0 + jnp.exp(-x))


def _mm(a, b):
    return jnp.dot(a.astype(BF16), b.astype(BF16), preferred_element_type=F32)


def _head_sums(x):
    outs = []
    lane = lax.broadcasted_iota(jnp.int32, (x.shape[0], 2 * HEAD_DIM), 1)
    low = lane < HEAD_DIM
    for p in range(HEADS // 2):
        xp = x[:, 2 * HEAD_DIM * p:2 * HEAD_DIM * (p + 1)]
        s_lo = jnp.sum(jnp.where(low, xp, 0.0), axis=-1, keepdims=True)
        s_hi = jnp.sum(jnp.where(low, 0.0, xp), axis=-1, keepdims=True)
        outs.append(jnp.where(low, s_lo, s_hi))
    return jnp.concatenate(outs, axis=-1)


def _rel_bias_kernel(tab_ref, out_ref):
    rows = tab_ref.shape[1]
    n = lax.broadcasted_iota(jnp.int32, (rows, CHUNK * 128), 1)
    r = lax.broadcasted_iota(jnp.int32, (rows, CHUNK * 128), 0)
    q = n >> 7
    kk = n & 127
    idx = jnp.clip(CHUNK + q - kk, -REL_CLIP, REL_CLIP) + REL_CLIP
    onehot = jnp.where(r == idx, 1.0, 0.0).astype(F32)
    out_ref[...] = _dot_hi(tab_ref[...], onehot)


def _rel_bias(rel_table):
    rows = 136
    tab = jnp.pad(rel_table.astype(F32), ((0, 0), (0, rows - N_REL)))
    tail = pl.pallas_call(
        _rel_bias_kernel,
        out_shape=jax.ShapeDtypeStruct((HEADS, CHUNK * 128), F32),
        name="rel_bias",
    )(tab)
    tail = tail.reshape(HEADS, CHUNK, 128)
    head = jnp.broadcast_to(rel_table.astype(F32)[:, N_REL - 1][:, None, None],
                            (HEADS, CHUNK, BAND - 128))
    return jnp.concatenate([head, tail], axis=-1).reshape(HEADS * CHUNK, BAND)


def _in_proj_kernel(x_ref, g_ref, w_ref, mu_ref, q_ref, k_ref, v_ref, rw_ref, gate_ref,
                    carry_ref, *, tiles_per_seq):
    i = pl.program_id(0)

    @pl.when(i == 0)
    def _():
        carry_ref[...] = jnp.zeros(carry_ref.shape, F32)

    x = x_ref[...]
    h = x * lax.rsqrt(jnp.mean(x * x, axis=-1, keepdims=True) + RMS_EPS) * g_ref[...]
    hb = h.astype(BF16)
    q_ref[...] = _dot(hb, w_ref[:, 0:WIDTH]).astype(BF16)
    k_ref[...] = _dot(hb, w_ref[:, WIDTH:2 * WIDTH]).astype(BF16)
    v_ref[...] = _dot(hb, w_ref[:, 2 * WIDTH:ATT_PROJ]).astype(BF16)
    rw = _dot(hb, w_ref[:, ATT_PROJ:ATT_PROJ + RWKV_PROJ])
    tm = rw.shape[0]
    first_prev = jnp.where(i % tiles_per_seq == 0, 0.0, carry_ref[0:1, :])
    rolled = pltpu.roll(rw, 1, axis=0)
    row = lax.broadcasted_iota(jnp.int32, rw.shape, 0)
    prev = jnp.where(row == 0, first_prev, rolled)
    carry_ref[0:1, :] = rw[tm - 1:tm, :]
    rw_ref[...] = rw + (prev - rw) * mu_ref[...]
    gate_ref[...] = _sigmoid(_dot(hb, w_ref[:, ATT_PROJ + RWKV_PROJ:D_IN])).astype(BF16)


def _in_proj(x2, ln_g, w_in_b, mu, seq):
    n = x2.shape[0]
    tm = 256
    row = lambda i: (i, 0)
    const = lambda i: (0, 0)
    return pl.pallas_call(
        functools.partial(_in_proj_kernel, tiles_per_seq=seq // tm),
        grid=(n // tm,),
        in_specs=[
            pl.BlockSpec((tm, D_MODEL), row),
            pl.BlockSpec((1, D_MODEL), const),
            pl.BlockSpec((D_MODEL, D_IN), const),
            pl.BlockSpec((1, RWKV_PROJ), const),
        ],
        out_specs=[
            pl.BlockSpec((tm, WIDTH), row),
            pl.BlockSpec((tm, WIDTH), row),
            pl.BlockSpec((tm, WIDTH), row),
            pl.BlockSpec((tm, RWKV_PROJ), row),
            pl.BlockSpec((tm, 2 * D_MODEL), row),
        ],
        out_shape=[
            jax.ShapeDtypeStruct((n, WIDTH), BF16),
            jax.ShapeDtypeStruct((n, WIDTH), BF16),
            jax.ShapeDtypeStruct((n, WIDTH), BF16),
            jax.ShapeDtypeStruct((n, RWKV_PROJ), F32),
            jax.ShapeDtypeStruct((n, 2 * D_MODEL), BF16),
        ],
        scratch_shapes=[pltpu.VMEM((8, RWKV_PROJ), F32)],
        compiler_params=pltpu.CompilerParams(
            dimension_semantics=("arbitrary",), vmem_limit_bytes=VMEM_LIMIT),
        name="in_proj",
    )(x2, ln_g, w_in_b, mu)


def _band_attn_kernel(q_ref, k_ref, v_ref, bias_ref, o_ref, kpad_ref, vpad_ref):
    c = pl.program_id(1)
    seq = k_ref.shape[1]
    pad = LEFT_CHUNKS * CHUNK

    @pl.when(c == 0)
    def _():
        kpad_ref[0:pad, :] = jnp.zeros((pad, WIDTH), BF16)
        vpad_ref[0:pad, :] = jnp.zeros((pad, WIDTH), BF16)
        kpad_ref[pad:pad + seq, :] = k_ref[0]
        vpad_ref[pad:pad + seq, :] = v_ref[0]

    start = pl.multiple_of(c * CHUNK, CHUNK)
    kb = kpad_ref[pl.ds(start, BAND), :]
    vb = vpad_ref[pl.ds(start, BAND), :]
    q = q_ref[0]
    gw = ATT_GROUP * HEAD_DIM
    rows = ATT_GROUP * CHUNK
    r_head = lax.broadcasted_iota(jnp.int32, (rows, gw), 0) // CHUNK
    l_head = lax.broadcasted_iota(jnp.int32, (rows, gw), 1) // HEAD_DIM
    own = r_head == l_head
    kpos = lax.broadcasted_iota(jnp.int32, (rows, BAND), 1)
    valid = kpos >= (LEFT_CHUNKS - c) * CHUNK
    neg = jnp.finfo(F32).min
    groups = range(HEADS // ATT_GROUP)
    lanes = [slice(g * gw, (g + 1) * gw) for g in groups]
    qrows = [jnp.where(own, jnp.concatenate([q[:, sl]] * ATT_GROUP, axis=0), jnp.zeros((), BF16))
             for sl in lanes]
    s = [_dot_nt(qrows[g], kb[:, lanes[g]]) * (HEAD_DIM ** -0.5) + bias_ref[g * rows:(g + 1) * rows, :]
         for g in groups]
    s = [jnp.where(valid, s[g], neg) for g in groups]
    p = [jnp.exp(s[g] - jnp.max(s[g], axis=-1, keepdims=True)) for g in groups]
    denom = [jnp.sum(p[g], axis=-1, keepdims=True) for g in groups]
    o_all = [_dot(p[g].astype(BF16), vb[:, lanes[g]]) / denom[g] for g in groups]
    for g in groups:
        o_own = jnp.where(own, o_all[g], 0.0)
        o = o_own[0:CHUNK]
        for h in range(1, ATT_GROUP):
            o = o + o_own[h * CHUNK:(h + 1) * CHUNK]
        o_ref[0, :, lanes[g]] = o.astype(BF16)


def _band_attn(q, k, v, bias):
    b, seq, _ = q.shape
    nc = seq // CHUNK
    return pl.pallas_call(
        _band_attn_kernel,
        grid=(b, nc),
        in_specs=[
            pl.BlockSpec((1, CHUNK, WIDTH), lambda i, c: (i, c, 0)),
            pl.BlockSpec((1, seq, WIDTH), lambda i, c: (i, 0, 0)),
            pl.BlockSpec((1, seq, WIDTH), lambda i, c: (i, 0, 0)),
            pl.BlockSpec((HEADS * CHUNK, BAND), lambda i, c: (0, 0)),
        ],
        out_specs=pl.BlockSpec((1, CHUNK, WIDTH), lambda i, c: (i, c, 0)),
        out_shape=jax.ShapeDtypeStruct((b, seq, WIDTH), BF16),
        scratch_shapes=[pltpu.VMEM((seq + LEFT_CHUNKS * CHUNK, WIDTH), BF16),
                        pltpu.VMEM((seq + LEFT_CHUNKS * CHUNK, WIDTH), BF16)],
        compiler_params=pltpu.CompilerParams(
            dimension_semantics=("arbitrary", "arbitrary"), vmem_limit_bytes=VMEM_LIMIT),
        name="band_attn",
    )(q, k, v, bias)


def _rwkv_kernel(rw_ref, w2a2_ref, g2_ref, w0_ref, a0_ref, kk_ref, ka_ref, rk_ref,
                 gng_ref, gnb_ref, y_ref, state_ref):
    c = pl.program_id(1)
    t = CHUNK

    @pl.when(c == 0)
    def _():
        state_ref[...] = jnp.zeros(state_ref.shape, F32)

    rw = rw_ref[0]
    r = rw[:, 0:WIDTH]
    k = rw[:, WIDTH:2 * WIDTH]
    v = rw[:, 2 * WIDTH:3 * WIDTH]
    lora = rw[:, 3 * WIDTH:3 * WIDTH + DECAY_LORA + AAA_LORA]
    g_lo = rw[:, 3 * WIDTH + DECAY_LORA + AAA_LORA:RWKV_PROJ]

    lane128 = lax.broadcasted_iota(jnp.int32, lora.shape, 1)
    lora = jnp.where(lane128 < DECAY_LORA, jnp.tanh(lora), lora)
    wa = _dot_hi(lora, w2a2_ref[...])
    log_decay = -math.exp(-0.5) * _sigmoid(w0_ref[...] + wa[:, 0:WIDTH])
    lr = _sigmoid(a0_ref[...] + wa[:, WIDTH:2 * WIDTH])
    gate = _dot_hi(_sigmoid(g_lo), g2_ref[...])

    kk_raw = k * kk_ref[...]
    k_mod = k * (1.0 + (lr - 1.0) * ka_ref[...])

    row = lax.broadcasted_iota(jnp.int32, (t, t), 0)
    col = lax.broadcasted_iota(jnp.int32, (t, t), 1)
    tri_incl = jnp.where(row >= col, 1.0, 0.0).astype(F32)
    logp = _dot_hi(tri_incl, log_decay)
    p_in = jnp.exp(logp)
    p_ex = jnp.exp(logp - log_decay)
    p_inv = jnp.exp(-logp)
    p_end = p_in[t - 1:t, :]

    kk = kk_raw / jnp.maximum(jnp.sqrt(_head_sums(kk_raw * kk_raw)), 1e-12)
    a_hat = -kk * p_ex
    r_hat = r * p_in
    b_hat = kk * lr * p_inv
    k_hat = k_mod * p_inv
    b_til = b_hat * p_end
    k_til = k_hat * p_end
    bonus = _head_sums(r * k_mod * rk_ref[...]) * v

    lane = lax.broadcasted_iota(jnp.int32, (t, 2 * HEAD_DIM), 1)
    low = lane < HEAD_DIM
    r2 = lax.broadcasted_iota(jnp.int32, (2 * t, 2 * HEAD_DIM), 0)
    c2 = lax.broadcasted_iota(jnp.int32, (2 * t, 2 * HEAD_DIM), 1)
    own = (r2 < t) == (c2 < HEAD_DIM)
    strict = (r2 & (t - 1)) > (c2 & (t - 1))
    incl = (r2 & (t - 1)) >= (c2 & (t - 1))
    eye = jnp.where(r2 == c2, 1.0, 0.0).astype(F32)

    def stack2(xp):
        return jnp.concatenate([jnp.where(low, xp, 0.0), jnp.where(low, 0.0, xp)], axis=0)

    pairs = range(HEADS // 2)
    sls = [slice(2 * HEAD_DIM * p, 2 * HEAD_DIM * (p + 1)) for p in pairs]
    ar = [jnp.concatenate([stack2(a_hat[:, sl]), stack2(r_hat[:, sl])], axis=0).astype(BF16)
          for sl in sls]
    bk = [jnp.concatenate([stack2(b_hat[:, sl]), stack2(k_hat[:, sl])], axis=0).astype(BF16)
          for sl in sls]
    v2 = [stack2(v[:, sl]).astype(BF16) for sl in sls]
    btkt = [jnp.concatenate([stack2(b_til[:, sl]), stack2(k_til[:, sl])], axis=0).astype(BF16)
            for sl in sls]
    g = [_dot_nt(ar[p], bk[p]) for p in pairs]
    st = [state_ref[p] for p in pairs]
    ars = [_dot_nt(ar[p], st[p].astype(BF16)) for p in pairs]
    l_ab = [jnp.where(strict, g[p][0:2 * t, 0:2 * t], 0.0) for p in pairs]
    lm = [jnp.concatenate([jnp.where(strict, g[p][0:2 * t, 2 * t:4 * t], 0.0),
                           jnp.where(incl, g[p][2 * t:4 * t, 2 * t:4 * t], 0.0)], axis=0)
          for p in pairs]
    m_rb = [jnp.where(incl, g[p][2 * t:4 * t, 0:2 * t], 0.0).astype(BF16) for p in pairs]
    lv = [_mm(lm[p], v2[p]) for p in pairs]

    w_inv = [eye + l_ab[p] for p in pairs]
    l_pow = l_ab
    for _ in range(int(math.log2(t)) - 1):
        l_pow = [_mm(l_pow[p], l_pow[p]) for p in pairs]
        w_inv = [w_inv[p] + _mm(w_inv[p], l_pow[p]) for p in pairs]

    z = [_mm(w_inv[p], ars[p][0:2 * t] + lv[p][0:2 * t]) for p in pairs]
    y = [ars[p][2 * t:4 * t] + lv[p][2 * t:4 * t] + _mm(m_rb[p], z[p]) for p in pairs]
    for p in pairs:
        zv = jnp.concatenate([z[p].astype(BF16), v2[p]], axis=0)
        state_ref[p] = st[p] * p_end[:, sls[p]] + _dot_tn(zv, btkt[p])

    for p in pairs:
        sl = sls[p]
        mean = jnp.sum(y[p], axis=-1, keepdims=True) * (1.0 / HEAD_DIM)
        dev = jnp.where(own, y[p] - mean, 0.0)
        var = jnp.sum(dev * dev, axis=-1, keepdims=True) * (1.0 / HEAD_DIM)
        yn = dev * lax.rsqrt(var + GN_EPS)
        yn = yn[0:t] + yn[t:2 * t]
        out = (yn * gng_ref[:, sl] + gnb_ref[:, sl] + bonus[:, sl]) * gate[:, sl]
        y_ref[0, :, sl] = out.astype(BF16)


def _rwkv(rw, w2a2, g2, w0, a0, k_k, k_a, r_k, gn_g, gn_b):
    b, seq, _ = rw.shape
    nc = seq // CHUNK
    const = lambda i, c: (0, 0)
    vec = pl.BlockSpec((1, WIDTH), const)
    return pl.pallas_call(
        _rwkv_kernel,
        grid=(b, nc),
        in_specs=[
            pl.BlockSpec((1, CHUNK, RWKV_PROJ), lambda i, c: (i, c, 0)),
            pl.BlockSpec((DECAY_LORA + AAA_LORA, 2 * WIDTH), const),
            pl.BlockSpec((GATE_LORA, WIDTH), const),
            vec, vec, vec, vec, vec, vec, vec,
        ],
        out_specs=pl.BlockSpec((1, CHUNK, WIDTH), lambda i, c: (i, c, 0)),
        out_shape=jax.ShapeDtypeStruct((b, seq, WIDTH), BF16),
        scratch_shapes=[pltpu.VMEM((HEADS // 2, 2 * HEAD_DIM, 2 * HEAD_DIM), F32)],
        compiler_params=pltpu.CompilerParams(
            dimension_semantics=("arbitrary", "arbitrary"), vmem_limit_bytes=VMEM_LIMIT),
        name="rwkv7",
    )(rw, w2a2, g2, w0, a0, k_k, k_a, r_k, gn_g, gn_b)


def _merge_kernel(x_ref, att_ref, rwkv_ref, gate_ref, wa_ref, wb_ref, wo_ref, g_ref,
                  wr_ref, br_ref, x1_ref, h2_ref, comb_ref, cnt_ref):
    ga = gate_ref[:, 0:D_MODEL].astype(F32)
    gb = gate_ref[:, D_MODEL:2 * D_MODEL].astype(F32)
    merged = ga * _dot(att_ref[...], wa_ref[...]) + gb * _dot(rwkv_ref[...], wb_ref[...])
    x1 = x_ref[...] + _dot(merged.astype(BF16), wo_ref[...])
    x1_ref[...] = x1
    h2 = x1 * lax.rsqrt(jnp.mean(x1 * x1, axis=-1, keepdims=True) + RMS_EPS) * g_ref[...]
    h2_ref[...] = h2.astype(BF16)

    logits = _dot_hi(h2, wr_ref[...]) + br_ref[...]
    lane = lax.broadcasted_iota(jnp.int32, logits.shape, 1)
    neg = jnp.finfo(F32).min
    big = jnp.int32(ROUTER_LANES)

    def first_argmax(vals, mask):
        vm = jnp.where(mask, vals, neg)
        mx = jnp.max(vm, axis=-1, keepdims=True)
        idx = jnp.min(jnp.where(mask & (vm == mx), lane, big), axis=-1, keepdims=True)
        return mx, idx

    is_group = (lane >= GROUP_LANE0) & (lane < GROUP_LANE0 + N_GROUPS)
    g_max, g_lane = first_argmax(logits, is_group)
    g_prob = 1.0 / jnp.sum(jnp.where(is_group, jnp.exp(logits - g_max), 0.0),
                           axis=-1, keepdims=True)
    g_idx = g_lane - GROUP_LANE0
    in_group = (lane >> 3) == g_idx
    e1, i1 = first_argmax(logits, in_group)
    e2, i2 = first_argmax(logits, in_group & (lane != i1))
    w2 = jnp.exp(e2 - e1)
    p1 = 1.0 / (1.0 + w2)
    p2 = w2 / (1.0 + w2)
    comb = jnp.where(lane == i1, p1 * g_prob, jnp.where(lane == i2, p2 * g_prob, 0.0))
    comb_ref[...] = jnp.where(lane == GROUP_ID_LANE, g_idx.astype(F32), comb)
    counts = jnp.sum(jnp.where(lane == g_idx, 1.0, 0.0), axis=0, keepdims=True)
    cnt_ref[0] = jnp.broadcast_to(counts, (8, ROUTER_LANES)).astype(jnp.int32)


def _merge(x2, att, rwkv, gates, wa, wb, wo, ln_g, wr, br):
    n = x2.shape[0]
    tm = MERGE_TILE
    row = lambda i: (i, 0)
    const = lambda i: (0, 0)
    return pl.pallas_call(
        _merge_kernel,
        grid=(n // tm,),
        in_specs=[
            pl.BlockSpec((tm, D_MODEL), row),
            pl.BlockSpec((tm, WIDTH), row),
            pl.BlockSpec((tm, WIDTH), row),
            pl.BlockSpec((tm, 2 * D_MODEL), row),
            pl.BlockSpec((WIDTH, D_MODEL), const),
            pl.BlockSpec((WIDTH, D_MODEL), const),
            pl.BlockSpec((D_MODEL, D_MODEL), const),
            pl.BlockSpec((1, D_MODEL), const),
            pl.BlockSpec((D_MODEL, ROUTER_LANES), const),
            pl.BlockSpec((1, ROUTER_LANES), const),
        ],
        out_specs=[
            pl.BlockSpec((tm, D_MODEL), row),
            pl.BlockSpec((tm, D_MODEL), row),
            pl.BlockSpec((tm, ROUTER_LANES), row),
            pl.BlockSpec((1, 8, ROUTER_LANES), lambda i: (i, 0, 0)),
        ],
        out_shape=[
            jax.ShapeDtypeStruct((n, D_MODEL), F32),
            jax.ShapeDtypeStruct((n, D_MODEL), BF16),
            jax.ShapeDtypeStruct((n, ROUTER_LANES), F32),
            jax.ShapeDtypeStruct((n // tm, 8, ROUTER_LANES), jnp.int32),
        ],
        compiler_params=pltpu.CompilerParams(
            dimension_semantics=("arbitrary",), vmem_limit_bytes=VMEM_LIMIT),
        name="merge",
    )(x2, att, rwkv, gates, wa, wb, wo, ln_g, wr, br)


def _moe_kernel(cnt_ref, x1_ref, h2_ref, comb_ref, wg_ref, wu_ref, wd_ref, lnf_ref, out_ref,
                keyc_ref, keyr_ref, *, final_norm):
    i = pl.program_id(0)
    g = pl.program_id(1)
    t = MOE_TILE
    r = MOE_BLOCK

    @pl.when(g == 0)
    def _():
        out_ref[...] = jnp.zeros(out_ref.shape, F32)
        lane = lax.broadcasted_iota(jnp.int32, (t, ROUTER_LANES), 1).astype(F32)
        gid = comb_ref[:, GROUP_ID_LANE:GROUP_ID_LANE + 1]
        onehot = jnp.where(lane == gid, 1.0, 0.0)
        onehot_b = onehot.astype(BF16)
        for ch in range(t // r):
            rows = lax.broadcasted_iota(jnp.int32, (r, t), 0) + ch * r
            cols = lax.broadcasted_iota(jnp.int32, (r, t), 1)
            earlier = jnp.where(cols < rows, 1.0, 0.0).astype(BF16)
            before = _dot(earlier, onehot_b)
            sl = slice(ch * r, (ch + 1) * r)
            rank = jnp.sum(before * onehot[sl], axis=-1, keepdims=True)
            keyc_ref[sl, :] = jnp.broadcast_to(gid[sl] * t + rank, (r, ROUTER_LANES))
        keyr_ref[...] = jnp.transpose(keyc_ref[...])[0:8, :]

    quarter = t // N_GROUPS
    row0 = pl.multiple_of(g * quarter, quarter)
    out_ref[pl.ds(row0, quarter), :] += x1_ref[...]

    n_tok = cnt_ref[(i * MERGE_PER_MOE) * N_GROUPS + g]
    for m in range(1, MERGE_PER_MOE):
        n_tok = n_tok + cnt_ref[(i * MERGE_PER_MOE + m) * N_GROUPS + g]
    n_blocks = (n_tok + r - 1) // r

    lane_e = lax.broadcasted_iota(jnp.int32, (ROUTER_LANES, GROUP_FF), 0)
    col_e = lax.broadcasted_iota(jnp.int32, (ROUTER_LANES, GROUP_FF), 1) // D_EXPERT
    expand = jnp.where(lane_e == g * EXPERTS_PER_GROUP + col_e, 1.0, 0.0).astype(BF16)
    comb_b = comb_ref[...].astype(BF16)
    key_row = keyr_ref[0:1, :]
    key_col = jnp.concatenate([keyc_ref[...]] * (r // ROUTER_LANES), axis=1)

    def block(j, carry):
        base = (g * t + j * r).astype(F32)
        want_r = lax.broadcasted_iota(jnp.int32, (r, t), 0).astype(F32) + base
        take = jnp.where(want_r == key_row, 1.0, 0.0).astype(BF16)
        hs = _dot(take, h2_ref[...]).astype(BF16)
        cs = _dot(take, comb_b).astype(BF16)
        cw = _dot(cs, expand)
        y = jnp.zeros((r, D_MODEL), F32)
        for e in range(EXPERTS_PER_GROUP):
            hg = _dot(hs, wg_ref[e])
            hu = _dot(hs, wu_ref[e])
            act = hg * _sigmoid(hg) * hu * cw[:, e * D_EXPERT:(e + 1) * D_EXPERT]
            y = y + _dot(act.astype(BF16), wd_ref[e])
        want_c = lax.broadcasted_iota(jnp.int32, (t, r), 1).astype(F32) + base
        put = jnp.where(want_c == key_col, 1.0, 0.0).astype(BF16)
        out_ref[...] += _dot(put, y.astype(BF16))
        return carry

    lax.fori_loop(0, n_blocks, block, 0)

    if final_norm:
        @pl.when(g == N_GROUPS - 1)
        def _():
            y = out_ref[...]
            out_ref[...] = y * lax.rsqrt(jnp.mean(y * y, axis=-1, keepdims=True) + RMS_EPS) * lnf_ref[...]


def _moe(cnt, x1, h2, comb, wg, wu, wd, lnf, final_norm):
    n = x1.shape[0]
    t = MOE_TILE
    quarter = t // N_GROUPS
    row = lambda i, g, cnt: (i, 0)
    grp = lambda i, g, cnt: (g, 0, 0)
    grid_spec = pltpu.PrefetchScalarGridSpec(
        num_scalar_prefetch=1,
        grid=(n // t, N_GROUPS),
        in_specs=[
            pl.BlockSpec((quarter, D_MODEL), lambda i, g, cnt: (i * N_GROUPS + g, 0)),
            pl.BlockSpec((t, D_MODEL), row),
            pl.BlockSpec((t, ROUTER_LANES), row),
            pl.BlockSpec((EXPERTS_PER_GROUP, D_MODEL, D_EXPERT), grp),
            pl.BlockSpec((EXPERTS_PER_GROUP, D_MODEL, D_EXPERT), grp),
            pl.BlockSpec((EXPERTS_PER_GROUP, D_EXPERT, D_MODEL), grp),
            pl.BlockSpec((1, D_MODEL), lambda i, g, cnt: (0, 0)),
        ],
        out_specs=pl.BlockSpec((t, D_MODEL), row),
        scratch_shapes=[pltpu.VMEM((t, ROUTER_LANES), F32), pltpu.VMEM((8, t), F32)],
    )
    return pl.pallas_call(
        functools.partial(_moe_kernel, final_norm=final_norm),
        grid_spec=grid_spec,
        out_shape=jax.ShapeDtypeStruct((n, D_MODEL), F32),
        compiler_params=pltpu.CompilerParams(
            dimension_semantics=("arbitrary", "arbitrary"), vmem_limit_bytes=MOE_VMEM_LIMIT),
        name="moe",
    )(cnt, x1, h2, comb, wg, wu, wd, lnf)


def kernel(x, ln_mix_g, w_in, att_rel_bias, rwkv_mu, rwkv_w0, rwkv_w2, rwkv_a0, rwkv_a2, rwkv_g2,
           rwkv_k_k, rwkv_k_a, rwkv_r_k, rwkv_gn_g, rwkv_gn_b, w_branch_att, w_branch_rwkv, w_out,
           ln_ffn_g, router_group_w, router_group_b, router_expert_w, router_expert_b,
           expert_w_gate, expert_w_up, expert_w_down, ln_final_g):
    bsz, seq, d = x.shape
    depth = w_in.shape[0]
    n = bsz * seq
    x2 = x.reshape(n, d)
    for l in range(depth):
        q, k, v, rw, gates = _in_proj(x2, ln_mix_g[l][None, :], w_in[l].astype(BF16),
                                      rwkv_mu[l][None, :], seq)
        bias = _rel_bias(att_rel_bias[l])
        att = _band_attn(q.reshape(bsz, seq, WIDTH), k.reshape(bsz, seq, WIDTH),
                         v.reshape(bsz, seq, WIDTH), bias)
        zeros = jnp.zeros((DECAY_LORA, WIDTH), F32)
        w2a2 = jnp.concatenate(
            [jnp.concatenate([rwkv_w2[l], zeros], axis=1),
             jnp.concatenate([zeros, rwkv_a2[l]], axis=1)], axis=0)
        rwkv = _rwkv(rw.reshape(bsz, seq, RWKV_PROJ), w2a2, rwkv_g2[l],
                     rwkv_w0[l][None, :], rwkv_a0[l][None, :], rwkv_k_k[l][None, :],
                     rwkv_k_a[l][None, :], rwkv_r_k[l].reshape(1, WIDTH),
                     rwkv_gn_g[l][None, :], rwkv_gn_b[l][None, :])
        wr = jnp.concatenate([router_expert_w[l], router_group_w[l]], axis=1)
        wr = jnp.pad(wr, ((0, 0), (0, ROUTER_LANES - wr.shape[1])))
        br = jnp.concatenate([router_expert_b[l], router_group_b[l]])
        br = jnp.pad(br, (0, ROUTER_LANES - br.shape[0]))[None, :]
        x1, h2, comb, counts = _merge(x2, att.reshape(n, WIDTH), rwkv.reshape(n, WIDTH), gates,
                                      w_branch_att[l].astype(BF16), w_branch_rwkv[l].astype(BF16),
                                      w_out[l].astype(BF16), ln_ffn_g[l][None, :], wr, br)
        cnt = counts[:, 0, :N_GROUPS].reshape(-1)
        x2 = _moe(cnt, x1, h2, comb, expert_w_gate[l].astype(BF16), expert_w_up[l].astype(BF16),
                  expert_w_down[l].astype(BF16), ln_final_g[None, :], final_norm=(l == depth - 1))
    return x2.reshape(bsz, seq, d)
```

```python
import functools
import math

import jax
import jax.numpy as jnp
from jax import lax
from jax.experimental import pallas as pl
from jax.experimental.pallas import tpu as pltpu

F32 = jnp.float32
BF16 = jnp.bfloat16
HIGHEST = lax.Precision.HIGHEST

D_MODEL = 1024
CHUNK = 64
HEADS = 8
HEAD_DIM = 64
WIDTH = HEADS * HEAD_DIM
LEFT_CHUNKS = 8
BAND = (LEFT_CHUNKS + 1) * CHUNK
REL_CLIP = 64
N_REL = 2 * REL_CLIP + 1
DECAY_LORA = 64
AAA_LORA = 64
GATE_LORA = 128
GN_EPS = 64e-5
RMS_EPS = 1e-6
ATT_PROJ = 3 * WIDTH
RWKV_PROJ = 3 * WIDTH + DECAY_LORA + AAA_LORA + GATE_LORA
D_IN = ATT_PROJ + RWKV_PROJ + 2 * D_MODEL
N_GROUPS = 4
EXPERTS_PER_GROUP = 8
N_EXPERTS = N_GROUPS * EXPERTS_PER_GROUP
D_EXPERT = 256
GROUP_FF = EXPERTS_PER_GROUP * D_EXPERT
RWKV_SEQS = 4
ATT_GROUP = 4
ROUTER_LANES = 128
GROUP_LANE0 = N_EXPERTS
GROUP_ID_LANE = ROUTER_LANES - 1
MERGE_TILE = 512
MOE_TILE = 1024
MOE_BLOCK = 256
MERGE_PER_MOE = MOE_TILE // MERGE_TILE

VMEM_LIMIT = 52 * 1024 * 1024
MOE_VMEM_LIMIT = 58 * 1024 * 1024


def _dot(a, b):
    return jnp.dot(a, b, preferred_element_type=F32)


def _dot_hi(a, b):
    return jnp.dot(a, b, preferred_element_type=F32, precision=HIGHEST)


def _dot_nt(a, b, precision=None):
    return lax.dot_general(a, b, (((1,), (1,)), ((), ())),
                           preferred_element_type=F32, precision=precision)


def _dot_tn(a, b, precision=None):
    return lax.dot_general(a, b, (((0,), (0,)), ((), ())),
                           preferred_element_type=F32, precision=precision)


def _sigmoid(x):
    return 1.0 / (1.0 + jnp.exp(-x))


def _mm(a, b):
    return jnp.dot(a.astype(BF16), b.astype(BF16), preferred_element_type=F32)


def _head_sums(x):
    outs = []
    lane = lax.broadcasted_iota(jnp.int32, (x.shape[0], 2 * HEAD_DIM), 1)
    low = lane < HEAD_DIM
    for p in range(HEADS // 2):
        xp = x[:, 2 * HEAD_DIM * p:2 * HEAD_DIM * (p + 1)]
        s_lo = jnp.sum(jnp.where(low, xp, 0.0), axis=-1, keepdims=True)
        s_hi = jnp.sum(jnp.where(low, 0.0, xp), axis=-1, keepdims=True)
        outs.append(jnp.where(low, s_lo, s_hi))
    return jnp.concatenate(outs, axis=-1)


def _rel_bias_kernel(tab_ref, out_ref):
    rows = tab_ref.shape[1]
    n = lax.broadcasted_iota(jnp.int32, (rows, CHUNK * 128), 1)
    r = lax.broadcasted_iota(jnp.int32, (rows, CHUNK * 128), 0)
    q = n >> 7
    kk = n & 127
    idx = jnp.clip(CHUNK + q - kk, -REL_CLIP, REL_CLIP) + REL_CLIP
    onehot = jnp.where(r == idx, 1.0, 0.0).astype(F32)
    out_ref[...] = _dot_hi(tab_ref[...], onehot)


def _rel_bias(rel_table):
    rows = 136
    tab = jnp.pad(rel_table.astype(F32), ((0, 0), (0, rows - N_REL)))
    tail = pl.pallas_call(
        _rel_bias_kernel,
        out_shape=jax.ShapeDtypeStruct((HEADS, CHUNK * 128), F32),
        name="rel_bias",
    )(tab)
    tail = tail.reshape(HEADS, CHUNK, 128)
    head = jnp.broadcast_to(rel_table.astype(F32)[:, N_REL - 1][:, None, None],
                            (HEADS, CHUNK, BAND - 128))
    return jnp.concatenate([head, tail], axis=-1).reshape(HEADS * CHUNK, BAND)


def _in_proj_kernel(x_ref, g_ref, w_ref, mu_ref, q_ref, k_ref, v_ref, rw_ref, gate_ref,
                    carry_ref, *, tiles_per_seq):
    i = pl.program_id(0)

    @pl.when(i == 0)
    def _():
        carry_ref[...] = jnp.zeros(carry_ref.shape, F32)

    x = x_ref[...]
    h = x * lax.rsqrt(jnp.mean(x * x, axis=-1, keepdims=True) + RMS_EPS) * g_ref[...]
    hb = h.astype(BF16)
    q_ref[...] = _dot(hb, w_ref[:, 0:WIDTH]).astype(BF16)
    k_ref[...] = _dot(hb, w_ref[:, WIDTH:2 * WIDTH]).astype(BF16)
    v_ref[...] = _dot(hb, w_ref[:, 2 * WIDTH:ATT_PROJ]).astype(BF16)
    rw = _dot(hb, w_ref[:, ATT_PROJ:ATT_PROJ + RWKV_PROJ])
    tm = rw.shape[0]
    first_prev = jnp.where(i % tiles_per_seq == 0, 0.0, carry_ref[0:1, :])
    rolled = pltpu.roll(rw, 1, axis=0)
    row = lax.broadcasted_iota(jnp.int32, rw.shape, 0)
    prev = jnp.where(row == 0, first_prev, rolled)
    carry_ref[0:1, :] = rw[tm - 1:tm, :]
    rw_ref[...] = rw + (prev - rw) * mu_ref[...]
    gate_ref[...] = _sigmoid(_dot(hb, w_ref[:, ATT_PROJ + RWKV_PROJ:D_IN])).astype(BF16)


def _in_proj(x2, ln_g, w_in_b, mu, seq):
    n = x2.shape[0]
    tm = 256
    row = lambda i: (i, 0)
    const = lambda i: (0, 0)
    return pl.pallas_call(
        functools.partial(_in_proj_kernel, tiles_per_seq=seq // tm),
        grid=(n // tm,),
        in_specs=[
            pl.BlockSpec((tm, D_MODEL), row),
            pl.BlockSpec((1, D_MODEL), const),
            pl.BlockSpec((D_MODEL, D_IN), const),
            pl.BlockSpec((1, RWKV_PROJ), const),
        ],
        out_specs=[
            pl.BlockSpec((tm, WIDTH), row),
            pl.BlockSpec((tm, WIDTH), row),
            pl.BlockSpec((tm, WIDTH), row),
            pl.BlockSpec((tm, RWKV_PROJ), row),
            pl.BlockSpec((tm, 2 * D_MODEL), row),
        ],
        out_shape=[
            jax.ShapeDtypeStruct((n, WIDTH), BF16),
            jax.ShapeDtypeStruct((n, WIDTH), BF16),
            jax.ShapeDtypeStruct((n, WIDTH), BF16),
            jax.ShapeDtypeStruct((n, RWKV_PROJ), F32),
            jax.ShapeDtypeStruct((n, 2 * D_MODEL), BF16),
        ],
        scratch_shapes=[pltpu.VMEM((8, RWKV_PROJ), F32)],
        compiler_params=pltpu.CompilerParams(
            dimension_semantics=("arbitrary",), vmem_limit_bytes=VMEM_LIMIT),
        name="in_proj",
    )(x2, ln_g, w_in_b, mu)


def _band_attn_kernel(q_ref, k_ref, v_ref, bias_ref, o_ref, kpad_ref, vpad_ref):
    seq = k_ref.shape[1]
    pad = LEFT_CHUNKS * CHUNK
    kpad_ref[0:pad, :] = jnp.zeros((pad, WIDTH), BF16)
    vpad_ref[0:pad, :] = jnp.zeros((pad, WIDTH), BF16)
    kpad_ref[pad:pad + seq, :] = k_ref[0]
    vpad_ref[pad:pad + seq, :] = v_ref[0]

    gw = ATT_GROUP * HEAD_DIM
    rows = ATT_GROUP * CHUNK
    r_head = lax.broadcasted_iota(jnp.int32, (rows, gw), 0) // CHUNK
    l_head = lax.broadcasted_iota(jnp.int32, (rows, gw), 1) // HEAD_DIM
    own = r_head == l_head
    kpos = lax.broadcasted_iota(jnp.int32, (rows, BAND), 1)
    neg = jnp.finfo(F32).min
    groups = range(HEADS // ATT_GROUP)
    lanes = [slice(g * gw, (g + 1) * gw) for g in groups]

    def chunk(c, carry):
        start = pl.multiple_of(c * CHUNK, CHUNK)
        kb = kpad_ref[pl.ds(start, BAND), :]
        vb = vpad_ref[pl.ds(start, BAND), :]
        q = q_ref[0, pl.ds(start, CHUNK), :]
        valid = kpos >= (LEFT_CHUNKS - c) * CHUNK
        qrows = [jnp.where(own, jnp.concatenate([q[:, sl]] * ATT_GROUP, axis=0), jnp.zeros((), BF16))
                 for sl in lanes]
        s = [_dot_nt(qrows[g], kb[:, lanes[g]]) * (HEAD_DIM ** -0.5)
             + bias_ref[g * rows:(g + 1) * rows, :] for g in groups]
        s = [jnp.where(valid, s[g], neg) for g in groups]
        p = [jnp.exp(s[g] - jnp.max(s[g], axis=-1, keepdims=True)) for g in groups]
        denom = [jnp.sum(p[g], axis=-1, keepdims=True) for g in groups]
        o_all = [_dot(p[g].astype(BF16), vb[:, lanes[g]]) / denom[g] for g in groups]
        for g in groups:
            o_own = jnp.where(own, o_all[g], 0.0)
            o = o_own[0:CHUNK]
            for h in range(1, ATT_GROUP):
                o = o + o_own[h * CHUNK:(h + 1) * CHUNK]
            o_ref[0, pl.ds(start, CHUNK), lanes[g]] = o.astype(BF16)
        return carry

    lax.fori_loop(0, seq // CHUNK, chunk, 0)


def _band_attn(q, k, v, bias):
    b, seq, _ = q.shape
    whole = pl.BlockSpec((1, seq, WIDTH), lambda i: (i, 0, 0))
    return pl.pallas_call(
        _band_attn_kernel,
        grid=(b,),
        in_specs=[whole, whole, whole, pl.BlockSpec((HEADS * CHUNK, BAND), lambda i: (0, 0))],
        out_specs=whole,
        out_shape=jax.ShapeDtypeStruct((b, seq, WIDTH), BF16),
        scratch_shapes=[pltpu.VMEM((seq + LEFT_CHUNKS * CHUNK, WIDTH), BF16),
                        pltpu.VMEM((seq + LEFT_CHUNKS * CHUNK, WIDTH), BF16)],
        compiler_params=pltpu.CompilerParams(
            dimension_semantics=("arbitrary",), vmem_limit_bytes=VMEM_LIMIT),
        name="band_attn",
    )(q, k, v, bias)


def _rwkv_kernel(rw_ref, w2a2_ref, g2_ref, w0_ref, a0_ref, kk_ref, ka_ref, rk_ref,
                 gng_ref, gnb_ref, y_ref, state_ref):
    c = pl.program_id(1)
    t = CHUNK
    nb = rw_ref.shape[0]
    rows = nb * t

    @pl.when(c == 0)
    def _():
        state_ref[...] = jnp.zeros(state_ref.shape, F32)

    rw = rw_ref[...].reshape(rows, RWKV_PROJ)
    r = rw[:, 0:WIDTH]
    k = rw[:, WIDTH:2 * WIDTH]
    v = rw[:, 2 * WIDTH:3 * WIDTH]
    lora = rw[:, 3 * WIDTH:3 * WIDTH + DECAY_LORA + AAA_LORA]
    g_lo = rw[:, 3 * WIDTH + DECAY_LORA + AAA_LORA:RWKV_PROJ]

    lane128 = lax.broadcasted_iota(jnp.int32, lora.shape, 1)
    lora = jnp.where(lane128 < DECAY_LORA, jnp.tanh(lora), lora)
    wa = _mm(lora, w2a2_ref[...])
    log_decay = -math.exp(-0.5) * _sigmoid(w0_ref[...] + wa[:, 0:WIDTH])
    lr = _sigmoid(a0_ref[...] + wa[:, WIDTH:2 * WIDTH])
    gate = _mm(_sigmoid(g_lo), g2_ref[...])

    kk_raw = k * kk_ref[...]
    k_mod = k * (1.0 + (lr - 1.0) * ka_ref[...])

    row = lax.broadcasted_iota(jnp.int32, (rows, rows), 0)
    col = lax.broadcasted_iota(jnp.int32, (rows, rows), 1)
    tri = jnp.where((row >= col) & ((row // t) == (col // t)), 1.0, 0.0).astype(BF16)
    ld1 = log_decay.astype(BF16)
    rem = log_decay - ld1.astype(F32)
    ld2 = rem.astype(BF16)
    ld3 = (rem - ld2.astype(F32)).astype(BF16)
    parts = _dot(tri, jnp.concatenate([ld1, ld2, ld3], axis=1))
    logp = parts[:, 0:WIDTH] + parts[:, WIDTH:2 * WIDTH] + parts[:, 2 * WIDTH:3 * WIDTH]
    p_in = jnp.exp(logp)
    p_ex = jnp.exp(logp - log_decay)
    p_inv = jnp.exp(-logp)

    kk = kk_raw / jnp.maximum(jnp.sqrt(_head_sums(kk_raw * kk_raw)), 1e-12)
    a_hat = -kk * p_ex
    r_hat = r * p_in
    b_hat = kk * lr * p_inv
    k_hat = k_mod * p_inv
    bonus = _head_sums(r * k_mod * rk_ref[...]) * v

    lane = lax.broadcasted_iota(jnp.int32, (t, 2 * HEAD_DIM), 1)
    low = lane < HEAD_DIM
    r2 = lax.broadcasted_iota(jnp.int32, (2 * t, 2 * HEAD_DIM), 0)
    c2 = lax.broadcasted_iota(jnp.int32, (2 * t, 2 * HEAD_DIM), 1)
    own = (r2 < t) == (c2 < HEAD_DIM)
    strict = (r2 & (t - 1)) > (c2 & (t - 1))
    incl = (r2 & (t - 1)) >= (c2 & (t - 1))
    eye = jnp.where(r2 == c2, 1.0, 0.0).astype(F32)

    def stack2(xp):
        return jnp.concatenate([jnp.where(low, xp, 0.0), jnp.where(low, 0.0, xp)], axis=0)

    chains = [(b, p) for b in range(nb) for p in range(HEADS // 2)]
    ids = range(len(chains))
    rs = [slice(b * t, (b + 1) * t) for b, _ in chains]
    ls = [slice(2 * HEAD_DIM * p, 2 * HEAD_DIM * (p + 1)) for _, p in chains]
    p_end = [p_in[(b + 1) * t - 1:(b + 1) * t, ls[i]] for i, (b, _) in enumerate(chains)]
    ar = [jnp.concatenate([stack2(a_hat[rs[i], ls[i]]), stack2(r_hat[rs[i], ls[i]])], axis=0).astype(BF16)
          for i in ids]
    bk2 = [(stack2(b_hat[rs[i], ls[i]]), stack2(k_hat[rs[i], ls[i]])) for i in ids]
    bk = [jnp.concatenate(bk2[i], axis=0).astype(BF16) for i in ids]
    btkt = [(jnp.concatenate(bk2[i], axis=0) * p_end[i]).astype(BF16) for i in ids]
    v2 = [stack2(v[rs[i], ls[i]]).astype(BF16) for i in ids]
    g = [_dot_nt(ar[i], bk[i]) for i in ids]
    st = [state_ref[b, p] for b, p in chains]
    ars = [_dot_nt(ar[i], st[i].astype(BF16)) for i in ids]
    l_ab = [jnp.where(strict, g[i][0:2 * t, 0:2 * t], 0.0) for i in ids]
    lm = [jnp.concatenate([jnp.where(strict, g[i][0:2 * t, 2 * t:4 * t], 0.0),
                           jnp.where(incl, g[i][2 * t:4 * t, 2 * t:4 * t], 0.0)], axis=0)
          for i in ids]
    m_rb = [jnp.where(incl, g[i][2 * t:4 * t, 0:2 * t], 0.0).astype(BF16) for i in ids]
    lv = [_mm(lm[i], v2[i]) for i in ids]

    w_inv = [eye + l_ab[i] for i in ids]
    l_pow = l_ab
    for _ in range(int(math.log2(t)) - 1):
        l_pow = [_mm(l_pow[i], l_pow[i]) for i in ids]
        w_inv = [w_inv[i] + _mm(w_inv[i], l_pow[i]) for i in ids]

    z = [_mm(w_inv[i], ars[i][0:2 * t] + lv[i][0:2 * t]) for i in ids]
    y = [ars[i][2 * t:4 * t] + lv[i][2 * t:4 * t] + _mm(m_rb[i], z[i]) for i in ids]
    for i, (b, p) in enumerate(chains):
        zv = jnp.concatenate([z[i].astype(BF16), v2[i]], axis=0)
        state_ref[b, p] = st[i] * p_end[i] + _dot_tn(zv, btkt[i])

    for i, (b, p) in enumerate(chains):
        mean = jnp.sum(y[i], axis=-1, keepdims=True) * (1.0 / HEAD_DIM)
        dev = jnp.where(own, y[i] - mean, 0.0)
        var = jnp.sum(dev * dev, axis=-1, keepdims=True) * (1.0 / HEAD_DIM)
        yn = dev * lax.rsqrt(var + GN_EPS)
        yn = yn[0:t] + yn[t:2 * t]
        out = (yn * gng_ref[:, ls[i]] + gnb_ref[:, ls[i]] + bonus[rs[i], ls[i]]) * gate[rs[i], ls[i]]
        y_ref[b, :, ls[i]] = out.astype(BF16)


def _rwkv(rw, w2a2, g2, w0, a0, k_k, k_a, r_k, gn_g, gn_b):
    b, seq, _ = rw.shape
    nc = seq // CHUNK
    nb = RWKV_SEQS
    const = lambda i, c: (0, 0)
    vec = pl.BlockSpec((1, WIDTH), const)
    return pl.pallas_call(
        _rwkv_kernel,
        grid=(b // nb, nc),
        in_specs=[
            pl.BlockSpec((nb, CHUNK, RWKV_PROJ), lambda i, c: (i, c, 0)),
            pl.BlockSpec((DECAY_LORA + AAA_LORA, 2 * WIDTH), const),
            pl.BlockSpec((GATE_LORA, WIDTH), const),
            vec, vec, vec, vec, vec, vec, vec,
        ],
        out_specs=pl.BlockSpec((nb, CHUNK, WIDTH), lambda i, c: (i, c, 0)),
        out_shape=jax.ShapeDtypeStruct((b, seq, WIDTH), BF16),
        scratch_shapes=[pltpu.VMEM((nb, HEADS // 2, 2 * HEAD_DIM, 2 * HEAD_DIM), F32)],
        compiler_params=pltpu.CompilerParams(
            dimension_semantics=("arbitrary", "arbitrary"), vmem_limit_bytes=VMEM_LIMIT),
        name="rwkv7",
    )(rw, w2a2, g2, w0, a0, k_k, k_a, r_k, gn_g, gn_b)


def _merge_kernel(x_ref, att_ref, rwkv_ref, gate_ref, wa_ref, wb_ref, wo_ref, g_ref,
                  wr_ref, br_ref, x1_ref, h2_ref, comb_ref, cnt_ref):
    ga = gate_ref[:, 0:D_MODEL].astype(F32)
    gb = gate_ref[:, D_MODEL:2 * D_MODEL].astype(F32)
    merged = ga * _dot(att_ref[...], wa_ref[...]) + gb * _dot(rwkv_ref[...], wb_ref[...])
    x1 = x_ref[...] + _dot(merged.astype(BF16), wo_ref[...])
    x1_ref[...] = x1
    h2 = x1 * lax.rsqrt(jnp.mean(x1 * x1, axis=-1, keepdims=True) + RMS_EPS) * g_ref[...]
    h2_hi = h2.astype(BF16)
    h2_ref[...] = h2_hi
    h2_lo = (h2 - h2_hi.astype(F32)).astype(BF16)
    hw = _dot(h2_hi, wr_ref[...])
    lw = _dot(h2_lo, wr_ref[:, 0:ROUTER_LANES])
    logits = hw[:, 0:ROUTER_LANES] + (hw[:, ROUTER_LANES:2 * ROUTER_LANES] + lw) + br_ref[...]
    lane_i = lax.broadcasted_iota(jnp.int32, logits.shape, 1)
    lane = lane_i.astype(F32)
    lane_group = (lane_i >> 3).astype(F32)
    neg = jnp.finfo(F32).min
    big = float(ROUTER_LANES)

    def first_argmax(vals, mask):
        vm = jnp.where(mask, vals, neg)
        mx = jnp.max(vm, axis=-1, keepdims=True)
        idx = jnp.min(jnp.where(vm == mx, jnp.where(mask, lane, big), big), axis=-1, keepdims=True)
        return mx, idx

    is_group = (lane_i >= GROUP_LANE0) & (lane_i < GROUP_LANE0 + N_GROUPS)
    g_max, g_lane = first_argmax(logits, is_group)
    g_prob = 1.0 / jnp.sum(jnp.where(is_group, jnp.exp(logits - g_max), 0.0),
                           axis=-1, keepdims=True)
    g_idx = g_lane - GROUP_LANE0
    in_group = lane_group == g_idx
    e1, i1 = first_argmax(logits, in_group)
    e2, i2 = first_argmax(logits, in_group & (lane != i1))
    w2 = jnp.exp(e2 - e1)
    p1 = 1.0 / (1.0 + w2)
    p2 = w2 / (1.0 + w2)
    comb = jnp.where(lane == i1, p1 * g_prob, jnp.where(lane == i2, p2 * g_prob, 0.0))
    comb_ref[...] = jnp.where(lane_i == GROUP_ID_LANE, g_idx, comb)
    counts = jnp.sum(jnp.where(lane == g_idx, 1.0, 0.0), axis=0, keepdims=True)
    cnt_ref[0] = jnp.broadcast_to(counts, (8, ROUTER_LANES)).astype(jnp.int32)


def _merge(x2, att, rwkv, gates, wa, wb, wo, ln_g, wr, br):
    n = x2.shape[0]
    tm = MERGE_TILE
    row = lambda i: (i, 0)
    const = lambda i: (0, 0)
    return pl.pallas_call(
        _merge_kernel,
        grid=(n // tm,),
        in_specs=[
            pl.BlockSpec((tm, D_MODEL), row),
            pl.BlockSpec((tm, WIDTH), row),
            pl.BlockSpec((tm, WIDTH), row),
            pl.BlockSpec((tm, 2 * D_MODEL), row),
            pl.BlockSpec((WIDTH, D_MODEL), const),
            pl.BlockSpec((WIDTH, D_MODEL), const),
            pl.BlockSpec((D_MODEL, D_MODEL), const),
            pl.BlockSpec((1, D_MODEL), const),
            pl.BlockSpec((D_MODEL, 2 * ROUTER_LANES), const),
            pl.BlockSpec((1, ROUTER_LANES), const),
        ],
        out_specs=[
            pl.BlockSpec((tm, D_MODEL), row),
            pl.BlockSpec((tm, D_MODEL), row),
            pl.BlockSpec((tm, ROUTER_LANES), row),
            pl.BlockSpec((1, 8, ROUTER_LANES), lambda i: (i, 0, 0)),
        ],
        out_shape=[
            jax.ShapeDtypeStruct((n, D_MODEL), F32),
            jax.ShapeDtypeStruct((n, D_MODEL), BF16),
            jax.ShapeDtypeStruct((n, ROUTER_LANES), F32),
            jax.ShapeDtypeStruct((n // tm, 8, ROUTER_LANES), jnp.int32),
        ],
        compiler_params=pltpu.CompilerParams(
            dimension_semantics=("arbitrary",), vmem_limit_bytes=VMEM_LIMIT),
        name="merge",
    )(x2, att, rwkv, gates, wa, wb, wo, ln_g, wr, br)


def _moe_kernel(cnt_ref, x1_ref, h2_ref, comb_ref, wg_ref, wu_ref, wd_ref, lnf_ref, out_ref,
                keyc_ref, keyr_ref, *, final_norm):
    i = pl.program_id(0)
    g = pl.program_id(1)
    t = MOE_TILE
    r = MOE_BLOCK

    @pl.when(g == 0)
    def _():
        out_ref[...] = jnp.zeros(out_ref.shape, F32)
        lane = lax.broadcasted_iota(jnp.int32, (t, ROUTER_LANES), 1).astype(F32)
        gid = comb_ref[:, GROUP_ID_LANE:GROUP_ID_LANE + 1]
        onehot = jnp.where(lane == gid, 1.0, 0.0)
        onehot_b = onehot.astype(BF16)
        for ch in range(t // r):
            rows = lax.broadcasted_iota(jnp.int32, (r, t), 0) + ch * r
            cols = lax.broadcasted_iota(jnp.int32, (r, t), 1)
            earlier = jnp.where(cols < rows, 1.0, 0.0).astype(BF16)
            before = _dot(earlier, onehot_b)
            sl = slice(ch * r, (ch + 1) * r)
            rank = jnp.sum(before * onehot[sl], axis=-1, keepdims=True)
            keyc_ref[sl, :] = jnp.broadcast_to(gid[sl] * t + rank, (r, ROUTER_LANES))
        keyr_ref[...] = jnp.transpose(keyc_ref[...])[0:8, :]

    quarter = t // N_GROUPS
    row0 = pl.multiple_of(g * quarter, quarter)
    out_ref[pl.ds(row0, quarter), :] += x1_ref[...]

    n_tok = cnt_ref[(i * MERGE_PER_MOE) * N_GROUPS + g]
    for m in range(1, MERGE_PER_MOE):
        n_tok = n_tok + cnt_ref[(i * MERGE_PER_MOE + m) * N_GROUPS + g]
    n_blocks = (n_tok + r - 1) // r

    lane_e = lax.broadcasted_iota(jnp.int32, (ROUTER_LANES, GROUP_FF), 0)
    col_e = lax.broadcasted_iota(jnp.int32, (ROUTER_LANES, GROUP_FF), 1) // D_EXPERT
    expand = jnp.where(lane_e == g * EXPERTS_PER_GROUP + col_e, 1.0, 0.0).astype(BF16)
    comb_b = comb_ref[...].astype(BF16)
    key_row = keyr_ref[0:1, :]
    key_col = jnp.concatenate([keyc_ref[...]] * (r // ROUTER_LANES), axis=1)

    def block(j, carry):
        base = (g * t + j * r).astype(F32)
        want_r = lax.broadcasted_iota(jnp.int32, (r, t), 0).astype(F32) + base
        take = jnp.where(want_r == key_row, 1.0, 0.0).astype(BF16)
        hs = _dot(take, h2_ref[...]).astype(BF16)
        cs = _dot(take, comb_b).astype(BF16)
        cw = _dot(cs, expand)
        y = jnp.zeros((r, D_MODEL), F32)
        for e in range(EXPERTS_PER_GROUP):
            hg = _dot(hs, wg_ref[e])
            hu = _dot(hs, wu_ref[e])
            act = hg * _sigmoid(hg) * hu * cw[:, e * D_EXPERT:(e + 1) * D_EXPERT]
            y = y + _dot(act.astype(BF16), wd_ref[e])
        want_c = lax.broadcasted_iota(jnp.int32, (t, r), 1).astype(F32) + base
        put = jnp.where(want_c == key_col, 1.0, 0.0).astype(BF16)
        out_ref[...] += _dot(put, y.astype(BF16))
        return carry

    lax.fori_loop(0, n_blocks, block, 0)

    if final_norm:
        @pl.when(g == N_GROUPS - 1)
        def _():
            y = out_ref[...]
            out_ref[...] = y * lax.rsqrt(jnp.mean(y * y, axis=-1, keepdims=True) + RMS_EPS) * lnf_ref[...]


def _moe(cnt, x1, h2, comb, wg, wu, wd, lnf, final_norm):
    n = x1.shape[0]
    t = MOE_TILE
    quarter = t // N_GROUPS
    row = lambda i, g, cnt: (i, 0)
    grp = lambda i, g, cnt: (g, 0, 0)
    grid_spec = pltpu.PrefetchScalarGridSpec(
        num_scalar_prefetch=1,
        grid=(n // t, N_GROUPS),
        in_specs=[
            pl.BlockSpec((quarter, D_MODEL), lambda i, g, cnt: (i * N_GROUPS + g, 0)),
            pl.BlockSpec((t, D_MODEL), row),
            pl.BlockSpec((t, ROUTER_LANES), row),
            pl.BlockSpec((EXPERTS_PER_GROUP, D_MODEL, D_EXPERT), grp),
            pl.BlockSpec((EXPERTS_PER_GROUP, D_MODEL, D_EXPERT), grp),
            pl.BlockSpec((EXPERTS_PER_GROUP, D_EXPERT, D_MODEL), grp),
            pl.BlockSpec((1, D_MODEL), lambda i, g, cnt: (0, 0)),
        ],
        out_specs=pl.BlockSpec((t, D_MODEL), row),
        scratch_shapes=[pltpu.VMEM((t, ROUTER_LANES), F32), pltpu.VMEM((8, t), F32)],
    )
    return pl.pallas_call(
        functools.partial(_moe_kernel, final_norm=final_norm),
        grid_spec=grid_spec,
        out_shape=jax.ShapeDtypeStruct((n, D_MODEL), F32),
        compiler_params=pltpu.CompilerParams(
            dimension_semantics=("arbitrary", "arbitrary"), vmem_limit_bytes=MOE_VMEM_LIMIT),
        name="moe",
    )(cnt, x1, h2, comb, wg, wu, wd, lnf)


def kernel(x, ln_mix_g, w_in, att_rel_bias, rwkv_mu, rwkv_w0, rwkv_w2, rwkv_a0, rwkv_a2, rwkv_g2,
           rwkv_k_k, rwkv_k_a, rwkv_r_k, rwkv_gn_g, rwkv_gn_b, w_branch_att, w_branch_rwkv, w_out,
           ln_ffn_g, router_group_w, router_group_b, router_expert_w, router_expert_b,
           expert_w_gate, expert_w_up, expert_w_down, ln_final_g):
    bsz, seq, d = x.shape
    depth = w_in.shape[0]
    n = bsz * seq
    x2 = x.reshape(n, d)
    for l in range(depth):
        q, k, v, rw, gates = _in_proj(x2, ln_mix_g[l][None, :], w_in[l].astype(BF16),
                                      rwkv_mu[l][None, :], seq)
        bias = _rel_bias(att_rel_bias[l])
        att = _band_attn(q.reshape(bsz, seq, WIDTH), k.reshape(bsz, seq, WIDTH),
                         v.reshape(bsz, seq, WIDTH), bias)
        zeros = jnp.zeros((DECAY_LORA, WIDTH), F32)
        w2a2 = jnp.concatenate(
            [jnp.concatenate([rwkv_w2[l], zeros], axis=1),
             jnp.concatenate([zeros, rwkv_a2[l]], axis=1)], axis=0)
        rwkv = _rwkv(rw.reshape(bsz, seq, RWKV_PROJ), w2a2.astype(BF16), rwkv_g2[l].astype(BF16),
                     rwkv_w0[l][None, :], rwkv_a0[l][None, :], rwkv_k_k[l][None, :],
                     rwkv_k_a[l][None, :], rwkv_r_k[l].reshape(1, WIDTH),
                     rwkv_gn_g[l][None, :], rwkv_gn_b[l][None, :])
        wr = jnp.concatenate([router_expert_w[l], router_group_w[l]], axis=1)
        wr = jnp.pad(wr, ((0, 0), (0, ROUTER_LANES - wr.shape[1])))
        wr_hi = wr.astype(BF16)
        wr = jnp.concatenate([wr_hi, (wr - wr_hi.astype(F32)).astype(BF16)], axis=1)
        br = jnp.concatenate([router_expert_b[l], router_group_b[l]])
        br = jnp.pad(br, (0, ROUTER_LANES - br.shape[0]))[None, :]
        x1, h2, comb, counts = _merge(x2, att.reshape(n, WIDTH), rwkv.reshape(n, WIDTH), gates,
                                      w_branch_att[l].astype(BF16), w_branch_rwkv[l].astype(BF16),
                                      w_out[l].astype(BF16), ln_ffn_g[l][None, :], wr, br)
        cnt = counts[:, 0, :N_GROUPS].reshape(-1)
        x2 = _moe(cnt, x1, h2, comb, expert_w_gate[l].astype(BF16), expert_w_up[l].astype(BF16),
                  expert_w_down[l].astype(BF16), ln_final_g[None, :], final_norm=(l == depth - 1))
    return x2.reshape(bsz, seq, d)
```

```python
import functools
import math

import jax
import jax.numpy as jnp
from jax import lax
from jax.experimental import pallas as pl
from jax.experimental.pallas import tpu as pltpu

F32 = jnp.float32
BF16 = jnp.bfloat16
HIGHEST = lax.Precision.HIGHEST

D_MODEL = 1024
CHUNK = 64
HEADS = 8
HEAD_DIM = 64
WIDTH = HEADS * HEAD_DIM
LEFT_CHUNKS = 8
BAND = (LEFT_CHUNKS + 1) * CHUNK
REL_CLIP = 64
N_REL = 2 * REL_CLIP + 1
DECAY_LORA = 64
AAA_LORA = 64
GATE_LORA = 128
GN_EPS = 64e-5
RMS_EPS = 1e-6
ATT_PROJ = 3 * WIDTH
RWKV_PROJ = 3 * WIDTH + DECAY_LORA + AAA_LORA + GATE_LORA
D_IN = ATT_PROJ + RWKV_PROJ + 2 * D_MODEL
N_GROUPS = 4
EXPERTS_PER_GROUP = 8
N_EXPERTS = N_GROUPS * EXPERTS_PER_GROUP
D_EXPERT = 256
GROUP_FF = EXPERTS_PER_GROUP * D_EXPERT
RWKV_SEQS = 4
BIAS_KEYS = 192
ATT_CHUNKS = 2
ATT_GROUP = 4
ROUTER_LANES = 128
GROUP_LANE0 = N_EXPERTS
GROUP_ID_LANE = ROUTER_LANES - 1
MERGE_TILE = 512
MOE_TILE = 1024
MOE_BLOCK = 256
MOE_TAIL = 64
MERGE_PER_MOE = MOE_TILE // MERGE_TILE

VMEM_LIMIT = 52 * 1024 * 1024
MOE_VMEM_LIMIT = 58 * 1024 * 1024


def _dot(a, b):
    return jnp.dot(a, b, preferred_element_type=F32)


def _dot_hi(a, b):
    return jnp.dot(a, b, preferred_element_type=F32, precision=HIGHEST)


def _dot_nt(a, b, precision=None):
    return lax.dot_general(a, b, (((1,), (1,)), ((), ())),
                           preferred_element_type=F32, precision=precision)


def _dot_tn(a, b, precision=None):
    return lax.dot_general(a, b, (((0,), (0,)), ((), ())),
                           preferred_element_type=F32, precision=precision)


def _sigmoid(x):
    return 1.0 / (1.0 + jnp.exp(-x))


def _mm(a, b):
    return jnp.dot(a.astype(BF16), b.astype(BF16), preferred_element_type=F32)


def _head_sums(x):
    outs = []
    lane = lax.broadcasted_iota(jnp.int32, (x.shape[0], 2 * HEAD_DIM), 1)
    low = lane < HEAD_DIM
    for p in range(HEADS // 2):
        xp = x[:, 2 * HEAD_DIM * p:2 * HEAD_DIM * (p + 1)]
        s_lo = jnp.sum(jnp.where(low, xp, 0.0), axis=-1, keepdims=True)
        s_hi = jnp.sum(jnp.where(low, 0.0, xp), axis=-1, keepdims=True)
        outs.append(jnp.where(low, s_lo, s_hi))
    return jnp.concatenate(outs, axis=-1)


def _rel_bias_kernel(tab_ref, out_ref):
    rows = tab_ref.shape[1]
    n = lax.broadcasted_iota(jnp.int32, (rows, CHUNK * 128), 1)
    r = lax.broadcasted_iota(jnp.int32, (rows, CHUNK * 128), 0)
    q = n >> 7
    kk = n & 127
    idx = jnp.clip(CHUNK + q - kk, -REL_CLIP, REL_CLIP) + REL_CLIP
    pick = jnp.where(r == idx, 1.0, 0.0) - jnp.where(r == N_REL - 1, 1.0, 0.0)
    out_ref[...] = _dot_hi(tab_ref[...], pick.astype(F32))


def _rel_bias(rel_table):
    rows = 136
    tab = jnp.pad(rel_table.astype(F32), ((0, 0), (0, rows - N_REL)))
    tail = pl.pallas_call(
        _rel_bias_kernel,
        out_shape=jax.ShapeDtypeStruct((HEADS, CHUNK * 128), F32),
        name="rel_bias",
    )(tab)
    tail = tail.reshape(HEADS, CHUNK, 128)
    bias = jnp.concatenate([jnp.zeros((HEADS, CHUNK, BIAS_KEYS - 128), F32), tail], axis=-1)
    return bias.reshape(HEADS * CHUNK, BIAS_KEYS)


def _in_proj_kernel(x_ref, g_ref, w_ref, mu_ref, q_ref, k_ref, v_ref, rw_ref, gate_ref,
                    carry_ref, *, tiles_per_seq):
    i = pl.program_id(0)

    @pl.when(i == 0)
    def _():
        carry_ref[...] = jnp.zeros(carry_ref.shape, F32)

    x = x_ref[...]
    h = x * lax.rsqrt(jnp.mean(x * x, axis=-1, keepdims=True) + RMS_EPS) * g_ref[...]
    hb = h.astype(BF16)
    q_ref[...] = _dot(hb, w_ref[:, 0:WIDTH]).astype(BF16)
    k_ref[...] = _dot(hb, w_ref[:, WIDTH:2 * WIDTH]).astype(BF16)
    v_ref[...] = _dot(hb, w_ref[:, 2 * WIDTH:ATT_PROJ]).astype(BF16)
    rw = _dot(hb, w_ref[:, ATT_PROJ:ATT_PROJ + RWKV_PROJ])
    tm = rw.shape[0]
    first_prev = jnp.where(i % tiles_per_seq == 0, 0.0, carry_ref[0:1, :])
    rolled = pltpu.roll(rw, 1, axis=0)
    row = lax.broadcasted_iota(jnp.int32, rw.shape, 0)
    prev = jnp.where(row == 0, first_prev, rolled)
    carry_ref[0:1, :] = rw[tm - 1:tm, :]
    rw_ref[...] = rw + (prev - rw) * mu_ref[...]
    gate_ref[...] = _sigmoid(_dot(hb, w_ref[:, ATT_PROJ + RWKV_PROJ:D_IN])).astype(BF16)


def _in_proj(x2, ln_g, w_in_b, mu, seq):
    n = x2.shape[0]
    tm = 256
    row = lambda i: (i, 0)
    const = lambda i: (0, 0)
    return pl.pallas_call(
        functools.partial(_in_proj_kernel, tiles_per_seq=seq // tm),
        grid=(n // tm,),
        in_specs=[
            pl.BlockSpec((tm, D_MODEL), row),
            pl.BlockSpec((1, D_MODEL), const),
            pl.BlockSpec((D_MODEL, D_IN), const),
            pl.BlockSpec((1, RWKV_PROJ), const),
        ],
        out_specs=[
            pl.BlockSpec((tm, WIDTH), row),
            pl.BlockSpec((tm, WIDTH), row),
            pl.BlockSpec((tm, WIDTH), row),
            pl.BlockSpec((tm, RWKV_PROJ), row),
            pl.BlockSpec((tm, 2 * D_MODEL), row),
        ],
        out_shape=[
            jax.ShapeDtypeStruct((n, WIDTH), BF16),
            jax.ShapeDtypeStruct((n, WIDTH), BF16),
            jax.ShapeDtypeStruct((n, WIDTH), BF16),
            jax.ShapeDtypeStruct((n, RWKV_PROJ), F32),
            jax.ShapeDtypeStruct((n, 2 * D_MODEL), BF16),
        ],
        scratch_shapes=[pltpu.VMEM((8, RWKV_PROJ), F32)],
        compiler_params=pltpu.CompilerParams(
            dimension_semantics=("arbitrary",), vmem_limit_bytes=VMEM_LIMIT),
        name="in_proj",
    )(x2, ln_g, w_in_b, mu)


def _band_attn_kernel(q_ref, k_ref, v_ref, bias_ref, o_ref, kpad_ref, vpad_ref):
    seq = k_ref.shape[1]
    pad = LEFT_CHUNKS * CHUNK
    kpad_ref[0:pad, :] = jnp.zeros((pad, WIDTH), BF16)
    vpad_ref[0:pad, :] = jnp.zeros((pad, WIDTH), BF16)
    kpad_ref[pad:pad + seq, :] = k_ref[0]
    vpad_ref[pad:pad + seq, :] = v_ref[0]

    gw = ATT_GROUP * HEAD_DIM
    rows = ATT_GROUP * CHUNK
    r_head = lax.broadcasted_iota(jnp.int32, (rows, gw), 0) // CHUNK
    l_head = lax.broadcasted_iota(jnp.int32, (rows, gw), 1) // HEAD_DIM
    own = r_head == l_head
    kpos_lo = lax.broadcasted_iota(jnp.int32, (rows, BAND - BIAS_KEYS), 1)
    kpos_hi = lax.broadcasted_iota(jnp.int32, (rows, BIAS_KEYS), 1) + (BAND - BIAS_KEYS)
    neg = jnp.finfo(F32).min
    groups = range(HEADS // ATT_GROUP)
    lanes = [slice(g * gw, (g + 1) * gw) for g in groups]

    def chunk_pair(i, carry, masked):
        units = [(j, g) for j in range(ATT_CHUNKS) for g in groups]
        ids = range(len(units))
        starts = [pl.multiple_of((i * ATT_CHUNKS + j) * CHUNK, CHUNK) for j in range(ATT_CHUNKS)]
        kb = [kpad_ref[pl.ds(st, BAND), :] for st in starts]
        vb = [vpad_ref[pl.ds(st, BAND), :] for st in starts]
        q = [q_ref[0, pl.ds(st, CHUNK), :] * (HEAD_DIM ** -0.5) for st in starts]
        qrows = [jnp.where(own, jnp.concatenate([q[j][:, lanes[g]]] * ATT_GROUP, axis=0),
                           jnp.zeros((), BF16)) for j, g in units]
        s = [_dot_nt(qrows[u], kb[j][:, lanes[g]]) for u, (j, g) in enumerate(units)]
        s_lo = [s[u][:, 0:BAND - BIAS_KEYS] for u in ids]
        s_hi = [s[u][:, BAND - BIAS_KEYS:BAND] + bias_ref[g * rows:(g + 1) * rows, :]
                for u, (j, g) in enumerate(units)]
        if masked:
            first = [(LEFT_CHUNKS - (i * ATT_CHUNKS + j)) * CHUNK for j in range(ATT_CHUNKS)]
            s_lo = [jnp.where(kpos_lo >= first[j], s_lo[u], neg) for u, (j, g) in enumerate(units)]
            s_hi = [jnp.where(kpos_hi >= first[j], s_hi[u], neg) for u, (j, g) in enumerate(units)]
        m = [jnp.maximum(jnp.max(s_lo[u], axis=-1, keepdims=True),
                         jnp.max(s_hi[u], axis=-1, keepdims=True)) for u in ids]
        p_lo = [jnp.exp(s_lo[u] - m[u]) for u in ids]
        p_hi = [jnp.exp(s_hi[u] - m[u]) for u in ids]
        denom = [jnp.sum(p_lo[u], axis=-1, keepdims=True) + jnp.sum(p_hi[u], axis=-1, keepdims=True)
                 for u in ids]
        o_all = [(_dot(p_lo[u].astype(BF16), vb[j][0:BAND - BIAS_KEYS, lanes[g]])
                  + _dot(p_hi[u].astype(BF16), vb[j][BAND - BIAS_KEYS:BAND, lanes[g]])) / denom[u]
                 for u, (j, g) in enumerate(units)]
        for u, (j, g) in enumerate(units):
            o_own = jnp.where(own, o_all[u], 0.0)
            o = o_own[0:CHUNK]
            for h in range(1, ATT_GROUP):
                o = o + o_own[h * CHUNK:(h + 1) * CHUNK]
            o_ref[0, pl.ds(starts[j], CHUNK), lanes[g]] = o.astype(BF16)
        return carry

    n_trips = seq // (CHUNK * ATT_CHUNKS)
    n_masked = min(LEFT_CHUNKS // ATT_CHUNKS, n_trips)
    lax.fori_loop(0, n_masked, functools.partial(chunk_pair, masked=True), 0)
    lax.fori_loop(n_masked, n_trips, functools.partial(chunk_pair, masked=False), 0)


def _band_attn(q, k, v, bias):
    b, seq, _ = q.shape
    whole = pl.BlockSpec((1, seq, WIDTH), lambda i: (i, 0, 0))
    return pl.pallas_call(
        _band_attn_kernel,
        grid=(b,),
        in_specs=[whole, whole, whole, pl.BlockSpec((HEADS * CHUNK, BIAS_KEYS), lambda i: (0, 0))],
        out_specs=whole,
        out_shape=jax.ShapeDtypeStruct((b, seq, WIDTH), BF16),
        scratch_shapes=[pltpu.VMEM((seq + LEFT_CHUNKS * CHUNK, WIDTH), BF16),
                        pltpu.VMEM((seq + LEFT_CHUNKS * CHUNK, WIDTH), BF16)],
        compiler_params=pltpu.CompilerParams(
            dimension_semantics=("arbitrary",), vmem_limit_bytes=VMEM_LIMIT),
        name="band_attn",
    )(q, k, v, bias)


def _rwkv_kernel(rw_ref, w2a2_ref, g2_ref, w0_ref, a0_ref, kk_ref, ka_ref, rk_ref,
                 gng_ref, gnb_ref, y_ref, state_ref):
    c = pl.program_id(1)
    t = CHUNK
    nb = rw_ref.shape[0]
    rows = nb * t

    @pl.when(c == 0)
    def _():
        state_ref[...] = jnp.zeros(state_ref.shape, F32)

    rw = rw_ref[...].reshape(rows, RWKV_PROJ)
    r = rw[:, 0:WIDTH]
    k = rw[:, WIDTH:2 * WIDTH]
    v = rw[:, 2 * WIDTH:3 * WIDTH]
    lora = rw[:, 3 * WIDTH:3 * WIDTH + DECAY_LORA + AAA_LORA]
    g_lo = rw[:, 3 * WIDTH + DECAY_LORA + AAA_LORA:RWKV_PROJ]

    lane128 = lax.broadcasted_iota(jnp.int32, lora.shape, 1)
    lora = jnp.where(lane128 < DECAY_LORA, jnp.tanh(lora), lora)
    wa = _mm(lora, w2a2_ref[...])
    log_decay = -math.exp(-0.5) * _sigmoid(w0_ref[...] + wa[:, 0:WIDTH])
    lr = _sigmoid(a0_ref[...] + wa[:, WIDTH:2 * WIDTH])
    gate = _mm(_sigmoid(g_lo), g2_ref[...])

    kk_raw = k * kk_ref[...]
    k_mod = k * (1.0 + (lr - 1.0) * ka_ref[...])

    row = lax.broadcasted_iota(jnp.int32, (rows, rows), 0)
    col = lax.broadcasted_iota(jnp.int32, (rows, rows), 1)
    tri = jnp.where((row >= col) & ((row // t) == (col // t)), 1.0, 0.0).astype(BF16)
    ld1 = log_decay.astype(BF16)
    rem = log_decay - ld1.astype(F32)
    ld2 = rem.astype(BF16)
    ld3 = (rem - ld2.astype(F32)).astype(BF16)
    parts = _dot(tri, jnp.concatenate([ld1, ld2, ld3], axis=1))
    logp = parts[:, 0:WIDTH] + parts[:, WIDTH:2 * WIDTH] + parts[:, 2 * WIDTH:3 * WIDTH]
    p_in = jnp.exp(logp)
    p_ex = jnp.exp(logp - log_decay)
    p_inv = jnp.exp(-logp)

    kk = kk_raw / jnp.maximum(jnp.sqrt(_head_sums(kk_raw * kk_raw)), 1e-12)
    a_hat = -kk * p_ex
    r_hat = r * p_in
    b_hat = kk * lr * p_inv
    k_hat = k_mod * p_inv
    bonus = _head_sums(r * k_mod * rk_ref[...]) * v

    lane = lax.broadcasted_iota(jnp.int32, (t, 2 * HEAD_DIM), 1)
    low = lane < HEAD_DIM
    r2 = lax.broadcasted_iota(jnp.int32, (2 * t, 2 * HEAD_DIM), 0)
    c2 = lax.broadcasted_iota(jnp.int32, (2 * t, 2 * HEAD_DIM), 1)
    own = (r2 < t) == (c2 < HEAD_DIM)
    strict = (r2 & (t - 1)) > (c2 & (t - 1))
    incl = (r2 & (t - 1)) >= (c2 & (t - 1))
    eye = jnp.where(r2 == c2, 1.0, 0.0).astype(F32)

    def stack2(xp):
        return jnp.concatenate([jnp.where(low, xp, 0.0), jnp.where(low, 0.0, xp)], axis=0)

    chains = [(b, p) for b in range(nb) for p in range(HEADS // 2)]
    ids = range(len(chains))
    rs = [slice(b * t, (b + 1) * t) for b, _ in chains]
    ls = [slice(2 * HEAD_DIM * p, 2 * HEAD_DIM * (p + 1)) for _, p in chains]
    p_end = [p_in[(b + 1) * t - 1:(b + 1) * t, ls[i]] for i, (b, _) in enumerate(chains)]
    ar = [jnp.concatenate([stack2(a_hat[rs[i], ls[i]]), stack2(r_hat[rs[i], ls[i]])], axis=0).astype(BF16)
          for i in ids]
    bk2 = [(stack2(b_hat[rs[i], ls[i]]), stack2(k_hat[rs[i], ls[i]])) for i in ids]
    bk = [jnp.concatenate(bk2[i], axis=0).astype(BF16) for i in ids]
    btkt = [(jnp.concatenate(bk2[i], axis=0) * p_end[i]).astype(BF16) for i in ids]
    v2 = [stack2(v[rs[i], ls[i]]).astype(BF16) for i in ids]
    g = [_dot_nt(ar[i], bk[i]) for i in ids]
    st = [state_ref[b, p] for b, p in chains]
    ars = [_dot_nt(ar[i], st[i].astype(BF16)) for i in ids]
    l_ab = [jnp.where(strict, g[i][0:2 * t, 0:2 * t], 0.0) for i in ids]
    lm = [jnp.concatenate([jnp.where(strict, g[i][0:2 * t, 2 * t:4 * t], 0.0),
                           jnp.where(incl, g[i][2 * t:4 * t, 2 * t:4 * t], 0.0)], axis=0)
          for i in ids]
    m_rb = [jnp.where(incl, g[i][2 * t:4 * t, 0:2 * t], 0.0).astype(BF16) for i in ids]
    lv = [_mm(lm[i], v2[i]) for i in ids]

    w_inv = [eye + l_ab[i] for i in ids]
    l_pow = l_ab
    for _ in range(int(math.log2(t)) - 1):
        l_pow = [_mm(l_pow[i], l_pow[i]) for i in ids]
        w_inv = [w_inv[i] + _mm(w_inv[i], l_pow[i]) for i in ids]

    z = [_mm(w_inv[i], ars[i][0:2 * t] + lv[i][0:2 * t]) for i in ids]
    y = [ars[i][2 * t:4 * t] + lv[i][2 * t:4 * t] + _mm(m_rb[i], z[i]) for i in ids]
    for i, (b, p) in enumerate(chains):
        zv = jnp.concatenate([z[i].astype(BF16), v2[i]], axis=0)
        state_ref[b, p] = st[i] * p_end[i] + _dot_tn(zv, btkt[i])

    for i, (b, p) in enumerate(chains):
        mean = jnp.sum(y[i], axis=-1, keepdims=True) * (1.0 / HEAD_DIM)
        dev = jnp.where(own, y[i] - mean, 0.0)
        var = jnp.sum(dev * dev, axis=-1, keepdims=True) * (1.0 / HEAD_DIM)
        yn = dev * lax.rsqrt(var + GN_EPS)
        yn = yn[0:t] + yn[t:2 * t]
        out = (yn * gng_ref[:, ls[i]] + gnb_ref[:, ls[i]] + bonus[rs[i], ls[i]]) * gate[rs[i], ls[i]]
        y_ref[b, :, ls[i]] = out.astype(BF16)


def _rwkv(rw, w2a2, g2, w0, a0, k_k, k_a, r_k, gn_g, gn_b):
    b, seq, _ = rw.shape
    nc = seq // CHUNK
    nb = RWKV_SEQS
    const = lambda i, c: (0, 0)
    vec = pl.BlockSpec((1, WIDTH), const)
    return pl.pallas_call(
        _rwkv_kernel,
        grid=(b // nb, nc),
        in_specs=[
            pl.BlockSpec((nb, CHUNK, RWKV_PROJ), lambda i, c: (i, c, 0)),
            pl.BlockSpec((DECAY_LORA + AAA_LORA, 2 * WIDTH), const),
            pl.BlockSpec((GATE_LORA, WIDTH), const),
            vec, vec, vec, vec, vec, vec, vec,
        ],
        out_specs=pl.BlockSpec((nb, CHUNK, WIDTH), lambda i, c: (i, c, 0)),
        out_shape=jax.ShapeDtypeStruct((b, seq, WIDTH), BF16),
        scratch_shapes=[pltpu.VMEM((nb, HEADS // 2, 2 * HEAD_DIM, 2 * HEAD_DIM), F32)],
        compiler_params=pltpu.CompilerParams(
            dimension_semantics=("arbitrary", "arbitrary"), vmem_limit_bytes=VMEM_LIMIT),
        name="rwkv7",
    )(rw, w2a2, g2, w0, a0, k_k, k_a, r_k, gn_g, gn_b)


def _merge_kernel(x_ref, att_ref, rwkv_ref, gate_ref, wa_ref, wb_ref, wo_ref, g_ref,
                  wr_ref, br_ref, x1_ref, h2_ref, comb_ref, cnt_ref):
    ga = gate_ref[:, 0:D_MODEL].astype(F32)
    gb = gate_ref[:, D_MODEL:2 * D_MODEL].astype(F32)
    merged = ga * _dot(att_ref[...], wa_ref[...]) + gb * _dot(rwkv_ref[...], wb_ref[...])
    x1 = x_ref[...] + _dot(merged.astype(BF16), wo_ref[...])
    x1_ref[...] = x1
    h2 = x1 * lax.rsqrt(jnp.mean(x1 * x1, axis=-1, keepdims=True) + RMS_EPS) * g_ref[...]
    h2_hi = h2.astype(BF16)
    h2_ref[...] = h2_hi
    h2_lo = (h2 - h2_hi.astype(F32)).astype(BF16)
    hw = _dot(h2_hi, wr_ref[...])
    lw = _dot(h2_lo, wr_ref[:, 0:ROUTER_LANES])
    logits = hw[:, 0:ROUTER_LANES] + (hw[:, ROUTER_LANES:2 * ROUTER_LANES] + lw) + br_ref[...]
    lane_i = lax.broadcasted_iota(jnp.int32, logits.shape, 1)
    lane = lane_i.astype(F32)
    lane_group = (lane_i >> 3).astype(F32)
    neg = jnp.finfo(F32).min
    big = float(ROUTER_LANES)

    def first_argmax(vals, mask):
        vm = jnp.where(mask, vals, neg)
        mx = jnp.max(vm, axis=-1, keepdims=True)
        idx = jnp.min(jnp.where(vm == mx, jnp.where(mask, lane, big), big), axis=-1, keepdims=True)
        return mx, idx

    is_group = (lane_i >= GROUP_LANE0) & (lane_i < GROUP_LANE0 + N_GROUPS)
    g_max, g_lane = first_argmax(logits, is_group)
    g_prob = 1.0 / jnp.sum(jnp.where(is_group, jnp.exp(logits - g_max), 0.0),
                           axis=-1, keepdims=True)
    g_idx = g_lane - GROUP_LANE0
    in_group = lane_group == g_idx
    e1, i1 = first_argmax(logits, in_group)
    e2, i2 = first_argmax(logits, in_group & (lane != i1))
    w2 = jnp.exp(e2 - e1)
    p1 = 1.0 / (1.0 + w2)
    p2 = w2 / (1.0 + w2)
    comb = jnp.where(lane == i1, p1 * g_prob, jnp.where(lane == i2, p2 * g_prob, 0.0))
    comb_ref[...] = jnp.where(lane_i == GROUP_ID_LANE, g_idx, comb)
    counts = jnp.sum(jnp.where(lane == g_idx, 1.0, 0.0), axis=0, keepdims=True)
    cnt_ref[0] = jnp.broadcast_to(counts, (8, ROUTER_LANES)).astype(jnp.int32)


def _merge(x2, att, rwkv, gates, wa, wb, wo, ln_g, wr, br):
    n = x2.shape[0]
    tm = MERGE_TILE
    row = lambda i: (i, 0)
    const = lambda i: (0, 0)
    return pl.pallas_call(
        _merge_kernel,
        grid=(n // tm,),
        in_specs=[
            pl.BlockSpec((tm, D_MODEL), row),
            pl.BlockSpec((tm, WIDTH), row),
            pl.BlockSpec((tm, WIDTH), row),
            pl.BlockSpec((tm, 2 * D_MODEL), row),
            pl.BlockSpec((WIDTH, D_MODEL), const),
            pl.BlockSpec((WIDTH, D_MODEL), const),
            pl.BlockSpec((D_MODEL, D_MODEL), const),
            pl.BlockSpec((1, D_MODEL), const),
            pl.BlockSpec((D_MODEL, 2 * ROUTER_LANES), const),
            pl.BlockSpec((1, ROUTER_LANES), const),
        ],
        out_specs=[
            pl.BlockSpec((tm, D_MODEL), row),
            pl.BlockSpec((tm, D_MODEL), row),
            pl.BlockSpec((tm, ROUTER_LANES), row),
            pl.BlockSpec((1, 8, ROUTER_LANES), lambda i: (i, 0, 0)),
        ],
        out_shape=[
            jax.ShapeDtypeStruct((n, D_MODEL), F32),
            jax.ShapeDtypeStruct((n, D_MODEL), BF16),
            jax.ShapeDtypeStruct((n, ROUTER_LANES), F32),
            jax.ShapeDtypeStruct((n // tm, 8, ROUTER_LANES), jnp.int32),
        ],
        compiler_params=pltpu.CompilerParams(
            dimension_semantics=("arbitrary",), vmem_limit_bytes=VMEM_LIMIT),
        name="merge",
    )(x2, att, rwkv, gates, wa, wb, wo, ln_g, wr, br)


def _moe_kernel(cnt_ref, x1_ref, h2_ref, comb_ref, wg_ref, wu_ref, wd_ref, lnf_ref, out_ref,
                keyc_ref, keyr_ref, *, final_norm):
    i = pl.program_id(0)
    g = pl.program_id(1)
    t = MOE_TILE
    r = MOE_BLOCK

    @pl.when(g == 0)
    def _():
        out_ref[...] = jnp.zeros(out_ref.shape, F32)
        lane = lax.broadcasted_iota(jnp.int32, (t, ROUTER_LANES), 1).astype(F32)
        gid = comb_ref[:, GROUP_ID_LANE:GROUP_ID_LANE + 1]
        onehot = jnp.where(lane == gid, 1.0, 0.0)
        onehot_b = onehot.astype(BF16)
        for ch in range(t // r):
            rows = lax.broadcasted_iota(jnp.int32, (r, t), 0) + ch * r
            cols = lax.broadcasted_iota(jnp.int32, (r, t), 1)
            earlier = jnp.where(cols < rows, 1.0, 0.0).astype(BF16)
            before = _dot(earlier, onehot_b)
            sl = slice(ch * r, (ch + 1) * r)
            rank = jnp.sum(before * onehot[sl], axis=-1, keepdims=True)
            keyc_ref[sl, :] = jnp.broadcast_to(gid[sl] * t + rank, (r, ROUTER_LANES))
        keyr_ref[...] = jnp.transpose(keyc_ref[...])[0:8, :]

    quarter = t // N_GROUPS
    row0 = pl.multiple_of(g * quarter, quarter)
    out_ref[pl.ds(row0, quarter), :] += x1_ref[...]

    n_tok = cnt_ref[(i * MERGE_PER_MOE) * N_GROUPS + g]
    for m in range(1, MERGE_PER_MOE):
        n_tok = n_tok + cnt_ref[(i * MERGE_PER_MOE + m) * N_GROUPS + g]
    n_full = n_tok // r
    n_tail = (n_tok - n_full * r + MOE_TAIL - 1) // MOE_TAIL

    lane_e = lax.broadcasted_iota(jnp.int32, (ROUTER_LANES, GROUP_FF), 0)
    col_e = lax.broadcasted_iota(jnp.int32, (ROUTER_LANES, GROUP_FF), 1) // D_EXPERT
    expand = jnp.where(lane_e == g * EXPERTS_PER_GROUP + col_e, 1.0, 0.0).astype(BF16)
    comb_b = comb_ref[...].astype(BF16)
    key_row = keyr_ref[0:1, :]

    def make_block(rb, rank0):
        if rb >= ROUTER_LANES:
            key_col = jnp.concatenate([keyc_ref[...]] * (rb // ROUTER_LANES), axis=1)
        else:
            key_col = keyc_ref[:, 0:rb]

        def block(j, carry):
            base = (g * t + rank0 + j * rb).astype(F32)
            want_r = lax.broadcasted_iota(jnp.int32, (rb, t), 0).astype(F32) + base
            take = jnp.where(want_r == key_row, 1.0, 0.0).astype(BF16)
            hs = _dot(take, h2_ref[...]).astype(BF16)
            cs = _dot(take, comb_b).astype(BF16)
            cw = _dot(cs, expand)
            experts = range(EXPERTS_PER_GROUP)
            hg = [_dot(hs, wg_ref[e]) for e in experts]
            hu = [_dot(hs, wu_ref[e]) for e in experts]
            act = [(hg[e] * _sigmoid(hg[e]) * hu[e]
                    * cw[:, e * D_EXPERT:(e + 1) * D_EXPERT]).astype(BF16) for e in experts]
            y = _dot(act[0], wd_ref[0])
            for e in experts[1:]:
                y = y + _dot(act[e], wd_ref[e])
            want_c = lax.broadcasted_iota(jnp.int32, (t, rb), 1).astype(F32) + base
            put = jnp.where(want_c == key_col, 1.0, 0.0).astype(BF16)
            out_ref[...] += _dot(put, y.astype(BF16))
            return carry

        return block

    lax.fori_loop(0, n_full, make_block(r, 0), 0)
    lax.fori_loop(0, n_tail, make_block(MOE_TAIL, n_full * r), 0)

    if final_norm:
        @pl.when(g == N_GROUPS - 1)
        def _():
            y = out_ref[...]
            out_ref[...] = y * lax.rsqrt(jnp.mean(y * y, axis=-1, keepdims=True) + RMS_EPS) * lnf_ref[...]


def _moe(cnt, x1, h2, comb, wg, wu, wd, lnf, final_norm):
    n = x1.shape[0]
    t = MOE_TILE
    quarter = t // N_GROUPS
    row = lambda i, g, cnt: (i, 0)
    grp = lambda i, g, cnt: (g, 0, 0)
    grid_spec = pltpu.PrefetchScalarGridSpec(
        num_scalar_prefetch=1,
        grid=(n // t, N_GROUPS),
        in_specs=[
            pl.BlockSpec((quarter, D_MODEL), lambda i, g, cnt: (i * N_GROUPS + g, 0)),
            pl.BlockSpec((t, D_MODEL), row),
            pl.BlockSpec((t, ROUTER_LANES), row),
            pl.BlockSpec((EXPERTS_PER_GROUP, D_MODEL, D_EXPERT), grp),
            pl.BlockSpec((EXPERTS_PER_GROUP, D_MODEL, D_EXPERT), grp),
            pl.BlockSpec((EXPERTS_PER_GROUP, D_EXPERT, D_MODEL), grp),
            pl.BlockSpec((1, D_MODEL), lambda i, g, cnt: (0, 0)),
        ],
        out_specs=pl.BlockSpec((t, D_MODEL), row),
        scratch_shapes=[pltpu.VMEM((t, ROUTER_LANES), F32), pltpu.VMEM((8, t), F32)],
    )
    return pl.pallas_call(
        functools.partial(_moe_kernel, final_norm=final_norm),
        grid_spec=grid_spec,
        out_shape=jax.ShapeDtypeStruct((n, D_MODEL), F32),
        compiler_params=pltpu.CompilerParams(
            dimension_semantics=("arbitrary", "arbitrary"), vmem_limit_bytes=MOE_VMEM_LIMIT),
        name="moe",
    )(cnt, x1, h2, comb, wg, wu, wd, lnf)


def kernel(x, ln_mix_g, w_in, att_rel_bias, rwkv_mu, rwkv_w0, rwkv_w2, rwkv_a0, rwkv_a2, rwkv_g2,
           rwkv_k_k, rwkv_k_a, rwkv_r_k, rwkv_gn_g, rwkv_gn_b, w_branch_att, w_branch_rwkv, w_out,
           ln_ffn_g, router_group_w, router_group_b, router_expert_w, router_expert_b,
           expert_w_gate, expert_w_up, expert_w_down, ln_final_g):
    bsz, seq, d = x.shape
    depth = w_in.shape[0]
    n = bsz * seq
    x2 = x.reshape(n, d)
    for l in range(depth):
        q, k, v, rw, gates = _in_proj(x2, ln_mix_g[l][None, :], w_in[l].astype(BF16),
                                      rwkv_mu[l][None, :], seq)
        bias = _rel_bias(att_rel_bias[l])
        att = _band_attn(q.reshape(bsz, seq, WIDTH), k.reshape(bsz, seq, WIDTH),
                         v.reshape(bsz, seq, WIDTH), bias)
        zeros = jnp.zeros((DECAY_LORA, WIDTH), F32)
        w2a2 = jnp.concatenate(
            [jnp.concatenate([rwkv_w2[l], zeros], axis=1),
             jnp.concatenate([zeros, rwkv_a2[l]], axis=1)], axis=0)
        rwkv = _rwkv(rw.reshape(bsz, seq, RWKV_PROJ), w2a2.astype(BF16), rwkv_g2[l].astype(BF16),
                     rwkv_w0[l][None, :], rwkv_a0[l][None, :], rwkv_k_k[l][None, :],
                     rwkv_k_a[l][None, :], rwkv_r_k[l].reshape(1, WIDTH),
                     rwkv_gn_g[l][None, :], rwkv_gn_b[l][None, :])
        wr = jnp.concatenate([router_expert_w[l], router_group_w[l]], axis=1)
        wr = jnp.pad(wr, ((0, 0), (0, ROUTER_LANES - wr.shape[1])))
        wr_hi = wr.astype(BF16)
        wr = jnp.concatenate([wr_hi, (wr - wr_hi.astype(F32)).astype(BF16)], axis=1)
        br = jnp.concatenate([router_expert_b[l], router_group_b[l]])
        br = jnp.pad(br, (0, ROUTER_LANES - br.shape[0]))[None, :]
        x1, h2, comb, counts = _merge(x2, att.reshape(n, WIDTH), rwkv.reshape(n, WIDTH), gates,
                                      w_branch_att[l].astype(BF16), w_branch_rwkv[l].astype(BF16),
                                      w_out[l].astype(BF16), ln_ffn_g[l][None, :], wr, br)
        cnt = counts[:, 0, :N_GROUPS].reshape(-1)
        x2 = _moe(cnt, x1, h2, comb, expert_w_gate[l].astype(BF16), expert_w_up[l].astype(BF16),
                  expert_w_down[l].astype(BF16), ln_final_g[None, :], final_norm=(l == depth - 1))
    return x2.reshape(bsz, seq, d)
```

```python
import functools
import math

import jax
import jax.numpy as jnp
from jax import lax
from jax.experimental import pallas as pl
from jax.experimental.pallas import tpu as pltpu

F32 = jnp.float32
BF16 = jnp.bfloat16
HIGHEST = lax.Precision.HIGHEST

D_MODEL = 1024
CHUNK = 64
HEADS = 8
HEAD_DIM = 64
WIDTH = HEADS * HEAD_DIM
LEFT_CHUNKS = 8
BAND = (LEFT_CHUNKS + 1) * CHUNK
REL_CLIP = 64
N_REL = 2 * REL_CLIP + 1
DECAY_LORA = 64
AAA_LORA = 64
GATE_LORA = 128
GN_EPS = 64e-5
RMS_EPS = 1e-6
ATT_PROJ = 3 * WIDTH
RWKV_PROJ = 3 * WIDTH + DECAY_LORA + AAA_LORA + GATE_LORA
D_IN = ATT_PROJ + RWKV_PROJ + 2 * D_MODEL
N_GROUPS = 4
EXPERTS_PER_GROUP = 8
N_EXPERTS = N_GROUPS * EXPERTS_PER_GROUP
D_EXPERT = 256
GROUP_FF = EXPERTS_PER_GROUP * D_EXPERT
RWKV_SEQS = 4
BIAS_KEYS = 192
ATT_CHUNKS = 2
ATT_GROUP = 4
ROUTER_LANES = 128
GROUP_LANE0 = N_EXPERTS
GROUP_ID_LANE = ROUTER_LANES - 1
MERGE_TILE = 512
MOE_TILE = 1024
MOE_BLOCK = 256
MOE_TAIL = 64
MERGE_PER_MOE = MOE_TILE // MERGE_TILE

VMEM_LIMIT = 52 * 1024 * 1024
MOE_VMEM_LIMIT = 58 * 1024 * 1024


def _dot(a, b):
    return jnp.dot(a, b, preferred_element_type=F32)


def _dot_hi(a, b):
    return jnp.dot(a, b, preferred_element_type=F32, precision=HIGHEST)


def _dot_nt(a, b, precision=None):
    return lax.dot_general(a, b, (((1,), (1,)), ((), ())),
                           preferred_element_type=F32, precision=precision)


def _dot_tn(a, b, precision=None):
    return lax.dot_general(a, b, (((0,), (0,)), ((), ())),
                           preferred_element_type=F32, precision=precision)


def _sigmoid(x):
    return 1.0 / (1.0 + jnp.exp(-x))


def _mm(a, b):
    return jnp.dot(a.astype(BF16), b.astype(BF16), preferred_element_type=F32)


def _head_sums(x):
    outs = []
    lane = lax.broadcasted_iota(jnp.int32, (x.shape[0], 2 * HEAD_DIM), 1)
    low = lane < HEAD_DIM
    for p in range(HEADS // 2):
        xp = x[:, 2 * HEAD_DIM * p:2 * HEAD_DIM * (p + 1)]
        s_lo = jnp.sum(jnp.where(low, xp, 0.0), axis=-1, keepdims=True)
        s_hi = jnp.sum(jnp.where(low, 0.0, xp), axis=-1, keepdims=True)
        outs.append(jnp.where(low, s_lo, s_hi))
    return jnp.concatenate(outs, axis=-1)


def _rel_bias_kernel(tab_ref, out_ref):
    rows = tab_ref.shape[1]
    n = lax.broadcasted_iota(jnp.int32, (rows, CHUNK * 128), 1)
    r = lax.broadcasted_iota(jnp.int32, (rows, CHUNK * 128), 0)
    q = n >> 7
    kk = n & 127
    idx = jnp.clip(CHUNK + q - kk, -REL_CLIP, REL_CLIP) + REL_CLIP
    pick = jnp.where(r == idx, 1.0, 0.0) - jnp.where(r == N_REL - 1, 1.0, 0.0)
    out_ref[...] = _dot_hi(tab_ref[...], pick.astype(F32))


def _rel_bias(rel_table):
    rows = 136
    tab = jnp.pad(rel_table.astype(F32), ((0, 0), (0, rows - N_REL)))
    tail = pl.pallas_call(
        _rel_bias_kernel,
        out_shape=jax.ShapeDtypeStruct((HEADS, CHUNK * 128), F32),
        name="rel_bias",
    )(tab)
    tail = tail.reshape(HEADS, CHUNK, 128)
    bias = jnp.concatenate([jnp.zeros((HEADS, CHUNK, BIAS_KEYS - 128), F32), tail], axis=-1)
    return bias.reshape(HEADS * CHUNK, BIAS_KEYS)


def _in_proj_kernel(x_ref, g_ref, w_ref, mu_ref, q_ref, k_ref, v_ref, rw_ref, gate_ref,
                    carry_ref, *, tiles_per_seq):
    i = pl.program_id(0)

    @pl.when(i == 0)
    def _():
        carry_ref[...] = jnp.zeros(carry_ref.shape, F32)

    x = x_ref[...]
    h = x * lax.rsqrt(jnp.mean(x * x, axis=-1, keepdims=True) + RMS_EPS) * g_ref[...]
    hb = h.astype(BF16)
    q_ref[...] = _dot(hb, w_ref[:, 0:WIDTH]).astype(BF16)
    k_ref[...] = _dot(hb, w_ref[:, WIDTH:2 * WIDTH]).astype(BF16)
    v_ref[...] = _dot(hb, w_ref[:, 2 * WIDTH:ATT_PROJ]).astype(BF16)
    rw = _dot(hb, w_ref[:, ATT_PROJ:ATT_PROJ + RWKV_PROJ])
    tm = rw.shape[0]
    first_prev = jnp.where(i % tiles_per_seq == 0, 0.0, carry_ref[0:1, :])
    rolled = pltpu.roll(rw, 1, axis=0)
    row = lax.broadcasted_iota(jnp.int32, rw.shape, 0)
    prev = jnp.where(row == 0, first_prev, rolled)
    carry_ref[0:1, :] = rw[tm - 1:tm, :]
    rw_ref[...] = rw + (prev - rw) * mu_ref[...]
    gate_ref[...] = _sigmoid(_dot(hb, w_ref[:, ATT_PROJ + RWKV_PROJ:D_IN])).astype(BF16)


def _in_proj(x2, ln_g, w_in_b, mu, seq):
    n = x2.shape[0]
    tm = 256
    row = lambda i: (i, 0)
    const = lambda i: (0, 0)
    return pl.pallas_call(
        functools.partial(_in_proj_kernel, tiles_per_seq=seq // tm),
        grid=(n // tm,),
        in_specs=[
            pl.BlockSpec((tm, D_MODEL), row),
            pl.BlockSpec((1, D_MODEL), const),
            pl.BlockSpec((D_MODEL, D_IN), const),
            pl.BlockSpec((1, RWKV_PROJ), const),
        ],
        out_specs=[
            pl.BlockSpec((tm, WIDTH), row),
            pl.BlockSpec((tm, WIDTH), row),
            pl.BlockSpec((tm, WIDTH), row),
            pl.BlockSpec((tm, RWKV_PROJ), row),
            pl.BlockSpec((tm, 2 * D_MODEL), row),
        ],
        out_shape=[
            jax.ShapeDtypeStruct((n, WIDTH), BF16),
            jax.ShapeDtypeStruct((n, WIDTH), BF16),
            jax.ShapeDtypeStruct((n, WIDTH), BF16),
            jax.ShapeDtypeStruct((n, RWKV_PROJ), F32),
            jax.ShapeDtypeStruct((n, 2 * D_MODEL), BF16),
        ],
        scratch_shapes=[pltpu.VMEM((8, RWKV_PROJ), F32)],
        compiler_params=pltpu.CompilerParams(
            dimension_semantics=("arbitrary",), vmem_limit_bytes=VMEM_LIMIT),
        name="in_proj",
    )(x2, ln_g, w_in_b, mu)


def _band_attn_kernel(q_ref, k_ref, v_ref, bias_ref, o_ref, kpad_ref, vpad_ref):
    seq = k_ref.shape[1]
    pad = LEFT_CHUNKS * CHUNK
    kpad_ref[0:pad, :] = jnp.zeros((pad, WIDTH), BF16)
    vpad_ref[0:pad, :] = jnp.zeros((pad, WIDTH), BF16)
    kpad_ref[pad:pad + seq, :] = k_ref[0]
    vpad_ref[pad:pad + seq, :] = v_ref[0]

    gw = ATT_GROUP * HEAD_DIM
    rows = ATT_GROUP * CHUNK
    r_head = lax.broadcasted_iota(jnp.int32, (rows, gw), 0) // CHUNK
    l_head = lax.broadcasted_iota(jnp.int32, (rows, gw), 1) // HEAD_DIM
    own = r_head == l_head
    kpos_lo = lax.broadcasted_iota(jnp.int32, (rows, BAND - BIAS_KEYS), 1)
    kpos_hi = lax.broadcasted_iota(jnp.int32, (rows, BIAS_KEYS), 1) + (BAND - BIAS_KEYS)
    neg = jnp.finfo(F32).min
    groups = range(HEADS // ATT_GROUP)
    lanes = [slice(g * gw, (g + 1) * gw) for g in groups]

    def chunk_pair(i, carry, masked):
        units = [(j, g) for j in range(ATT_CHUNKS) for g in groups]
        ids = range(len(units))
        starts = [pl.multiple_of((i * ATT_CHUNKS + j) * CHUNK, CHUNK) for j in range(ATT_CHUNKS)]
        kb = [kpad_ref[pl.ds(st, BAND), :] for st in starts]
        vb = [vpad_ref[pl.ds(st, BAND), :] for st in starts]
        q = [q_ref[0, pl.ds(st, CHUNK), :] * (HEAD_DIM ** -0.5) for st in starts]
        qrows = [jnp.where(own, jnp.concatenate([q[j][:, lanes[g]]] * ATT_GROUP, axis=0),
                           jnp.zeros((), BF16)) for j, g in units]
        s = [_dot_nt(qrows[u], kb[j][:, lanes[g]]) for u, (j, g) in enumerate(units)]
        s_lo = [s[u][:, 0:BAND - BIAS_KEYS] for u in ids]
        s_hi = [s[u][:, BAND - BIAS_KEYS:BAND] + bias_ref[g * rows:(g + 1) * rows, :]
                for u, (j, g) in enumerate(units)]
        if masked:
            first = [(LEFT_CHUNKS - (i * ATT_CHUNKS + j)) * CHUNK for j in range(ATT_CHUNKS)]
            s_lo = [jnp.where(kpos_lo >= first[j], s_lo[u], neg) for u, (j, g) in enumerate(units)]
            s_hi = [jnp.where(kpos_hi >= first[j], s_hi[u], neg) for u, (j, g) in enumerate(units)]
        m = [jnp.maximum(jnp.max(s_lo[u], axis=-1, keepdims=True),
                         jnp.max(s_hi[u], axis=-1, keepdims=True)) for u in ids]
        p_lo = [jnp.exp(s_lo[u] - m[u]) for u in ids]
        p_hi = [jnp.exp(s_hi[u] - m[u]) for u in ids]
        denom = [jnp.sum(p_lo[u], axis=-1, keepdims=True) + jnp.sum(p_hi[u], axis=-1, keepdims=True)
                 for u in ids]
        o_all = [(_dot(p_lo[u].astype(BF16), vb[j][0:BAND - BIAS_KEYS, lanes[g]])
                  + _dot(p_hi[u].astype(BF16), vb[j][BAND - BIAS_KEYS:BAND, lanes[g]])) / denom[u]
                 for u, (j, g) in enumerate(units)]
        for u, (j, g) in enumerate(units):
            o_own = jnp.where(own, o_all[u], 0.0)
            o = o_own[0:CHUNK]
            for h in range(1, ATT_GROUP):
                o = o + o_own[h * CHUNK:(h + 1) * CHUNK]
            o_ref[0, pl.ds(starts[j], CHUNK), lanes[g]] = o.astype(BF16)
        return carry

    n_trips = seq // (CHUNK * ATT_CHUNKS)
    n_masked = min(LEFT_CHUNKS // ATT_CHUNKS, n_trips)
    lax.fori_loop(0, n_masked, functools.partial(chunk_pair, masked=True), 0)
    lax.fori_loop(n_masked, n_trips, functools.partial(chunk_pair, masked=False), 0)


def _band_attn(q, k, v, bias):
    b, seq, _ = q.shape
    whole = pl.BlockSpec((1, seq, WIDTH), lambda i: (i, 0, 0))
    return pl.pallas_call(
        _band_attn_kernel,
        grid=(b,),
        in_specs=[whole, whole, whole, pl.BlockSpec((HEADS * CHUNK, BIAS_KEYS), lambda i: (0, 0))],
        out_specs=whole,
        out_shape=jax.ShapeDtypeStruct((b, seq, WIDTH), BF16),
        scratch_shapes=[pltpu.VMEM((seq + LEFT_CHUNKS * CHUNK, WIDTH), BF16),
                        pltpu.VMEM((seq + LEFT_CHUNKS * CHUNK, WIDTH), BF16)],
        compiler_params=pltpu.CompilerParams(
            dimension_semantics=("arbitrary",), vmem_limit_bytes=VMEM_LIMIT),
        name="band_attn",
    )(q, k, v, bias)


def _rwkv_kernel(rw_ref, w2a2_ref, g2_ref, w0_ref, a0_ref, kk_ref, ka_ref, rk_ref,
                 gng_ref, gnb_ref, y_ref, state_ref):
    c = pl.program_id(1)
    t = CHUNK
    nb = rw_ref.shape[0]
    rows = nb * t

    @pl.when(c == 0)
    def _():
        state_ref[...] = jnp.zeros(state_ref.shape, F32)

    rw = rw_ref[...].reshape(rows, RWKV_PROJ)
    r = rw[:, 0:WIDTH]
    k = rw[:, WIDTH:2 * WIDTH]
    v = rw[:, 2 * WIDTH:3 * WIDTH]
    lora = rw[:, 3 * WIDTH:3 * WIDTH + DECAY_LORA + AAA_LORA]
    g_lo = rw[:, 3 * WIDTH + DECAY_LORA + AAA_LORA:RWKV_PROJ]

    lane128 = lax.broadcasted_iota(jnp.int32, lora.shape, 1)
    lora = jnp.where(lane128 < DECAY_LORA, jnp.tanh(lora), lora)
    wa = _mm(lora, w2a2_ref[...])
    log_decay = -math.exp(-0.5) * _sigmoid(w0_ref[...] + wa[:, 0:WIDTH])
    lr = _sigmoid(a0_ref[...] + wa[:, WIDTH:2 * WIDTH])
    gate = _mm(_sigmoid(g_lo), g2_ref[...])

    kk_raw = k * kk_ref[...]
    k_mod = k * (1.0 + (lr - 1.0) * ka_ref[...])

    row = lax.broadcasted_iota(jnp.int32, (rows, rows), 0)
    col = lax.broadcasted_iota(jnp.int32, (rows, rows), 1)
    tri = jnp.where((row >= col) & ((row // t) == (col // t)), 1.0, 0.0).astype(BF16)
    ld1 = log_decay.astype(BF16)
    rem = log_decay - ld1.astype(F32)
    ld2 = rem.astype(BF16)
    ld3 = (rem - ld2.astype(F32)).astype(BF16)
    parts = _dot(tri, jnp.concatenate([ld1, ld2, ld3], axis=1))
    logp = parts[:, 0:WIDTH] + parts[:, WIDTH:2 * WIDTH] + parts[:, 2 * WIDTH:3 * WIDTH]
    p_in = jnp.exp(logp)
    p_ex = jnp.exp(logp - log_decay)
    p_inv = jnp.exp(-logp)

    kk = kk_raw / jnp.maximum(jnp.sqrt(_head_sums(kk_raw * kk_raw)), 1e-12)
    a_hat = -kk * p_ex
    r_hat = r * p_in
    b_hat = kk * lr * p_inv
    k_hat = k_mod * p_inv
    bonus = _head_sums(r * k_mod * rk_ref[...]) * v

    lane = lax.broadcasted_iota(jnp.int32, (t, 2 * HEAD_DIM), 1)
    low = lane < HEAD_DIM
    r2 = lax.broadcasted_iota(jnp.int32, (2 * t, 2 * HEAD_DIM), 0)
    c2 = lax.broadcasted_iota(jnp.int32, (2 * t, 2 * HEAD_DIM), 1)
    own = (r2 < t) == (c2 < HEAD_DIM)
    strict = (r2 & (t - 1)) > (c2 & (t - 1))
    incl = (r2 & (t - 1)) >= (c2 & (t - 1))
    eye = jnp.where(r2 == c2, 1.0, 0.0).astype(F32)

    def stack2(xp):
        return jnp.concatenate([jnp.where(low, xp, 0.0), jnp.where(low, 0.0, xp)], axis=0)

    chains = [(b, p) for b in range(nb) for p in range(HEADS // 2)]
    ids = range(len(chains))
    rs = [slice(b * t, (b + 1) * t) for b, _ in chains]
    ls = [slice(2 * HEAD_DIM * p, 2 * HEAD_DIM * (p + 1)) for _, p in chains]
    p_end = [p_in[(b + 1) * t - 1:(b + 1) * t, ls[i]] for i, (b, _) in enumerate(chains)]
    ar = [jnp.concatenate([stack2(a_hat[rs[i], ls[i]]), stack2(r_hat[rs[i], ls[i]])], axis=0).astype(BF16)
          for i in ids]
    bk2 = [(stack2(b_hat[rs[i], ls[i]]), stack2(k_hat[rs[i], ls[i]])) for i in ids]
    bk = [jnp.concatenate(bk2[i], axis=0).astype(BF16) for i in ids]
    btkt = [(jnp.concatenate(bk2[i], axis=0) * p_end[i]).astype(BF16) for i in ids]
    v2 = [stack2(v[rs[i], ls[i]]).astype(BF16) for i in ids]
    g = [_dot_nt(ar[i], bk[i]) for i in ids]
    st = [state_ref[b, p] for b, p in chains]
    ars = [_dot_nt(ar[i], st[i].astype(BF16)) for i in ids]
    l_ab = [jnp.where(strict, g[i][0:2 * t, 0:2 * t], 0.0) for i in ids]
    lm = [jnp.concatenate([jnp.where(strict, g[i][0:2 * t, 2 * t:4 * t], 0.0),
                           jnp.where(incl, g[i][2 * t:4 * t, 2 * t:4 * t], 0.0)], axis=0)
          for i in ids]
    m_rb = [jnp.where(incl, g[i][2 * t:4 * t, 0:2 * t], 0.0).astype(BF16) for i in ids]
    lv = [_mm(lm[i], v2[i]) for i in ids]

    w_inv = [eye + l_ab[i] for i in ids]
    l_pow = l_ab
    for _ in range(int(math.log2(t)) - 1):
        l_pow = [_mm(l_pow[i], l_pow[i]) for i in ids]
        w_inv = [w_inv[i] + _mm(w_inv[i], l_pow[i]) for i in ids]

    z = [_mm(w_inv[i], ars[i][0:2 * t] + lv[i][0:2 * t]) for i in ids]
    y = [ars[i][2 * t:4 * t] + lv[i][2 * t:4 * t] + _mm(m_rb[i], z[i]) for i in ids]
    for i, (b, p) in enumerate(chains):
        zv = jnp.concatenate([z[i].astype(BF16), v2[i]], axis=0)
        state_ref[b, p] = st[i] * p_end[i] + _dot_tn(zv, btkt[i])

    for i, (b, p) in enumerate(chains):
        mean = jnp.sum(y[i], axis=-1, keepdims=True) * (1.0 / HEAD_DIM)
        dev = jnp.where(own, y[i] - mean, 0.0)
        var = jnp.sum(dev * dev, axis=-1, keepdims=True) * (1.0 / HEAD_DIM)
        yn = dev * lax.rsqrt(var + GN_EPS)
        yn = yn[0:t] + yn[t:2 * t]
        out = (yn * gng_ref[:, ls[i]] + gnb_ref[:, ls[i]] + bonus[rs[i], ls[i]]) * gate[rs[i], ls[i]]
        y_ref[b, :, ls[i]] = out.astype(BF16)


def _rwkv(rw, w2a2, g2, w0, a0, k_k, k_a, r_k, gn_g, gn_b):
    b, seq, _ = rw.shape
    nc = seq // CHUNK
    nb = RWKV_SEQS
    const = lambda i, c: (0, 0)
    vec = pl.BlockSpec((1, WIDTH), const)
    return pl.pallas_call(
        _rwkv_kernel,
        grid=(b // nb, nc),
        in_specs=[
            pl.BlockSpec((nb, CHUNK, RWKV_PROJ), lambda i, c: (i, c, 0)),
            pl.BlockSpec((DECAY_LORA + AAA_LORA, 2 * WIDTH), const),
            pl.BlockSpec((GATE_LORA, WIDTH), const),
            vec, vec, vec, vec, vec, vec, vec,
        ],
        out_specs=pl.BlockSpec((nb, CHUNK, WIDTH), lambda i, c: (i, c, 0)),
        out_shape=jax.ShapeDtypeStruct((b, seq, WIDTH), BF16),
        scratch_shapes=[pltpu.VMEM((nb, HEADS // 2, 2 * HEAD_DIM, 2 * HEAD_DIM), F32)],
        compiler_params=pltpu.CompilerParams(
            dimension_semantics=("arbitrary", "arbitrary"), vmem_limit_bytes=VMEM_LIMIT),
        name="rwkv7",
    )(rw, w2a2, g2, w0, a0, k_k, k_a, r_k, gn_g, gn_b)


def _merge_kernel(x_ref, att_ref, rwkv_ref, gate_ref, wa_ref, wb_ref, wo_ref, g_ref,
                  wr_ref, br_ref, x1_ref, h2_ref, comb_ref, cnt_ref):
    ga = gate_ref[:, 0:D_MODEL].astype(F32)
    gb = gate_ref[:, D_MODEL:2 * D_MODEL].astype(F32)
    merged = ga * _dot(att_ref[...], wa_ref[...]) + gb * _dot(rwkv_ref[...], wb_ref[...])
    x1 = x_ref[...] + _dot(merged.astype(BF16), wo_ref[...])
    x1_ref[...] = x1
    h2 = x1 * lax.rsqrt(jnp.mean(x1 * x1, axis=-1, keepdims=True) + RMS_EPS) * g_ref[...]
    h2_hi = h2.astype(BF16)
    h2_ref[...] = h2_hi
    h2_lo = (h2 - h2_hi.astype(F32)).astype(BF16)
    hw = _dot(h2_hi, wr_ref[...])
    lw = _dot(h2_lo, wr_ref[:, 0:ROUTER_LANES])
    logits = hw[:, 0:ROUTER_LANES] + (hw[:, ROUTER_LANES:2 * ROUTER_LANES] + lw) + br_ref[...]
    lane_i = lax.broadcasted_iota(jnp.int32, logits.shape, 1)
    lane = lane_i.astype(F32)
    lane_group = (lane_i >> 3).astype(F32)
    neg = jnp.finfo(F32).min
    big = float(ROUTER_LANES)

    def first_argmax(vals, mask):
        vm = jnp.where(mask, vals, neg)
        mx = jnp.max(vm, axis=-1, keepdims=True)
        idx = jnp.min(jnp.where(vm == mx, jnp.where(mask, lane, big), big), axis=-1, keepdims=True)
        return mx, idx

    is_group = (lane_i >= GROUP_LANE0) & (lane_i < GROUP_LANE0 + N_GROUPS)
    g_max, g_lane = first_argmax(logits, is_group)
    g_prob = 1.0 / jnp.sum(jnp.where(is_group, jnp.exp(logits - g_max), 0.0),
                           axis=-1, keepdims=True)
    g_idx = g_lane - GROUP_LANE0
    in_group = lane_group == g_idx
    e1, i1 = first_argmax(logits, in_group)
    e2, i2 = first_argmax(logits, in_group & (lane != i1))
    w2 = jnp.exp(e2 - e1)
    p1 = 1.0 / (1.0 + w2)
    p2 = w2 / (1.0 + w2)
    comb = jnp.where(lane == i1, p1 * g_prob, jnp.where(lane == i2, p2 * g_prob, 0.0))
    comb_ref[...] = jnp.where(lane_i == GROUP_ID_LANE, g_idx, comb)
    counts = jnp.sum(jnp.where(lane == g_idx, 1.0, 0.0), axis=0, keepdims=True)
    cnt_ref[0] = jnp.broadcast_to(counts, (8, ROUTER_LANES)).astype(jnp.int32)


def _merge(x2, att, rwkv, gates, wa, wb, wo, ln_g, wr, br):
    n = x2.shape[0]
    tm = MERGE_TILE
    row = lambda i: (i, 0)
    const = lambda i: (0, 0)
    return pl.pallas_call(
        _merge_kernel,
        grid=(n // tm,),
        in_specs=[
            pl.BlockSpec((tm, D_MODEL), row),
            pl.BlockSpec((tm, WIDTH), row),
            pl.BlockSpec((tm, WIDTH), row),
            pl.BlockSpec((tm, 2 * D_MODEL), row),
            pl.BlockSpec((WIDTH, D_MODEL), const),
            pl.BlockSpec((WIDTH, D_MODEL), const),
            pl.BlockSpec((D_MODEL, D_MODEL), const),
            pl.BlockSpec((1, D_MODEL), const),
            pl.BlockSpec((D_MODEL, 2 * ROUTER_LANES), const),
            pl.BlockSpec((1, ROUTER_LANES), const),
        ],
        out_specs=[
            pl.BlockSpec((tm, D_MODEL), row),
            pl.BlockSpec((tm, D_MODEL), row),
            pl.BlockSpec((tm, ROUTER_LANES), row),
            pl.BlockSpec((1, 8, ROUTER_LANES), lambda i: (i, 0, 0)),
        ],
        out_shape=[
            jax.ShapeDtypeStruct((n, D_MODEL), F32),
            jax.ShapeDtypeStruct((n, D_MODEL), BF16),
            jax.ShapeDtypeStruct((n, ROUTER_LANES), F32),
            jax.ShapeDtypeStruct((n // tm, 8, ROUTER_LANES), jnp.int32),
        ],
        compiler_params=pltpu.CompilerParams(
            dimension_semantics=("arbitrary",), vmem_limit_bytes=VMEM_LIMIT),
        name="merge",
    )(x2, att, rwkv, gates, wa, wb, wo, ln_g, wr, br)


def _moe_kernel(cnt_ref, x1_ref, h2_ref, comb_ref, wg_ref, wu_ref, wd_ref, lnf_ref, out_ref,
                keyc_ref, keyr_ref, *, final_norm):
    i = pl.program_id(0)
    g = pl.program_id(1)
    t = MOE_TILE
    r = MOE_BLOCK

    @pl.when(g == 0)
    def _():
        out_ref[...] = jnp.zeros(out_ref.shape, F32)
        lane = lax.broadcasted_iota(jnp.int32, (t, ROUTER_LANES), 1).astype(F32)
        gid = comb_ref[:, GROUP_ID_LANE:GROUP_ID_LANE + 1]
        onehot = jnp.where(lane == gid, 1.0, 0.0)
        onehot_b = onehot.astype(BF16)
        for ch in range(t // r):
            rows = lax.broadcasted_iota(jnp.int32, (r, t), 0) + ch * r
            cols = lax.broadcasted_iota(jnp.int32, (r, t), 1)
            earlier = jnp.where(cols < rows, 1.0, 0.0).astype(BF16)
            before = _dot(earlier, onehot_b)
            sl = slice(ch * r, (ch + 1) * r)
            rank = jnp.sum(before * onehot[sl], axis=-1, keepdims=True)
            keyc_ref[sl, :] = jnp.broadcast_to(gid[sl] * t + rank, (r, ROUTER_LANES))
        keyr_ref[...] = jnp.transpose(keyc_ref[...])[0:8, :]

    quarter = t // N_GROUPS
    row0 = pl.multiple_of(g * quarter, quarter)
    out_ref[pl.ds(row0, quarter), :] += x1_ref[...]

    n_tok = cnt_ref[(i * MERGE_PER_MOE) * N_GROUPS + g]
    for m in range(1, MERGE_PER_MOE):
        n_tok = n_tok + cnt_ref[(i * MERGE_PER_MOE + m) * N_GROUPS + g]
    n_full = (n_tok + r - 1 - MOE_TAIL) // r
    n_tail = (jnp.maximum(n_tok - n_full * r, 0) + MOE_TAIL - 1) // MOE_TAIL

    lane_e = lax.broadcasted_iota(jnp.int32, (ROUTER_LANES, GROUP_FF), 0)
    col_e = lax.broadcasted_iota(jnp.int32, (ROUTER_LANES, GROUP_FF), 1) // D_EXPERT
    expand = jnp.where(lane_e == g * EXPERTS_PER_GROUP + col_e, 1.0, 0.0).astype(BF16)
    comb_b = comb_ref[...].astype(BF16)
    key_row = keyr_ref[0:1, :]

    def make_block(rb, rank0):
        if rb >= ROUTER_LANES:
            key_col = jnp.concatenate([keyc_ref[...]] * (rb // ROUTER_LANES), axis=1)
        else:
            key_col = keyc_ref[:, 0:rb]

        def block(j, carry):
            base = (g * t + rank0 + j * rb).astype(F32)
            want_r = lax.broadcasted_iota(jnp.int32, (rb, t), 0).astype(F32) + base
            take = jnp.where(want_r == key_row, 1.0, 0.0).astype(BF16)
            hs = _dot(take, h2_ref[...]).astype(BF16)
            cs = _dot(take, comb_b).astype(BF16)
            cw = _dot(cs, expand)
            experts = range(EXPERTS_PER_GROUP)
            hg = [_dot(hs, wg_ref[e]) for e in experts]
            hu = [_dot(hs, wu_ref[e]) for e in experts]
            act = [(hg[e] * _sigmoid(hg[e]) * hu[e]
                    * cw[:, e * D_EXPERT:(e + 1) * D_EXPERT]).astype(BF16) for e in experts]
            y = _dot(act[0], wd_ref[0])
            for e in experts[1:]:
                y = y + _dot(act[e], wd_ref[e])
            want_c = lax.broadcasted_iota(jnp.int32, (t, rb), 1).astype(F32) + base
            put = jnp.where(want_c == key_col, 1.0, 0.0).astype(BF16)
            out_ref[...] += _dot(put, y.astype(BF16))
            return carry

        return block

    lax.fori_loop(0, n_full, make_block(r, 0), 0)
    lax.fori_loop(0, n_tail, make_block(MOE_TAIL, n_full * r), 0)

    if final_norm:
        @pl.when(g == N_GROUPS - 1)
        def _():
            y = out_ref[...]
            out_ref[...] = y * lax.rsqrt(jnp.mean(y * y, axis=-1, keepdims=True) + RMS_EPS) * lnf_ref[...]


def _moe(cnt, x1, h2, comb, wg, wu, wd, lnf, final_norm):
    n = x1.shape[0]
    t = MOE_TILE
    quarter = t // N_GROUPS
    row = lambda i, g, cnt: (i, 0)
    grp = lambda i, g, cnt: (g, 0, 0)
    grid_spec = pltpu.PrefetchScalarGridSpec(
        num_scalar_prefetch=1,
        grid=(n // t, N_GROUPS),
        in_specs=[
            pl.BlockSpec((quarter, D_MODEL), lambda i, g, cnt: (i * N_GROUPS + g, 0)),
            pl.BlockSpec((t, D_MODEL), row),
            pl.BlockSpec((t, ROUTER_LANES), row),
            pl.BlockSpec((EXPERTS_PER_GROUP, D_MODEL, D_EXPERT), grp),
            pl.BlockSpec((EXPERTS_PER_GROUP, D_MODEL, D_EXPERT), grp),
            pl.BlockSpec((EXPERTS_PER_GROUP, D_EXPERT, D_MODEL), grp),
            pl.BlockSpec((1, D_MODEL), lambda i, g, cnt: (0, 0)),
        ],
        out_specs=pl.BlockSpec((t, D_MODEL), row),
        scratch_shapes=[pltpu.VMEM((t, ROUTER_LANES), F32), pltpu.VMEM((8, t), F32)],
    )
    return pl.pallas_call(
        functools.partial(_moe_kernel, final_norm=final_norm),
        grid_spec=grid_spec,
        out_shape=jax.ShapeDtypeStruct((n, D_MODEL), F32),
        compiler_params=pltpu.CompilerParams(
            dimension_semantics=("arbitrary", "arbitrary"), vmem_limit_bytes=MOE_VMEM_LIMIT),
        name="moe",
    )(cnt, x1, h2, comb, wg, wu, wd, lnf)


def kernel(x, ln_mix_g, w_in, att_rel_bias, rwkv_mu, rwkv_w0, rwkv_w2, rwkv_a0, rwkv_a2, rwkv_g2,
           rwkv_k_k, rwkv_k_a, rwkv_r_k, rwkv_gn_g, rwkv_gn_b, w_branch_att, w_branch_rwkv, w_out,
           ln_ffn_g, router_group_w, router_group_b, router_expert_w, router_expert_b,
           expert_w_gate, expert_w_up, expert_w_down, ln_final_g):
    bsz, seq, d = x.shape
    depth = w_in.shape[0]
    n = bsz * seq
    x2 = x.reshape(n, d)
    for l in range(depth):
        q, k, v, rw, gates = _in_proj(x2, ln_mix_g[l][None, :], w_in[l].astype(BF16),
                                      rwkv_mu[l][None, :], seq)
        bias = _rel_bias(att_rel_bias[l])
        att = _band_attn(q.reshape(bsz, seq, WIDTH), k.reshape(bsz, seq, WIDTH),
                         v.reshape(bsz, seq, WIDTH), bias)
        zeros = jnp.zeros((DECAY_LORA, WIDTH), F32)
        w2a2 = jnp.concatenate(
            [jnp.concatenate([rwkv_w2[l], zeros], axis=1),
             jnp.concatenate([zeros, rwkv_a2[l]], axis=1)], axis=0)
        rwkv = _rwkv(rw.reshape(bsz, seq, RWKV_PROJ), w2a2.astype(BF16), rwkv_g2[l].astype(BF16),
                     rwkv_w0[l][None, :], rwkv_a0[l][None, :], rwkv_k_k[l][None, :],
                     rwkv_k_a[l][None, :], rwkv_r_k[l].reshape(1, WIDTH),
                     rwkv_gn_g[l][None, :], rwkv_gn_b[l][None, :])
        wr = jnp.concatenate([router_expert_w[l], router_group_w[l]], axis=1)
        wr = jnp.pad(wr, ((0, 0), (0, ROUTER_LANES - wr.shape[1])))
        wr_hi = wr.astype(BF16)
        wr = jnp.concatenate([wr_hi, (wr - wr_hi.astype(F32)).astype(BF16)], axis=1)
        br = jnp.concatenate([router_expert_b[l], router_group_b[l]])
        br = jnp.pad(br, (0, ROUTER_LANES - br.shape[0]))[None, :]
        x1, h2, comb, counts = _merge(x2, att.reshape(n, WIDTH), rwkv.reshape(n, WIDTH), gates,
                                      w_branch_att[l].astype(BF16), w_branch_rwkv[l].astype(BF16),
                                      w_out[l].astype(BF16), ln_ffn_g[l][None, :], wr, br)
        cnt = counts[:, 0, :N_GROUPS].reshape(-1)
        x2 = _moe(cnt, x1, h2, comb, expert_w_gate[l].astype(BF16), expert_w_up[l].astype(BF16),
                  expert_w_down[l].astype(BF16), ln_final_g[None, :], final_norm=(l == depth - 1))
    return x2.reshape(bsz, seq, d)
```

```python
import functools
import math

import jax
import jax.numpy as jnp
from jax import lax
from jax.experimental import pallas as pl
from jax.experimental.pallas import tpu as pltpu
from jax.experimental.pallas import tpu_sc as plsc

F32 = jnp.float32
BF16 = jnp.bfloat16
HIGHEST = lax.Precision.HIGHEST

D_MODEL = 1024
CHUNK = 64
HEADS = 8
HEAD_DIM = 64
WIDTH = HEADS * HEAD_DIM
LEFT_CHUNKS = 8
BAND = (LEFT_CHUNKS + 1) * CHUNK
REL_CLIP = 64
N_REL = 2 * REL_CLIP + 1
DECAY_LORA = 64
AAA_LORA = 64
GATE_LORA = 128
GN_EPS = 64e-5
RMS_EPS = 1e-6
ATT_PROJ = 3 * WIDTH
RWKV_PROJ = 3 * WIDTH + DECAY_LORA + AAA_LORA + GATE_LORA
D_IN = ATT_PROJ + RWKV_PROJ + 2 * D_MODEL
N_GROUPS = 4
EXPERTS_PER_GROUP = 8
N_EXPERTS = N_GROUPS * EXPERTS_PER_GROUP
D_EXPERT = 256
GROUP_FF = EXPERTS_PER_GROUP * D_EXPERT
RWKV_SEQS = 4
BIAS_KEYS = 192
ATT_CHUNKS = 2
ATT_GROUP = 4
ROUTER_LANES = 128
GROUP_LANE0 = N_EXPERTS
EXPERT1_LANE, EXPERT2_LANE, WEIGHT1_LANE, WEIGHT2_LANE = 126, 125, 124, 123
MERGE_TILE = 512
MOE_BLOCK = 256
PACKED = D_MODEL // 2
SC_WINDOW = 128
SC_WORKERS = 32
FINAL_TILE = 1024

VMEM_LIMIT = 52 * 1024 * 1024


def _dot(a, b):
    return jnp.dot(a, b, preferred_element_type=F32)


def _dot_hi(a, b):
    return jnp.dot(a, b, preferred_element_type=F32, precision=HIGHEST)


def _dot_nt(a, b, precision=None):
    return lax.dot_general(a, b, (((1,), (1,)), ((), ())),
                           preferred_element_type=F32, precision=precision)


def _dot_tn(a, b, precision=None):
    return lax.dot_general(a, b, (((0,), (0,)), ((), ())),
                           preferred_element_type=F32, precision=precision)


def _sigmoid(x):
    return 1.0 / (1.0 + jnp.exp(-x))


def _pack_bf16(x):
    w = x.shape[1] // 2
    hi = pltpu.bitcast(x[:, :w].astype(BF16).astype(F32), jnp.uint32)
    lo = pltpu.bitcast(x[:, w:].astype(BF16).astype(F32), jnp.uint32)
    return hi | lax.shift_right_logical(lo, jnp.uint32(16))


def _unpack_bf16(p):
    hi = pltpu.bitcast(p & jnp.uint32(0xFFFF0000), F32)
    lo = pltpu.bitcast(lax.shift_left(p, jnp.uint32(16)), F32)
    return hi, lo


def _mm(a, b):
    return jnp.dot(a.astype(BF16), b.astype(BF16), preferred_element_type=F32)


def _head_sums(x):
    outs = []
    lane = lax.broadcasted_iota(jnp.int32, (x.shape[0], 2 * HEAD_DIM), 1)
    low = lane < HEAD_DIM
    for p in range(HEADS // 2):
        xp = x[:, 2 * HEAD_DIM * p:2 * HEAD_DIM * (p + 1)]
        s_lo = jnp.sum(jnp.where(low, xp, 0.0), axis=-1, keepdims=True)
        s_hi = jnp.sum(jnp.where(low, 0.0, xp), axis=-1, keepdims=True)
        outs.append(jnp.where(low, s_lo, s_hi))
    return jnp.concatenate(outs, axis=-1)


def _rel_bias_kernel(tab_ref, out_ref):
    rows = tab_ref.shape[1]
    n = lax.broadcasted_iota(jnp.int32, (rows, CHUNK * 128), 1)
    r = lax.broadcasted_iota(jnp.int32, (rows, CHUNK * 128), 0)
    q = n >> 7
    kk = n & 127
    idx = jnp.clip(CHUNK + q - kk, -REL_CLIP, REL_CLIP) + REL_CLIP
    pick = jnp.where(r == idx, 1.0, 0.0) - jnp.where(r == N_REL - 1, 1.0, 0.0)
    out_ref[...] = _dot_hi(tab_ref[...], pick.astype(F32))


def _rel_bias(rel_table):
    rows = 136
    tab = jnp.pad(rel_table.astype(F32), ((0, 0), (0, rows - N_REL)))
    tail = pl.pallas_call(
        _rel_bias_kernel,
        out_shape=jax.ShapeDtypeStruct((HEADS, CHUNK * 128), F32),
        name="rel_bias",
    )(tab)
    tail = tail.reshape(HEADS, CHUNK, 128)
    bias = jnp.concatenate([jnp.zeros((HEADS, CHUNK, BIAS_KEYS - 128), F32), tail], axis=-1)
    return bias.reshape(HEADS * CHUNK, BIAS_KEYS)


def _in_proj_kernel(x_ref, g_ref, w_ref, mu_ref, q_ref, k_ref, v_ref, rw_ref, gate_ref,
                    carry_ref, *, tiles_per_seq):
    i = pl.program_id(0)

    @pl.when(i == 0)
    def _():
        carry_ref[...] = jnp.zeros(carry_ref.shape, F32)

    x = x_ref[...]
    h = x * lax.rsqrt(jnp.mean(x * x, axis=-1, keepdims=True) + RMS_EPS) * g_ref[...]
    hb = h.astype(BF16)
    q_ref[...] = _dot(hb, w_ref[:, 0:WIDTH]).astype(BF16)
    k_ref[...] = _dot(hb, w_ref[:, WIDTH:2 * WIDTH]).astype(BF16)
    v_ref[...] = _dot(hb, w_ref[:, 2 * WIDTH:ATT_PROJ]).astype(BF16)
    rw = _dot(hb, w_ref[:, ATT_PROJ:ATT_PROJ + RWKV_PROJ])
    tm = rw.shape[0]
    first_prev = jnp.where(i % tiles_per_seq == 0, 0.0, carry_ref[0:1, :])
    rolled = pltpu.roll(rw, 1, axis=0)
    row = lax.broadcasted_iota(jnp.int32, rw.shape, 0)
    prev = jnp.where(row == 0, first_prev, rolled)
    carry_ref[0:1, :] = rw[tm - 1:tm, :]
    rw_ref[...] = rw + (prev - rw) * mu_ref[...]
    gate_ref[...] = _sigmoid(_dot(hb, w_ref[:, ATT_PROJ + RWKV_PROJ:D_IN])).astype(BF16)


def _in_proj(x2, ln_g, w_in_b, mu, seq):
    n = x2.shape[0]
    tm = 256
    row = lambda i: (i, 0)
    const = lambda i: (0, 0)
    return pl.pallas_call(
        functools.partial(_in_proj_kernel, tiles_per_seq=seq // tm),
        grid=(n // tm,),
        in_specs=[
            pl.BlockSpec((tm, D_MODEL), row),
            pl.BlockSpec((1, D_MODEL), const),
            pl.BlockSpec((D_MODEL, D_IN), const),
            pl.BlockSpec((1, RWKV_PROJ), const),
        ],
        out_specs=[
            pl.BlockSpec((tm, WIDTH), row),
            pl.BlockSpec((tm, WIDTH), row),
            pl.BlockSpec((tm, WIDTH), row),
            pl.BlockSpec((tm, RWKV_PROJ), row),
            pl.BlockSpec((tm, 2 * D_MODEL), row),
        ],
        out_shape=[
            jax.ShapeDtypeStruct((n, WIDTH), BF16),
            jax.ShapeDtypeStruct((n, WIDTH), BF16),
            jax.ShapeDtypeStruct((n, WIDTH), BF16),
            jax.ShapeDtypeStruct((n, RWKV_PROJ), F32),
            jax.ShapeDtypeStruct((n, 2 * D_MODEL), BF16),
        ],
        scratch_shapes=[pltpu.VMEM((8, RWKV_PROJ), F32)],
        compiler_params=pltpu.CompilerParams(
            dimension_semantics=("arbitrary",), vmem_limit_bytes=VMEM_LIMIT),
        name="in_proj",
    )(x2, ln_g, w_in_b, mu)


def _band_attn_kernel(q_ref, k_ref, v_ref, bias_ref, o_ref, kpad_ref, vpad_ref):
    seq = k_ref.shape[1]
    pad = LEFT_CHUNKS * CHUNK
    kpad_ref[0:pad, :] = jnp.zeros((pad, WIDTH), BF16)
    vpad_ref[0:pad, :] = jnp.zeros((pad, WIDTH), BF16)
    kpad_ref[pad:pad + seq, :] = k_ref[0]
    vpad_ref[pad:pad + seq, :] = v_ref[0]

    gw = ATT_GROUP * HEAD_DIM
    rows = ATT_GROUP * CHUNK
    r_head = lax.broadcasted_iota(jnp.int32, (rows, gw), 0) // CHUNK
    l_head = lax.broadcasted_iota(jnp.int32, (rows, gw), 1) // HEAD_DIM
    own = r_head == l_head
    kpos_lo = lax.broadcasted_iota(jnp.int32, (rows, BAND - BIAS_KEYS), 1)
    kpos_hi = lax.broadcasted_iota(jnp.int32, (rows, BIAS_KEYS), 1) + (BAND - BIAS_KEYS)
    neg = jnp.finfo(F32).min
    groups = range(HEADS // ATT_GROUP)
    lanes = [slice(g * gw, (g + 1) * gw) for g in groups]

    def chunk_pair(i, carry, masked):
        units = [(j, g) for j in range(ATT_CHUNKS) for g in groups]
        ids = range(len(units))
        starts = [pl.multiple_of((i * ATT_CHUNKS + j) * CHUNK, CHUNK) for j in range(ATT_CHUNKS)]
        kb = [kpad_ref[pl.ds(st, BAND), :] for st in starts]
        vb = [vpad_ref[pl.ds(st, BAND), :] for st in starts]
        q = [q_ref[0, pl.ds(st, CHUNK), :] * (HEAD_DIM ** -0.5) for st in starts]
        qrows = [jnp.where(own, jnp.concatenate([q[j][:, lanes[g]]] * ATT_GROUP, axis=0),
                           jnp.zeros((), BF16)) for j, g in units]
        s = [_dot_nt(qrows[u], kb[j][:, lanes[g]]) for u, (j, g) in enumerate(units)]
        s_lo = [s[u][:, 0:BAND - BIAS_KEYS] for u in ids]
        s_hi = [s[u][:, BAND - BIAS_KEYS:BAND] + bias_ref[g * rows:(g + 1) * rows, :]
                for u, (j, g) in enumerate(units)]
        if masked:
            first = [(LEFT_CHUNKS - (i * ATT_CHUNKS + j)) * CHUNK for j in range(ATT_CHUNKS)]
            s_lo = [jnp.where(kpos_lo >= first[j], s_lo[u], neg) for u, (j, g) in enumerate(units)]
            s_hi = [jnp.where(kpos_hi >= first[j], s_hi[u], neg) for u, (j, g) in enumerate(units)]
        m = [jnp.maximum(jnp.max(s_lo[u], axis=-1, keepdims=True),
                         jnp.max(s_hi[u], axis=-1, keepdims=True)) for u in ids]
        p_lo = [jnp.exp(s_lo[u] - m[u]) for u in ids]
        p_hi = [jnp.exp(s_hi[u] - m[u]) for u in ids]
        denom = [jnp.sum(p_lo[u], axis=-1, keepdims=True) + jnp.sum(p_hi[u], axis=-1, keepdims=True)
                 for u in ids]
        o_all = [(_dot(p_lo[u].astype(BF16), vb[j][0:BAND - BIAS_KEYS, lanes[g]])
                  + _dot(p_hi[u].astype(BF16), vb[j][BAND - BIAS_KEYS:BAND, lanes[g]])) / denom[u]
                 for u, (j, g) in enumerate(units)]
        for u, (j, g) in enumerate(units):
            o_own = jnp.where(own, o_all[u], 0.0)
            o = o_own[0:CHUNK]
            for h in range(1, ATT_GROUP):
                o = o + o_own[h * CHUNK:(h + 1) * CHUNK]
            o_ref[0, pl.ds(starts[j], CHUNK), lanes[g]] = o.astype(BF16)
        return carry

    n_trips = seq // (CHUNK * ATT_CHUNKS)
    n_masked = min(LEFT_CHUNKS // ATT_CHUNKS, n_trips)
    lax.fori_loop(0, n_masked, functools.partial(chunk_pair, masked=True), 0)
    lax.fori_loop(n_masked, n_trips, functools.partial(chunk_pair, masked=False), 0)


def _band_attn(q, k, v, bias):
    b, seq, _ = q.shape
    whole = pl.BlockSpec((1, seq, WIDTH), lambda i: (i, 0, 0))
    return pl.pallas_call(
        _band_attn_kernel,
        grid=(b,),
        in_specs=[whole, whole, whole, pl.BlockSpec((HEADS * CHUNK, BIAS_KEYS), lambda i: (0, 0))],
        out_specs=whole,
        out_shape=jax.ShapeDtypeStruct((b, seq, WIDTH), BF16),
        scratch_shapes=[pltpu.VMEM((seq + LEFT_CHUNKS * CHUNK, WIDTH), BF16),
                        pltpu.VMEM((seq + LEFT_CHUNKS * CHUNK, WIDTH), BF16)],
        compiler_params=pltpu.CompilerParams(
            dimension_semantics=("arbitrary",), vmem_limit_bytes=VMEM_LIMIT),
        name="band_attn",
    )(q, k, v, bias)


def _rwkv_kernel(rw_ref, w2a2_ref, g2_ref, w0_ref, a0_ref, kk_ref, ka_ref, rk_ref,
                 gng_ref, gnb_ref, y_ref, state_ref):
    c = pl.program_id(1)
    t = CHUNK
    nb = rw_ref.shape[0]
    rows = nb * t

    @pl.when(c == 0)
    def _():
        state_ref[...] = jnp.zeros(state_ref.shape, F32)

    rw = rw_ref[...].reshape(rows, RWKV_PROJ)
    r = rw[:, 0:WIDTH]
    k = rw[:, WIDTH:2 * WIDTH]
    v = rw[:, 2 * WIDTH:3 * WIDTH]
    lora = rw[:, 3 * WIDTH:3 * WIDTH + DECAY_LORA + AAA_LORA]
    g_lo = rw[:, 3 * WIDTH + DECAY_LORA + AAA_LORA:RWKV_PROJ]

    lane128 = lax.broadcasted_iota(jnp.int32, lora.shape, 1)
    lora = jnp.where(lane128 < DECAY_LORA, jnp.tanh(lora), lora)
    wa = _mm(lora, w2a2_ref[...])
    log_decay = -math.exp(-0.5) * _sigmoid(w0_ref[...] + wa[:, 0:WIDTH])
    lr = _sigmoid(a0_ref[...] + wa[:, WIDTH:2 * WIDTH])
    gate = _mm(_sigmoid(g_lo), g2_ref[...])

    kk_raw = k * kk_ref[...]
    k_mod = k * (1.0 + (lr - 1.0) * ka_ref[...])

    row = lax.broadcasted_iota(jnp.int32, (rows, rows), 0)
    col = lax.broadcasted_iota(jnp.int32, (rows, rows), 1)
    tri = jnp.where((row >= col) & ((row // t) == (col // t)), 1.0, 0.0).astype(BF16)
    ld1 = log_decay.astype(BF16)
    rem = log_decay - ld1.astype(F32)
    ld2 = rem.astype(BF16)
    ld3 = (rem - ld2.astype(F32)).astype(BF16)
    parts = _dot(tri, jnp.concatenate([ld1, ld2, ld3], axis=1))
    logp = parts[:, 0:WIDTH] + parts[:, WIDTH:2 * WIDTH] + parts[:, 2 * WIDTH:3 * WIDTH]
    p_in = jnp.exp(logp)
    p_ex = jnp.exp(logp - log_decay)
    p_inv = jnp.exp(-logp)

    kk = kk_raw / jnp.maximum(jnp.sqrt(_head_sums(kk_raw * kk_raw)), 1e-12)
    a_hat = -kk * p_ex
    r_hat = r * p_in
    b_hat = kk * lr * p_inv
    k_hat = k_mod * p_inv
    bonus = _head_sums(r * k_mod * rk_ref[...]) * v

    lane = lax.broadcasted_iota(jnp.int32, (t, 2 * HEAD_DIM), 1)
    low = lane < HEAD_DIM
    r2 = lax.broadcasted_iota(jnp.int32, (2 * t, 2 * HEAD_DIM), 0)
    c2 = lax.broadcasted_iota(jnp.int32, (2 * t, 2 * HEAD_DIM), 1)
    own = (r2 < t) == (c2 < HEAD_DIM)
    strict = (r2 & (t - 1)) > (c2 & (t - 1))
    incl = (r2 & (t - 1)) >= (c2 & (t - 1))
    eye = jnp.where(r2 == c2, 1.0, 0.0).astype(F32)

    def stack2(xp):
        return jnp.concatenate([jnp.where(low, xp, 0.0), jnp.where(low, 0.0, xp)], axis=0)

    chains = [(b, p) for b in range(nb) for p in range(HEADS // 2)]
    ids = range(len(chains))
    rs = [slice(b * t, (b + 1) * t) for b, _ in chains]
    ls = [slice(2 * HEAD_DIM * p, 2 * HEAD_DIM * (p + 1)) for _, p in chains]
    p_end = [p_in[(b + 1) * t - 1:(b + 1) * t, ls[i]] for i, (b, _) in enumerate(chains)]
    ar = [jnp.concatenate([stack2(a_hat[rs[i], ls[i]]), stack2(r_hat[rs[i], ls[i]])], axis=0).astype(BF16)
          for i in ids]
    bk2 = [(stack2(b_hat[rs[i], ls[i]]), stack2(k_hat[rs[i], ls[i]])) for i in ids]
    bk = [jnp.concatenate(bk2[i], axis=0).astype(BF16) for i in ids]
    btkt = [(jnp.concatenate(bk2[i], axis=0) * p_end[i]).astype(BF16) for i in ids]
    v2 = [stack2(v[rs[i], ls[i]]).astype(BF16) for i in ids]
    g = [_dot_nt(ar[i], bk[i]) for i in ids]
    st = [state_ref[b, p] for b, p in chains]
    ars = [_dot_nt(ar[i], st[i].astype(BF16)) for i in ids]
    l_ab = [jnp.where(strict, g[i][0:2 * t, 0:2 * t], 0.0) for i in ids]
    lm = [jnp.concatenate([jnp.where(strict, g[i][0:2 * t, 2 * t:4 * t], 0.0),
                           jnp.where(incl, g[i][2 * t:4 * t, 2 * t:4 * t], 0.0)], axis=0)
          for i in ids]
    m_rb = [jnp.where(incl, g[i][2 * t:4 * t, 0:2 * t], 0.0).astype(BF16) for i in ids]
    lv = [_mm(lm[i], v2[i]) for i in ids]

    w_inv = [eye + l_ab[i] for i in ids]
    l_pow = l_ab
    for _ in range(int(math.log2(t)) - 1):
        l_pow = [_mm(l_pow[i], l_pow[i]) for i in ids]
        w_inv = [w_inv[i] + _mm(w_inv[i], l_pow[i]) for i in ids]

    z = [_mm(w_inv[i], ars[i][0:2 * t] + lv[i][0:2 * t]) for i in ids]
    y = [ars[i][2 * t:4 * t] + lv[i][2 * t:4 * t] + _mm(m_rb[i], z[i]) for i in ids]
    for i, (b, p) in enumerate(chains):
        zv = jnp.concatenate([z[i].astype(BF16), v2[i]], axis=0)
        state_ref[b, p] = st[i] * p_end[i] + _dot_tn(zv, btkt[i])

    for i, (b, p) in enumerate(chains):
        mean = jnp.sum(y[i], axis=-1, keepdims=True) * (1.0 / HEAD_DIM)
        dev = jnp.where(own, y[i] - mean, 0.0)
        var = jnp.sum(dev * dev, axis=-1, keepdims=True) * (1.0 / HEAD_DIM)
        yn = dev * lax.rsqrt(var + GN_EPS)
        yn = yn[0:t] + yn[t:2 * t]
        out = (yn * gng_ref[:, ls[i]] + gnb_ref[:, ls[i]] + bonus[rs[i], ls[i]]) * gate[rs[i], ls[i]]
        y_ref[b, :, ls[i]] = out.astype(BF16)


def _rwkv(rw, w2a2, g2, w0, a0, k_k, k_a, r_k, gn_g, gn_b):
    b, seq, _ = rw.shape
    nc = seq // CHUNK
    nb = RWKV_SEQS
    const = lambda i, c: (0, 0)
    vec = pl.BlockSpec((1, WIDTH), const)
    return pl.pallas_call(
        _rwkv_kernel,
        grid=(b // nb, nc),
        in_specs=[
            pl.BlockSpec((nb, CHUNK, RWKV_PROJ), lambda i, c: (i, c, 0)),
            pl.BlockSpec((DECAY_LORA + AAA_LORA, 2 * WIDTH), const),
            pl.BlockSpec((GATE_LORA, WIDTH), const),
            vec, vec, vec, vec, vec, vec, vec,
        ],
        out_specs=pl.BlockSpec((nb, CHUNK, WIDTH), lambda i, c: (i, c, 0)),
        out_shape=jax.ShapeDtypeStruct((b, seq, WIDTH), BF16),
        scratch_shapes=[pltpu.VMEM((nb, HEADS // 2, 2 * HEAD_DIM, 2 * HEAD_DIM), F32)],
        compiler_params=pltpu.CompilerParams(
            dimension_semantics=("arbitrary", "arbitrary"), vmem_limit_bytes=VMEM_LIMIT),
        name="rwkv7",
    )(rw, w2a2, g2, w0, a0, k_k, k_a, r_k, gn_g, gn_b)


def _merge_kernel(x_ref, att_ref, rwkv_ref, gate_ref, wa_ref, wb_ref, wo_ref, g_ref,
                  wr_ref, br_ref, x1_ref, h2_ref, comb_ref, cnt_ref):
    ga = gate_ref[:, 0:D_MODEL].astype(F32)
    gb = gate_ref[:, D_MODEL:2 * D_MODEL].astype(F32)
    merged = ga * _dot(att_ref[...], wa_ref[...]) + gb * _dot(rwkv_ref[...], wb_ref[...])
    x1 = x_ref[...] + _dot(merged.astype(BF16), wo_ref[...])
    x1_ref[...] = x1
    h2 = x1 * lax.rsqrt(jnp.mean(x1 * x1, axis=-1, keepdims=True) + RMS_EPS) * g_ref[...]
    h2_hi = h2.astype(BF16)
    h2_ref[...] = _pack_bf16(h2)
    h2_lo = (h2 - h2_hi.astype(F32)).astype(BF16)
    hw = _dot(h2_hi, wr_ref[...])
    lw = _dot(h2_lo, wr_ref[:, 0:ROUTER_LANES])
    logits = hw[:, 0:ROUTER_LANES] + (hw[:, ROUTER_LANES:2 * ROUTER_LANES] + lw) + br_ref[...]
    lane_i = lax.broadcasted_iota(jnp.int32, logits.shape, 1)
    lane = lane_i.astype(F32)
    lane_group = (lane_i >> 3).astype(F32)
    neg = jnp.finfo(F32).min
    big = float(ROUTER_LANES)

    def first_argmax(vals, mask):
        vm = jnp.where(mask, vals, neg)
        mx = jnp.max(vm, axis=-1, keepdims=True)
        idx = jnp.min(jnp.where(vm == mx, jnp.where(mask, lane, big), big), axis=-1, keepdims=True)
        return mx, idx

    is_group = (lane_i >= GROUP_LANE0) & (lane_i < GROUP_LANE0 + N_GROUPS)
    g_max, g_lane = first_argmax(logits, is_group)
    g_prob = 1.0 / jnp.sum(jnp.where(is_group, jnp.exp(logits - g_max), 0.0),
                           axis=-1, keepdims=True)
    g_idx = g_lane - GROUP_LANE0
    in_group = lane_group == g_idx
    e1, i1 = first_argmax(logits, in_group)
    e2, i2 = first_argmax(logits, in_group & (lane != i1))
    w2 = jnp.exp(e2 - e1)
    p1 = 1.0 / (1.0 + w2)
    p2 = w2 / (1.0 + w2)
    rec = jnp.where(lane_i == EXPERT1_LANE, i1, jnp.where(lane_i == EXPERT2_LANE, i2, 0.0))
    rec = jnp.where(lane_i == WEIGHT1_LANE, p1 * g_prob, jnp.where(lane_i == WEIGHT2_LANE, p2 * g_prob, rec))
    comb_ref[...] = rec
    counts = jnp.sum(jnp.where(lane == i1, 1.0, 0.0) + jnp.where(lane == i2, 1.0, 0.0),
                     axis=0, keepdims=True)
    cnt_ref[0] = jnp.broadcast_to(counts, (8, ROUTER_LANES)).astype(jnp.int32)


def _merge(x2, att, rwkv, gates, wa, wb, wo, ln_g, wr, br):
    n = x2.shape[0]
    tm = MERGE_TILE
    row = lambda i: (i, 0)
    const = lambda i: (0, 0)
    return pl.pallas_call(
        _merge_kernel,
        grid=(n // tm,),
        in_specs=[
            pl.BlockSpec((tm, D_MODEL), row),
            pl.BlockSpec((tm, WIDTH), row),
            pl.BlockSpec((tm, WIDTH), row),
            pl.BlockSpec((tm, 2 * D_MODEL), row),
            pl.BlockSpec((WIDTH, D_MODEL), const),
            pl.BlockSpec((WIDTH, D_MODEL), const),
            pl.BlockSpec((D_MODEL, D_MODEL), const),
            pl.BlockSpec((1, D_MODEL), const),
            pl.BlockSpec((D_MODEL, 2 * ROUTER_LANES), const),
            pl.BlockSpec((1, ROUTER_LANES), const),
        ],
        out_specs=[
            pl.BlockSpec((tm, D_MODEL), row),
            pl.BlockSpec((tm, PACKED), row),
            pl.BlockSpec((tm, ROUTER_LANES), row),
            pl.BlockSpec((1, 8, ROUTER_LANES), lambda i: (i, 0, 0)),
        ],
        out_shape=[
            jax.ShapeDtypeStruct((n, D_MODEL), F32),
            jax.ShapeDtypeStruct((n, PACKED), jnp.uint32),
            jax.ShapeDtypeStruct((n, ROUTER_LANES), F32),
            jax.ShapeDtypeStruct((n // tm, 8, ROUTER_LANES), jnp.int32),
        ],
        compiler_params=pltpu.CompilerParams(
            dimension_semantics=("arbitrary",), vmem_limit_bytes=VMEM_LIMIT),
        name="merge",
    )(x2, att, rwkv, gates, wa, wb, wo, ln_g, wr, br)


def _moe_plan_kernel(cnt_ref, base_ref, blk_ref):
    nt = cnt_ref.shape[0]
    cnt = cnt_ref[...].astype(F32)
    lane = lax.broadcasted_iota(jnp.int32, (8, ROUTER_LANES), 1)
    total = jnp.broadcast_to(jnp.sum(cnt, axis=0, keepdims=True), (8, ROUTER_LANES))
    padded = jnp.floor((total + (MOE_BLOCK - 1)) * (1.0 / MOE_BLOCK)) * MOE_BLOCK
    r = lax.broadcasted_iota(jnp.int32, (ROUTER_LANES, ROUTER_LANES), 0)
    c = lax.broadcasted_iota(jnp.int32, (ROUTER_LANES, ROUTER_LANES), 1)
    seg_start = _dot_hi(padded, jnp.where(r < c, 1.0, 0.0).astype(F32))
    tr = lax.broadcasted_iota(jnp.int32, (nt, nt), 0)
    tc = lax.broadcasted_iota(jnp.int32, (nt, nt), 1)
    tile_off = _dot_hi(jnp.where(tc < tr, 1.0, 0.0).astype(F32), cnt)
    base_ref[...] = seg_start[0:1, :] + tile_off
    seg_end = (seg_start + padded)[0:1, :]
    rows_total = jnp.sum(jnp.where(lane[0:1, :] < N_EXPERTS, padded[0:1, :], 0.0), axis=-1, keepdims=True)
    nblk = blk_ref.shape[0]
    blk_row = lax.broadcasted_iota(jnp.int32, (nblk, ROUTER_LANES), 0).astype(F32) * MOE_BLOCK
    blk_lane = lax.broadcasted_iota(jnp.int32, (nblk, ROUTER_LANES), 1)
    done = jnp.where((seg_end <= blk_row) & (blk_lane < N_EXPERTS), 1.0, 0.0)
    expert = jnp.minimum(jnp.sum(done, axis=-1, keepdims=True), N_EXPERTS - 1.0)
    valid = jnp.where(blk_row < rows_total, 1.0, 0.0)
    blk_ref[...] = jnp.where(blk_lane == 0, expert, jnp.where(blk_lane == 1, valid, 0.0)).astype(jnp.int32)


def _moe_plan(cnt, n_blocks):
    nt = cnt.shape[0]
    return pl.pallas_call(
        _moe_plan_kernel,
        out_shape=[jax.ShapeDtypeStruct((nt, ROUTER_LANES), F32),
                   jax.ShapeDtypeStruct((n_blocks, ROUTER_LANES), jnp.int32)],
        name="moe_plan",
    )(cnt)


def _moe_pos_kernel(comb_ref, base_ref, pos_ref):
    t = comb_ref.shape[0]
    comb = comb_ref[...]
    lane_i = lax.broadcasted_iota(jnp.int32, (t, ROUTER_LANES), 1)
    lane = lane_i.astype(F32)
    pick1 = jnp.where(lane == comb[:, EXPERT1_LANE:EXPERT1_LANE + 1], 1.0, 0.0)
    pick2 = jnp.where(lane == comb[:, EXPERT2_LANE:EXPERT2_LANE + 1], 1.0, 0.0)
    rows = lax.broadcasted_iota(jnp.int32, (t, t), 0)
    cols = lax.broadcasted_iota(jnp.int32, (t, t), 1)
    earlier = jnp.where(cols < rows, 1.0, 0.0).astype(BF16)
    before1 = _dot(earlier, pick1.astype(BF16))
    before2 = _dot(earlier, pick2.astype(BF16))
    base = base_ref[0]
    firsts = jnp.sum(pick1, axis=0, keepdims=True)
    pos1 = jnp.sum(pick1 * (base + before1), axis=-1, keepdims=True)
    pos2 = jnp.sum(pick2 * (base + firsts + before2), axis=-1, keepdims=True)
    both = jnp.where(lane_i == 0, pos1, jnp.where(lane_i == 1, pos2, 0.0))
    pos_ref[...] = jnp.transpose(both)[0:8, :].astype(jnp.int32)


def _moe_pos(comb, base):
    n = comb.shape[0]
    t = MERGE_TILE
    return pl.pallas_call(
        _moe_pos_kernel,
        grid=(n // t,),
        in_specs=[pl.BlockSpec((t, ROUTER_LANES), lambda i: (i, 0)),
                  pl.BlockSpec((1, 1, ROUTER_LANES), lambda i: (i, 0, 0))],
        out_specs=pl.BlockSpec((8, t), lambda i: (0, i)),
        out_shape=jax.ShapeDtypeStruct((8, n), jnp.int32),
        compiler_params=pltpu.CompilerParams(dimension_semantics=("arbitrary",)),
        name="moe_pos",
    )(comb, base.reshape(n // t, 1, ROUTER_LANES))


def _sc_scatter_rows(h, pos, n_rows):
    n = h.shape[0]
    per_worker = n // (SC_WINDOW * SC_WORKERS)
    mesh = plsc.VectorSubcoreMesh(core_axis_name="core", subcore_axis_name="subcore")

    @pl.kernel(out_type=jax.ShapeDtypeStruct((n_rows, PACKED), jnp.uint32), mesh=mesh,
               scratch_types=[pltpu.VMEM((8, SC_WINDOW), jnp.int32),
                              pltpu.VMEM((SC_WINDOW, PACKED), jnp.uint32)])
    def scatter(h_hbm, pos_hbm, out_hbm, idx, buf):
        worker = lax.axis_index("core") * (SC_WORKERS // 2) + lax.axis_index("subcore")

        @pl.loop(0, per_worker)
        def _(b):
            start = (worker * per_worker + b) * SC_WINDOW
            pltpu.sync_copy(pos_hbm.at[:, pl.ds(start, SC_WINDOW)], idx)
            pltpu.sync_copy(h_hbm.at[pl.ds(start, SC_WINDOW)], buf)
            pltpu.sync_copy(buf, out_hbm.at[idx.at[0]])
            pltpu.sync_copy(buf, out_hbm.at[idx.at[1]])

    return scatter(h, pos)


def _sc_gather_rows(y, pos):
    n = pos.shape[1]
    per_worker = n // (SC_WINDOW * SC_WORKERS)
    mesh = plsc.VectorSubcoreMesh(core_axis_name="core", subcore_axis_name="subcore")
    out = jax.ShapeDtypeStruct((n, PACKED), jnp.uint32)

    @pl.kernel(out_type=(out, out), mesh=mesh,
               scratch_types=[pltpu.VMEM((8, SC_WINDOW), jnp.int32),
                              pltpu.VMEM((SC_WINDOW, PACKED), jnp.uint32)])
    def gather(y_hbm, pos_hbm, o1_hbm, o2_hbm, idx, buf):
        worker = lax.axis_index("core") * (SC_WORKERS // 2) + lax.axis_index("subcore")

        @pl.loop(0, per_worker)
        def _(b):
            start = (worker * per_worker + b) * SC_WINDOW
            pltpu.sync_copy(pos_hbm.at[:, pl.ds(start, SC_WINDOW)], idx)
            pltpu.sync_copy(y_hbm.at[idx.at[0]], buf)
            pltpu.sync_copy(buf, o1_hbm.at[pl.ds(start, SC_WINDOW)])
            pltpu.sync_copy(y_hbm.at[idx.at[1]], buf)
            pltpu.sync_copy(buf, o2_hbm.at[pl.ds(start, SC_WINDOW)])

    return gather(y, pos)


def _expert_mlp_kernel(expert_ref, valid_ref, hs_ref, wg_ref, wu_ref, wd_ref, y_ref):
    b = pl.program_id(0)

    @pl.when(valid_ref[b] == 0)
    def _():
        y_ref[...] = jnp.zeros(y_ref.shape, jnp.uint32)

    @pl.when(valid_ref[b] != 0)
    def _():
        half = D_MODEL // 2
        h_lo, h_hi = _unpack_bf16(hs_ref[...])
        h_lo = h_lo.astype(BF16)
        h_hi = h_hi.astype(BF16)
        hg = _dot(h_lo, wg_ref[0, 0:half, :]) + _dot(h_hi, wg_ref[0, half:D_MODEL, :])
        hu = _dot(h_lo, wu_ref[0, 0:half, :]) + _dot(h_hi, wu_ref[0, half:D_MODEL, :])
        act = (hg * _sigmoid(hg) * hu).astype(BF16)
        y_ref[...] = _pack_bf16(_dot(act, wd_ref[0]))


def _expert_mlp(blk_expert, blk_valid, hs, wg, wu, wd):
    n_blocks = hs.shape[0] // MOE_BLOCK
    by_expert = lambda b, expert, valid: (expert[b], 0, 0)
    grid_spec = pltpu.PrefetchScalarGridSpec(
        num_scalar_prefetch=2,
        grid=(n_blocks,),
        in_specs=[
            pl.BlockSpec((MOE_BLOCK, PACKED), lambda b, expert, valid: (b, 0)),
            pl.BlockSpec((1, D_MODEL, D_EXPERT), by_expert),
            pl.BlockSpec((1, D_MODEL, D_EXPERT), by_expert),
            pl.BlockSpec((1, D_EXPERT, D_MODEL), by_expert),
        ],
        out_specs=pl.BlockSpec((MOE_BLOCK, PACKED), lambda b, expert, valid: (b, 0)),
    )
    return pl.pallas_call(
        _expert_mlp_kernel,
        grid_spec=grid_spec,
        out_shape=jax.ShapeDtypeStruct((hs.shape[0], PACKED), jnp.uint32),
        compiler_params=pltpu.CompilerParams(
            dimension_semantics=("arbitrary",), vmem_limit_bytes=VMEM_LIMIT),
        name="expert_mlp",
    )(blk_expert, blk_valid, hs, wg, wu, wd)


def _moe_out_kernel(x1_ref, y1_ref, y2_ref, comb_ref, lnf_ref, out_ref, *, final_norm):
    w1 = comb_ref[:, WEIGHT1_LANE:WEIGHT1_LANE + 1]
    w2 = comb_ref[:, WEIGHT2_LANE:WEIGHT2_LANE + 1]
    a_lo, a_hi = _unpack_bf16(y1_ref[...])
    b_lo, b_hi = _unpack_bf16(y2_ref[...])
    moe = jnp.concatenate([w1 * a_lo + w2 * b_lo, w1 * a_hi + w2 * b_hi], axis=1)
    y = x1_ref[...] + moe
    if final_norm:
        y = y * lax.rsqrt(jnp.mean(y * y, axis=-1, keepdims=True) + RMS_EPS) * lnf_ref[...]
    out_ref[...] = y


def _moe_out(x1, y1, y2, comb, lnf, final_norm):
    n = x1.shape[0]
    t = FINAL_TILE
    row = lambda i: (i, 0)
    return pl.pallas_call(
        functools.partial(_moe_out_kernel, final_norm=final_norm),
        grid=(n // t,),
        in_specs=[pl.BlockSpec((t, D_MODEL), row), pl.BlockSpec((t, PACKED), row),
                  pl.BlockSpec((t, PACKED), row), pl.BlockSpec((t, ROUTER_LANES), row),
                  pl.BlockSpec((1, D_MODEL), lambda i: (0, 0))],
        out_specs=pl.BlockSpec((t, D_MODEL), row),
        out_shape=jax.ShapeDtypeStruct((n, D_MODEL), F32),
        compiler_params=pltpu.CompilerParams(
            dimension_semantics=("arbitrary",), vmem_limit_bytes=VMEM_LIMIT),
        name="moe_out",
    )(x1, y1, y2, comb, lnf)


def _moe(counts, x1, h2p, comb, wg, wu, wd, lnf, final_norm):
    n = x1.shape[0]
    n_blocks = (2 * n) // MOE_BLOCK + N_EXPERTS
    base, blk = _moe_plan(counts[:, 0, :], n_blocks)
    pos = _moe_pos(comb, base)
    hs = _sc_scatter_rows(h2p, pos, n_blocks * MOE_BLOCK)
    ys = _expert_mlp(blk[:, 0], blk[:, 1], hs, wg, wu, wd)
    y1, y2 = _sc_gather_rows(ys, pos)
    return _moe_out(x1, y1, y2, comb, lnf, final_norm)


def kernel(x, ln_mix_g, w_in, att_rel_bias, rwkv_mu, rwkv_w0, rwkv_w2, rwkv_a0, rwkv_a2, rwkv_g2,
           rwkv_k_k, rwkv_k_a, rwkv_r_k, rwkv_gn_g, rwkv_gn_b, w_branch_att, w_branch_rwkv, w_out,
           ln_ffn_g, router_group_w, router_group_b, router_expert_w, router_expert_b,
           expert_w_gate, expert_w_up, expert_w_down, ln_final_g):
    bsz, seq, d = x.shape
    depth = w_in.shape[0]
    n = bsz * seq
    x2 = x.reshape(n, d)
    for l in range(depth):
        q, k, v, rw, gates = _in_proj(x2, ln_mix_g[l][None, :], w_in[l].astype(BF16),
                                      rwkv_mu[l][None, :], seq)
        bias = _rel_bias(att_rel_bias[l])
        att = _band_attn(q.reshape(bsz, seq, WIDTH), k.reshape(bsz, seq, WIDTH),
                         v.reshape(bsz, seq, WIDTH), bias)
        zeros = jnp.zeros((DECAY_LORA, WIDTH), F32)
        w2a2 = jnp.concatenate(
            [jnp.concatenate([rwkv_w2[l], zeros], axis=1),
             jnp.concatenate([zeros, rwkv_a2[l]], axis=1)], axis=0)
        rwkv = _rwkv(rw.reshape(bsz, seq, RWKV_PROJ), w2a2.astype(BF16), rwkv_g2[l].astype(BF16),
                     rwkv_w0[l][None, :], rwkv_a0[l][None, :], rwkv_k_k[l][None, :],
                     rwkv_k_a[l][None, :], rwkv_r_k[l].reshape(1, WIDTH),
                     rwkv_gn_g[l][None, :], rwkv_gn_b[l][None, :])
        wr = jnp.concatenate([router_expert_w[l], router_group_w[l]], axis=1)
        wr = jnp.pad(wr, ((0, 0), (0, ROUTER_LANES - wr.shape[1])))
        wr_hi = wr.astype(BF16)
        wr = jnp.concatenate([wr_hi, (wr - wr_hi.astype(F32)).astype(BF16)], axis=1)
        br = jnp.concatenate([router_expert_b[l], router_group_b[l]])
        br = jnp.pad(br, (0, ROUTER_LANES - br.shape[0]))[None, :]
        x1, h2, comb, counts = _merge(x2, att.reshape(n, WIDTH), rwkv.reshape(n, WIDTH), gates,
                                      w_branch_att[l].astype(BF16), w_branch_rwkv[l].astype(BF16),
                                      w_out[l].astype(BF16), ln_ffn_g[l][None, :], wr, br)
        x2 = _moe(counts, x1, h2, comb, expert_w_gate[l].astype(BF16), expert_w_up[l].astype(BF16),
                  expert_w_down[l].astype(BF16), ln_final_g[None, :], final_norm=(l == depth - 1))
    return x2.reshape(bsz, seq, d)
```

```python
import functools
import math

import jax
import jax.numpy as jnp
from jax import lax
from jax.experimental import pallas as pl
from jax.experimental.pallas import tpu as pltpu
from jax.experimental.pallas import tpu_sc as plsc

F32 = jnp.float32
BF16 = jnp.bfloat16
HIGHEST = lax.Precision.HIGHEST

D_MODEL = 1024
CHUNK = 64
HEADS = 8
HEAD_DIM = 64
WIDTH = HEADS * HEAD_DIM
LEFT_CHUNKS = 8
BAND = (LEFT_CHUNKS + 1) * CHUNK
REL_CLIP = 64
N_REL = 2 * REL_CLIP + 1
DECAY_LORA = 64
AAA_LORA = 64
GATE_LORA = 128
GN_EPS = 64e-5
RMS_EPS = 1e-6
ATT_PROJ = 3 * WIDTH
RWKV_PROJ = 3 * WIDTH + DECAY_LORA + AAA_LORA + GATE_LORA
D_IN = ATT_PROJ + RWKV_PROJ + 2 * D_MODEL
N_GROUPS = 4
EXPERTS_PER_GROUP = 8
N_EXPERTS = N_GROUPS * EXPERTS_PER_GROUP
D_EXPERT = 256
GROUP_FF = EXPERTS_PER_GROUP * D_EXPERT
RWKV_SEQS = 4
BIAS_KEYS = 192
ATT_CHUNKS = 2
ATT_GROUP = 4
ROUTER_LANES = 128
GROUP_LANE0 = N_EXPERTS
EXPERT1_LANE, EXPERT2_LANE, WEIGHT1_LANE, WEIGHT2_LANE = 126, 125, 124, 123
MERGE_TILE = 512
MOE_BLOCK = 256
PACKED = D_MODEL // 2
SC_WINDOW = 128
SC_WORKERS = 32
FINAL_TILE = 1024
MLP_SUB = 2

VMEM_LIMIT = 52 * 1024 * 1024


def _dot(a, b):
    return jnp.dot(a, b, preferred_element_type=F32)


def _dot_hi(a, b):
    return jnp.dot(a, b, preferred_element_type=F32, precision=HIGHEST)


def _dot_nt(a, b, precision=None):
    return lax.dot_general(a, b, (((1,), (1,)), ((), ())),
                           preferred_element_type=F32, precision=precision)


def _dot_tn(a, b, precision=None):
    return lax.dot_general(a, b, (((0,), (0,)), ((), ())),
                           preferred_element_type=F32, precision=precision)


def _sigmoid(x):
    return 1.0 / (1.0 + jnp.exp(-x))


def _pack_bf16(x):
    w = x.shape[1] // 2
    hi = pltpu.bitcast(x[:, :w].astype(BF16).astype(F32), jnp.uint32)
    lo = pltpu.bitcast(x[:, w:].astype(BF16).astype(F32), jnp.uint32)
    return hi | lax.shift_right_logical(lo, jnp.uint32(16))


def _unpack_bf16(p):
    hi = pltpu.bitcast(p & jnp.uint32(0xFFFF0000), F32)
    lo = pltpu.bitcast(lax.shift_left(p, jnp.uint32(16)), F32)
    return hi, lo


def _mm(a, b):
    return jnp.dot(a.astype(BF16), b.astype(BF16), preferred_element_type=F32)


def _head_sums(x):
    outs = []
    lane = lax.broadcasted_iota(jnp.int32, (x.shape[0], 2 * HEAD_DIM), 1)
    low = lane < HEAD_DIM
    for p in range(HEADS // 2):
        xp = x[:, 2 * HEAD_DIM * p:2 * HEAD_DIM * (p + 1)]
        s_lo = jnp.sum(jnp.where(low, xp, 0.0), axis=-1, keepdims=True)
        s_hi = jnp.sum(jnp.where(low, 0.0, xp), axis=-1, keepdims=True)
        outs.append(jnp.where(low, s_lo, s_hi))
    return jnp.concatenate(outs, axis=-1)


def _rel_bias_kernel(tab_ref, out_ref):
    rows = tab_ref.shape[1]
    n = lax.broadcasted_iota(jnp.int32, (rows, CHUNK * 128), 1)
    r = lax.broadcasted_iota(jnp.int32, (rows, CHUNK * 128), 0)
    q = n >> 7
    kk = n & 127
    idx = jnp.clip(CHUNK + q - kk, -REL_CLIP, REL_CLIP) + REL_CLIP
    pick = jnp.where(r == idx, 1.0, 0.0) - jnp.where(r == N_REL - 1, 1.0, 0.0)
    out_ref[...] = _dot_hi(tab_ref[...], pick.astype(F32))


def _rel_bias(rel_table):
    rows = 136
    tab = jnp.pad(rel_table.astype(F32), ((0, 0), (0, rows - N_REL)))
    tail = pl.pallas_call(
        _rel_bias_kernel,
        out_shape=jax.ShapeDtypeStruct((HEADS, CHUNK * 128), F32),
        name="rel_bias",
    )(tab)
    tail = tail.reshape(HEADS, CHUNK, 128)
    bias = jnp.concatenate([jnp.zeros((HEADS, CHUNK, BIAS_KEYS - 128), F32), tail], axis=-1)
    return bias.reshape(HEADS * CHUNK, BIAS_KEYS)


def _in_proj_kernel(x_ref, g_ref, w_ref, mu_ref, q_ref, k_ref, v_ref, rw_ref, gate_ref,
                    carry_ref, *, tiles_per_seq):
    i = pl.program_id(0)

    @pl.when(i == 0)
    def _():
        carry_ref[...] = jnp.zeros(carry_ref.shape, F32)

    x = x_ref[...]
    h = x * lax.rsqrt(jnp.mean(x * x, axis=-1, keepdims=True) + RMS_EPS) * g_ref[...]
    hb = h.astype(BF16)
    q_ref[...] = _dot(hb, w_ref[:, 0:WIDTH]).astype(BF16)
    k_ref[...] = _dot(hb, w_ref[:, WIDTH:2 * WIDTH]).astype(BF16)
    v_ref[...] = _dot(hb, w_ref[:, 2 * WIDTH:ATT_PROJ]).astype(BF16)
    rw = _dot(hb, w_ref[:, ATT_PROJ:ATT_PROJ + RWKV_PROJ])
    tm = rw.shape[0]
    first_prev = jnp.where(i % tiles_per_seq == 0, 0.0, carry_ref[0:1, :])
    rolled = pltpu.roll(rw, 1, axis=0)
    row = lax.broadcasted_iota(jnp.int32, rw.shape, 0)
    prev = jnp.where(row == 0, first_prev, rolled)
    carry_ref[0:1, :] = rw[tm - 1:tm, :]
    rw_ref[...] = rw + (prev - rw) * mu_ref[...]
    gate_ref[...] = _sigmoid(_dot(hb, w_ref[:, ATT_PROJ + RWKV_PROJ:D_IN])).astype(BF16)


def _in_proj(x2, ln_g, w_in_b, mu, seq):
    n = x2.shape[0]
    tm = 512
    row = lambda i: (i, 0)
    const = lambda i: (0, 0)
    return pl.pallas_call(
        functools.partial(_in_proj_kernel, tiles_per_seq=seq // tm),
        grid=(n // tm,),
        in_specs=[
            pl.BlockSpec((tm, D_MODEL), row),
            pl.BlockSpec((1, D_MODEL), const),
            pl.BlockSpec((D_MODEL, D_IN), const),
            pl.BlockSpec((1, RWKV_PROJ), const),
        ],
        out_specs=[
            pl.BlockSpec((tm, WIDTH), row),
            pl.BlockSpec((tm, WIDTH), row),
            pl.BlockSpec((tm, WIDTH), row),
            pl.BlockSpec((tm, RWKV_PROJ), row),
            pl.BlockSpec((tm, 2 * D_MODEL), row),
        ],
        out_shape=[
            jax.ShapeDtypeStruct((n, WIDTH), BF16),
            jax.ShapeDtypeStruct((n, WIDTH), BF16),
            jax.ShapeDtypeStruct((n, WIDTH), BF16),
            jax.ShapeDtypeStruct((n, RWKV_PROJ), F32),
            jax.ShapeDtypeStruct((n, 2 * D_MODEL), BF16),
        ],
        scratch_shapes=[pltpu.VMEM((8, RWKV_PROJ), F32)],
        compiler_params=pltpu.CompilerParams(
            dimension_semantics=("arbitrary",), vmem_limit_bytes=VMEM_LIMIT),
        name="in_proj",
    )(x2, ln_g, w_in_b, mu)


def _band_attn_kernel(q_ref, k_ref, v_ref, bias_ref, o_ref, kpad_ref, vpad_ref):
    seq = k_ref.shape[1]
    pad = LEFT_CHUNKS * CHUNK
    kpad_ref[0:pad, :] = jnp.zeros((pad, WIDTH), BF16)
    vpad_ref[0:pad, :] = jnp.zeros((pad, WIDTH), BF16)
    kpad_ref[pad:pad + seq, :] = k_ref[0]
    vpad_ref[pad:pad + seq, :] = v_ref[0]

    gw = ATT_GROUP * HEAD_DIM
    rows = ATT_GROUP * CHUNK
    r_head = lax.broadcasted_iota(jnp.int32, (rows, gw), 0) // CHUNK
    l_head = lax.broadcasted_iota(jnp.int32, (rows, gw), 1) // HEAD_DIM
    own = r_head == l_head
    kpos_lo = lax.broadcasted_iota(jnp.int32, (rows, BAND - BIAS_KEYS), 1)
    kpos_hi = lax.broadcasted_iota(jnp.int32, (rows, BIAS_KEYS), 1) + (BAND - BIAS_KEYS)
    neg = jnp.finfo(F32).min
    groups = range(HEADS // ATT_GROUP)
    lanes = [slice(g * gw, (g + 1) * gw) for g in groups]

    def chunk_pair(i, carry, masked):
        units = [(j, g) for j in range(ATT_CHUNKS) for g in groups]
        ids = range(len(units))
        starts = [pl.multiple_of((i * ATT_CHUNKS + j) * CHUNK, CHUNK) for j in range(ATT_CHUNKS)]
        kb = [kpad_ref[pl.ds(st, BAND), :] for st in starts]
        vb = [vpad_ref[pl.ds(st, BAND), :] for st in starts]
        q = [q_ref[0, pl.ds(st, CHUNK), :] * (HEAD_DIM ** -0.5) for st in starts]
        qrows = [jnp.where(own, jnp.concatenate([q[j][:, lanes[g]]] * ATT_GROUP, axis=0),
                           jnp.zeros((), BF16)) for j, g in units]
        s = [_dot_nt(qrows[u], kb[j][:, lanes[g]]) for u, (j, g) in enumerate(units)]
        s_lo = [s[u][:, 0:BAND - BIAS_KEYS] for u in ids]
        s_hi = [s[u][:, BAND - BIAS_KEYS:BAND] + bias_ref[g * rows:(g + 1) * rows, :]
                for u, (j, g) in enumerate(units)]
        if masked:
            first = [(LEFT_CHUNKS - (i * ATT_CHUNKS + j)) * CHUNK for j in range(ATT_CHUNKS)]
            s_lo = [jnp.where(kpos_lo >= first[j], s_lo[u], neg) for u, (j, g) in enumerate(units)]
            s_hi = [jnp.where(kpos_hi >= first[j], s_hi[u], neg) for u, (j, g) in enumerate(units)]
        m = [jnp.maximum(jnp.max(s_lo[u], axis=-1, keepdims=True),
                         jnp.max(s_hi[u], axis=-1, keepdims=True)) for u in ids]
        p_lo = [jnp.exp(s_lo[u] - m[u]) for u in ids]
        p_hi = [jnp.exp(s_hi[u] - m[u]) for u in ids]
        denom = [jnp.sum(p_lo[u], axis=-1, keepdims=True) + jnp.sum(p_hi[u], axis=-1, keepdims=True)
                 for u in ids]
        o_all = [(_dot(p_lo[u].astype(BF16), vb[j][0:BAND - BIAS_KEYS, lanes[g]])
                  + _dot(p_hi[u].astype(BF16), vb[j][BAND - BIAS_KEYS:BAND, lanes[g]])) / denom[u]
                 for u, (j, g) in enumerate(units)]
        for u, (j, g) in enumerate(units):
            o_own = jnp.where(own, o_all[u], 0.0)
            o = o_own[0:CHUNK]
            for h in range(1, ATT_GROUP):
                o = o + o_own[h * CHUNK:(h + 1) * CHUNK]
            o_ref[0, pl.ds(starts[j], CHUNK), lanes[g]] = o.astype(BF16)
        return carry

    n_trips = seq // (CHUNK * ATT_CHUNKS)
    n_masked = min(LEFT_CHUNKS // ATT_CHUNKS, n_trips)
    lax.fori_loop(0, n_masked, functools.partial(chunk_pair, masked=True), 0)
    lax.fori_loop(n_masked, n_trips, functools.partial(chunk_pair, masked=False), 0)


def _band_attn(q, k, v, bias):
    b, seq, _ = q.shape
    whole = pl.BlockSpec((1, seq, WIDTH), lambda i: (i, 0, 0))
    return pl.pallas_call(
        _band_attn_kernel,
        grid=(b,),
        in_specs=[whole, whole, whole, pl.BlockSpec((HEADS * CHUNK, BIAS_KEYS), lambda i: (0, 0))],
        out_specs=whole,
        out_shape=jax.ShapeDtypeStruct((b, seq, WIDTH), BF16),
        scratch_shapes=[pltpu.VMEM((seq + LEFT_CHUNKS * CHUNK, WIDTH), BF16),
                        pltpu.VMEM((seq + LEFT_CHUNKS * CHUNK, WIDTH), BF16)],
        compiler_params=pltpu.CompilerParams(
            dimension_semantics=("arbitrary",), vmem_limit_bytes=VMEM_LIMIT),
        name="band_attn",
    )(q, k, v, bias)


def _rwkv_kernel(rw_ref, w2a2_ref, g2_ref, w0_ref, a0_ref, kk_ref, ka_ref, rk_ref,
                 gng_ref, gnb_ref, y_ref, state_ref):
    c = pl.program_id(1)
    t = CHUNK
    nb = rw_ref.shape[0]
    rows = nb * t

    @pl.when(c == 0)
    def _():
        state_ref[...] = jnp.zeros(state_ref.shape, F32)

    rw = rw_ref[...].reshape(rows, RWKV_PROJ)
    r = rw[:, 0:WIDTH]
    k = rw[:, WIDTH:2 * WIDTH]
    v = rw[:, 2 * WIDTH:3 * WIDTH]
    lora = rw[:, 3 * WIDTH:3 * WIDTH + DECAY_LORA + AAA_LORA]
    g_lo = rw[:, 3 * WIDTH + DECAY_LORA + AAA_LORA:RWKV_PROJ]

    lane128 = lax.broadcasted_iota(jnp.int32, lora.shape, 1)
    lora = jnp.where(lane128 < DECAY_LORA, jnp.tanh(lora), lora)
    wa = _mm(lora, w2a2_ref[...])
    log_decay = -math.exp(-0.5) * _sigmoid(w0_ref[...] + wa[:, 0:WIDTH])
    lr = _sigmoid(a0_ref[...] + wa[:, WIDTH:2 * WIDTH])
    gate = _mm(_sigmoid(g_lo), g2_ref[...])

    kk_raw = k * kk_ref[...]
    k_mod = k * (1.0 + (lr - 1.0) * ka_ref[...])

    row = lax.broadcasted_iota(jnp.int32, (rows, rows), 0)
    col = lax.broadcasted_iota(jnp.int32, (rows, rows), 1)
    tri = jnp.where((row >= col) & ((row // t) == (col // t)), 1.0, 0.0).astype(BF16)
    ld1 = log_decay.astype(BF16)
    rem = log_decay - ld1.astype(F32)
    ld2 = rem.astype(BF16)
    ld3 = (rem - ld2.astype(F32)).astype(BF16)
    parts = _dot(tri, jnp.concatenate([ld1, ld2, ld3], axis=1))
    logp = parts[:, 0:WIDTH] + parts[:, WIDTH:2 * WIDTH] + parts[:, 2 * WIDTH:3 * WIDTH]
    p_in = jnp.exp(logp)
    p_ex = jnp.exp(logp - log_decay)
    p_inv = jnp.exp(-logp)

    kk = kk_raw / jnp.maximum(jnp.sqrt(_head_sums(kk_raw * kk_raw)), 1e-12)
    a_hat = -kk * p_ex
    r_hat = r * p_in
    b_hat = kk * lr * p_inv
    k_hat = k_mod * p_inv
    bonus = _head_sums(r * k_mod * rk_ref[...]) * v

    lane = lax.broadcasted_iota(jnp.int32, (t, 2 * HEAD_DIM), 1)
    low = lane < HEAD_DIM
    r2 = lax.broadcasted_iota(jnp.int32, (2 * t, 2 * HEAD_DIM), 0)
    c2 = lax.broadcasted_iota(jnp.int32, (2 * t, 2 * HEAD_DIM), 1)
    own = (r2 < t) == (c2 < HEAD_DIM)
    strict = (r2 & (t - 1)) > (c2 & (t - 1))
    incl = (r2 & (t - 1)) >= (c2 & (t - 1))
    eye = jnp.where(r2 == c2, 1.0, 0.0).astype(F32)

    def stack2(xp):
        return jnp.concatenate([jnp.where(low, xp, 0.0), jnp.where(low, 0.0, xp)], axis=0)

    chains = [(b, p) for b in range(nb) for p in range(HEADS // 2)]
    ids = range(len(chains))
    rs = [slice(b * t, (b + 1) * t) for b, _ in chains]
    ls = [slice(2 * HEAD_DIM * p, 2 * HEAD_DIM * (p + 1)) for _, p in chains]
    p_end = [p_in[(b + 1) * t - 1:(b + 1) * t, ls[i]] for i, (b, _) in enumerate(chains)]
    ar = [jnp.concatenate([stack2(a_hat[rs[i], ls[i]]), stack2(r_hat[rs[i], ls[i]])], axis=0).astype(BF16)
          for i in ids]
    bk2 = [(stack2(b_hat[rs[i], ls[i]]), stack2(k_hat[rs[i], ls[i]])) for i in ids]
    bk = [jnp.concatenate(bk2[i], axis=0).astype(BF16) for i in ids]
    btkt = [(jnp.concatenate(bk2[i], axis=0) * p_end[i]).astype(BF16) for i in ids]
    v2 = [stack2(v[rs[i], ls[i]]).astype(BF16) for i in ids]
    g = [_dot_nt(ar[i], bk[i]) for i in ids]
    st = [state_ref[b, p] for b, p in chains]
    ars = [_dot_nt(ar[i], st[i].astype(BF16)) for i in ids]
    l_ab = [jnp.where(strict, g[i][0:2 * t, 0:2 * t], 0.0) for i in ids]
    lm = [jnp.concatenate([jnp.where(strict, g[i][0:2 * t, 2 * t:4 * t], 0.0),
                           jnp.where(incl, g[i][2 * t:4 * t, 2 * t:4 * t], 0.0)], axis=0)
          for i in ids]
    m_rb = [jnp.where(incl, g[i][2 * t:4 * t, 0:2 * t], 0.0).astype(BF16) for i in ids]
    lv = [_mm(lm[i], v2[i]) for i in ids]

    w_inv = [eye + l_ab[i] for i in ids]
    l_pow = l_ab
    for _ in range(int(math.log2(t)) - 1):
        l_pow = [_mm(l_pow[i], l_pow[i]) for i in ids]
        w_inv = [w_inv[i] + _mm(w_inv[i], l_pow[i]) for i in ids]

    z = [_mm(w_inv[i], ars[i][0:2 * t] + lv[i][0:2 * t]) for i in ids]
    y = [ars[i][2 * t:4 * t] + lv[i][2 * t:4 * t] + _mm(m_rb[i], z[i]) for i in ids]
    for i, (b, p) in enumerate(chains):
        zv = jnp.concatenate([z[i].astype(BF16), v2[i]], axis=0)
        state_ref[b, p] = st[i] * p_end[i] + _dot_tn(zv, btkt[i])

    for i, (b, p) in enumerate(chains):
        mean = jnp.sum(y[i], axis=-1, keepdims=True) * (1.0 / HEAD_DIM)
        dev = jnp.where(own, y[i] - mean, 0.0)
        var = jnp.sum(dev * dev, axis=-1, keepdims=True) * (1.0 / HEAD_DIM)
        yn = dev * lax.rsqrt(var + GN_EPS)
        yn = yn[0:t] + yn[t:2 * t]
        out = (yn * gng_ref[:, ls[i]] + gnb_ref[:, ls[i]] + bonus[rs[i], ls[i]]) * gate[rs[i], ls[i]]
        y_ref[b, :, ls[i]] = out.astype(BF16)


def _rwkv(rw, w2a2, g2, w0, a0, k_k, k_a, r_k, gn_g, gn_b):
    b, seq, _ = rw.shape
    nc = seq // CHUNK
    nb = RWKV_SEQS
    const = lambda i, c: (0, 0)
    vec = pl.BlockSpec((1, WIDTH), const)
    return pl.pallas_call(
        _rwkv_kernel,
        grid=(b // nb, nc),
        in_specs=[
            pl.BlockSpec((nb, CHUNK, RWKV_PROJ), lambda i, c: (i, c, 0)),
            pl.BlockSpec((DECAY_LORA + AAA_LORA, 2 * WIDTH), const),
            pl.BlockSpec((GATE_LORA, WIDTH), const),
            vec, vec, vec, vec, vec, vec, vec,
        ],
        out_specs=pl.BlockSpec((nb, CHUNK, WIDTH), lambda i, c: (i, c, 0)),
        out_shape=jax.ShapeDtypeStruct((b, seq, WIDTH), BF16),
        scratch_shapes=[pltpu.VMEM((nb, HEADS // 2, 2 * HEAD_DIM, 2 * HEAD_DIM), F32)],
        compiler_params=pltpu.CompilerParams(
            dimension_semantics=("arbitrary", "arbitrary"), vmem_limit_bytes=VMEM_LIMIT),
        name="rwkv7",
    )(rw, w2a2, g2, w0, a0, k_k, k_a, r_k, gn_g, gn_b)


def _merge_kernel(x_ref, att_ref, rwkv_ref, gate_ref, wa32_ref, wb32_ref, wo32_ref, g_ref,
                  wr_ref, br_ref, x1_ref, h2_ref, comb_ref, cnt_ref, wa_ref, wb_ref, wo_ref):
    @pl.when(pl.program_id(0) == 0)
    def _():
        wa_ref[...] = wa32_ref[...].astype(BF16)
        wb_ref[...] = wb32_ref[...].astype(BF16)
        wo_ref[...] = wo32_ref[...].astype(BF16)

    ga = gate_ref[:, 0:D_MODEL].astype(F32)
    gb = gate_ref[:, D_MODEL:2 * D_MODEL].astype(F32)
    merged = ga * _dot(att_ref[...], wa_ref[...]) + gb * _dot(rwkv_ref[...], wb_ref[...])
    x1 = x_ref[...] + _dot(merged.astype(BF16), wo_ref[...])
    x1_ref[...] = x1
    h2 = x1 * lax.rsqrt(jnp.mean(x1 * x1, axis=-1, keepdims=True) + RMS_EPS) * g_ref[...]
    h2_hi = h2.astype(BF16)
    h2_ref[...] = _pack_bf16(h2)
    h2_lo = (h2 - h2_hi.astype(F32)).astype(BF16)
    hw = _dot(h2_hi, wr_ref[...])
    lw = _dot(h2_lo, wr_ref[:, 0:ROUTER_LANES])
    logits = hw[:, 0:ROUTER_LANES] + (hw[:, ROUTER_LANES:2 * ROUTER_LANES] + lw) + br_ref[...]
    lane_i = lax.broadcasted_iota(jnp.int32, logits.shape, 1)
    lane = lane_i.astype(F32)
    lane_group = (lane_i >> 3).astype(F32)
    neg = jnp.finfo(F32).min
    big = float(ROUTER_LANES)

    def first_argmax(vals, mask):
        vm = jnp.where(mask, vals, neg)
        mx = jnp.max(vm, axis=-1, keepdims=True)
        idx = jnp.min(jnp.where(vm == mx, jnp.where(mask, lane, big), big), axis=-1, keepdims=True)
        return mx, idx

    is_group = (lane_i >= GROUP_LANE0) & (lane_i < GROUP_LANE0 + N_GROUPS)
    g_max, g_lane = first_argmax(logits, is_group)
    g_prob = 1.0 / jnp.sum(jnp.where(is_group, jnp.exp(logits - g_max), 0.0),
                           axis=-1, keepdims=True)
    g_idx = g_lane - GROUP_LANE0
    in_group = lane_group == g_idx
    e1, i1 = first_argmax(logits, in_group)
    e2, i2 = first_argmax(logits, in_group & (lane != i1))
    w2 = jnp.exp(e2 - e1)
    p1 = 1.0 / (1.0 + w2)
    p2 = w2 / (1.0 + w2)
    rec = jnp.where(lane_i == EXPERT1_LANE, i1, jnp.where(lane_i == EXPERT2_LANE, i2, 0.0))
    rec = jnp.where(lane_i == WEIGHT1_LANE, p1 * g_prob, jnp.where(lane_i == WEIGHT2_LANE, p2 * g_prob, rec))
    comb_ref[...] = rec
    counts = jnp.sum(jnp.where(lane == i1, 1.0, 0.0) + jnp.where(lane == i2, 1.0, 0.0),
                     axis=0, keepdims=True)
    cnt_ref[0] = jnp.broadcast_to(counts, (8, ROUTER_LANES)).astype(jnp.int32)


def _merge(x2, att, rwkv, gates, wa, wb, wo, ln_g, wr, br):
    n = x2.shape[0]
    tm = MERGE_TILE
    row = lambda i: (i, 0)
    const = lambda i: (0, 0)
    return pl.pallas_call(
        _merge_kernel,
        grid=(n // tm,),
        in_specs=[
            pl.BlockSpec((tm, D_MODEL), row),
            pl.BlockSpec((tm, WIDTH), row),
            pl.BlockSpec((tm, WIDTH), row),
            pl.BlockSpec((tm, 2 * D_MODEL), row),
            pl.BlockSpec((WIDTH, D_MODEL), const),
            pl.BlockSpec((WIDTH, D_MODEL), const),
            pl.BlockSpec((D_MODEL, D_MODEL), const),
            pl.BlockSpec((1, D_MODEL), const),
            pl.BlockSpec((D_MODEL, 2 * ROUTER_LANES), const),
            pl.BlockSpec((1, ROUTER_LANES), const),
        ],
        out_specs=[
            pl.BlockSpec((tm, D_MODEL), row),
            pl.BlockSpec((tm, PACKED), row),
            pl.BlockSpec((tm, ROUTER_LANES), row),
            pl.BlockSpec((1, 8, ROUTER_LANES), lambda i: (i, 0, 0)),
        ],
        out_shape=[
            jax.ShapeDtypeStruct((n, D_MODEL), F32),
            jax.ShapeDtypeStruct((n, PACKED), jnp.uint32),
            jax.ShapeDtypeStruct((n, ROUTER_LANES), F32),
            jax.ShapeDtypeStruct((n // tm, 8, ROUTER_LANES), jnp.int32),
        ],
        scratch_shapes=[pltpu.VMEM((WIDTH, D_MODEL), BF16), pltpu.VMEM((WIDTH, D_MODEL), BF16),
                        pltpu.VMEM((D_MODEL, D_MODEL), BF16)],
        compiler_params=pltpu.CompilerParams(
            dimension_semantics=("arbitrary",), vmem_limit_bytes=VMEM_LIMIT),
        name="merge",
    )(x2, att, rwkv, gates, wa, wb, wo, ln_g, wr, br)


def _moe_plan_kernel(cnt_ref, base_ref, blk_ref):
    nt = cnt_ref.shape[0]
    cnt = cnt_ref[...].astype(F32)
    lane = lax.broadcasted_iota(jnp.int32, (8, ROUTER_LANES), 1)
    total = jnp.broadcast_to(jnp.sum(cnt, axis=0, keepdims=True), (8, ROUTER_LANES))
    padded = jnp.floor((total + (MOE_BLOCK - 1)) * (1.0 / MOE_BLOCK)) * MOE_BLOCK
    r = lax.broadcasted_iota(jnp.int32, (ROUTER_LANES, ROUTER_LANES), 0)
    c = lax.broadcasted_iota(jnp.int32, (ROUTER_LANES, ROUTER_LANES), 1)
    seg_start = _dot_hi(padded, jnp.where(r < c, 1.0, 0.0).astype(F32))
    tr = lax.broadcasted_iota(jnp.int32, (nt, nt), 0)
    tc = lax.broadcasted_iota(jnp.int32, (nt, nt), 1)
    tile_off = _dot_hi(jnp.where(tc < tr, 1.0, 0.0).astype(F32), cnt)
    base_ref[...] = seg_start[0:1, :] + tile_off
    seg_end = (seg_start + padded)[0:1, :]
    rows_total = jnp.sum(jnp.where(lane[0:1, :] < N_EXPERTS, padded[0:1, :], 0.0), axis=-1, keepdims=True)
    nblk = blk_ref.shape[0]
    blk_row = lax.broadcasted_iota(jnp.int32, (nblk, ROUTER_LANES), 0).astype(F32) * MOE_BLOCK
    blk_lane = lax.broadcasted_iota(jnp.int32, (nblk, ROUTER_LANES), 1)
    done = jnp.where((seg_end <= blk_row) & (blk_lane < N_EXPERTS), 1.0, 0.0)
    expert = jnp.minimum(jnp.sum(done, axis=-1, keepdims=True), N_EXPERTS - 1.0)
    valid = jnp.where(blk_row < rows_total, 1.0, 0.0)
    blk_ref[...] = jnp.where(blk_lane == 0, expert, jnp.where(blk_lane == 1, valid, 0.0)).astype(jnp.int32)


def _moe_plan(cnt, n_blocks):
    nt = cnt.shape[0]
    return pl.pallas_call(
        _moe_plan_kernel,
        out_shape=[jax.ShapeDtypeStruct((nt, ROUTER_LANES), F32),
                   jax.ShapeDtypeStruct((n_blocks, ROUTER_LANES), jnp.int32)],
        name="moe_plan",
    )(cnt)


def _moe_pos_kernel(comb_ref, base_ref, pos_ref):
    t = comb_ref.shape[0]
    comb = comb_ref[...]
    lane_i = lax.broadcasted_iota(jnp.int32, (t, ROUTER_LANES), 1)
    lane = lane_i.astype(F32)
    pick1 = jnp.where(lane == comb[:, EXPERT1_LANE:EXPERT1_LANE + 1], 1.0, 0.0)
    pick2 = jnp.where(lane == comb[:, EXPERT2_LANE:EXPERT2_LANE + 1], 1.0, 0.0)
    rows = lax.broadcasted_iota(jnp.int32, (t, t), 0)
    cols = lax.broadcasted_iota(jnp.int32, (t, t), 1)
    earlier = jnp.where(cols < rows, 1.0, 0.0).astype(BF16)
    before1 = _dot(earlier, pick1.astype(BF16))
    before2 = _dot(earlier, pick2.astype(BF16))
    base = base_ref[0]
    firsts = jnp.sum(pick1, axis=0, keepdims=True)
    pos1 = jnp.sum(pick1 * (base + before1), axis=-1, keepdims=True)
    pos2 = jnp.sum(pick2 * (base + firsts + before2), axis=-1, keepdims=True)
    both = jnp.where(lane_i == 0, pos1, jnp.where(lane_i == 1, pos2, 0.0))
    pos_ref[...] = jnp.transpose(both)[0:8, :].astype(jnp.int32)


def _moe_pos(comb, base):
    n = comb.shape[0]
    t = MERGE_TILE
    return pl.pallas_call(
        _moe_pos_kernel,
        grid=(n // t,),
        in_specs=[pl.BlockSpec((t, ROUTER_LANES), lambda i: (i, 0)),
                  pl.BlockSpec((1, 1, ROUTER_LANES), lambda i: (i, 0, 0))],
        out_specs=pl.BlockSpec((8, t), lambda i: (0, i)),
        out_shape=jax.ShapeDtypeStruct((8, n), jnp.int32),
        compiler_params=pltpu.CompilerParams(dimension_semantics=("arbitrary",)),
        name="moe_pos",
    )(comb, base.reshape(n // t, 1, ROUTER_LANES))


def _sc_scatter_rows(h, pos, n_rows):
    n = h.shape[0]
    per_worker = n // (SC_WINDOW * SC_WORKERS)
    mesh = plsc.VectorSubcoreMesh(core_axis_name="core", subcore_axis_name="subcore")

    @pl.kernel(out_type=jax.ShapeDtypeStruct((n_rows, PACKED), jnp.uint32), mesh=mesh,
               scratch_types=[pltpu.VMEM((8, SC_WINDOW), jnp.int32),
                              pltpu.VMEM((SC_WINDOW, PACKED), jnp.uint32)])
    def scatter(h_hbm, pos_hbm, out_hbm, idx, buf):
        worker = lax.axis_index("core") * (SC_WORKERS // 2) + lax.axis_index("subcore")

        @pl.loop(0, per_worker)
        def _(b):
            start = (worker * per_worker + b) * SC_WINDOW
            pltpu.sync_copy(pos_hbm.at[:, pl.ds(start, SC_WINDOW)], idx)
            pltpu.sync_copy(h_hbm.at[pl.ds(start, SC_WINDOW)], buf)
            pltpu.sync_copy(buf, out_hbm.at[idx.at[0]])
            pltpu.sync_copy(buf, out_hbm.at[idx.at[1]])

    return scatter(h, pos)


def _sc_gather_rows(y, pos):
    n = pos.shape[1]
    per_worker = n // (SC_WINDOW * SC_WORKERS)
    mesh = plsc.VectorSubcoreMesh(core_axis_name="core", subcore_axis_name="subcore")
    out = jax.ShapeDtypeStruct((n, PACKED), jnp.uint32)

    @pl.kernel(out_type=(out, out), mesh=mesh,
               scratch_types=[pltpu.VMEM((8, SC_WINDOW), jnp.int32),
                              pltpu.VMEM((SC_WINDOW, PACKED), jnp.uint32)])
    def gather(y_hbm, pos_hbm, o1_hbm, o2_hbm, idx, buf):
        worker = lax.axis_index("core") * (SC_WORKERS // 2) + lax.axis_index("subcore")

        @pl.loop(0, per_worker)
        def _(b):
            start = (worker * per_worker + b) * SC_WINDOW
            pltpu.sync_copy(pos_hbm.at[:, pl.ds(start, SC_WINDOW)], idx)
            pltpu.sync_copy(y_hbm.at[idx.at[0]], buf)
            pltpu.sync_copy(buf, o1_hbm.at[pl.ds(start, SC_WINDOW)])
            pltpu.sync_copy(y_hbm.at[idx.at[1]], buf)
            pltpu.sync_copy(buf, o2_hbm.at[pl.ds(start, SC_WINDOW)])

    return gather(y, pos)


def _expert_mlp_kernel(expert_ref, valid_ref, hs_ref, *refs):
    step = pl.program_id(0)
    w_f32 = refs[0:3 * MLP_SUB]
    y_ref = refs[3 * MLP_SUB]
    w_bf16 = refs[3 * MLP_SUB + 1:]
    first = step * MLP_SUB

    @pl.when(valid_ref[first] == 0)
    def _():
        y_ref[...] = jnp.zeros(y_ref.shape, jnp.uint32)

    @pl.when(valid_ref[first] != 0)
    def _():
        for j in range(MLP_SUB):
            prev = jnp.maximum(first + j - MLP_SUB, 0)
            stale = (step == 0) | (expert_ref[first + j] != expert_ref[prev])

            @pl.when(stale)
            def _():
                for k in range(3):
                    w_bf16[3 * j + k][...] = w_f32[3 * j + k][0].astype(BF16)

        half = D_MODEL // 2
        subs = range(MLP_SUB)
        parts = [_unpack_bf16(hs_ref[j * MOE_BLOCK:(j + 1) * MOE_BLOCK, :]) for j in subs]
        h_lo = [parts[j][0].astype(BF16) for j in subs]
        h_hi = [parts[j][1].astype(BF16) for j in subs]
        hg = [_dot(h_lo[j], w_bf16[3 * j][0:half, :]) + _dot(h_hi[j], w_bf16[3 * j][half:D_MODEL, :])
              for j in subs]
        hu = [_dot(h_lo[j], w_bf16[3 * j + 1][0:half, :]) + _dot(h_hi[j], w_bf16[3 * j + 1][half:D_MODEL, :])
              for j in subs]
        act = [(hg[j] * _sigmoid(hg[j]) * hu[j]).astype(BF16) for j in subs]
        for j in subs:
            y_ref[j * MOE_BLOCK:(j + 1) * MOE_BLOCK, :] = _pack_bf16(_dot(act[j], w_bf16[3 * j + 2][...]))


def _expert_mlp(blk_expert, blk_valid, hs, wg, wu, wd):
    n_steps = hs.shape[0] // (MOE_BLOCK * MLP_SUB)
    rows = lambda s, expert, valid: (s, 0)
    w_specs, w_args, w_scratch = [], [], []
    for j in range(MLP_SUB):
        by_expert = lambda s, expert, valid, j=j: (expert[s * MLP_SUB + j], 0, 0)
        w_specs += [pl.BlockSpec((1, D_MODEL, D_EXPERT), by_expert),
                    pl.BlockSpec((1, D_MODEL, D_EXPERT), by_expert),
                    pl.BlockSpec((1, D_EXPERT, D_MODEL), by_expert)]
        w_args += [wg, wu, wd]
        w_scratch += [pltpu.VMEM((D_MODEL, D_EXPERT), BF16), pltpu.VMEM((D_MODEL, D_EXPERT), BF16),
                      pltpu.VMEM((D_EXPERT, D_MODEL), BF16)]
    grid_spec = pltpu.PrefetchScalarGridSpec(
        num_scalar_prefetch=2,
        grid=(n_steps,),
        in_specs=[pl.BlockSpec((MOE_BLOCK * MLP_SUB, PACKED), rows)] + w_specs,
        out_specs=pl.BlockSpec((MOE_BLOCK * MLP_SUB, PACKED), rows),
        scratch_shapes=w_scratch,
    )
    return pl.pallas_call(
        _expert_mlp_kernel,
        grid_spec=grid_spec,
        out_shape=jax.ShapeDtypeStruct((hs.shape[0], PACKED), jnp.uint32),
        compiler_params=pltpu.CompilerParams(
            dimension_semantics=("arbitrary",), vmem_limit_bytes=VMEM_LIMIT),
        name="expert_mlp",
    )(blk_expert, blk_valid, hs, *w_args)


def _moe_out_kernel(x1_ref, y1_ref, y2_ref, comb_ref, lnf_ref, out_ref, *, final_norm):
    w1 = comb_ref[:, WEIGHT1_LANE:WEIGHT1_LANE + 1]
    w2 = comb_ref[:, WEIGHT2_LANE:WEIGHT2_LANE + 1]
    a_lo, a_hi = _unpack_bf16(y1_ref[...])
    b_lo, b_hi = _unpack_bf16(y2_ref[...])
    moe = jnp.concatenate([w1 * a_lo + w2 * b_lo, w1 * a_hi + w2 * b_hi], axis=1)
    y = x1_ref[...] + moe
    if final_norm:
        y = y * lax.rsqrt(jnp.mean(y * y, axis=-1, keepdims=True) + RMS_EPS) * lnf_ref[...]
    out_ref[...] = y


def _moe_out(x1, y1, y2, comb, lnf, final_norm):
    n = x1.shape[0]
    t = FINAL_TILE
    row = lambda i: (i, 0)
    return pl.pallas_call(
        functools.partial(_moe_out_kernel, final_norm=final_norm),
        grid=(n // t,),
        in_specs=[pl.BlockSpec((t, D_MODEL), row), pl.BlockSpec((t, PACKED), row),
                  pl.BlockSpec((t, PACKED), row), pl.BlockSpec((t, ROUTER_LANES), row),
                  pl.BlockSpec((1, D_MODEL), lambda i: (0, 0))],
        out_specs=pl.BlockSpec((t, D_MODEL), row),
        out_shape=jax.ShapeDtypeStruct((n, D_MODEL), F32),
        compiler_params=pltpu.CompilerParams(
            dimension_semantics=("arbitrary",), vmem_limit_bytes=VMEM_LIMIT),
        name="moe_out",
    )(x1, y1, y2, comb, lnf)


def _moe(counts, x1, h2p, comb, wg, wu, wd, lnf, final_norm):
    n = x1.shape[0]
    n_blocks = (2 * n) // MOE_BLOCK + N_EXPERTS
    base, blk = _moe_plan(counts[:, 0, :], n_blocks)
    pos = _moe_pos(comb, base)
    hs = _sc_scatter_rows(h2p, pos, n_blocks * MOE_BLOCK)
    ys = _expert_mlp(blk[:, 0], blk[:, 1], hs, wg, wu, wd)
    y1, y2 = _sc_gather_rows(ys, pos)
    return _moe_out(x1, y1, y2, comb, lnf, final_norm)


def kernel(x, ln_mix_g, w_in, att_rel_bias, rwkv_mu, rwkv_w0, rwkv_w2, rwkv_a0, rwkv_a2, rwkv_g2,
           rwkv_k_k, rwkv_k_a, rwkv_r_k, rwkv_gn_g, rwkv_gn_b, w_branch_att, w_branch_rwkv, w_out,
           ln_ffn_g, router_group_w, router_group_b, router_expert_w, router_expert_b,
           expert_w_gate, expert_w_up, expert_w_down, ln_final_g):
    bsz, seq, d = x.shape
    depth = w_in.shape[0]
    n = bsz * seq
    x2 = x.reshape(n, d)
    for l in range(depth):
        q, k, v, rw, gates = _in_proj(x2, ln_mix_g[l][None, :], w_in[l].astype(BF16),
                                      rwkv_mu[l][None, :], seq)
        bias = _rel_bias(att_rel_bias[l])
        att = _band_attn(q.reshape(bsz, seq, WIDTH), k.reshape(bsz, seq, WIDTH),
                         v.reshape(bsz, seq, WIDTH), bias)
        zeros = jnp.zeros((DECAY_LORA, WIDTH), F32)
        w2a2 = jnp.concatenate(
            [jnp.concatenate([rwkv_w2[l], zeros], axis=1),
             jnp.concatenate([zeros, rwkv_a2[l]], axis=1)], axis=0)
        rwkv = _rwkv(rw.reshape(bsz, seq, RWKV_PROJ), w2a2.astype(BF16), rwkv_g2[l].astype(BF16),
                     rwkv_w0[l][None, :], rwkv_a0[l][None, :], rwkv_k_k[l][None, :],
                     rwkv_k_a[l][None, :], rwkv_r_k[l].reshape(1, WIDTH),
                     rwkv_gn_g[l][None, :], rwkv_gn_b[l][None, :])
        wr = jnp.concatenate([router_expert_w[l], router_group_w[l]], axis=1)
        wr = jnp.pad(wr, ((0, 0), (0, ROUTER_LANES - wr.shape[1])))
        wr_hi = wr.astype(BF16)
        wr = jnp.concatenate([wr_hi, (wr - wr_hi.astype(F32)).astype(BF16)], axis=1)
        br = jnp.concatenate([router_expert_b[l], router_group_b[l]])
        br = jnp.pad(br, (0, ROUTER_LANES - br.shape[0]))[None, :]
        x1, h2, comb, counts = _merge(x2, att.reshape(n, WIDTH), rwkv.reshape(n, WIDTH), gates,
                                      w_branch_att[l], w_branch_rwkv[l], w_out[l],
                                      ln_ffn_g[l][None, :], wr, br)
        x2 = _moe(counts, x1, h2, comb, expert_w_gate[l], expert_w_up[l], expert_w_down[l],
                  ln_final_g[None, :], final_norm=(l == depth - 1))
    return x2.reshape(bsz, seq, d)
```

```python
import functools
import math

import jax
import jax.numpy as jnp
from jax import lax
from jax.experimental import pallas as pl
from jax.experimental.pallas import tpu as pltpu
from jax.experimental.pallas import tpu_sc as plsc

F32 = jnp.float32
BF16 = jnp.bfloat16
HIGHEST = lax.Precision.HIGHEST

D_MODEL = 1024
CHUNK = 64
HEADS = 8
HEAD_DIM = 64
WIDTH = HEADS * HEAD_DIM
LEFT_CHUNKS = 8
BAND = (LEFT_CHUNKS + 1) * CHUNK
REL_CLIP = 64
N_REL = 2 * REL_CLIP + 1
DECAY_LORA = 64
AAA_LORA = 64
GATE_LORA = 128
GN_EPS = 64e-5
RMS_EPS = 1e-6
ATT_PROJ = 3 * WIDTH
RWKV_PROJ = 3 * WIDTH + DECAY_LORA + AAA_LORA + GATE_LORA
D_IN = ATT_PROJ + RWKV_PROJ + 2 * D_MODEL
N_GROUPS = 4
EXPERTS_PER_GROUP = 8
N_EXPERTS = N_GROUPS * EXPERTS_PER_GROUP
D_EXPERT = 256
GROUP_FF = EXPERTS_PER_GROUP * D_EXPERT
RWKV_SEQS = 4
BIAS_KEYS = 192
ATT_CHUNKS = 4
ATT_GROUP = 4
ROUTER_LANES = 128
GROUP_LANE0 = N_EXPERTS
EXPERT1_LANE, EXPERT2_LANE, WEIGHT1_LANE, WEIGHT2_LANE = 126, 125, 124, 123
MERGE_TILE = 512
MOE_BLOCK = 256
PACKED = D_MODEL // 2
SC_WINDOW = 128
SC_WORKERS = 32
FINAL_TILE = 1024
MLP_SUB = 2
MOE_PARTS = 2

VMEM_LIMIT = 52 * 1024 * 1024


def _dot(a, b):
    return jnp.dot(a, b, preferred_element_type=F32)


def _dot_hi(a, b):
    return jnp.dot(a, b, preferred_element_type=F32, precision=HIGHEST)


def _dot_nt(a, b, precision=None):
    return lax.dot_general(a, b, (((1,), (1,)), ((), ())),
                           preferred_element_type=F32, precision=precision)


def _dot_tn(a, b, precision=None):
    return lax.dot_general(a, b, (((0,), (0,)), ((), ())),
                           preferred_element_type=F32, precision=precision)


def _sigmoid(x):
    return 1.0 / (1.0 + jnp.exp(-x))


def _pack_bf16(x):
    w = x.shape[1] // 2
    hi = pltpu.bitcast(x[:, :w].astype(BF16).astype(F32), jnp.uint32)
    lo = pltpu.bitcast(x[:, w:].astype(BF16).astype(F32), jnp.uint32)
    return hi | lax.shift_right_logical(lo, jnp.uint32(16))


def _unpack_bf16(p):
    hi = pltpu.bitcast(p & jnp.uint32(0xFFFF0000), F32)
    lo = pltpu.bitcast(lax.shift_left(p, jnp.uint32(16)), F32)
    return hi, lo


def _mm(a, b):
    return jnp.dot(a.astype(BF16), b.astype(BF16), preferred_element_type=F32)


def _head_sums(x):
    outs = []
    lane = lax.broadcasted_iota(jnp.int32, (x.shape[0], 2 * HEAD_DIM), 1)
    low = lane < HEAD_DIM
    for p in range(HEADS // 2):
        xp = x[:, 2 * HEAD_DIM * p:2 * HEAD_DIM * (p + 1)]
        s_lo = jnp.sum(jnp.where(low, xp, 0.0), axis=-1, keepdims=True)
        s_hi = jnp.sum(jnp.where(low, 0.0, xp), axis=-1, keepdims=True)
        outs.append(jnp.where(low, s_lo, s_hi))
    return jnp.concatenate(outs, axis=-1)


def _rel_bias_kernel(tab_ref, out_ref):
    rows = tab_ref.shape[1]
    n = lax.broadcasted_iota(jnp.int32, (rows, CHUNK * 128), 1)
    r = lax.broadcasted_iota(jnp.int32, (rows, CHUNK * 128), 0)
    q = n >> 7
    kk = n & 127
    idx = jnp.clip(CHUNK + q - kk, -REL_CLIP, REL_CLIP) + REL_CLIP
    pick = jnp.where(r == idx, 1.0, 0.0) - jnp.where(r == N_REL - 1, 1.0, 0.0)
    out_ref[...] = _dot_hi(tab_ref[...], pick.astype(F32))


def _rel_bias(rel_table):
    rows = 136
    tab = jnp.pad(rel_table.astype(F32), ((0, 0), (0, rows - N_REL)))
    tail = pl.pallas_call(
        _rel_bias_kernel,
        out_shape=jax.ShapeDtypeStruct((HEADS, CHUNK * 128), F32),
        name="rel_bias",
    )(tab)
    tail = tail.reshape(HEADS, CHUNK, 128)
    bias = jnp.concatenate([jnp.zeros((HEADS, CHUNK, BIAS_KEYS - 128), F32), tail], axis=-1)
    return bias.reshape(HEADS * CHUNK, BIAS_KEYS)


def _in_proj_kernel(x_ref, g_ref, w_ref, mu_ref, q_ref, k_ref, v_ref, rw_ref, gate_ref,
                    carry_ref, *, tiles_per_seq):
    i = pl.program_id(0)

    @pl.when(i == 0)
    def _():
        carry_ref[...] = jnp.zeros(carry_ref.shape, F32)

    x = x_ref[...]
    h = x * lax.rsqrt(jnp.mean(x * x, axis=-1, keepdims=True) + RMS_EPS) * g_ref[...]
    hb = h.astype(BF16)
    q_ref[...] = _dot(hb, w_ref[:, 0:WIDTH]).astype(BF16)
    k_ref[...] = _dot(hb, w_ref[:, WIDTH:2 * WIDTH]).astype(BF16)
    v_ref[...] = _dot(hb, w_ref[:, 2 * WIDTH:ATT_PROJ]).astype(BF16)
    rw = _dot(hb, w_ref[:, ATT_PROJ:ATT_PROJ + RWKV_PROJ])
    tm = rw.shape[0]
    first_prev = jnp.where(i % tiles_per_seq == 0, 0.0, carry_ref[0:1, :])
    rolled = pltpu.roll(rw, 1, axis=0)
    row = lax.broadcasted_iota(jnp.int32, rw.shape, 0)
    prev = jnp.where(row == 0, first_prev, rolled)
    carry_ref[0:1, :] = rw[tm - 1:tm, :]
    rw_ref[...] = rw + (prev - rw) * mu_ref[...]
    gate_ref[...] = _sigmoid(_dot(hb, w_ref[:, ATT_PROJ + RWKV_PROJ:D_IN])).astype(BF16)


def _in_proj(x2, ln_g, w_in_b, mu, seq):
    n = x2.shape[0]
    tm = 512
    row = lambda i: (i, 0)
    const = lambda i: (0, 0)
    return pl.pallas_call(
        functools.partial(_in_proj_kernel, tiles_per_seq=seq // tm),
        grid=(n // tm,),
        in_specs=[
            pl.BlockSpec((tm, D_MODEL), row),
            pl.BlockSpec((1, D_MODEL), const),
            pl.BlockSpec((D_MODEL, D_IN), const),
            pl.BlockSpec((1, RWKV_PROJ), const),
        ],
        out_specs=[
            pl.BlockSpec((tm, WIDTH), row),
            pl.BlockSpec((tm, WIDTH), row),
            pl.BlockSpec((tm, WIDTH), row),
            pl.BlockSpec((tm, RWKV_PROJ), row),
            pl.BlockSpec((tm, 2 * D_MODEL), row),
        ],
        out_shape=[
            jax.ShapeDtypeStruct((n, WIDTH), BF16),
            jax.ShapeDtypeStruct((n, WIDTH), BF16),
            jax.ShapeDtypeStruct((n, WIDTH), BF16),
            jax.ShapeDtypeStruct((n, RWKV_PROJ), F32),
            jax.ShapeDtypeStruct((n, 2 * D_MODEL), BF16),
        ],
        scratch_shapes=[pltpu.VMEM((8, RWKV_PROJ), F32)],
        compiler_params=pltpu.CompilerParams(
            dimension_semantics=("arbitrary",), vmem_limit_bytes=VMEM_LIMIT),
        name="in_proj",
    )(x2, ln_g, w_in_b, mu)


def _band_attn_kernel(q_ref, k_ref, v_ref, bias_ref, o_ref, kpad_ref, vpad_ref):
    seq = k_ref.shape[1]
    pad = LEFT_CHUNKS * CHUNK
    kpad_ref[0:pad, :] = jnp.zeros((pad, WIDTH), BF16)
    vpad_ref[0:pad, :] = jnp.zeros((pad, WIDTH), BF16)
    kpad_ref[pad:pad + seq, :] = k_ref[0]
    vpad_ref[pad:pad + seq, :] = v_ref[0]

    gw = ATT_GROUP * HEAD_DIM
    rows = ATT_GROUP * CHUNK
    r_head = lax.broadcasted_iota(jnp.int32, (rows, gw), 0) // CHUNK
    l_head = lax.broadcasted_iota(jnp.int32, (rows, gw), 1) // HEAD_DIM
    own = r_head == l_head
    kpos_lo = lax.broadcasted_iota(jnp.int32, (rows, BAND - BIAS_KEYS), 1)
    kpos_hi = lax.broadcasted_iota(jnp.int32, (rows, BIAS_KEYS), 1) + (BAND - BIAS_KEYS)
    neg = jnp.finfo(F32).min
    groups = range(HEADS // ATT_GROUP)
    lanes = [slice(g * gw, (g + 1) * gw) for g in groups]

    def chunk_pair(i, carry, masked):
        units = [(j, g) for j in range(ATT_CHUNKS) for g in groups]
        ids = range(len(units))
        starts = [pl.multiple_of((i * ATT_CHUNKS + j) * CHUNK, CHUNK) for j in range(ATT_CHUNKS)]
        kb = [kpad_ref[pl.ds(st, BAND), :] for st in starts]
        vb = [vpad_ref[pl.ds(st, BAND), :] for st in starts]
        q = [q_ref[0, pl.ds(st, CHUNK), :] * (HEAD_DIM ** -0.5) for st in starts]
        qrows = [jnp.where(own, jnp.concatenate([q[j][:, lanes[g]]] * ATT_GROUP, axis=0),
                           jnp.zeros((), BF16)) for j, g in units]
        s = [_dot_nt(qrows[u], kb[j][:, lanes[g]]) for u, (j, g) in enumerate(units)]
        s_lo = [s[u][:, 0:BAND - BIAS_KEYS] for u in ids]
        s_hi = [s[u][:, BAND - BIAS_KEYS:BAND] + bias_ref[g * rows:(g + 1) * rows, :]
                for u, (j, g) in enumerate(units)]
        if masked:
            first = [(LEFT_CHUNKS - (i * ATT_CHUNKS + j)) * CHUNK for j in range(ATT_CHUNKS)]
            s_lo = [jnp.where(kpos_lo >= first[j], s_lo[u], neg) for u, (j, g) in enumerate(units)]
            s_hi = [jnp.where(kpos_hi >= first[j], s_hi[u], neg) for u, (j, g) in enumerate(units)]
        m = [jnp.maximum(jnp.max(s_lo[u], axis=-1, keepdims=True),
                         jnp.max(s_hi[u], axis=-1, keepdims=True)) for u in ids]
        p_lo = [jnp.exp(s_lo[u] - m[u]) for u in ids]
        p_hi = [jnp.exp(s_hi[u] - m[u]) for u in ids]
        denom = [jnp.sum(p_lo[u], axis=-1, keepdims=True) + jnp.sum(p_hi[u], axis=-1, keepdims=True)
                 for u in ids]
        o_all = [(_dot(p_lo[u].astype(BF16), vb[j][0:BAND - BIAS_KEYS, lanes[g]])
                  + _dot(p_hi[u].astype(BF16), vb[j][BAND - BIAS_KEYS:BAND, lanes[g]])) / denom[u]
                 for u, (j, g) in enumerate(units)]
        for u, (j, g) in enumerate(units):
            o_own = jnp.where(own, o_all[u], 0.0)
            o = o_own[0:CHUNK]
            for h in range(1, ATT_GROUP):
                o = o + o_own[h * CHUNK:(h + 1) * CHUNK]
            o_ref[0, pl.ds(starts[j], CHUNK), lanes[g]] = o.astype(BF16)
        return carry

    n_trips = seq // (CHUNK * ATT_CHUNKS)
    n_masked = min(LEFT_CHUNKS // ATT_CHUNKS, n_trips)
    lax.fori_loop(0, n_masked, functools.partial(chunk_pair, masked=True), 0)
    lax.fori_loop(n_masked, n_trips, functools.partial(chunk_pair, masked=False), 0)


def _band_attn(q, k, v, bias):
    b, seq, _ = q.shape
    whole = pl.BlockSpec((1, seq, WIDTH), lambda i: (i, 0, 0))
    return pl.pallas_call(
        _band_attn_kernel,
        grid=(b,),
        in_specs=[whole, whole, whole, pl.BlockSpec((HEADS * CHUNK, BIAS_KEYS), lambda i: (0, 0))],
        out_specs=whole,
        out_shape=jax.ShapeDtypeStruct((b, seq, WIDTH), BF16),
        scratch_shapes=[pltpu.VMEM((seq + LEFT_CHUNKS * CHUNK, WIDTH), BF16),
                        pltpu.VMEM((seq + LEFT_CHUNKS * CHUNK, WIDTH), BF16)],
        compiler_params=pltpu.CompilerParams(
            dimension_semantics=("arbitrary",), vmem_limit_bytes=VMEM_LIMIT),
        name="band_attn",
    )(q, k, v, bias)


def _rwkv_kernel(rw_ref, w2a2_ref, g2_ref, w0_ref, a0_ref, kk_ref, ka_ref, rk_ref,
                 gng_ref, gnb_ref, y_ref, state_ref):
    c = pl.program_id(1)
    t = CHUNK
    nb = rw_ref.shape[0]
    rows = nb * t

    @pl.when(c == 0)
    def _():
        state_ref[...] = jnp.zeros(state_ref.shape, F32)

    rw = rw_ref[...].reshape(rows, RWKV_PROJ)
    r = rw[:, 0:WIDTH]
    k = rw[:, WIDTH:2 * WIDTH]
    v = rw[:, 2 * WIDTH:3 * WIDTH]
    lora = rw[:, 3 * WIDTH:3 * WIDTH + DECAY_LORA + AAA_LORA]
    g_lo = rw[:, 3 * WIDTH + DECAY_LORA + AAA_LORA:RWKV_PROJ]

    lane128 = lax.broadcasted_iota(jnp.int32, lora.shape, 1)
    lora = jnp.where(lane128 < DECAY_LORA, jnp.tanh(lora), lora)
    wa = _mm(lora, w2a2_ref[...])
    log_decay = -math.exp(-0.5) * _sigmoid(w0_ref[...] + wa[:, 0:WIDTH])
    lr = _sigmoid(a0_ref[...] + wa[:, WIDTH:2 * WIDTH])
    gate = _mm(_sigmoid(g_lo), g2_ref[...])

    kk_raw = k * kk_ref[...]
    k_mod = k * (1.0 + (lr - 1.0) * ka_ref[...])

    row = lax.broadcasted_iota(jnp.int32, (rows, rows), 0)
    col = lax.broadcasted_iota(jnp.int32, (rows, rows), 1)
    tri = jnp.where((row >= col) & ((row // t) == (col // t)), 1.0, 0.0).astype(BF16)
    ld1 = log_decay.astype(BF16)
    rem = log_decay - ld1.astype(F32)
    ld2 = rem.astype(BF16)
    ld3 = (rem - ld2.astype(F32)).astype(BF16)
    parts = _dot(tri, jnp.concatenate([ld1, ld2, ld3], axis=1))
    logp = parts[:, 0:WIDTH] + parts[:, WIDTH:2 * WIDTH] + parts[:, 2 * WIDTH:3 * WIDTH]
    p_in = jnp.exp(logp)
    p_ex = jnp.exp(logp - log_decay)
    p_inv = jnp.exp(-logp)

    kk = kk_raw / jnp.maximum(jnp.sqrt(_head_sums(kk_raw * kk_raw)), 1e-12)
    a_hat = -kk * p_ex
    r_hat = r * p_in
    b_hat = kk * lr * p_inv
    k_hat = k_mod * p_inv
    bonus = _head_sums(r * k_mod * rk_ref[...]) * v

    lane = lax.broadcasted_iota(jnp.int32, (t, 2 * HEAD_DIM), 1)
    low = lane < HEAD_DIM
    r2 = lax.broadcasted_iota(jnp.int32, (2 * t, 2 * HEAD_DIM), 0)
    c2 = lax.broadcasted_iota(jnp.int32, (2 * t, 2 * HEAD_DIM), 1)
    own = (r2 < t) == (c2 < HEAD_DIM)
    strict = (r2 & (t - 1)) > (c2 & (t - 1))
    incl = (r2 & (t - 1)) >= (c2 & (t - 1))
    eye = jnp.where(r2 == c2, 1.0, 0.0).astype(F32)

    def stack2(xp):
        return jnp.concatenate([jnp.where(low, xp, 0.0), jnp.where(low, 0.0, xp)], axis=0)

    chains = [(b, p) for b in range(nb) for p in range(HEADS // 2)]
    ids = range(len(chains))
    rs = [slice(b * t, (b + 1) * t) for b, _ in chains]
    ls = [slice(2 * HEAD_DIM * p, 2 * HEAD_DIM * (p + 1)) for _, p in chains]
    p_end = [p_in[(b + 1) * t - 1:(b + 1) * t, ls[i]] for i, (b, _) in enumerate(chains)]
    ar = [jnp.concatenate([stack2(a_hat[rs[i], ls[i]]), stack2(r_hat[rs[i], ls[i]])], axis=0).astype(BF16)
          for i in ids]
    bk2 = [(stack2(b_hat[rs[i], ls[i]]), stack2(k_hat[rs[i], ls[i]])) for i in ids]
    bk = [jnp.concatenate(bk2[i], axis=0).astype(BF16) for i in ids]
    btkt = [(jnp.concatenate(bk2[i], axis=0) * p_end[i]).astype(BF16) for i in ids]
    v2 = [stack2(v[rs[i], ls[i]]).astype(BF16) for i in ids]
    g = [_dot_nt(ar[i], bk[i]) for i in ids]
    st = [state_ref[b, p] for b, p in chains]
    ars = [_dot_nt(ar[i], st[i].astype(BF16)) for i in ids]
    l_ab = [jnp.where(strict, g[i][0:2 * t, 0:2 * t], 0.0) for i in ids]
    lm = [jnp.concatenate([jnp.where(strict, g[i][0:2 * t, 2 * t:4 * t], 0.0),
                           jnp.where(incl, g[i][2 * t:4 * t, 2 * t:4 * t], 0.0)], axis=0)
          for i in ids]
    m_rb = [jnp.where(incl, g[i][2 * t:4 * t, 0:2 * t], 0.0).astype(BF16) for i in ids]
    lv = [_mm(lm[i], v2[i]) for i in ids]

    w_inv = [eye + l_ab[i] for i in ids]
    l_pow = l_ab
    for _ in range(int(math.log2(t)) - 1):
        l_pow = [_mm(l_pow[i], l_pow[i]) for i in ids]
        w_inv = [w_inv[i] + _mm(w_inv[i], l_pow[i]) for i in ids]

    z = [_mm(w_inv[i], ars[i][0:2 * t] + lv[i][0:2 * t]) for i in ids]
    y = [ars[i][2 * t:4 * t] + lv[i][2 * t:4 * t] + _mm(m_rb[i], z[i]) for i in ids]
    for i, (b, p) in enumerate(chains):
        zv = jnp.concatenate([z[i].astype(BF16), v2[i]], axis=0)
        state_ref[b, p] = st[i] * p_end[i] + _dot_tn(zv, btkt[i])

    for i, (b, p) in enumerate(chains):
        mean = jnp.sum(y[i], axis=-1, keepdims=True) * (1.0 / HEAD_DIM)
        dev = jnp.where(own, y[i] - mean, 0.0)
        var = jnp.sum(dev * dev, axis=-1, keepdims=True) * (1.0 / HEAD_DIM)
        yn = dev * lax.rsqrt(var + GN_EPS)
        yn = yn[0:t] + yn[t:2 * t]
        out = (yn * gng_ref[:, ls[i]] + gnb_ref[:, ls[i]] + bonus[rs[i], ls[i]]) * gate[rs[i], ls[i]]
        y_ref[b, :, ls[i]] = out.astype(BF16)


def _rwkv(rw, w2a2, g2, w0, a0, k_k, k_a, r_k, gn_g, gn_b):
    b, seq, _ = rw.shape
    nc = seq // CHUNK
    nb = RWKV_SEQS
    const = lambda i, c: (0, 0)
    vec = pl.BlockSpec((1, WIDTH), const)
    return pl.pallas_call(
        _rwkv_kernel,
        grid=(b // nb, nc),
        in_specs=[
            pl.BlockSpec((nb, CHUNK, RWKV_PROJ), lambda i, c: (i, c, 0)),
            pl.BlockSpec((DECAY_LORA + AAA_LORA, 2 * WIDTH), const),
            pl.BlockSpec((GATE_LORA, WIDTH), const),
            vec, vec, vec, vec, vec, vec, vec,
        ],
        out_specs=pl.BlockSpec((nb, CHUNK, WIDTH), lambda i, c: (i, c, 0)),
        out_shape=jax.ShapeDtypeStruct((b, seq, WIDTH), BF16),
        scratch_shapes=[pltpu.VMEM((nb, HEADS // 2, 2 * HEAD_DIM, 2 * HEAD_DIM), F32)],
        compiler_params=pltpu.CompilerParams(
            dimension_semantics=("arbitrary", "arbitrary"), vmem_limit_bytes=VMEM_LIMIT),
        name="rwkv7",
    )(rw, w2a2, g2, w0, a0, k_k, k_a, r_k, gn_g, gn_b)


def _merge_kernel(x_ref, att_ref, rwkv_ref, gate_ref, wa32_ref, wb32_ref, wo32_ref, g_ref,
                  wr_ref, br_ref, x1_ref, h2_ref, comb_ref, cnt_ref, wa_ref, wb_ref, wo_ref):
    @pl.when(pl.program_id(0) == 0)
    def _():
        wa_ref[...] = wa32_ref[...].astype(BF16)
        wb_ref[...] = wb32_ref[...].astype(BF16)
        wo_ref[...] = wo32_ref[...].astype(BF16)

    ga = gate_ref[:, 0:D_MODEL].astype(F32)
    gb = gate_ref[:, D_MODEL:2 * D_MODEL].astype(F32)
    merged = ga * _dot(att_ref[...], wa_ref[...]) + gb * _dot(rwkv_ref[...], wb_ref[...])
    x1 = x_ref[...] + _dot(merged.astype(BF16), wo_ref[...])
    x1_ref[...] = x1
    h2 = x1 * lax.rsqrt(jnp.mean(x1 * x1, axis=-1, keepdims=True) + RMS_EPS) * g_ref[...]
    h2_hi = h2.astype(BF16)
    h2_ref[...] = _pack_bf16(h2)
    h2_lo = (h2 - h2_hi.astype(F32)).astype(BF16)
    hw = _dot(h2_hi, wr_ref[...])
    lw = _dot(h2_lo, wr_ref[:, 0:ROUTER_LANES])
    logits = hw[:, 0:ROUTER_LANES] + (hw[:, ROUTER_LANES:2 * ROUTER_LANES] + lw) + br_ref[...]
    lane_i = lax.broadcasted_iota(jnp.int32, logits.shape, 1)
    lane = lane_i.astype(F32)
    lane_group = (lane_i >> 3).astype(F32)
    neg = jnp.finfo(F32).min
    big = float(ROUTER_LANES)

    def first_argmax(vals, mask):
        vm = jnp.where(mask, vals, neg)
        mx = jnp.max(vm, axis=-1, keepdims=True)
        idx = jnp.min(jnp.where(vm == mx, jnp.where(mask, lane, big), big), axis=-1, keepdims=True)
        return mx, idx

    is_group = (lane_i >= GROUP_LANE0) & (lane_i < GROUP_LANE0 + N_GROUPS)
    g_max, g_lane = first_argmax(logits, is_group)
    g_prob = 1.0 / jnp.sum(jnp.where(is_group, jnp.exp(logits - g_max), 0.0),
                           axis=-1, keepdims=True)
    g_idx = g_lane - GROUP_LANE0
    in_group = lane_group == g_idx
    e1, i1 = first_argmax(logits, in_group)
    e2, i2 = first_argmax(logits, in_group & (lane != i1))
    w2 = jnp.exp(e2 - e1)
    p1 = 1.0 / (1.0 + w2)
    p2 = w2 / (1.0 + w2)
    rec = jnp.where(lane_i == EXPERT1_LANE, i1, jnp.where(lane_i == EXPERT2_LANE, i2, 0.0))
    rec = jnp.where(lane_i == WEIGHT1_LANE, p1 * g_prob, jnp.where(lane_i == WEIGHT2_LANE, p2 * g_prob, rec))
    comb_ref[...] = rec
    counts = jnp.sum(jnp.where(lane == i1, 1.0, 0.0) + jnp.where(lane == i2, 1.0, 0.0),
                     axis=0, keepdims=True)
    cnt_ref[0] = jnp.broadcast_to(counts, (8, ROUTER_LANES)).astype(jnp.int32)


def _merge(x2, att, rwkv, gates, wa, wb, wo, ln_g, wr, br):
    n = x2.shape[0]
    tm = MERGE_TILE
    row = lambda i: (i, 0)
    const = lambda i: (0, 0)
    return pl.pallas_call(
        _merge_kernel,
        grid=(n // tm,),
        in_specs=[
            pl.BlockSpec((tm, D_MODEL), row),
            pl.BlockSpec((tm, WIDTH), row),
            pl.BlockSpec((tm, WIDTH), row),
            pl.BlockSpec((tm, 2 * D_MODEL), row),
            pl.BlockSpec((WIDTH, D_MODEL), const),
            pl.BlockSpec((WIDTH, D_MODEL), const),
            pl.BlockSpec((D_MODEL, D_MODEL), const),
            pl.BlockSpec((1, D_MODEL), const),
            pl.BlockSpec((D_MODEL, 2 * ROUTER_LANES), const),
            pl.BlockSpec((1, ROUTER_LANES), const),
        ],
        out_specs=[
            pl.BlockSpec((tm, D_MODEL), row),
            pl.BlockSpec((tm, PACKED), row),
            pl.BlockSpec((tm, ROUTER_LANES), row),
            pl.BlockSpec((1, 8, ROUTER_LANES), lambda i: (i, 0, 0)),
        ],
        out_shape=[
            jax.ShapeDtypeStruct((n, D_MODEL), F32),
            jax.ShapeDtypeStruct((n, PACKED), jnp.uint32),
            jax.ShapeDtypeStruct((n, ROUTER_LANES), F32),
            jax.ShapeDtypeStruct((n // tm, 8, ROUTER_LANES), jnp.int32),
        ],
        scratch_shapes=[pltpu.VMEM((WIDTH, D_MODEL), BF16), pltpu.VMEM((WIDTH, D_MODEL), BF16),
                        pltpu.VMEM((D_MODEL, D_MODEL), BF16)],
        compiler_params=pltpu.CompilerParams(
            dimension_semantics=("arbitrary",), vmem_limit_bytes=VMEM_LIMIT),
        name="merge",
    )(x2, att, rwkv, gates, wa, wb, wo, ln_g, wr, br)


def _moe_plan_kernel(cnt_ref, base_ref, blk_ref):
    nt = cnt_ref.shape[0]
    cnt = cnt_ref[...].astype(F32)
    lane = lax.broadcasted_iota(jnp.int32, (8, ROUTER_LANES), 1)
    total = jnp.broadcast_to(jnp.sum(cnt, axis=0, keepdims=True), (8, ROUTER_LANES))
    padded = jnp.floor((total + (MOE_BLOCK - 1)) * (1.0 / MOE_BLOCK)) * MOE_BLOCK
    r = lax.broadcasted_iota(jnp.int32, (ROUTER_LANES, ROUTER_LANES), 0)
    c = lax.broadcasted_iota(jnp.int32, (ROUTER_LANES, ROUTER_LANES), 1)
    seg_start = _dot_hi(padded, jnp.where(r < c, 1.0, 0.0).astype(F32))
    tr = lax.broadcasted_iota(jnp.int32, (nt, nt), 0)
    tc = lax.broadcasted_iota(jnp.int32, (nt, nt), 1)
    tile_off = _dot_hi(jnp.where(tc < tr, 1.0, 0.0).astype(F32), cnt)
    base_ref[...] = seg_start[0:1, :] + tile_off
    seg_end = (seg_start + padded)[0:1, :]
    rows_total = jnp.sum(jnp.where(lane[0:1, :] < N_EXPERTS, padded[0:1, :], 0.0), axis=-1, keepdims=True)
    nblk = blk_ref.shape[0]
    blk_row = lax.broadcasted_iota(jnp.int32, (nblk, ROUTER_LANES), 0).astype(F32) * MOE_BLOCK
    blk_lane = lax.broadcasted_iota(jnp.int32, (nblk, ROUTER_LANES), 1)
    done = jnp.where((seg_end <= blk_row) & (blk_lane < N_EXPERTS), 1.0, 0.0)
    expert = jnp.minimum(jnp.sum(done, axis=-1, keepdims=True), N_EXPERTS - 1.0)
    valid = jnp.where(blk_row < rows_total, 1.0, 0.0)
    blk_ref[...] = jnp.where(blk_lane == 0, expert, jnp.where(blk_lane == 1, valid, 0.0)).astype(jnp.int32)


def _moe_plan(cnt, n_blocks):
    nt = cnt.shape[0]
    return pl.pallas_call(
        _moe_plan_kernel,
        out_shape=[jax.ShapeDtypeStruct((nt, ROUTER_LANES), F32),
                   jax.ShapeDtypeStruct((n_blocks, ROUTER_LANES), jnp.int32)],
        name="moe_plan",
    )(cnt)


def _moe_pos_kernel(comb_ref, base_ref, pos_ref):
    t = comb_ref.shape[0]
    comb = comb_ref[...]
    lane_i = lax.broadcasted_iota(jnp.int32, (t, ROUTER_LANES), 1)
    lane = lane_i.astype(F32)
    pick1 = jnp.where(lane == comb[:, EXPERT1_LANE:EXPERT1_LANE + 1], 1.0, 0.0)
    pick2 = jnp.where(lane == comb[:, EXPERT2_LANE:EXPERT2_LANE + 1], 1.0, 0.0)
    rows = lax.broadcasted_iota(jnp.int32, (t, t), 0)
    cols = lax.broadcasted_iota(jnp.int32, (t, t), 1)
    earlier = jnp.where(cols < rows, 1.0, 0.0).astype(BF16)
    before1 = _dot(earlier, pick1.astype(BF16))
    before2 = _dot(earlier, pick2.astype(BF16))
    base = base_ref[0]
    firsts = jnp.sum(pick1, axis=0, keepdims=True)
    pos1 = jnp.sum(pick1 * (base + before1), axis=-1, keepdims=True)
    pos2 = jnp.sum(pick2 * (base + firsts + before2), axis=-1, keepdims=True)
    both = jnp.where(lane_i == 0, pos1, jnp.where(lane_i == 1, pos2, 0.0))
    pos_ref[...] = jnp.transpose(both)[0:8, :].astype(jnp.int32)


def _moe_pos(comb, base):
    n = comb.shape[0]
    t = MERGE_TILE
    return pl.pallas_call(
        _moe_pos_kernel,
        grid=(n // t,),
        in_specs=[pl.BlockSpec((t, ROUTER_LANES), lambda i: (i, 0)),
                  pl.BlockSpec((1, 1, ROUTER_LANES), lambda i: (i, 0, 0))],
        out_specs=pl.BlockSpec((8, t), lambda i: (0, i)),
        out_shape=jax.ShapeDtypeStruct((8, n), jnp.int32),
        compiler_params=pltpu.CompilerParams(dimension_semantics=("arbitrary",)),
        name="moe_pos",
    )(comb, base.reshape(n // t, 1, ROUTER_LANES))


def _sc_scratch():
    half = SC_WINDOW // 2
    return [pltpu.VMEM((8, SC_WINDOW), jnp.int32),
            pltpu.VMEM((half, PACKED), jnp.uint32), pltpu.VMEM((half, PACKED), jnp.uint32),
            pltpu.SemaphoreType.DMA((2,)), pltpu.SemaphoreType.DMA((2,))]


def _sc_scatter_rows(h, pos, n_rows):
    n = h.shape[0]
    per_worker = n // (SC_WINDOW * SC_WORKERS)
    half = SC_WINDOW // 2
    mesh = plsc.VectorSubcoreMesh(core_axis_name="core", subcore_axis_name="subcore")

    @pl.kernel(out_type=jax.ShapeDtypeStruct((n_rows, PACKED), jnp.uint32), mesh=mesh,
               scratch_types=_sc_scratch())
    def scatter(h_hbm, pos_hbm, out_hbm, idx, buf0, buf1, sem_in, sem_out):
        worker = lax.axis_index("core") * (SC_WORKERS // 2) + lax.axis_index("subcore")
        bufs = (buf0, buf1)

        @pl.loop(0, per_worker)
        def _(b):
            start = (worker * per_worker + b) * SC_WINDOW
            pltpu.sync_copy(pos_hbm.at[:, pl.ds(start, SC_WINDOW)], idx)
            loads = [pltpu.async_copy(h_hbm.at[pl.ds(start + s * half, half)], bufs[s], sem_in.at[s])
                     for s in range(2)]
            sends = []
            for s in range(2):
                loads[s].wait()
                for k in range(2):
                    sends.append(pltpu.async_copy(
                        bufs[s], out_hbm.at[idx.at[k, pl.ds(s * half, half)]], sem_out.at[s]))
            for send in sends:
                send.wait()

    return scatter(h, pos)


def _sc_gather_rows(y, pos, first, n):
    per_worker = n // (SC_WINDOW * SC_WORKERS)
    half = SC_WINDOW // 2
    mesh = plsc.VectorSubcoreMesh(core_axis_name="core", subcore_axis_name="subcore")
    out = jax.ShapeDtypeStruct((n, PACKED), jnp.uint32)

    @pl.kernel(out_type=(out, out), mesh=mesh, scratch_types=_sc_scratch())
    def gather(y_hbm, pos_hbm, o1_hbm, o2_hbm, idx, buf0, buf1, sem_in, sem_out):
        worker = lax.axis_index("core") * (SC_WORKERS // 2) + lax.axis_index("subcore")
        bufs = (buf0, buf1)
        outs = (o1_hbm, o2_hbm)

        @pl.loop(0, per_worker)
        def _(b):
            start = (worker * per_worker + b) * SC_WINDOW
            pltpu.sync_copy(pos_hbm.at[:, pl.ds(first + start, SC_WINDOW)], idx)
            for s in range(2):
                loads = [pltpu.async_copy(y_hbm.at[idx.at[k, pl.ds(s * half, half)]], bufs[k], sem_in.at[k])
                         for k in range(2)]
                stores = []
                for k in range(2):
                    loads[k].wait()
                    stores.append(pltpu.async_copy(
                        bufs[k], outs[k].at[pl.ds(start + s * half, half)], sem_out.at[k]))
                for store in stores:
                    store.wait()

    return gather(y, pos)


def _expert_mlp_kernel(expert_ref, valid_ref, hs_ref, *refs):
    step = pl.program_id(0)
    w_f32 = refs[0:3 * MLP_SUB]
    y_ref = refs[3 * MLP_SUB]
    w_bf16 = refs[3 * MLP_SUB + 1:]
    first = step * MLP_SUB

    @pl.when(valid_ref[first] == 0)
    def _():
        y_ref[...] = jnp.zeros(y_ref.shape, jnp.uint32)

    @pl.when(valid_ref[first] != 0)
    def _():
        for j in range(MLP_SUB):
            prev = jnp.maximum(first + j - MLP_SUB, 0)
            stale = (step == 0) | (expert_ref[first + j] != expert_ref[prev])

            @pl.when(stale)
            def _():
                for k in range(3):
                    w_bf16[3 * j + k][...] = w_f32[3 * j + k][0].astype(BF16)

        half = D_MODEL // 2
        subs = range(MLP_SUB)
        parts = [_unpack_bf16(hs_ref[j * MOE_BLOCK:(j + 1) * MOE_BLOCK, :]) for j in subs]
        h_lo = [parts[j][0].astype(BF16) for j in subs]
        h_hi = [parts[j][1].astype(BF16) for j in subs]
        hg = [_dot(h_lo[j], w_bf16[3 * j][0:half, :]) + _dot(h_hi[j], w_bf16[3 * j][half:D_MODEL, :])
              for j in subs]
        hu = [_dot(h_lo[j], w_bf16[3 * j + 1][0:half, :]) + _dot(h_hi[j], w_bf16[3 * j + 1][half:D_MODEL, :])
              for j in subs]
        act = [(hg[j] * _sigmoid(hg[j]) * hu[j]).astype(BF16) for j in subs]
        for j in subs:
            y_ref[j * MOE_BLOCK:(j + 1) * MOE_BLOCK, :] = _pack_bf16(_dot(act[j], w_bf16[3 * j + 2][...]))


def _expert_mlp(blk_expert, blk_valid, hs, wg, wu, wd):
    n_steps = hs.shape[0] // (MOE_BLOCK * MLP_SUB)
    rows = lambda s, expert, valid: (s, 0)
    w_specs, w_args, w_scratch = [], [], []
    for j in range(MLP_SUB):
        by_expert = lambda s, expert, valid, j=j: (expert[s * MLP_SUB + j], 0, 0)
        w_specs += [pl.BlockSpec((1, D_MODEL, D_EXPERT), by_expert),
                    pl.BlockSpec((1, D_MODEL, D_EXPERT), by_expert),
                    pl.BlockSpec((1, D_EXPERT, D_MODEL), by_expert)]
        w_args += [wg, wu, wd]
        w_scratch += [pltpu.VMEM((D_MODEL, D_EXPERT), BF16), pltpu.VMEM((D_MODEL, D_EXPERT), BF16),
                      pltpu.VMEM((D_EXPERT, D_MODEL), BF16)]
    grid_spec = pltpu.PrefetchScalarGridSpec(
        num_scalar_prefetch=2,
        grid=(n_steps,),
        in_specs=[pl.BlockSpec((MOE_BLOCK * MLP_SUB, PACKED), rows)] + w_specs,
        out_specs=pl.BlockSpec((MOE_BLOCK * MLP_SUB, PACKED), rows),
        scratch_shapes=w_scratch,
    )
    return pl.pallas_call(
        _expert_mlp_kernel,
        grid_spec=grid_spec,
        out_shape=jax.ShapeDtypeStruct((hs.shape[0], PACKED), jnp.uint32),
        compiler_params=pltpu.CompilerParams(
            dimension_semantics=("arbitrary",), vmem_limit_bytes=VMEM_LIMIT),
        name="expert_mlp",
    )(blk_expert, blk_valid, hs, *w_args)


def _moe_out_kernel(x1_ref, y1_ref, y2_ref, comb_ref, lnf_ref, *rest, final_norm):
    out_ref = rest[-1]
    w1 = comb_ref[:, WEIGHT1_LANE:WEIGHT1_LANE + 1]
    w2 = comb_ref[:, WEIGHT2_LANE:WEIGHT2_LANE + 1]
    a_lo, a_hi = _unpack_bf16(y1_ref[...])
    b_lo, b_hi = _unpack_bf16(y2_ref[...])
    moe = jnp.concatenate([w1 * a_lo + w2 * b_lo, w1 * a_hi + w2 * b_hi], axis=1)
    y = x1_ref[...] + moe
    if final_norm:
        y = y * lax.rsqrt(jnp.mean(y * y, axis=-1, keepdims=True) + RMS_EPS) * lnf_ref[...]
    out_ref[...] = y


def _moe_out(x1, y1, y2, comb, lnf, final_norm, first, out_so_far):
    n = x1.shape[0]
    t = FINAL_TILE
    off = first // t
    here = lambda i: (i, 0)
    there = lambda i: (i + off, 0)
    in_specs = [pl.BlockSpec((t, D_MODEL), there), pl.BlockSpec((t, PACKED), here),
                pl.BlockSpec((t, PACKED), here), pl.BlockSpec((t, ROUTER_LANES), there),
                pl.BlockSpec((1, D_MODEL), lambda i: (0, 0))]
    args = [x1, y1, y2, comb, lnf]
    aliases = {}
    if out_so_far is not None:
        in_specs.append(pl.BlockSpec(memory_space=pl.ANY))
        args.append(out_so_far)
        aliases = {len(args) - 1: 0}
    return pl.pallas_call(
        functools.partial(_moe_out_kernel, final_norm=final_norm),
        grid=(y1.shape[0] // t,),
        in_specs=in_specs,
        out_specs=pl.BlockSpec((t, D_MODEL), there),
        out_shape=jax.ShapeDtypeStruct((n, D_MODEL), F32),
        input_output_aliases=aliases,
        compiler_params=pltpu.CompilerParams(
            dimension_semantics=("arbitrary",), vmem_limit_bytes=VMEM_LIMIT),
        name="moe_out",
    )(*args)


def _moe(counts, x1, h2p, comb, wg, wu, wd, lnf, final_norm):
    n = x1.shape[0]
    n_blocks = (2 * n) // MOE_BLOCK + N_EXPERTS
    base, blk = _moe_plan(counts[:, 0, :], n_blocks)
    pos = _moe_pos(comb, base)
    hs = _sc_scatter_rows(h2p, pos, n_blocks * MOE_BLOCK)
    ys = _expert_mlp(blk[:, 0], blk[:, 1], hs, wg, wu, wd)
    out = None
    part = n // MOE_PARTS
    for p in range(MOE_PARTS):
        y1, y2 = _sc_gather_rows(ys, pos, p * part, part)
        out = _moe_out(x1, y1, y2, comb, lnf, final_norm, p * part, out)
    return out


def kernel(x, ln_mix_g, w_in, att_rel_bias, rwkv_mu, rwkv_w0, rwkv_w2, rwkv_a0, rwkv_a2, rwkv_g2,
           rwkv_k_k, rwkv_k_a, rwkv_r_k, rwkv_gn_g, rwkv_gn_b, w_branch_att, w_branch_rwkv, w_out,
           ln_ffn_g, router_group_w, router_group_b, router_expert_w, router_expert_b,
           expert_w_gate, expert_w_up, expert_w_down, ln_final_g):
    bsz, seq, d = x.shape
    depth = w_in.shape[0]
    n = bsz * seq
    x2 = x.reshape(n, d)
    for l in range(depth):
        q, k, v, rw, gates = _in_proj(x2, ln_mix_g[l][None, :], w_in[l].astype(BF16),
                                      rwkv_mu[l][None, :], seq)
        bias = _rel_bias(att_rel_bias[l])
        att = _band_attn(q.reshape(bsz, seq, WIDTH), k.reshape(bsz, seq, WIDTH),
                         v.reshape(bsz, seq, WIDTH), bias)
        zeros = jnp.zeros((DECAY_LORA, WIDTH), F32)
        w2a2 = jnp.concatenate(
            [jnp.concatenate([rwkv_w2[l], zeros], axis=1),
             jnp.concatenate([zeros, rwkv_a2[l]], axis=1)], axis=0)
        rwkv = _rwkv(rw.reshape(bsz, seq, RWKV_PROJ), w2a2.astype(BF16), rwkv_g2[l].astype(BF16),
                     rwkv_w0[l][None, :], rwkv_a0[l][None, :], rwkv_k_k[l][None, :],
                     rwkv_k_a[l][None, :], rwkv_r_k[l].reshape(1, WIDTH),
                     rwkv_gn_g[l][None, :], rwkv_gn_b[l][None, :])
        wr = jnp.concatenate([router_expert_w[l], router_group_w[l]], axis=1)
        wr = jnp.pad(wr, ((0, 0), (0, ROUTER_LANES - wr.shape[1])))
        wr_hi = wr.astype(BF16)
        wr = jnp.concatenate([wr_hi, (wr - wr_hi.astype(F32)).astype(BF16)], axis=1)
        br = jnp.concatenate([router_expert_b[l], router_group_b[l]])
        br = jnp.pad(br, (0, ROUTER_LANES - br.shape[0]))[None, :]
        x1, h2, comb, counts = _merge(x2, att.reshape(n, WIDTH), rwkv.reshape(n, WIDTH), gates,
                                      w_branch_att[l], w_branch_rwkv[l], w_out[l],
                                      ln_ffn_g[l][None, :], wr, br)
        x2 = _moe(counts, x1, h2, comb, expert_w_gate[l], expert_w_up[l], expert_w_down[l],
                  ln_final_g[None, :], final_norm=(l == depth - 1))
    return x2.reshape(bsz, seq, d)
```

```python
import functools
import math

import jax
import jax.numpy as jnp
from jax import lax
from jax.experimental import pallas as pl
from jax.experimental.pallas import tpu as pltpu
from jax.experimental.pallas import tpu_sc as plsc

F32 = jnp.float32
BF16 = jnp.bfloat16
HIGHEST = lax.Precision.HIGHEST

D_MODEL = 1024
CHUNK = 64
HEADS = 8
HEAD_DIM = 64
WIDTH = HEADS * HEAD_DIM
LEFT_CHUNKS = 8
BAND = (LEFT_CHUNKS + 1) * CHUNK
REL_CLIP = 64
N_REL = 2 * REL_CLIP + 1
DECAY_LORA = 64
AAA_LORA = 64
GATE_LORA = 128
GN_EPS = 64e-5
RMS_EPS = 1e-6
ATT_PROJ = 3 * WIDTH
RWKV_PROJ = 3 * WIDTH + DECAY_LORA + AAA_LORA + GATE_LORA
D_IN = ATT_PROJ + RWKV_PROJ + 2 * D_MODEL
N_GROUPS = 4
EXPERTS_PER_GROUP = 8
N_EXPERTS = N_GROUPS * EXPERTS_PER_GROUP
D_EXPERT = 256
GROUP_FF = EXPERTS_PER_GROUP * D_EXPERT
RWKV_SEQS = 4
BIAS_KEYS = 192
ATT_CHUNKS = 4
ATT_GROUP = 4
ROUTER_LANES = 128
GROUP_LANE0 = N_EXPERTS
EXPERT1_LANE, EXPERT2_LANE, WEIGHT1_LANE, WEIGHT2_LANE = 126, 125, 124, 123
MERGE_TILE = 512
MOE_BLOCK = 512
PACKED = D_MODEL // 2
SC_WINDOW = 128
SC_WORKERS = 32
FINAL_TILE = 1024
MLP_SUB = 1
MOE_PARTS = 1

VMEM_LIMIT = 52 * 1024 * 1024


def _dot(a, b):
    return jnp.dot(a, b, preferred_element_type=F32)


def _dot_hi(a, b):
    return jnp.dot(a, b, preferred_element_type=F32, precision=HIGHEST)


def _dot_nt(a, b, precision=None):
    return lax.dot_general(a, b, (((1,), (1,)), ((), ())),
                           preferred_element_type=F32, precision=precision)


def _dot_tn(a, b, precision=None):
    return lax.dot_general(a, b, (((0,), (0,)), ((), ())),
                           preferred_element_type=F32, precision=precision)


def _sigmoid(x):
    return 1.0 / (1.0 + jnp.exp(-x))


def _pack_bf16(x):
    w = x.shape[1] // 2
    hi = pltpu.bitcast(x[:, :w].astype(BF16).astype(F32), jnp.uint32)
    lo = pltpu.bitcast(x[:, w:].astype(BF16).astype(F32), jnp.uint32)
    return hi | lax.shift_right_logical(lo, jnp.uint32(16))


def _unpack_bf16(p):
    hi = pltpu.bitcast(p & jnp.uint32(0xFFFF0000), F32)
    lo = pltpu.bitcast(lax.shift_left(p, jnp.uint32(16)), F32)
    return hi, lo


def _mm(a, b):
    return jnp.dot(a.astype(BF16), b.astype(BF16), preferred_element_type=F32)


def _head_sums(x):
    outs = []
    lane = lax.broadcasted_iota(jnp.int32, (x.shape[0], 2 * HEAD_DIM), 1)
    low = lane < HEAD_DIM
    for p in range(HEADS // 2):
        xp = x[:, 2 * HEAD_DIM * p:2 * HEAD_DIM * (p + 1)]
        s_lo = jnp.sum(jnp.where(low, xp, 0.0), axis=-1, keepdims=True)
        s_hi = jnp.sum(jnp.where(low, 0.0, xp), axis=-1, keepdims=True)
        outs.append(jnp.where(low, s_lo, s_hi))
    return jnp.concatenate(outs, axis=-1)


def _rel_bias_kernel(tab_ref, out_ref):
    rows = tab_ref.shape[1]
    n = lax.broadcasted_iota(jnp.int32, (rows, CHUNK * 128), 1)
    r = lax.broadcasted_iota(jnp.int32, (rows, CHUNK * 128), 0)
    q = n >> 7
    kk = n & 127
    idx = jnp.clip(CHUNK + q - kk, -REL_CLIP, REL_CLIP) + REL_CLIP
    pick = jnp.where(r == idx, 1.0, 0.0) - jnp.where(r == N_REL - 1, 1.0, 0.0)
    out_ref[...] = _dot_hi(tab_ref[...], pick.astype(F32))


def _rel_bias(rel_table):
    rows = 136
    tab = jnp.pad(rel_table.astype(F32), ((0, 0), (0, rows - N_REL)))
    tail = pl.pallas_call(
        _rel_bias_kernel,
        out_shape=jax.ShapeDtypeStruct((HEADS, CHUNK * 128), F32),
        name="rel_bias",
    )(tab)
    tail = tail.reshape(HEADS, CHUNK, 128)
    bias = jnp.concatenate([jnp.zeros((HEADS, CHUNK, BIAS_KEYS - 128), F32), tail], axis=-1)
    return bias.reshape(HEADS * CHUNK, BIAS_KEYS)


def _in_proj_kernel(x_ref, g_ref, w_ref, mu_ref, q_ref, k_ref, v_ref, rw_ref, gate_ref,
                    carry_ref, *, tiles_per_seq):
    i = pl.program_id(0)

    @pl.when(i == 0)
    def _():
        carry_ref[...] = jnp.zeros(carry_ref.shape, F32)

    x = x_ref[...]
    h = x * lax.rsqrt(jnp.mean(x * x, axis=-1, keepdims=True) + RMS_EPS) * g_ref[...]
    hb = h.astype(BF16)
    q_ref[...] = _dot(hb, w_ref[:, 0:WIDTH]).astype(BF16)
    k_ref[...] = _dot(hb, w_ref[:, WIDTH:2 * WIDTH]).astype(BF16)
    v_ref[...] = _dot(hb, w_ref[:, 2 * WIDTH:ATT_PROJ]).astype(BF16)
    rw = _dot(hb, w_ref[:, ATT_PROJ:ATT_PROJ + RWKV_PROJ])
    tm = rw.shape[0]
    first_prev = jnp.where(i % tiles_per_seq == 0, 0.0, carry_ref[0:1, :])
    rolled = pltpu.roll(rw, 1, axis=0)
    row = lax.broadcasted_iota(jnp.int32, rw.shape, 0)
    prev = jnp.where(row == 0, first_prev, rolled)
    carry_ref[0:1, :] = rw[tm - 1:tm, :]
    rw_ref[...] = rw + (prev - rw) * mu_ref[...]
    gate_ref[...] = _sigmoid(_dot(hb, w_ref[:, ATT_PROJ + RWKV_PROJ:D_IN])).astype(BF16)


def _in_proj(x2, ln_g, w_in_b, mu, seq):
    n = x2.shape[0]
    tm = 512
    row = lambda i: (i, 0)
    const = lambda i: (0, 0)
    return pl.pallas_call(
        functools.partial(_in_proj_kernel, tiles_per_seq=seq // tm),
        grid=(n // tm,),
        in_specs=[
            pl.BlockSpec((tm, D_MODEL), row),
            pl.BlockSpec((1, D_MODEL), const),
            pl.BlockSpec((D_MODEL, D_IN), const),
            pl.BlockSpec((1, RWKV_PROJ), const),
        ],
        out_specs=[
            pl.BlockSpec((tm, WIDTH), row),
            pl.BlockSpec((tm, WIDTH), row),
            pl.BlockSpec((tm, WIDTH), row),
            pl.BlockSpec((tm, RWKV_PROJ), row),
            pl.BlockSpec((tm, 2 * D_MODEL), row),
        ],
        out_shape=[
            jax.ShapeDtypeStruct((n, WIDTH), BF16),
            jax.ShapeDtypeStruct((n, WIDTH), BF16),
            jax.ShapeDtypeStruct((n, WIDTH), BF16),
            jax.ShapeDtypeStruct((n, RWKV_PROJ), F32),
            jax.ShapeDtypeStruct((n, 2 * D_MODEL), BF16),
        ],
        scratch_shapes=[pltpu.VMEM((8, RWKV_PROJ), F32)],
        compiler_params=pltpu.CompilerParams(
            dimension_semantics=("arbitrary",), vmem_limit_bytes=VMEM_LIMIT),
        name="in_proj",
    )(x2, ln_g, w_in_b, mu)


def _band_attn_kernel(q_ref, k_ref, v_ref, bias_ref, o_ref, kpad_ref, vpad_ref):
    seq = k_ref.shape[1]
    pad = LEFT_CHUNKS * CHUNK
    kpad_ref[0:pad, :] = jnp.zeros((pad, WIDTH), BF16)
    vpad_ref[0:pad, :] = jnp.zeros((pad, WIDTH), BF16)
    kpad_ref[pad:pad + seq, :] = k_ref[0]
    vpad_ref[pad:pad + seq, :] = v_ref[0]

    gw = ATT_GROUP * HEAD_DIM
    rows = ATT_GROUP * CHUNK
    r_head = lax.broadcasted_iota(jnp.int32, (rows, gw), 0) // CHUNK
    l_head = lax.broadcasted_iota(jnp.int32, (rows, gw), 1) // HEAD_DIM
    own = r_head == l_head
    kpos_lo = lax.broadcasted_iota(jnp.int32, (rows, BAND - BIAS_KEYS), 1)
    kpos_hi = lax.broadcasted_iota(jnp.int32, (rows, BIAS_KEYS), 1) + (BAND - BIAS_KEYS)
    neg = jnp.finfo(F32).min
    groups = range(HEADS // ATT_GROUP)
    lanes = [slice(g * gw, (g + 1) * gw) for g in groups]

    def chunk_pair(i, carry, masked):
        units = [(j, g) for j in range(ATT_CHUNKS) for g in groups]
        ids = range(len(units))
        starts = [pl.multiple_of((i * ATT_CHUNKS + j) * CHUNK, CHUNK) for j in range(ATT_CHUNKS)]
        kb = [kpad_ref[pl.ds(st, BAND), :] for st in starts]
        vb = [vpad_ref[pl.ds(st, BAND), :] for st in starts]
        q = [q_ref[0, pl.ds(st, CHUNK), :] * (HEAD_DIM ** -0.5) for st in starts]
        qrows = [jnp.where(own, jnp.concatenate([q[j][:, lanes[g]]] * ATT_GROUP, axis=0),
                           jnp.zeros((), BF16)) for j, g in units]
        s = [_dot_nt(qrows[u], kb[j][:, lanes[g]]) for u, (j, g) in enumerate(units)]
        s_lo = [s[u][:, 0:BAND - BIAS_KEYS] for u in ids]
        s_hi = [s[u][:, BAND - BIAS_KEYS:BAND] + bias_ref[g * rows:(g + 1) * rows, :]
                for u, (j, g) in enumerate(units)]
        if masked:
            first = [(LEFT_CHUNKS - (i * ATT_CHUNKS + j)) * CHUNK for j in range(ATT_CHUNKS)]
            s_lo = [jnp.where(kpos_lo >= first[j], s_lo[u], neg) for u, (j, g) in enumerate(units)]
            s_hi = [jnp.where(kpos_hi >= first[j], s_hi[u], neg) for u, (j, g) in enumerate(units)]
        m = [jnp.maximum(jnp.max(s_lo[u], axis=-1, keepdims=True),
                         jnp.max(s_hi[u], axis=-1, keepdims=True)) for u in ids]
        p_lo = [jnp.exp(s_lo[u] - m[u]) for u in ids]
        p_hi = [jnp.exp(s_hi[u] - m[u]) for u in ids]
        denom = [jnp.sum(p_lo[u], axis=-1, keepdims=True) + jnp.sum(p_hi[u], axis=-1, keepdims=True)
                 for u in ids]
        o_all = [(_dot(p_lo[u].astype(BF16), vb[j][0:BAND - BIAS_KEYS, lanes[g]])
                  + _dot(p_hi[u].astype(BF16), vb[j][BAND - BIAS_KEYS:BAND, lanes[g]])) / denom[u]
                 for u, (j, g) in enumerate(units)]
        for u, (j, g) in enumerate(units):
            o_own = jnp.where(own, o_all[u], 0.0)
            o = o_own[0:CHUNK]
            for h in range(1, ATT_GROUP):
                o = o + o_own[h * CHUNK:(h + 1) * CHUNK]
            o_ref[0, pl.ds(starts[j], CHUNK), lanes[g]] = o.astype(BF16)
        return carry

    n_trips = seq // (CHUNK * ATT_CHUNKS)
    n_masked = min(LEFT_CHUNKS // ATT_CHUNKS, n_trips)
    lax.fori_loop(0, n_masked, functools.partial(chunk_pair, masked=True), 0)
    lax.fori_loop(n_masked, n_trips, functools.partial(chunk_pair, masked=False), 0)


def _band_attn(q, k, v, bias):
    b, seq, _ = q.shape
    whole = pl.BlockSpec((1, seq, WIDTH), lambda i: (i, 0, 0))
    return pl.pallas_call(
        _band_attn_kernel,
        grid=(b,),
        in_specs=[whole, whole, whole, pl.BlockSpec((HEADS * CHUNK, BIAS_KEYS), lambda i: (0, 0))],
        out_specs=whole,
        out_shape=jax.ShapeDtypeStruct((b, seq, WIDTH), BF16),
        scratch_shapes=[pltpu.VMEM((seq + LEFT_CHUNKS * CHUNK, WIDTH), BF16),
                        pltpu.VMEM((seq + LEFT_CHUNKS * CHUNK, WIDTH), BF16)],
        compiler_params=pltpu.CompilerParams(
            dimension_semantics=("arbitrary",), vmem_limit_bytes=VMEM_LIMIT),
        name="band_attn",
    )(q, k, v, bias)


def _rwkv_kernel(rw_ref, w2a2_ref, g2_ref, w0_ref, a0_ref, kk_ref, ka_ref, rk_ref,
                 gng_ref, gnb_ref, y_ref, state_ref):
    c = pl.program_id(1)
    t = CHUNK
    nb = rw_ref.shape[0]
    rows = nb * t

    @pl.when(c == 0)
    def _():
        state_ref[...] = jnp.zeros(state_ref.shape, F32)

    rw = rw_ref[...].reshape(rows, RWKV_PROJ)
    r = rw[:, 0:WIDTH]
    k = rw[:, WIDTH:2 * WIDTH]
    v = rw[:, 2 * WIDTH:3 * WIDTH]
    lora = rw[:, 3 * WIDTH:3 * WIDTH + DECAY_LORA + AAA_LORA]
    g_lo = rw[:, 3 * WIDTH + DECAY_LORA + AAA_LORA:RWKV_PROJ]

    lane128 = lax.broadcasted_iota(jnp.int32, lora.shape, 1)
    lora = jnp.where(lane128 < DECAY_LORA, jnp.tanh(lora), lora)
    wa = _mm(lora, w2a2_ref[...])
    log_decay = -math.exp(-0.5) * _sigmoid(w0_ref[...] + wa[:, 0:WIDTH])
    lr = _sigmoid(a0_ref[...] + wa[:, WIDTH:2 * WIDTH])
    gate = _mm(_sigmoid(g_lo), g2_ref[...])

    kk_raw = k * kk_ref[...]
    k_mod = k * (1.0 + (lr - 1.0) * ka_ref[...])

    row = lax.broadcasted_iota(jnp.int32, (rows, rows), 0)
    col = lax.broadcasted_iota(jnp.int32, (rows, rows), 1)
    tri = jnp.where((row >= col) & ((row // t) == (col // t)), 1.0, 0.0).astype(BF16)
    ld1 = log_decay.astype(BF16)
    rem = log_decay - ld1.astype(F32)
    ld2 = rem.astype(BF16)
    ld3 = (rem - ld2.astype(F32)).astype(BF16)
    parts = _dot(tri, jnp.concatenate([ld1, ld2, ld3], axis=1))
    logp = parts[:, 0:WIDTH] + parts[:, WIDTH:2 * WIDTH] + parts[:, 2 * WIDTH:3 * WIDTH]
    p_in = jnp.exp(logp)
    p_ex = jnp.exp(logp - log_decay)
    p_inv = jnp.exp(-logp)

    kk = kk_raw / jnp.maximum(jnp.sqrt(_head_sums(kk_raw * kk_raw)), 1e-12)
    a_hat = -kk * p_ex
    r_hat = r * p_in
    b_hat = kk * lr * p_inv
    k_hat = k_mod * p_inv
    bonus = _head_sums(r * k_mod * rk_ref[...]) * v

    lane = lax.broadcasted_iota(jnp.int32, (t, 2 * HEAD_DIM), 1)
    low = lane < HEAD_DIM
    r2 = lax.broadcasted_iota(jnp.int32, (2 * t, 2 * HEAD_DIM), 0)
    c2 = lax.broadcasted_iota(jnp.int32, (2 * t, 2 * HEAD_DIM), 1)
    own = (r2 < t) == (c2 < HEAD_DIM)
    strict = (r2 & (t - 1)) > (c2 & (t - 1))
    incl = (r2 & (t - 1)) >= (c2 & (t - 1))
    eye = jnp.where(r2 == c2, 1.0, 0.0).astype(F32)

    def stack2(xp):
        return jnp.concatenate([jnp.where(low, xp, 0.0), jnp.where(low, 0.0, xp)], axis=0)

    chains = [(b, p) for b in range(nb) for p in range(HEADS // 2)]
    ids = range(len(chains))
    rs = [slice(b * t, (b + 1) * t) for b, _ in chains]
    ls = [slice(2 * HEAD_DIM * p, 2 * HEAD_DIM * (p + 1)) for _, p in chains]
    p_end = [p_in[(b + 1) * t - 1:(b + 1) * t, ls[i]] for i, (b, _) in enumerate(chains)]
    ar = [jnp.concatenate([stack2(a_hat[rs[i], ls[i]]), stack2(r_hat[rs[i], ls[i]])], axis=0).astype(BF16)
          for i in ids]
    bk2 = [(stack2(b_hat[rs[i], ls[i]]), stack2(k_hat[rs[i], ls[i]])) for i in ids]
    bk = [jnp.concatenate(bk2[i], axis=0).astype(BF16) for i in ids]
    btkt = [(jnp.concatenate(bk2[i], axis=0) * p_end[i]).astype(BF16) for i in ids]
    v2 = [stack2(v[rs[i], ls[i]]).astype(BF16) for i in ids]
    g = [_dot_nt(ar[i], bk[i]) for i in ids]
    st = [state_ref[b, p] for b, p in chains]
    ars = [_dot_nt(ar[i], st[i].astype(BF16)) for i in ids]
    l_ab = [jnp.where(strict, g[i][0:2 * t, 0:2 * t], 0.0) for i in ids]
    lm = [jnp.concatenate([jnp.where(strict, g[i][0:2 * t, 2 * t:4 * t], 0.0),
                           jnp.where(incl, g[i][2 * t:4 * t, 2 * t:4 * t], 0.0)], axis=0)
          for i in ids]
    m_rb = [jnp.where(incl, g[i][2 * t:4 * t, 0:2 * t], 0.0).astype(BF16) for i in ids]
    lv = [_mm(lm[i], v2[i]) for i in ids]

    w_inv = [eye + l_ab[i] for i in ids]
    l_pow = l_ab
    for _ in range(int(math.log2(t)) - 1):
        l_pow = [_mm(l_pow[i], l_pow[i]) for i in ids]
        w_inv = [w_inv[i] + _mm(w_inv[i], l_pow[i]) for i in ids]

    z = [_mm(w_inv[i], ars[i][0:2 * t] + lv[i][0:2 * t]) for i in ids]
    y = [ars[i][2 * t:4 * t] + lv[i][2 * t:4 * t] + _mm(m_rb[i], z[i]) for i in ids]
    for i, (b, p) in enumerate(chains):
        zv = jnp.concatenate([z[i].astype(BF16), v2[i]], axis=0)
        state_ref[b, p] = st[i] * p_end[i] + _dot_tn(zv, btkt[i])

    for i, (b, p) in enumerate(chains):
        mean = jnp.sum(y[i], axis=-1, keepdims=True) * (1.0 / HEAD_DIM)
        dev = jnp.where(own, y[i] - mean, 0.0)
        var = jnp.sum(dev * dev, axis=-1, keepdims=True) * (1.0 / HEAD_DIM)
        yn = dev * lax.rsqrt(var + GN_EPS)
        yn = yn[0:t] + yn[t:2 * t]
        out = (yn * gng_ref[:, ls[i]] + gnb_ref[:, ls[i]] + bonus[rs[i], ls[i]]) * gate[rs[i], ls[i]]
        y_ref[b, :, ls[i]] = out.astype(BF16)


def _rwkv(rw, w2a2, g2, w0, a0, k_k, k_a, r_k, gn_g, gn_b):
    b, seq, _ = rw.shape
    nc = seq // CHUNK
    nb = RWKV_SEQS
    const = lambda i, c: (0, 0)
    vec = pl.BlockSpec((1, WIDTH), const)
    return pl.pallas_call(
        _rwkv_kernel,
        grid=(b // nb, nc),
        in_specs=[
            pl.BlockSpec((nb, CHUNK, RWKV_PROJ), lambda i, c: (i, c, 0)),
            pl.BlockSpec((DECAY_LORA + AAA_LORA, 2 * WIDTH), const),
            pl.BlockSpec((GATE_LORA, WIDTH), const),
            vec, vec, vec, vec, vec, vec, vec,
        ],
        out_specs=pl.BlockSpec((nb, CHUNK, WIDTH), lambda i, c: (i, c, 0)),
        out_shape=jax.ShapeDtypeStruct((b, seq, WIDTH), BF16),
        scratch_shapes=[pltpu.VMEM((nb, HEADS // 2, 2 * HEAD_DIM, 2 * HEAD_DIM), F32)],
        compiler_params=pltpu.CompilerParams(
            dimension_semantics=("arbitrary", "arbitrary"), vmem_limit_bytes=VMEM_LIMIT),
        name="rwkv7",
    )(rw, w2a2, g2, w0, a0, k_k, k_a, r_k, gn_g, gn_b)


def _merge_kernel(x_ref, att_ref, rwkv_ref, gate_ref, wa32_ref, wb32_ref, wo32_ref, g_ref,
                  wr_ref, br_ref, x1_ref, h2_ref, comb_ref, cnt_ref, wa_ref, wb_ref, wo_ref):
    @pl.when(pl.program_id(0) == 0)
    def _():
        wa_ref[...] = wa32_ref[...].astype(BF16)
        wb_ref[...] = wb32_ref[...].astype(BF16)
        wo_ref[...] = wo32_ref[...].astype(BF16)

    ga = gate_ref[:, 0:D_MODEL].astype(F32)
    gb = gate_ref[:, D_MODEL:2 * D_MODEL].astype(F32)
    merged = ga * _dot(att_ref[...], wa_ref[...]) + gb * _dot(rwkv_ref[...], wb_ref[...])
    x1 = x_ref[...] + _dot(merged.astype(BF16), wo_ref[...])
    x1_ref[...] = x1
    h2 = x1 * lax.rsqrt(jnp.mean(x1 * x1, axis=-1, keepdims=True) + RMS_EPS) * g_ref[...]
    h2_hi = h2.astype(BF16)
    h2_ref[...] = _pack_bf16(h2)
    h2_lo = (h2 - h2_hi.astype(F32)).astype(BF16)
    hw = _dot(h2_hi, wr_ref[...])
    lw = _dot(h2_lo, wr_ref[:, 0:ROUTER_LANES])
    logits = hw[:, 0:ROUTER_LANES] + (hw[:, ROUTER_LANES:2 * ROUTER_LANES] + lw) + br_ref[...]
    lane_i = lax.broadcasted_iota(jnp.int32, logits.shape, 1)
    lane = lane_i.astype(F32)
    lane_group = (lane_i >> 3).astype(F32)
    neg = jnp.finfo(F32).min
    big = float(ROUTER_LANES)

    def first_argmax(vals, mask):
        vm = jnp.where(mask, vals, neg)
        mx = jnp.max(vm, axis=-1, keepdims=True)
        idx = jnp.min(jnp.where(vm == mx, jnp.where(mask, lane, big), big), axis=-1, keepdims=True)
        return mx, idx

    is_group = (lane_i >= GROUP_LANE0) & (lane_i < GROUP_LANE0 + N_GROUPS)
    g_max, g_lane = first_argmax(logits, is_group)
    g_prob = 1.0 / jnp.sum(jnp.where(is_group, jnp.exp(logits - g_max), 0.0),
                           axis=-1, keepdims=True)
    g_idx = g_lane - GROUP_LANE0
    in_group = lane_group == g_idx
    e1, i1 = first_argmax(logits, in_group)
    e2, i2 = first_argmax(logits, in_group & (lane != i1))
    w2 = jnp.exp(e2 - e1)
    p1 = 1.0 / (1.0 + w2)
    p2 = w2 / (1.0 + w2)
    rec = jnp.where(lane_i == EXPERT1_LANE, i1, jnp.where(lane_i == EXPERT2_LANE, i2, 0.0))
    rec = jnp.where(lane_i == WEIGHT1_LANE, p1 * g_prob, jnp.where(lane_i == WEIGHT2_LANE, p2 * g_prob, rec))
    comb_ref[...] = rec
    counts = jnp.sum(jnp.where(lane == i1, 1.0, 0.0) + jnp.where(lane == i2, 1.0, 0.0),
                     axis=0, keepdims=True)
    cnt_ref[0] = jnp.broadcast_to(counts, (8, ROUTER_LANES)).astype(jnp.int32)


def _merge(x2, att, rwkv, gates, wa, wb, wo, ln_g, wr, br):
    n = x2.shape[0]
    tm = MERGE_TILE
    row = lambda i: (i, 0)
    const = lambda i: (0, 0)
    return pl.pallas_call(
        _merge_kernel,
        grid=(n // tm,),
        in_specs=[
            pl.BlockSpec((tm, D_MODEL), row),
            pl.BlockSpec((tm, WIDTH), row),
            pl.BlockSpec((tm, WIDTH), row),
            pl.BlockSpec((tm, 2 * D_MODEL), row),
            pl.BlockSpec((WIDTH, D_MODEL), const),
            pl.BlockSpec((WIDTH, D_MODEL), const),
            pl.BlockSpec((D_MODEL, D_MODEL), const),
            pl.BlockSpec((1, D_MODEL), const),
            pl.BlockSpec((D_MODEL, 2 * ROUTER_LANES), const),
            pl.BlockSpec((1, ROUTER_LANES), const),
        ],
        out_specs=[
            pl.BlockSpec((tm, D_MODEL), row),
            pl.BlockSpec((tm, PACKED), row),
            pl.BlockSpec((tm, ROUTER_LANES), row),
            pl.BlockSpec((1, 8, ROUTER_LANES), lambda i: (i, 0, 0)),
        ],
        out_shape=[
            jax.ShapeDtypeStruct((n, D_MODEL), F32),
            jax.ShapeDtypeStruct((n, PACKED), jnp.uint32),
            jax.ShapeDtypeStruct((n, ROUTER_LANES), F32),
            jax.ShapeDtypeStruct((n // tm, 8, ROUTER_LANES), jnp.int32),
        ],
        scratch_shapes=[pltpu.VMEM((WIDTH, D_MODEL), BF16), pltpu.VMEM((WIDTH, D_MODEL), BF16),
                        pltpu.VMEM((D_MODEL, D_MODEL), BF16)],
        compiler_params=pltpu.CompilerParams(
            dimension_semantics=("arbitrary",), vmem_limit_bytes=VMEM_LIMIT),
        name="merge",
    )(x2, att, rwkv, gates, wa, wb, wo, ln_g, wr, br)


def _moe_plan_kernel(cnt_ref, base_ref, blk_ref):
    nt = cnt_ref.shape[0]
    cnt = cnt_ref[...].astype(F32)
    lane = lax.broadcasted_iota(jnp.int32, (8, ROUTER_LANES), 1)
    total = jnp.broadcast_to(jnp.sum(cnt, axis=0, keepdims=True), (8, ROUTER_LANES))
    padded = jnp.floor((total + (MOE_BLOCK - 1)) * (1.0 / MOE_BLOCK)) * MOE_BLOCK
    r = lax.broadcasted_iota(jnp.int32, (ROUTER_LANES, ROUTER_LANES), 0)
    c = lax.broadcasted_iota(jnp.int32, (ROUTER_LANES, ROUTER_LANES), 1)
    seg_start = _dot_hi(padded, jnp.where(r < c, 1.0, 0.0).astype(F32))
    tr = lax.broadcasted_iota(jnp.int32, (nt, nt), 0)
    tc = lax.broadcasted_iota(jnp.int32, (nt, nt), 1)
    tile_off = _dot_hi(jnp.where(tc < tr, 1.0, 0.0).astype(F32), cnt)
    base_ref[...] = seg_start[0:1, :] + tile_off
    seg_end = (seg_start + padded)[0:1, :]
    rows_total = jnp.sum(jnp.where(lane[0:1, :] < N_EXPERTS, padded[0:1, :], 0.0), axis=-1, keepdims=True)
    nblk = blk_ref.shape[0]
    blk_row = lax.broadcasted_iota(jnp.int32, (nblk, ROUTER_LANES), 0).astype(F32) * MOE_BLOCK
    blk_lane = lax.broadcasted_iota(jnp.int32, (nblk, ROUTER_LANES), 1)
    done = jnp.where((seg_end <= blk_row) & (blk_lane < N_EXPERTS), 1.0, 0.0)
    expert = jnp.minimum(jnp.sum(done, axis=-1, keepdims=True), N_EXPERTS - 1.0)
    valid = jnp.where(blk_row < rows_total, 1.0, 0.0)
    blk_ref[...] = jnp.where(blk_lane == 0, expert, jnp.where(blk_lane == 1, valid, 0.0)).astype(jnp.int32)


def _moe_plan(cnt, n_blocks):
    nt = cnt.shape[0]
    return pl.pallas_call(
        _moe_plan_kernel,
        out_shape=[jax.ShapeDtypeStruct((nt, ROUTER_LANES), F32),
                   jax.ShapeDtypeStruct((n_blocks, ROUTER_LANES), jnp.int32)],
        name="moe_plan",
    )(cnt)


def _moe_pos_kernel(comb_ref, base_ref, pos_ref):
    t = comb_ref.shape[0]
    comb = comb_ref[...]
    lane_i = lax.broadcasted_iota(jnp.int32, (t, ROUTER_LANES), 1)
    lane = lane_i.astype(F32)
    pick1 = jnp.where(lane == comb[:, EXPERT1_LANE:EXPERT1_LANE + 1], 1.0, 0.0)
    pick2 = jnp.where(lane == comb[:, EXPERT2_LANE:EXPERT2_LANE + 1], 1.0, 0.0)
    rows = lax.broadcasted_iota(jnp.int32, (t, t), 0)
    cols = lax.broadcasted_iota(jnp.int32, (t, t), 1)
    earlier = jnp.where(cols < rows, 1.0, 0.0).astype(BF16)
    before1 = _dot(earlier, pick1.astype(BF16))
    before2 = _dot(earlier, pick2.astype(BF16))
    base = base_ref[0]
    firsts = jnp.sum(pick1, axis=0, keepdims=True)
    pos1 = jnp.sum(pick1 * (base + before1), axis=-1, keepdims=True)
    pos2 = jnp.sum(pick2 * (base + firsts + before2), axis=-1, keepdims=True)
    both = jnp.where(lane_i == 0, pos1, jnp.where(lane_i == 1, pos2, 0.0))
    pos_ref[...] = jnp.transpose(both)[0:8, :].astype(jnp.int32)


def _moe_pos(comb, base):
    n = comb.shape[0]
    t = MERGE_TILE
    return pl.pallas_call(
        _moe_pos_kernel,
        grid=(n // t,),
        in_specs=[pl.BlockSpec((t, ROUTER_LANES), lambda i: (i, 0)),
                  pl.BlockSpec((1, 1, ROUTER_LANES), lambda i: (i, 0, 0))],
        out_specs=pl.BlockSpec((8, t), lambda i: (0, i)),
        out_shape=jax.ShapeDtypeStruct((8, n), jnp.int32),
        compiler_params=pltpu.CompilerParams(dimension_semantics=("arbitrary",)),
        name="moe_pos",
    )(comb, base.reshape(n // t, 1, ROUTER_LANES))


def _sc_scratch():
    half = SC_WINDOW // 2
    return [pltpu.VMEM((8, SC_WINDOW), jnp.int32),
            pltpu.VMEM((half, PACKED), jnp.uint32), pltpu.VMEM((half, PACKED), jnp.uint32),
            pltpu.SemaphoreType.DMA((2,)), pltpu.SemaphoreType.DMA((2,))]


def _sc_scatter_rows(h, pos, n_rows):
    n = h.shape[0]
    per_worker = n // (SC_WINDOW * SC_WORKERS)
    half = SC_WINDOW // 2
    mesh = plsc.VectorSubcoreMesh(core_axis_name="core", subcore_axis_name="subcore")

    @pl.kernel(out_type=jax.ShapeDtypeStruct((n_rows, PACKED), jnp.uint32), mesh=mesh,
               scratch_types=_sc_scratch())
    def scatter(h_hbm, pos_hbm, out_hbm, idx, buf0, buf1, sem_in, sem_out):
        worker = lax.axis_index("core") * (SC_WORKERS // 2) + lax.axis_index("subcore")
        bufs = (buf0, buf1)

        @pl.loop(0, per_worker)
        def _(b):
            start = (worker * per_worker + b) * SC_WINDOW
            pltpu.sync_copy(pos_hbm.at[:, pl.ds(start, SC_WINDOW)], idx)
            loads = [pltpu.async_copy(h_hbm.at[pl.ds(start + s * half, half)], bufs[s], sem_in.at[s])
                     for s in range(2)]
            sends = []
            for s in range(2):
                loads[s].wait()
                for k in range(2):
                    sends.append(pltpu.async_copy(
                        bufs[s], out_hbm.at[idx.at[k, pl.ds(s * half, half)]], sem_out.at[s]))
            for send in sends:
                send.wait()

    return scatter(h, pos)


def _sc_gather_rows(y, pos, first, n):
    per_worker = n // (SC_WINDOW * SC_WORKERS)
    half = SC_WINDOW // 2
    mesh = plsc.VectorSubcoreMesh(core_axis_name="core", subcore_axis_name="subcore")
    out = jax.ShapeDtypeStruct((n, PACKED), jnp.uint32)

    @pl.kernel(out_type=(out, out), mesh=mesh, scratch_types=_sc_scratch())
    def gather(y_hbm, pos_hbm, o1_hbm, o2_hbm, idx, buf0, buf1, sem_in, sem_out):
        worker = lax.axis_index("core") * (SC_WORKERS // 2) + lax.axis_index("subcore")
        bufs = (buf0, buf1)
        outs = (o1_hbm, o2_hbm)

        @pl.loop(0, per_worker)
        def _(b):
            start = (worker * per_worker + b) * SC_WINDOW
            pltpu.sync_copy(pos_hbm.at[:, pl.ds(first + start, SC_WINDOW)], idx)
            for s in range(2):
                loads = [pltpu.async_copy(y_hbm.at[idx.at[k, pl.ds(s * half, half)]], bufs[k], sem_in.at[k])
                         for k in range(2)]
                stores = []
                for k in range(2):
                    loads[k].wait()
                    stores.append(pltpu.async_copy(
                        bufs[k], outs[k].at[pl.ds(start + s * half, half)], sem_out.at[k]))
                for store in stores:
                    store.wait()

    return gather(y, pos)


def _expert_mlp_kernel(expert_ref, valid_ref, hs_ref, *refs):
    step = pl.program_id(0)
    w_f32 = refs[0:3 * MLP_SUB]
    y_ref = refs[3 * MLP_SUB]
    w_bf16 = refs[3 * MLP_SUB + 1:]
    first = step * MLP_SUB

    @pl.when(valid_ref[first] == 0)
    def _():
        y_ref[...] = jnp.zeros(y_ref.shape, jnp.uint32)

    @pl.when(valid_ref[first] != 0)
    def _():
        for j in range(MLP_SUB):
            prev = jnp.maximum(first + j - MLP_SUB, 0)
            stale = (step == 0) | (expert_ref[first + j] != expert_ref[prev])

            @pl.when(stale)
            def _():
                for k in range(3):
                    w_bf16[3 * j + k][...] = w_f32[3 * j + k][0].astype(BF16)

        half = D_MODEL // 2
        subs = range(MLP_SUB)
        parts = [_unpack_bf16(hs_ref[j * MOE_BLOCK:(j + 1) * MOE_BLOCK, :]) for j in subs]
        h_lo = [parts[j][0].astype(BF16) for j in subs]
        h_hi = [parts[j][1].astype(BF16) for j in subs]
        hg = [_dot(h_lo[j], w_bf16[3 * j][0:half, :]) + _dot(h_hi[j], w_bf16[3 * j][half:D_MODEL, :])
              for j in subs]
        hu = [_dot(h_lo[j], w_bf16[3 * j + 1][0:half, :]) + _dot(h_hi[j], w_bf16[3 * j + 1][half:D_MODEL, :])
              for j in subs]
        act = [(hg[j] * _sigmoid(hg[j]) * hu[j]).astype(BF16) for j in subs]
        for j in subs:
            y_ref[j * MOE_BLOCK:(j + 1) * MOE_BLOCK, :] = _pack_bf16(_dot(act[j], w_bf16[3 * j + 2][...]))


def _expert_mlp(blk_expert, blk_valid, hs, wg, wu, wd):
    n_steps = hs.shape[0] // (MOE_BLOCK * MLP_SUB)
    rows = lambda s, expert, valid: (s, 0)
    w_specs, w_args, w_scratch = [], [], []
    for j in range(MLP_SUB):
        by_expert = lambda s, expert, valid, j=j: (expert[s * MLP_SUB + j], 0, 0)
        w_specs += [pl.BlockSpec((1, D_MODEL, D_EXPERT), by_expert),
                    pl.BlockSpec((1, D_MODEL, D_EXPERT), by_expert),
                    pl.BlockSpec((1, D_EXPERT, D_MODEL), by_expert)]
        w_args += [wg, wu, wd]
        w_scratch += [pltpu.VMEM((D_MODEL, D_EXPERT), BF16), pltpu.VMEM((D_MODEL, D_EXPERT), BF16),
                      pltpu.VMEM((D_EXPERT, D_MODEL), BF16)]
    grid_spec = pltpu.PrefetchScalarGridSpec(
        num_scalar_prefetch=2,
        grid=(n_steps,),
        in_specs=[pl.BlockSpec((MOE_BLOCK * MLP_SUB, PACKED), rows)] + w_specs,
        out_specs=pl.BlockSpec((MOE_BLOCK * MLP_SUB, PACKED), rows),
        scratch_shapes=w_scratch,
    )
    return pl.pallas_call(
        _expert_mlp_kernel,
        grid_spec=grid_spec,
        out_shape=jax.ShapeDtypeStruct((hs.shape[0], PACKED), jnp.uint32),
        compiler_params=pltpu.CompilerParams(
            dimension_semantics=("arbitrary",), vmem_limit_bytes=VMEM_LIMIT),
        name="expert_mlp",
    )(blk_expert, blk_valid, hs, *w_args)


def _moe_out_kernel(x1_ref, y1_ref, y2_ref, comb_ref, lnf_ref, *rest, final_norm):
    out_ref = rest[-1]
    w1 = comb_ref[:, WEIGHT1_LANE:WEIGHT1_LANE + 1]
    w2 = comb_ref[:, WEIGHT2_LANE:WEIGHT2_LANE + 1]
    a_lo, a_hi = _unpack_bf16(y1_ref[...])
    b_lo, b_hi = _unpack_bf16(y2_ref[...])
    moe = jnp.concatenate([w1 * a_lo + w2 * b_lo, w1 * a_hi + w2 * b_hi], axis=1)
    y = x1_ref[...] + moe
    if final_norm:
        y = y * lax.rsqrt(jnp.mean(y * y, axis=-1, keepdims=True) + RMS_EPS) * lnf_ref[...]
    out_ref[...] = y


def _moe_out(x1, y1, y2, comb, lnf, final_norm, first, out_so_far):
    n = x1.shape[0]
    t = FINAL_TILE
    off = first // t
    here = lambda i: (i, 0)
    there = lambda i: (i + off, 0)
    in_specs = [pl.BlockSpec((t, D_MODEL), there), pl.BlockSpec((t, PACKED), here),
                pl.BlockSpec((t, PACKED), here), pl.BlockSpec((t, ROUTER_LANES), there),
                pl.BlockSpec((1, D_MODEL), lambda i: (0, 0))]
    args = [x1, y1, y2, comb, lnf]
    aliases = {}
    if out_so_far is not None:
        in_specs.append(pl.BlockSpec(memory_space=pl.ANY))
        args.append(out_so_far)
        aliases = {len(args) - 1: 0}
    return pl.pallas_call(
        functools.partial(_moe_out_kernel, final_norm=final_norm),
        grid=(y1.shape[0] // t,),
        in_specs=in_specs,
        out_specs=pl.BlockSpec((t, D_MODEL), there),
        out_shape=jax.ShapeDtypeStruct((n, D_MODEL), F32),
        input_output_aliases=aliases,
        compiler_params=pltpu.CompilerParams(
            dimension_semantics=("arbitrary",), vmem_limit_bytes=VMEM_LIMIT),
        name="moe_out",
    )(*args)


def _moe(counts, x1, h2p, comb, wg, wu, wd, lnf, final_norm):
    n = x1.shape[0]
    n_blocks = (2 * n) // MOE_BLOCK + N_EXPERTS
    base, blk = _moe_plan(counts[:, 0, :], n_blocks)
    pos = _moe_pos(comb, base)
    hs = _sc_scatter_rows(h2p, pos, n_blocks * MOE_BLOCK)
    ys = _expert_mlp(blk[:, 0], blk[:, 1], hs, wg, wu, wd)
    out = None
    part = n // MOE_PARTS
    for p in range(MOE_PARTS):
        y1, y2 = _sc_gather_rows(ys, pos, p * part, part)
        out = _moe_out(x1, y1, y2, comb, lnf, final_norm, p * part, out)
    return out


def kernel(x, ln_mix_g, w_in, att_rel_bias, rwkv_mu, rwkv_w0, rwkv_w2, rwkv_a0, rwkv_a2, rwkv_g2,
           rwkv_k_k, rwkv_k_a, rwkv_r_k, rwkv_gn_g, rwkv_gn_b, w_branch_att, w_branch_rwkv, w_out,
           ln_ffn_g, router_group_w, router_group_b, router_expert_w, router_expert_b,
           expert_w_gate, expert_w_up, expert_w_down, ln_final_g):
    bsz, seq, d = x.shape
    depth = w_in.shape[0]
    n = bsz * seq
    x2 = x.reshape(n, d)
    for l in range(depth):
        q, k, v, rw, gates = _in_proj(x2, ln_mix_g[l][None, :], w_in[l].astype(BF16),
                                      rwkv_mu[l][None, :], seq)
        bias = _rel_bias(att_rel_bias[l])
        att = _band_attn(q.reshape(bsz, seq, WIDTH), k.reshape(bsz, seq, WIDTH),
                         v.reshape(bsz, seq, WIDTH), bias)
        zeros = jnp.zeros((DECAY_LORA, WIDTH), F32)
        w2a2 = jnp.concatenate(
            [jnp.concatenate([rwkv_w2[l], zeros], axis=1),
             jnp.concatenate([zeros, rwkv_a2[l]], axis=1)], axis=0)
        rwkv = _rwkv(rw.reshape(bsz, seq, RWKV_PROJ), w2a2.astype(BF16), rwkv_g2[l].astype(BF16),
                     rwkv_w0[l][None, :], rwkv_a0[l][None, :], rwkv_k_k[l][None, :],
                     rwkv_k_a[l][None, :], rwkv_r_k[l].reshape(1, WIDTH),
                     rwkv_gn_g[l][None, :], rwkv_gn_b[l][None, :])
        wr = jnp.concatenate([router_expert_w[l], router_group_w[l]], axis=1)
        wr = jnp.pad(wr, ((0, 0), (0, ROUTER_LANES - wr.shape[1])))
        wr_hi = wr.astype(BF16)
        wr = jnp.concatenate([wr_hi, (wr - wr_hi.astype(F32)).astype(BF16)], axis=1)
        br = jnp.concatenate([router_expert_b[l], router_group_b[l]])
        br = jnp.pad(br, (0, ROUTER_LANES - br.shape[0]))[None, :]
        x1, h2, comb, counts = _merge(x2, att.reshape(n, WIDTH), rwkv.reshape(n, WIDTH), gates,
                                      w_branch_att[l], w_branch_rwkv[l], w_out[l],
                                      ln_ffn_g[l][None, :], wr, br)
        x2 = _moe(counts, x1, h2, comb, expert_w_gate[l], expert_w_up[l], expert_w_down[l],
                  ln_final_g[None, :], final_norm=(l == depth - 1))
    return x2.reshape(bsz, seq, d)
```

```python
import functools
import math

import jax
import jax.numpy as jnp
from jax import lax
from jax.experimental import pallas as pl
from jax.experimental.pallas import tpu as pltpu
from jax.experimental.pallas import tpu_sc as plsc

F32 = jnp.float32
BF16 = jnp.bfloat16
HIGHEST = lax.Precision.HIGHEST

D_MODEL = 1024
CHUNK = 64
HEADS = 8
HEAD_DIM = 64
WIDTH = HEADS * HEAD_DIM
LEFT_CHUNKS = 8
BAND = (LEFT_CHUNKS + 1) * CHUNK
REL_CLIP = 64
N_REL = 2 * REL_CLIP + 1
DECAY_LORA = 64
AAA_LORA = 64
GATE_LORA = 128
GN_EPS = 64e-5
RMS_EPS = 1e-6
ATT_PROJ = 3 * WIDTH
RWKV_PROJ = 3 * WIDTH + DECAY_LORA + AAA_LORA + GATE_LORA
D_IN = ATT_PROJ + RWKV_PROJ + 2 * D_MODEL
N_GROUPS = 4
EXPERTS_PER_GROUP = 8
N_EXPERTS = N_GROUPS * EXPERTS_PER_GROUP
D_EXPERT = 256
RWKV_SEQS = 4
BIAS_KEYS = 192
ATT_CHUNKS = 4
ATT_GROUP = 4
ROUTER_LANES = 128
GROUP_LANE0 = N_EXPERTS
EXPERT1_LANE, EXPERT2_LANE, WEIGHT1_LANE, WEIGHT2_LANE = 126, 125, 124, 123
MERGE_TILE = 512
MOE_BLOCK = 512
PACKED = D_MODEL // 2
SC_WINDOW = 128
SC_CORES, SC_SUBCORES = 2, 16
SC_WORKERS = SC_CORES * SC_SUBCORES
FINAL_TILE = 1024

V7X_VMEM_BYTES = 64 * 1024 * 1024
VMEM_LIMIT = V7X_VMEM_BYTES - 12 * 1024 * 1024


def _dot(a, b):
    return jnp.dot(a, b, preferred_element_type=F32)


def _dot_hi(a, b):
    return jnp.dot(a, b, preferred_element_type=F32, precision=HIGHEST)


def _dot_nt(a, b, precision=None):
    return lax.dot_general(a, b, (((1,), (1,)), ((), ())),
                           preferred_element_type=F32, precision=precision)


def _dot_tn(a, b, precision=None):
    return lax.dot_general(a, b, (((0,), (0,)), ((), ())),
                           preferred_element_type=F32, precision=precision)


def _sigmoid(x):
    return 1.0 / (1.0 + jnp.exp(-x))


def _pack_bf16(x):
    w = x.shape[1] // 2
    hi = pltpu.bitcast(x[:, :w].astype(BF16).astype(F32), jnp.uint32)
    lo = pltpu.bitcast(x[:, w:].astype(BF16).astype(F32), jnp.uint32)
    return hi | lax.shift_right_logical(lo, jnp.uint32(16))


def _unpack_bf16(p):
    hi = pltpu.bitcast(p & jnp.uint32(0xFFFF0000), F32)
    lo = pltpu.bitcast(lax.shift_left(p, jnp.uint32(16)), F32)
    return hi, lo


def _mm(a, b):
    return jnp.dot(a.astype(BF16), b.astype(BF16), preferred_element_type=F32)


def _head_sums(x):
    outs = []
    lane = lax.broadcasted_iota(jnp.int32, (x.shape[0], 2 * HEAD_DIM), 1)
    low = lane < HEAD_DIM
    for p in range(HEADS // 2):
        xp = x[:, 2 * HEAD_DIM * p:2 * HEAD_DIM * (p + 1)]
        s_lo = jnp.sum(jnp.where(low, xp, 0.0), axis=-1, keepdims=True)
        s_hi = jnp.sum(jnp.where(low, 0.0, xp), axis=-1, keepdims=True)
        outs.append(jnp.where(low, s_lo, s_hi))
    return jnp.concatenate(outs, axis=-1)


def _rel_bias_kernel(tab_ref, out_ref):
    rows = tab_ref.shape[1]
    n = lax.broadcasted_iota(jnp.int32, (rows, CHUNK * 128), 1)
    r = lax.broadcasted_iota(jnp.int32, (rows, CHUNK * 128), 0)
    q = n >> 7
    kk = n & 127
    idx = jnp.clip(CHUNK + q - kk, -REL_CLIP, REL_CLIP) + REL_CLIP
    pick = jnp.where(r == idx, 1.0, 0.0) - jnp.where(r == N_REL - 1, 1.0, 0.0)
    out_ref[...] = _dot_hi(tab_ref[...], pick.astype(F32))


def _rel_bias(rel_table):
    rows = 136
    tab = jnp.pad(rel_table.astype(F32), ((0, 0), (0, rows - N_REL)))
    tail = pl.pallas_call(
        _rel_bias_kernel,
        out_shape=jax.ShapeDtypeStruct((HEADS, CHUNK * 128), F32),
        name="rel_bias",
    )(tab)
    tail = tail.reshape(HEADS, CHUNK, 128)
    bias = jnp.concatenate([jnp.zeros((HEADS, CHUNK, BIAS_KEYS - 128), F32), tail], axis=-1)
    return bias.reshape(HEADS * CHUNK, BIAS_KEYS)


def _in_proj_kernel(x_ref, g_ref, w_ref, mu_ref, q_ref, k_ref, v_ref, rw_ref, gate_ref,
                    carry_ref, *, tiles_per_seq):
    i = pl.program_id(0)

    @pl.when(i == 0)
    def _():
        carry_ref[...] = jnp.zeros(carry_ref.shape, F32)

    x = x_ref[...]
    h = x * lax.rsqrt(jnp.mean(x * x, axis=-1, keepdims=True) + RMS_EPS) * g_ref[...]
    hb = h.astype(BF16)
    q_ref[...] = _dot(hb, w_ref[:, 0:WIDTH]).astype(BF16)
    k_ref[...] = _dot(hb, w_ref[:, WIDTH:2 * WIDTH]).astype(BF16)
    v_ref[...] = _dot(hb, w_ref[:, 2 * WIDTH:ATT_PROJ]).astype(BF16)
    rw = _dot(hb, w_ref[:, ATT_PROJ:ATT_PROJ + RWKV_PROJ])
    tm = rw.shape[0]
    first_prev = jnp.where(i % tiles_per_seq == 0, 0.0, carry_ref[0:1, :])
    rolled = pltpu.roll(rw, 1, axis=0)
    row = lax.broadcasted_iota(jnp.int32, rw.shape, 0)
    prev = jnp.where(row == 0, first_prev, rolled)
    carry_ref[0:1, :] = rw[tm - 1:tm, :]
    rw_ref[...] = rw + (prev - rw) * mu_ref[...]
    gate_ref[...] = _sigmoid(_dot(hb, w_ref[:, ATT_PROJ + RWKV_PROJ:D_IN])).astype(BF16)


def _in_proj(x2, ln_g, w_in_b, mu, seq):
    n = x2.shape[0]
    tm = 512
    row = lambda i: (i, 0)
    const = lambda i: (0, 0)
    return pl.pallas_call(
        functools.partial(_in_proj_kernel, tiles_per_seq=seq // tm),
        grid=(n // tm,),
        in_specs=[
            pl.BlockSpec((tm, D_MODEL), row),
            pl.BlockSpec((1, D_MODEL), const),
            pl.BlockSpec((D_MODEL, D_IN), const),
            pl.BlockSpec((1, RWKV_PROJ), const),
        ],
        out_specs=[
            pl.BlockSpec((tm, WIDTH), row),
            pl.BlockSpec((tm, WIDTH), row),
            pl.BlockSpec((tm, WIDTH), row),
            pl.BlockSpec((tm, RWKV_PROJ), row),
            pl.BlockSpec((tm, 2 * D_MODEL), row),
        ],
        out_shape=[
            jax.ShapeDtypeStruct((n, WIDTH), BF16),
            jax.ShapeDtypeStruct((n, WIDTH), BF16),
            jax.ShapeDtypeStruct((n, WIDTH), BF16),
            jax.ShapeDtypeStruct((n, RWKV_PROJ), F32),
            jax.ShapeDtypeStruct((n, 2 * D_MODEL), BF16),
        ],
        scratch_shapes=[pltpu.VMEM((8, RWKV_PROJ), F32)],
        compiler_params=pltpu.CompilerParams(
            dimension_semantics=("arbitrary",), vmem_limit_bytes=VMEM_LIMIT),
        name="in_proj",
    )(x2, ln_g, w_in_b, mu)


def _band_attn_kernel(q_ref, k_ref, v_ref, bias_ref, o_ref, kpad_ref, vpad_ref):
    seq = k_ref.shape[1]
    pad = LEFT_CHUNKS * CHUNK
    kpad_ref[0:pad, :] = jnp.zeros((pad, WIDTH), BF16)
    vpad_ref[0:pad, :] = jnp.zeros((pad, WIDTH), BF16)
    kpad_ref[pad:pad + seq, :] = k_ref[0]
    vpad_ref[pad:pad + seq, :] = v_ref[0]

    gw = ATT_GROUP * HEAD_DIM
    rows = ATT_GROUP * CHUNK
    r_head = lax.broadcasted_iota(jnp.int32, (rows, gw), 0) // CHUNK
    l_head = lax.broadcasted_iota(jnp.int32, (rows, gw), 1) // HEAD_DIM
    own = r_head == l_head
    kpos_lo = lax.broadcasted_iota(jnp.int32, (rows, BAND - BIAS_KEYS), 1)
    kpos_hi = lax.broadcasted_iota(jnp.int32, (rows, BIAS_KEYS), 1) + (BAND - BIAS_KEYS)
    neg = jnp.finfo(F32).min
    groups = range(HEADS // ATT_GROUP)
    lanes = [slice(g * gw, (g + 1) * gw) for g in groups]

    def chunk_pair(i, carry, masked):
        units = [(j, g) for j in range(ATT_CHUNKS) for g in groups]
        ids = range(len(units))
        starts = [pl.multiple_of((i * ATT_CHUNKS + j) * CHUNK, CHUNK) for j in range(ATT_CHUNKS)]
        kb = [kpad_ref[pl.ds(st, BAND), :] for st in starts]
        vb = [vpad_ref[pl.ds(st, BAND), :] for st in starts]
        q = [q_ref[0, pl.ds(st, CHUNK), :] * (HEAD_DIM ** -0.5) for st in starts]
        qrows = [jnp.where(own, jnp.concatenate([q[j][:, lanes[g]]] * ATT_GROUP, axis=0),
                           jnp.zeros((), BF16)) for j, g in units]
        s = [_dot_nt(qrows[u], kb[j][:, lanes[g]]) for u, (j, g) in enumerate(units)]
        s_lo = [s[u][:, 0:BAND - BIAS_KEYS] for u in ids]
        s_hi = [s[u][:, BAND - BIAS_KEYS:BAND] + bias_ref[g * rows:(g + 1) * rows, :]
                for u, (j, g) in enumerate(units)]
        if masked:
            first = [(LEFT_CHUNKS - (i * ATT_CHUNKS + j)) * CHUNK for j in range(ATT_CHUNKS)]
            s_lo = [jnp.where(kpos_lo >= first[j], s_lo[u], neg) for u, (j, g) in enumerate(units)]
            s_hi = [jnp.where(kpos_hi >= first[j], s_hi[u], neg) for u, (j, g) in enumerate(units)]
        m = [jnp.maximum(jnp.max(s_lo[u], axis=-1, keepdims=True),
                         jnp.max(s_hi[u], axis=-1, keepdims=True)) for u in ids]
        p_lo = [jnp.exp(s_lo[u] - m[u]) for u in ids]
        p_hi = [jnp.exp(s_hi[u] - m[u]) for u in ids]
        denom = [jnp.sum(p_lo[u], axis=-1, keepdims=True) + jnp.sum(p_hi[u], axis=-1, keepdims=True)
                 for u in ids]
        o_all = [(_dot(p_lo[u].astype(BF16), vb[j][0:BAND - BIAS_KEYS, lanes[g]])
                  + _dot(p_hi[u].astype(BF16), vb[j][BAND - BIAS_KEYS:BAND, lanes[g]])) / denom[u]
                 for u, (j, g) in enumerate(units)]
        for u, (j, g) in enumerate(units):
            o_own = jnp.where(own, o_all[u], 0.0)
            o = o_own[0:CHUNK]
            for h in range(1, ATT_GROUP):
                o = o + o_own[h * CHUNK:(h + 1) * CHUNK]
            o_ref[0, pl.ds(starts[j], CHUNK), lanes[g]] = o.astype(BF16)
        return carry

    n_trips = seq // (CHUNK * ATT_CHUNKS)
    n_masked = min(LEFT_CHUNKS // ATT_CHUNKS, n_trips)
    lax.fori_loop(0, n_masked, functools.partial(chunk_pair, masked=True), 0)
    lax.fori_loop(n_masked, n_trips, functools.partial(chunk_pair, masked=False), 0)


def _band_attn(q, k, v, bias):
    b, seq, _ = q.shape
    whole = pl.BlockSpec((1, seq, WIDTH), lambda i: (i, 0, 0))
    return pl.pallas_call(
        _band_attn_kernel,
        grid=(b,),
        in_specs=[whole, whole, whole, pl.BlockSpec((HEADS * CHUNK, BIAS_KEYS), lambda i: (0, 0))],
        out_specs=whole,
        out_shape=jax.ShapeDtypeStruct((b, seq, WIDTH), BF16),
        scratch_shapes=[pltpu.VMEM((seq + LEFT_CHUNKS * CHUNK, WIDTH), BF16),
                        pltpu.VMEM((seq + LEFT_CHUNKS * CHUNK, WIDTH), BF16)],
        compiler_params=pltpu.CompilerParams(
            dimension_semantics=("arbitrary",), vmem_limit_bytes=VMEM_LIMIT),
        name="band_attn",
    )(q, k, v, bias)


def _rwkv_kernel(rw_ref, w2a2_ref, g2_ref, w0_ref, a0_ref, kk_ref, ka_ref, rk_ref,
                 gng_ref, gnb_ref, y_ref, state_ref):
    c = pl.program_id(1)
    t = CHUNK
    nb = rw_ref.shape[0]
    rows = nb * t

    @pl.when(c == 0)
    def _():
        state_ref[...] = jnp.zeros(state_ref.shape, F32)

    rw = rw_ref[...].reshape(rows, RWKV_PROJ)
    r = rw[:, 0:WIDTH]
    k = rw[:, WIDTH:2 * WIDTH]
    v = rw[:, 2 * WIDTH:3 * WIDTH]
    lora = rw[:, 3 * WIDTH:3 * WIDTH + DECAY_LORA + AAA_LORA]
    g_lo = rw[:, 3 * WIDTH + DECAY_LORA + AAA_LORA:RWKV_PROJ]

    lane128 = lax.broadcasted_iota(jnp.int32, lora.shape, 1)
    lora = jnp.where(lane128 < DECAY_LORA, jnp.tanh(lora), lora)
    wa = _mm(lora, w2a2_ref[...])
    log_decay = -math.exp(-0.5) * _sigmoid(w0_ref[...] + wa[:, 0:WIDTH])
    lr = _sigmoid(a0_ref[...] + wa[:, WIDTH:2 * WIDTH])
    gate = _mm(_sigmoid(g_lo), g2_ref[...])

    kk_raw = k * kk_ref[...]
    k_mod = k * (1.0 + (lr - 1.0) * ka_ref[...])

    row = lax.broadcasted_iota(jnp.int32, (rows, rows), 0)
    col = lax.broadcasted_iota(jnp.int32, (rows, rows), 1)
    tri = jnp.where((row >= col) & ((row // t) == (col // t)), 1.0, 0.0).astype(BF16)
    ld1 = log_decay.astype(BF16)
    rem = log_decay - ld1.astype(F32)
    ld2 = rem.astype(BF16)
    ld3 = (rem - ld2.astype(F32)).astype(BF16)
    parts = _dot(tri, jnp.concatenate([ld1, ld2, ld3], axis=1))
    logp = parts[:, 0:WIDTH] + parts[:, WIDTH:2 * WIDTH] + parts[:, 2 * WIDTH:3 * WIDTH]
    p_in = jnp.exp(logp)
    p_ex = jnp.exp(logp - log_decay)
    p_inv = jnp.exp(-logp)

    kk = kk_raw / jnp.maximum(jnp.sqrt(_head_sums(kk_raw * kk_raw)), 1e-12)
    a_hat = -kk * p_ex
    r_hat = r * p_in
    b_hat = kk * lr * p_inv
    k_hat = k_mod * p_inv
    bonus = _head_sums(r * k_mod * rk_ref[...]) * v

    lane = lax.broadcasted_iota(jnp.int32, (t, 2 * HEAD_DIM), 1)
    low = lane < HEAD_DIM
    r2 = lax.broadcasted_iota(jnp.int32, (2 * t, 2 * HEAD_DIM), 0)
    c2 = lax.broadcasted_iota(jnp.int32, (2 * t, 2 * HEAD_DIM), 1)
    own = (r2 < t) == (c2 < HEAD_DIM)
    strict = (r2 & (t - 1)) > (c2 & (t - 1))
    incl = (r2 & (t - 1)) >= (c2 & (t - 1))
    eye = jnp.where(r2 == c2, 1.0, 0.0).astype(F32)

    def stack2(xp):
        return jnp.concatenate([jnp.where(low, xp, 0.0), jnp.where(low, 0.0, xp)], axis=0)

    chains = [(b, p) for b in range(nb) for p in range(HEADS // 2)]
    ids = range(len(chains))
    rs = [slice(b * t, (b + 1) * t) for b, _ in chains]
    ls = [slice(2 * HEAD_DIM * p, 2 * HEAD_DIM * (p + 1)) for _, p in chains]
    p_end = [p_in[(b + 1) * t - 1:(b + 1) * t, ls[i]] for i, (b, _) in enumerate(chains)]
    ar = [jnp.concatenate([stack2(a_hat[rs[i], ls[i]]), stack2(r_hat[rs[i], ls[i]])], axis=0).astype(BF16)
          for i in ids]
    bk2 = [(stack2(b_hat[rs[i], ls[i]]), stack2(k_hat[rs[i], ls[i]])) for i in ids]
    bk = [jnp.concatenate(bk2[i], axis=0).astype(BF16) for i in ids]
    btkt = [(jnp.concatenate(bk2[i], axis=0) * p_end[i]).astype(BF16) for i in ids]
    v2 = [stack2(v[rs[i], ls[i]]).astype(BF16) for i in ids]
    g = [_dot_nt(ar[i], bk[i]) for i in ids]
    st = [state_ref[b, p] for b, p in chains]
    ars = [_dot_nt(ar[i], st[i].astype(BF16)) for i in ids]
    l_ab = [jnp.where(strict, g[i][0:2 * t, 0:2 * t], 0.0) for i in ids]
    lm = [jnp.concatenate([jnp.where(strict, g[i][0:2 * t, 2 * t:4 * t], 0.0),
                           jnp.where(incl, g[i][2 * t:4 * t, 2 * t:4 * t], 0.0)], axis=0)
          for i in ids]
    m_rb = [jnp.where(incl, g[i][2 * t:4 * t, 0:2 * t], 0.0).astype(BF16) for i in ids]
    lv = [_mm(lm[i], v2[i]) for i in ids]

    w_inv = [eye + l_ab[i] for i in ids]
    l_pow = l_ab
    for _ in range(int(math.log2(t)) - 1):
        l_pow = [_mm(l_pow[i], l_pow[i]) for i in ids]
        w_inv = [w_inv[i] + _mm(w_inv[i], l_pow[i]) for i in ids]

    z = [_mm(w_inv[i], ars[i][0:2 * t] + lv[i][0:2 * t]) for i in ids]
    y = [ars[i][2 * t:4 * t] + lv[i][2 * t:4 * t] + _mm(m_rb[i], z[i]) for i in ids]
    for i, (b, p) in enumerate(chains):
        zv = jnp.concatenate([z[i].astype(BF16), v2[i]], axis=0)
        state_ref[b, p] = st[i] * p_end[i] + _dot_tn(zv, btkt[i])

    for i, (b, p) in enumerate(chains):
        mean = jnp.sum(y[i], axis=-1, keepdims=True) * (1.0 / HEAD_DIM)
        dev = jnp.where(own, y[i] - mean, 0.0)
        var = jnp.sum(dev * dev, axis=-1, keepdims=True) * (1.0 / HEAD_DIM)
        yn = dev * lax.rsqrt(var + GN_EPS)
        yn = yn[0:t] + yn[t:2 * t]
        out = (yn * gng_ref[:, ls[i]] + gnb_ref[:, ls[i]] + bonus[rs[i], ls[i]]) * gate[rs[i], ls[i]]
        y_ref[b, :, ls[i]] = out.astype(BF16)


def _rwkv(rw, w2a2, g2, w0, a0, k_k, k_a, r_k, gn_g, gn_b):
    b, seq, _ = rw.shape
    nc = seq // CHUNK
    nb = RWKV_SEQS
    const = lambda i, c: (0, 0)
    vec = pl.BlockSpec((1, WIDTH), const)
    return pl.pallas_call(
        _rwkv_kernel,
        grid=(b // nb, nc),
        in_specs=[
            pl.BlockSpec((nb, CHUNK, RWKV_PROJ), lambda i, c: (i, c, 0)),
            pl.BlockSpec((DECAY_LORA + AAA_LORA, 2 * WIDTH), const),
            pl.BlockSpec((GATE_LORA, WIDTH), const),
            vec, vec, vec, vec, vec, vec, vec,
        ],
        out_specs=pl.BlockSpec((nb, CHUNK, WIDTH), lambda i, c: (i, c, 0)),
        out_shape=jax.ShapeDtypeStruct((b, seq, WIDTH), BF16),
        scratch_shapes=[pltpu.VMEM((nb, HEADS // 2, 2 * HEAD_DIM, 2 * HEAD_DIM), F32)],
        compiler_params=pltpu.CompilerParams(
            dimension_semantics=("arbitrary", "arbitrary"), vmem_limit_bytes=VMEM_LIMIT),
        name="rwkv7",
    )(rw, w2a2, g2, w0, a0, k_k, k_a, r_k, gn_g, gn_b)


def _merge_kernel(x_ref, att_ref, rwkv_ref, gate_ref, wa32_ref, wb32_ref, wo32_ref, g_ref,
                  wr_ref, br_ref, x1_ref, h2_ref, comb_ref, cnt_ref, wa_ref, wb_ref, wo_ref):
    @pl.when(pl.program_id(0) == 0)
    def _():
        wa_ref[...] = wa32_ref[...].astype(BF16)
        wb_ref[...] = wb32_ref[...].astype(BF16)
        wo_ref[...] = wo32_ref[...].astype(BF16)

    ga = gate_ref[:, 0:D_MODEL].astype(F32)
    gb = gate_ref[:, D_MODEL:2 * D_MODEL].astype(F32)
    merged = ga * _dot(att_ref[...], wa_ref[...]) + gb * _dot(rwkv_ref[...], wb_ref[...])
    x1 = x_ref[...] + _dot(merged.astype(BF16), wo_ref[...])
    x1_ref[...] = x1
    h2 = x1 * lax.rsqrt(jnp.mean(x1 * x1, axis=-1, keepdims=True) + RMS_EPS) * g_ref[...]
    h2_hi = h2.astype(BF16)
    h2_ref[...] = _pack_bf16(h2)
    h2_lo = (h2 - h2_hi.astype(F32)).astype(BF16)
    hw = _dot(h2_hi, wr_ref[...])
    lw = _dot(h2_lo, wr_ref[:, 0:ROUTER_LANES])
    logits = hw[:, 0:ROUTER_LANES] + (hw[:, ROUTER_LANES:2 * ROUTER_LANES] + lw) + br_ref[...]
    rec, counts = _route(logits)
    comb_ref[...] = rec
    cnt_ref[0] = jnp.broadcast_to(counts, (8, ROUTER_LANES)).astype(jnp.int32)


def _route(logits):
    lane_i = lax.broadcasted_iota(jnp.int32, logits.shape, 1)
    lane = lane_i.astype(F32)
    lane_group = (lane_i // EXPERTS_PER_GROUP).astype(F32)
    neg = jnp.finfo(F32).min
    big = float(ROUTER_LANES)

    def first_argmax(vals, mask):
        vm = jnp.where(mask, vals, neg)
        mx = jnp.max(vm, axis=-1, keepdims=True)
        idx = jnp.min(jnp.where(vm == mx, jnp.where(mask, lane, big), big), axis=-1, keepdims=True)
        return mx, idx

    is_group = (lane_i >= GROUP_LANE0) & (lane_i < GROUP_LANE0 + N_GROUPS)
    g_max, g_lane = first_argmax(logits, is_group)
    g_prob = 1.0 / jnp.sum(jnp.where(is_group, jnp.exp(logits - g_max), 0.0),
                           axis=-1, keepdims=True)
    g_idx = g_lane - GROUP_LANE0
    in_group = lane_group == g_idx
    e1, i1 = first_argmax(logits, in_group)
    e2, i2 = first_argmax(logits, in_group & (lane != i1))
    w2 = jnp.exp(e2 - e1)
    p1 = 1.0 / (1.0 + w2)
    p2 = w2 / (1.0 + w2)
    rec = jnp.where(lane_i == EXPERT1_LANE, i1, jnp.where(lane_i == EXPERT2_LANE, i2, 0.0))
    rec = jnp.where(lane_i == WEIGHT1_LANE, p1 * g_prob, jnp.where(lane_i == WEIGHT2_LANE, p2 * g_prob, rec))
    counts = jnp.sum(jnp.where(lane == i1, 1.0, 0.0) + jnp.where(lane == i2, 1.0, 0.0),
                     axis=0, keepdims=True)
    return rec, counts


def _merge(x2, att, rwkv, gates, wa, wb, wo, ln_g, wr, br):
    n = x2.shape[0]
    tm = MERGE_TILE
    row = lambda i: (i, 0)
    const = lambda i: (0, 0)
    return pl.pallas_call(
        _merge_kernel,
        grid=(n // tm,),
        in_specs=[
            pl.BlockSpec((tm, D_MODEL), row),
            pl.BlockSpec((tm, WIDTH), row),
            pl.BlockSpec((tm, WIDTH), row),
            pl.BlockSpec((tm, 2 * D_MODEL), row),
            pl.BlockSpec((WIDTH, D_MODEL), const),
            pl.BlockSpec((WIDTH, D_MODEL), const),
            pl.BlockSpec((D_MODEL, D_MODEL), const),
            pl.BlockSpec((1, D_MODEL), const),
            pl.BlockSpec((D_MODEL, 2 * ROUTER_LANES), const),
            pl.BlockSpec((1, ROUTER_LANES), const),
        ],
        out_specs=[
            pl.BlockSpec((tm, D_MODEL), row),
            pl.BlockSpec((tm, PACKED), row),
            pl.BlockSpec((tm, ROUTER_LANES), row),
            pl.BlockSpec((1, 8, ROUTER_LANES), lambda i: (i, 0, 0)),
        ],
        out_shape=[
            jax.ShapeDtypeStruct((n, D_MODEL), F32),
            jax.ShapeDtypeStruct((n, PACKED), jnp.uint32),
            jax.ShapeDtypeStruct((n, ROUTER_LANES), F32),
            jax.ShapeDtypeStruct((n // tm, 8, ROUTER_LANES), jnp.int32),
        ],
        scratch_shapes=[pltpu.VMEM((WIDTH, D_MODEL), BF16), pltpu.VMEM((WIDTH, D_MODEL), BF16),
                        pltpu.VMEM((D_MODEL, D_MODEL), BF16)],
        compiler_params=pltpu.CompilerParams(
            dimension_semantics=("arbitrary",), vmem_limit_bytes=VMEM_LIMIT),
        name="merge",
    )(x2, att, rwkv, gates, wa, wb, wo, ln_g, wr, br)


def _moe_plan_kernel(cnt_ref, base_ref, blk_ref):
    nt = cnt_ref.shape[0]
    cnt = cnt_ref[...].astype(F32)
    lane = lax.broadcasted_iota(jnp.int32, (8, ROUTER_LANES), 1)
    total = jnp.broadcast_to(jnp.sum(cnt, axis=0, keepdims=True), (8, ROUTER_LANES))
    padded = jnp.floor((total + (MOE_BLOCK - 1)) * (1.0 / MOE_BLOCK)) * MOE_BLOCK
    r = lax.broadcasted_iota(jnp.int32, (ROUTER_LANES, ROUTER_LANES), 0)
    c = lax.broadcasted_iota(jnp.int32, (ROUTER_LANES, ROUTER_LANES), 1)
    seg_start = _dot_hi(padded, jnp.where(r < c, 1.0, 0.0).astype(F32))
    tr = lax.broadcasted_iota(jnp.int32, (nt, nt), 0)
    tc = lax.broadcasted_iota(jnp.int32, (nt, nt), 1)
    tile_off = _dot_hi(jnp.where(tc < tr, 1.0, 0.0).astype(F32), cnt)
    base_ref[...] = seg_start[0:1, :] + tile_off
    seg_end = (seg_start + padded)[0:1, :]
    rows_total = jnp.sum(jnp.where(lane[0:1, :] < N_EXPERTS, padded[0:1, :], 0.0), axis=-1, keepdims=True)
    nblk = blk_ref.shape[0]
    blk_row = lax.broadcasted_iota(jnp.int32, (nblk, ROUTER_LANES), 0).astype(F32) * MOE_BLOCK
    blk_lane = lax.broadcasted_iota(jnp.int32, (nblk, ROUTER_LANES), 1)
    done = jnp.where((seg_end <= blk_row) & (blk_lane < N_EXPERTS), 1.0, 0.0)
    expert = jnp.minimum(jnp.sum(done, axis=-1, keepdims=True), N_EXPERTS - 1.0)
    valid = jnp.where(blk_row < rows_total, 1.0, 0.0)
    blk_ref[...] = jnp.where(blk_lane == 0, expert, jnp.where(blk_lane == 1, valid, 0.0)).astype(jnp.int32)


def _moe_plan(cnt, n_blocks):
    nt = cnt.shape[0]
    return pl.pallas_call(
        _moe_plan_kernel,
        out_shape=[jax.ShapeDtypeStruct((nt, ROUTER_LANES), F32),
                   jax.ShapeDtypeStruct((n_blocks, ROUTER_LANES), jnp.int32)],
        name="moe_plan",
    )(cnt)


def _moe_pos_kernel(comb_ref, base_ref, pos_ref):
    t = comb_ref.shape[0]
    comb = comb_ref[...]
    lane_i = lax.broadcasted_iota(jnp.int32, (t, ROUTER_LANES), 1)
    lane = lane_i.astype(F32)
    pick1 = jnp.where(lane == comb[:, EXPERT1_LANE:EXPERT1_LANE + 1], 1.0, 0.0)
    pick2 = jnp.where(lane == comb[:, EXPERT2_LANE:EXPERT2_LANE + 1], 1.0, 0.0)
    rows = lax.broadcasted_iota(jnp.int32, (t, t), 0)
    cols = lax.broadcasted_iota(jnp.int32, (t, t), 1)
    earlier = jnp.where(cols < rows, 1.0, 0.0).astype(BF16)
    before1 = _dot(earlier, pick1.astype(BF16))
    before2 = _dot(earlier, pick2.astype(BF16))
    base = base_ref[0]
    firsts = jnp.sum(pick1, axis=0, keepdims=True)
    pos1 = jnp.sum(pick1 * (base + before1), axis=-1, keepdims=True)
    pos2 = jnp.sum(pick2 * (base + firsts + before2), axis=-1, keepdims=True)
    both = jnp.where(lane_i == 0, pos1, jnp.where(lane_i == 1, pos2, 0.0))
    pos_ref[...] = jnp.transpose(both)[0:8, :].astype(jnp.int32)


def _moe_pos(comb, base):
    n = comb.shape[0]
    t = MERGE_TILE
    return pl.pallas_call(
        _moe_pos_kernel,
        grid=(n // t,),
        in_specs=[pl.BlockSpec((t, ROUTER_LANES), lambda i: (i, 0)),
                  pl.BlockSpec((1, 1, ROUTER_LANES), lambda i: (i, 0, 0))],
        out_specs=pl.BlockSpec((8, t), lambda i: (0, i)),
        out_shape=jax.ShapeDtypeStruct((8, n), jnp.int32),
        compiler_params=pltpu.CompilerParams(dimension_semantics=("arbitrary",)),
        name="moe_pos",
    )(comb, base.reshape(n // t, 1, ROUTER_LANES))


def _sc_mesh():
    return plsc.VectorSubcoreMesh(core_axis_name="core", subcore_axis_name="subcore")


def _sc_scratch():
    return [pltpu.VMEM((8, SC_WINDOW), jnp.int32), pltpu.VMEM((SC_WINDOW, PACKED), jnp.uint32)]


def _sc_scatter_rows(h, pos, n_rows):
    per_worker = h.shape[0] // (SC_WINDOW * SC_WORKERS)

    @pl.kernel(out_type=jax.ShapeDtypeStruct((n_rows, PACKED), jnp.uint32), mesh=_sc_mesh(),
               scratch_types=_sc_scratch())
    def scatter(h_hbm, pos_hbm, out_hbm, idx, buf):
        worker = lax.axis_index("core") * SC_SUBCORES + lax.axis_index("subcore")

        @pl.loop(0, per_worker)
        def _(b):
            start = (worker * per_worker + b) * SC_WINDOW
            pltpu.sync_copy(pos_hbm.at[:, pl.ds(start, SC_WINDOW)], idx)
            pltpu.sync_copy(h_hbm.at[pl.ds(start, SC_WINDOW)], buf)
            pltpu.sync_copy(buf, out_hbm.at[idx.at[0]])
            pltpu.sync_copy(buf, out_hbm.at[idx.at[1]])

    return scatter(h, pos)


def _sc_gather_rows(y, pos):
    n = pos.shape[1]
    per_worker = n // (SC_WINDOW * SC_WORKERS)
    out = jax.ShapeDtypeStruct((n, PACKED), jnp.uint32)

    @pl.kernel(out_type=(out, out), mesh=_sc_mesh(), scratch_types=_sc_scratch())
    def gather(y_hbm, pos_hbm, o1_hbm, o2_hbm, idx, buf):
        worker = lax.axis_index("core") * SC_SUBCORES + lax.axis_index("subcore")

        @pl.loop(0, per_worker)
        def _(b):
            start = (worker * per_worker + b) * SC_WINDOW
            pltpu.sync_copy(pos_hbm.at[:, pl.ds(start, SC_WINDOW)], idx)
            pltpu.sync_copy(y_hbm.at[idx.at[0]], buf)
            pltpu.sync_copy(buf, o1_hbm.at[pl.ds(start, SC_WINDOW)])
            pltpu.sync_copy(y_hbm.at[idx.at[1]], buf)
            pltpu.sync_copy(buf, o2_hbm.at[pl.ds(start, SC_WINDOW)])

    return gather(y, pos)


def _expert_mlp_kernel(expert_ref, valid_ref, hs_ref, wg32_ref, wu32_ref, wd32_ref, y_ref,
                       wg_ref, wu_ref, wd_ref):
    b = pl.program_id(0)

    @pl.when(valid_ref[b] == 0)
    def _():
        y_ref[...] = jnp.zeros(y_ref.shape, jnp.uint32)

    @pl.when(valid_ref[b] != 0)
    def _():
        @pl.when((b == 0) | (expert_ref[b] != expert_ref[jnp.maximum(b - 1, 0)]))
        def _():
            wg_ref[...] = wg32_ref[0].astype(BF16)
            wu_ref[...] = wu32_ref[0].astype(BF16)
            wd_ref[...] = wd32_ref[0].astype(BF16)

        half = D_MODEL // 2
        h_lo, h_hi = _unpack_bf16(hs_ref[...])
        h_lo = h_lo.astype(BF16)
        h_hi = h_hi.astype(BF16)
        hg = _dot(h_lo, wg_ref[0:half, :]) + _dot(h_hi, wg_ref[half:D_MODEL, :])
        hu = _dot(h_lo, wu_ref[0:half, :]) + _dot(h_hi, wu_ref[half:D_MODEL, :])
        act = (hg * _sigmoid(hg) * hu).astype(BF16)
        y_ref[...] = _pack_bf16(_dot(act, wd_ref[...]))


def _expert_mlp(blk_expert, blk_valid, hs, wg, wu, wd):
    rows = lambda b, expert, valid: (b, 0)
    by_expert = lambda b, expert, valid: (expert[b], 0, 0)
    grid_spec = pltpu.PrefetchScalarGridSpec(
        num_scalar_prefetch=2,
        grid=(hs.shape[0] // MOE_BLOCK,),
        in_specs=[
            pl.BlockSpec((MOE_BLOCK, PACKED), rows),
            pl.BlockSpec((1, D_MODEL, D_EXPERT), by_expert),
            pl.BlockSpec((1, D_MODEL, D_EXPERT), by_expert),
            pl.BlockSpec((1, D_EXPERT, D_MODEL), by_expert),
        ],
        out_specs=pl.BlockSpec((MOE_BLOCK, PACKED), rows),
        scratch_shapes=[pltpu.VMEM((D_MODEL, D_EXPERT), BF16), pltpu.VMEM((D_MODEL, D_EXPERT), BF16),
                        pltpu.VMEM((D_EXPERT, D_MODEL), BF16)],
    )
    return pl.pallas_call(
        _expert_mlp_kernel,
        grid_spec=grid_spec,
        out_shape=jax.ShapeDtypeStruct((hs.shape[0], PACKED), jnp.uint32),
        compiler_params=pltpu.CompilerParams(
            dimension_semantics=("arbitrary",), vmem_limit_bytes=VMEM_LIMIT),
        name="expert_mlp",
    )(blk_expert, blk_valid, hs, wg, wu, wd)


def _moe_out_kernel(x1_ref, y1_ref, y2_ref, comb_ref, lnf_ref, out_ref, *, final_norm):
    w1 = comb_ref[:, WEIGHT1_LANE:WEIGHT1_LANE + 1]
    w2 = comb_ref[:, WEIGHT2_LANE:WEIGHT2_LANE + 1]
    a_lo, a_hi = _unpack_bf16(y1_ref[...])
    b_lo, b_hi = _unpack_bf16(y2_ref[...])
    moe = jnp.concatenate([w1 * a_lo + w2 * b_lo, w1 * a_hi + w2 * b_hi], axis=1)
    y = x1_ref[...] + moe
    if final_norm:
        y = y * lax.rsqrt(jnp.mean(y * y, axis=-1, keepdims=True) + RMS_EPS) * lnf_ref[...]
    out_ref[...] = y


def _moe_out(x1, y1, y2, comb, lnf, final_norm):
    n = x1.shape[0]
    t = FINAL_TILE
    row = lambda i: (i, 0)
    return pl.pallas_call(
        functools.partial(_moe_out_kernel, final_norm=final_norm),
        grid=(n // t,),
        in_specs=[pl.BlockSpec((t, D_MODEL), row), pl.BlockSpec((t, PACKED), row),
                  pl.BlockSpec((t, PACKED), row), pl.BlockSpec((t, ROUTER_LANES), row),
                  pl.BlockSpec((1, D_MODEL), lambda i: (0, 0))],
        out_specs=pl.BlockSpec((t, D_MODEL), row),
        out_shape=jax.ShapeDtypeStruct((n, D_MODEL), F32),
        compiler_params=pltpu.CompilerParams(
            dimension_semantics=("arbitrary",), vmem_limit_bytes=VMEM_LIMIT),
        name="moe_out",
    )(x1, y1, y2, comb, lnf)


def _moe(counts, x1, h2p, comb, wg, wu, wd, lnf, final_norm):
    n = x1.shape[0]
    n_blocks = (2 * n) // MOE_BLOCK + N_EXPERTS
    base, blk = _moe_plan(counts[:, 0, :], n_blocks)
    pos = _moe_pos(comb, base)
    hs = _sc_scatter_rows(h2p, pos, n_blocks * MOE_BLOCK)
    ys = _expert_mlp(blk[:, 0], blk[:, 1], hs, wg, wu, wd)
    y1, y2 = _sc_gather_rows(ys, pos)
    return _moe_out(x1, y1, y2, comb, lnf, final_norm)


def kernel(x, ln_mix_g, w_in, att_rel_bias, rwkv_mu, rwkv_w0, rwkv_w2, rwkv_a0, rwkv_a2, rwkv_g2,
           rwkv_k_k, rwkv_k_a, rwkv_r_k, rwkv_gn_g, rwkv_gn_b, w_branch_att, w_branch_rwkv, w_out,
           ln_ffn_g, router_group_w, router_group_b, router_expert_w, router_expert_b,
           expert_w_gate, expert_w_up, expert_w_down, ln_final_g):
    bsz, seq, d = x.shape
    depth = w_in.shape[0]
    n = bsz * seq
    x2 = x.reshape(n, d)
    for l in range(depth):
        q, k, v, rw, gates = _in_proj(x2, ln_mix_g[l][None, :], w_in[l].astype(BF16),
                                      rwkv_mu[l][None, :], seq)
        bias = _rel_bias(att_rel_bias[l])
        att = _band_attn(q.reshape(bsz, seq, WIDTH), k.reshape(bsz, seq, WIDTH),
                         v.reshape(bsz, seq, WIDTH), bias)
        zeros = jnp.zeros((DECAY_LORA, WIDTH), F32)
        w2a2 = jnp.concatenate(
            [jnp.concatenate([rwkv_w2[l], zeros], axis=1),
             jnp.concatenate([zeros, rwkv_a2[l]], axis=1)], axis=0)
        rwkv = _rwkv(rw.reshape(bsz, seq, RWKV_PROJ), w2a2.astype(BF16), rwkv_g2[l].astype(BF16),
                     rwkv_w0[l][None, :], rwkv_a0[l][None, :], rwkv_k_k[l][None, :],
                     rwkv_k_a[l][None, :], rwkv_r_k[l].reshape(1, WIDTH),
                     rwkv_gn_g[l][None, :], rwkv_gn_b[l][None, :])
        wr = jnp.concatenate([router_expert_w[l], router_group_w[l]], axis=1)
        wr = jnp.pad(wr, ((0, 0), (0, ROUTER_LANES - wr.shape[1])))
        wr_hi = wr.astype(BF16)
        wr = jnp.concatenate([wr_hi, (wr - wr_hi.astype(F32)).astype(BF16)], axis=1)
        br = jnp.concatenate([router_expert_b[l], router_group_b[l]])
        br = jnp.pad(br, (0, ROUTER_LANES - br.shape[0]))[None, :]
        x1, h2, comb, counts = _merge(x2, att.reshape(n, WIDTH), rwkv.reshape(n, WIDTH), gates,
                                      w_branch_att[l], w_branch_rwkv[l], w_out[l],
                                      ln_ffn_g[l][None, :], wr, br)
        x2 = _moe(counts, x1, h2, comb, expert_w_gate[l], expert_w_up[l], expert_w_down[l],
                  ln_final_g[None, :], final_norm=(l == depth - 1))
    return x2.reshape(bsz, seq, d)
```

```python
import functools
import math

import jax
import jax.numpy as jnp
from jax import lax
from jax.experimental import pallas as pl
from jax.experimental.pallas import tpu as pltpu
from jax.experimental.pallas import tpu_sc as plsc

F32 = jnp.float32
BF16 = jnp.bfloat16
HIGHEST = lax.Precision.HIGHEST

D_MODEL = 1024
CHUNK = 64
HEADS = 8
HEAD_DIM = 64
WIDTH = HEADS * HEAD_DIM
LEFT_CHUNKS = 8
BAND = (LEFT_CHUNKS + 1) * CHUNK
REL_CLIP = 64
N_REL = 2 * REL_CLIP + 1
DECAY_LORA = 64
AAA_LORA = 64
GATE_LORA = 128
GN_EPS = 64e-5
RMS_EPS = 1e-6
ATT_PROJ = 3 * WIDTH
RWKV_PROJ = 3 * WIDTH + DECAY_LORA + AAA_LORA + GATE_LORA
D_IN = ATT_PROJ + RWKV_PROJ + 2 * D_MODEL
N_GROUPS = 4
EXPERTS_PER_GROUP = 8
N_EXPERTS = N_GROUPS * EXPERTS_PER_GROUP
D_EXPERT = 256
RWKV_SEQS = 4
BIAS_KEYS = 192
ATT_CHUNKS = 4
ATT_GROUP = 4
ROUTER_LANES = 128
GROUP_LANE0 = N_EXPERTS
EXPERT1_LANE, EXPERT2_LANE, WEIGHT1_LANE, WEIGHT2_LANE = 126, 125, 124, 123
MERGE_TILE = 512
MOE_BLOCK = 1024
PACKED = D_MODEL // 2
SC_WINDOW = 128
SC_CORES, SC_SUBCORES = 2, 16
SC_WORKERS = SC_CORES * SC_SUBCORES
FINAL_TILE = 1024

V7X_VMEM_BYTES = 64 * 1024 * 1024
VMEM_LIMIT = V7X_VMEM_BYTES - 12 * 1024 * 1024


def _dot(a, b):
    return jnp.dot(a, b, preferred_element_type=F32)


def _dot_hi(a, b):
    return jnp.dot(a, b, preferred_element_type=F32, precision=HIGHEST)


def _dot_nt(a, b, precision=None):
    return lax.dot_general(a, b, (((1,), (1,)), ((), ())),
                           preferred_element_type=F32, precision=precision)


def _dot_tn(a, b, precision=None):
    return lax.dot_general(a, b, (((0,), (0,)), ((), ())),
                           preferred_element_type=F32, precision=precision)


def _sigmoid(x):
    return 1.0 / (1.0 + jnp.exp(-x))


def _pack_bf16(x):
    w = x.shape[1] // 2
    hi = pltpu.bitcast(x[:, :w].astype(BF16).astype(F32), jnp.uint32)
    lo = pltpu.bitcast(x[:, w:].astype(BF16).astype(F32), jnp.uint32)
    return hi | lax.shift_right_logical(lo, jnp.uint32(16))


def _unpack_bf16(p):
    hi = pltpu.bitcast(p & jnp.uint32(0xFFFF0000), F32)
    lo = pltpu.bitcast(lax.shift_left(p, jnp.uint32(16)), F32)
    return hi, lo


def _mm(a, b):
    return jnp.dot(a.astype(BF16), b.astype(BF16), preferred_element_type=F32)


def _head_sums(x):
    outs = []
    lane = lax.broadcasted_iota(jnp.int32, (x.shape[0], 2 * HEAD_DIM), 1)
    low = lane < HEAD_DIM
    for p in range(HEADS // 2):
        xp = x[:, 2 * HEAD_DIM * p:2 * HEAD_DIM * (p + 1)]
        s_lo = jnp.sum(jnp.where(low, xp, 0.0), axis=-1, keepdims=True)
        s_hi = jnp.sum(jnp.where(low, 0.0, xp), axis=-1, keepdims=True)
        outs.append(jnp.where(low, s_lo, s_hi))
    return jnp.concatenate(outs, axis=-1)


def _rel_bias_kernel(tab_ref, out_ref):
    rows = tab_ref.shape[1]
    n = lax.broadcasted_iota(jnp.int32, (rows, CHUNK * 128), 1)
    r = lax.broadcasted_iota(jnp.int32, (rows, CHUNK * 128), 0)
    q = n >> 7
    kk = n & 127
    idx = jnp.clip(CHUNK + q - kk, -REL_CLIP, REL_CLIP) + REL_CLIP
    pick = jnp.where(r == idx, 1.0, 0.0) - jnp.where(r == N_REL - 1, 1.0, 0.0)
    out_ref[...] = _dot_hi(tab_ref[...], pick.astype(F32))


def _rel_bias(rel_table):
    rows = 136
    tab = jnp.pad(rel_table.astype(F32), ((0, 0), (0, rows - N_REL)))
    tail = pl.pallas_call(
        _rel_bias_kernel,
        out_shape=jax.ShapeDtypeStruct((HEADS, CHUNK * 128), F32),
        name="rel_bias",
    )(tab)
    tail = tail.reshape(HEADS, CHUNK, 128)
    bias = jnp.concatenate([jnp.zeros((HEADS, CHUNK, BIAS_KEYS - 128), F32), tail], axis=-1)
    return bias.reshape(HEADS * CHUNK, BIAS_KEYS)


def _in_proj_kernel(x_ref, g_ref, w_ref, mu_ref, q_ref, k_ref, v_ref, rw_ref, gate_ref,
                    carry_ref, *, tiles_per_seq):
    i = pl.program_id(0)

    @pl.when(i == 0)
    def _():
        carry_ref[...] = jnp.zeros(carry_ref.shape, F32)

    x = x_ref[...]
    h = x * lax.rsqrt(jnp.mean(x * x, axis=-1, keepdims=True) + RMS_EPS) * g_ref[...]
    hb = h.astype(BF16)
    q_ref[...] = _dot(hb, w_ref[:, 0:WIDTH]).astype(BF16)
    k_ref[...] = _dot(hb, w_ref[:, WIDTH:2 * WIDTH]).astype(BF16)
    v_ref[...] = _dot(hb, w_ref[:, 2 * WIDTH:ATT_PROJ]).astype(BF16)
    rw = _dot(hb, w_ref[:, ATT_PROJ:ATT_PROJ + RWKV_PROJ])
    tm = rw.shape[0]
    first_prev = jnp.where(i % tiles_per_seq == 0, 0.0, carry_ref[0:1, :])
    rolled = pltpu.roll(rw, 1, axis=0)
    row = lax.broadcasted_iota(jnp.int32, rw.shape, 0)
    prev = jnp.where(row == 0, first_prev, rolled)
    carry_ref[0:1, :] = rw[tm - 1:tm, :]
    rw_ref[...] = rw + (prev - rw) * mu_ref[...]
    gate_ref[...] = _sigmoid(_dot(hb, w_ref[:, ATT_PROJ + RWKV_PROJ:D_IN])).astype(BF16)


def _in_proj(x2, ln_g, w_in_b, mu, seq):
    n = x2.shape[0]
    tm = 512
    row = lambda i: (i, 0)
    const = lambda i: (0, 0)
    return pl.pallas_call(
        functools.partial(_in_proj_kernel, tiles_per_seq=seq // tm),
        grid=(n // tm,),
        in_specs=[
            pl.BlockSpec((tm, D_MODEL), row),
            pl.BlockSpec((1, D_MODEL), const),
            pl.BlockSpec((D_MODEL, D_IN), const),
            pl.BlockSpec((1, RWKV_PROJ), const),
        ],
        out_specs=[
            pl.BlockSpec((tm, WIDTH), row),
            pl.BlockSpec((tm, WIDTH), row),
            pl.BlockSpec((tm, WIDTH), row),
            pl.BlockSpec((tm, RWKV_PROJ), row),
            pl.BlockSpec((tm, 2 * D_MODEL), row),
        ],
        out_shape=[
            jax.ShapeDtypeStruct((n, WIDTH), BF16),
            jax.ShapeDtypeStruct((n, WIDTH), BF16),
            jax.ShapeDtypeStruct((n, WIDTH), BF16),
            jax.ShapeDtypeStruct((n, RWKV_PROJ), F32),
            jax.ShapeDtypeStruct((n, 2 * D_MODEL), BF16),
        ],
        scratch_shapes=[pltpu.VMEM((8, RWKV_PROJ), F32)],
        compiler_params=pltpu.CompilerParams(
            dimension_semantics=("arbitrary",), vmem_limit_bytes=VMEM_LIMIT),
        name="in_proj",
    )(x2, ln_g, w_in_b, mu)


def _band_attn_kernel(q_ref, k_ref, v_ref, bias_ref, o_ref, kpad_ref, vpad_ref):
    seq = k_ref.shape[1]
    pad = LEFT_CHUNKS * CHUNK
    kpad_ref[0:pad, :] = jnp.zeros((pad, WIDTH), BF16)
    vpad_ref[0:pad, :] = jnp.zeros((pad, WIDTH), BF16)
    kpad_ref[pad:pad + seq, :] = k_ref[0]
    vpad_ref[pad:pad + seq, :] = v_ref[0]

    gw = ATT_GROUP * HEAD_DIM
    rows = ATT_GROUP * CHUNK
    r_head = lax.broadcasted_iota(jnp.int32, (rows, gw), 0) // CHUNK
    l_head = lax.broadcasted_iota(jnp.int32, (rows, gw), 1) // HEAD_DIM
    own = r_head == l_head
    kpos_lo = lax.broadcasted_iota(jnp.int32, (rows, BAND - BIAS_KEYS), 1)
    kpos_hi = lax.broadcasted_iota(jnp.int32, (rows, BIAS_KEYS), 1) + (BAND - BIAS_KEYS)
    neg = jnp.finfo(F32).min
    groups = range(HEADS // ATT_GROUP)
    lanes = [slice(g * gw, (g + 1) * gw) for g in groups]

    def chunk_pair(i, carry, masked):
        units = [(j, g) for j in range(ATT_CHUNKS) for g in groups]
        ids = range(len(units))
        starts = [pl.multiple_of((i * ATT_CHUNKS + j) * CHUNK, CHUNK) for j in range(ATT_CHUNKS)]
        kb = [kpad_ref[pl.ds(st, BAND), :] for st in starts]
        vb = [vpad_ref[pl.ds(st, BAND), :] for st in starts]
        q = [q_ref[0, pl.ds(st, CHUNK), :] * (HEAD_DIM ** -0.5) for st in starts]
        qrows = [jnp.where(own, jnp.concatenate([q[j][:, lanes[g]]] * ATT_GROUP, axis=0),
                           jnp.zeros((), BF16)) for j, g in units]
        s = [_dot_nt(qrows[u], kb[j][:, lanes[g]]) for u, (j, g) in enumerate(units)]
        s_lo = [s[u][:, 0:BAND - BIAS_KEYS] for u in ids]
        s_hi = [s[u][:, BAND - BIAS_KEYS:BAND] + bias_ref[g * rows:(g + 1) * rows, :]
                for u, (j, g) in enumerate(units)]
        if masked:
            first = [(LEFT_CHUNKS - (i * ATT_CHUNKS + j)) * CHUNK for j in range(ATT_CHUNKS)]
            s_lo = [jnp.where(kpos_lo >= first[j], s_lo[u], neg) for u, (j, g) in enumerate(units)]
            s_hi = [jnp.where(kpos_hi >= first[j], s_hi[u], neg) for u, (j, g) in enumerate(units)]
        m = [jnp.maximum(jnp.max(s_lo[u], axis=-1, keepdims=True),
                         jnp.max(s_hi[u], axis=-1, keepdims=True)) for u in ids]
        p_lo = [jnp.exp(s_lo[u] - m[u]) for u in ids]
        p_hi = [jnp.exp(s_hi[u] - m[u]) for u in ids]
        denom = [jnp.sum(p_lo[u], axis=-1, keepdims=True) + jnp.sum(p_hi[u], axis=-1, keepdims=True)
                 for u in ids]
        o_all = [(_dot(p_lo[u].astype(BF16), vb[j][0:BAND - BIAS_KEYS, lanes[g]])
                  + _dot(p_hi[u].astype(BF16), vb[j][BAND - BIAS_KEYS:BAND, lanes[g]])) / denom[u]
                 for u, (j, g) in enumerate(units)]
        for u, (j, g) in enumerate(units):
            o_own = jnp.where(own, o_all[u], 0.0)
            o = o_own[0:CHUNK]
            for h in range(1, ATT_GROUP):
                o = o + o_own[h * CHUNK:(h + 1) * CHUNK]
            o_ref[0, pl.ds(starts[j], CHUNK), lanes[g]] = o.astype(BF16)
        return carry

    n_trips = seq // (CHUNK * ATT_CHUNKS)
    n_masked = min(LEFT_CHUNKS // ATT_CHUNKS, n_trips)
    lax.fori_loop(0, n_masked, functools.partial(chunk_pair, masked=True), 0)
    lax.fori_loop(n_masked, n_trips, functools.partial(chunk_pair, masked=False), 0)


def _band_attn(q, k, v, bias):
    b, seq, _ = q.shape
    whole = pl.BlockSpec((1, seq, WIDTH), lambda i: (i, 0, 0))
    return pl.pallas_call(
        _band_attn_kernel,
        grid=(b,),
        in_specs=[whole, whole, whole, pl.BlockSpec((HEADS * CHUNK, BIAS_KEYS), lambda i: (0, 0))],
        out_specs=whole,
        out_shape=jax.ShapeDtypeStruct((b, seq, WIDTH), BF16),
        scratch_shapes=[pltpu.VMEM((seq + LEFT_CHUNKS * CHUNK, WIDTH), BF16),
                        pltpu.VMEM((seq + LEFT_CHUNKS * CHUNK, WIDTH), BF16)],
        compiler_params=pltpu.CompilerParams(
            dimension_semantics=("arbitrary",), vmem_limit_bytes=VMEM_LIMIT),
        name="band_attn",
    )(q, k, v, bias)


def _rwkv_kernel(rw_ref, w2a2_ref, g2_ref, w0_ref, a0_ref, kk_ref, ka_ref, rk_ref,
                 gng_ref, gnb_ref, y_ref, state_ref):
    c = pl.program_id(1)
    t = CHUNK
    nb = rw_ref.shape[0]

    @pl.when(c == 0)
    def _():
        state_ref[...] = jnp.zeros(state_ref.shape, F32)

    lane = lax.broadcasted_iota(jnp.int32, (t, 2 * HEAD_DIM), 1)
    low = lane < HEAD_DIM
    r2 = lax.broadcasted_iota(jnp.int32, (2 * t, 2 * HEAD_DIM), 0)
    c2 = lax.broadcasted_iota(jnp.int32, (2 * t, 2 * HEAD_DIM), 1)
    own = (r2 < t) == (c2 < HEAD_DIM)
    strict = (r2 & (t - 1)) > (c2 & (t - 1))
    incl = (r2 & (t - 1)) >= (c2 & (t - 1))
    eye = jnp.where(r2 == c2, 1.0, 0.0).astype(F32)

    def stack2(xp):
        return jnp.concatenate([jnp.where(low, xp, 0.0), jnp.where(low, 0.0, xp)], axis=0)

    def wide(seqs):
        rows = len(seqs) * t
        rw = rw_ref[seqs[0]:seqs[-1] + 1].reshape(rows, RWKV_PROJ)
        r = rw[:, 0:WIDTH]
        k = rw[:, WIDTH:2 * WIDTH]
        v = rw[:, 2 * WIDTH:3 * WIDTH]
        lora = rw[:, 3 * WIDTH:3 * WIDTH + DECAY_LORA + AAA_LORA]
        g_lo = rw[:, 3 * WIDTH + DECAY_LORA + AAA_LORA:RWKV_PROJ]
        lane128 = lax.broadcasted_iota(jnp.int32, lora.shape, 1)
        lora = jnp.where(lane128 < DECAY_LORA, jnp.tanh(lora), lora)
        wa = _mm(lora, w2a2_ref[...])
        log_decay = -math.exp(-0.5) * _sigmoid(w0_ref[...] + wa[:, 0:WIDTH])
        lr = _sigmoid(a0_ref[...] + wa[:, WIDTH:2 * WIDTH])
        gate = _mm(_sigmoid(g_lo), g2_ref[...])
        kk_raw = k * kk_ref[...]
        k_mod = k * (1.0 + (lr - 1.0) * ka_ref[...])
        row = lax.broadcasted_iota(jnp.int32, (rows, rows), 0)
        col = lax.broadcasted_iota(jnp.int32, (rows, rows), 1)
        tri = jnp.where((row >= col) & ((row // t) == (col // t)), 1.0, 0.0).astype(BF16)
        ld1 = log_decay.astype(BF16)
        rem = log_decay - ld1.astype(F32)
        ld2 = rem.astype(BF16)
        ld3 = (rem - ld2.astype(F32)).astype(BF16)
        parts = _dot(tri, jnp.concatenate([ld1, ld2, ld3], axis=1))
        logp = parts[:, 0:WIDTH] + parts[:, WIDTH:2 * WIDTH] + parts[:, 2 * WIDTH:3 * WIDTH]
        p_in = jnp.exp(logp)
        p_ex = jnp.exp(logp - log_decay)
        p_inv = jnp.exp(-logp)
        kk = kk_raw / jnp.maximum(jnp.sqrt(_head_sums(kk_raw * kk_raw)), 1e-12)
        return dict(a_hat=-kk * p_ex, r_hat=r * p_in, b_hat=kk * lr * p_inv, k_hat=k_mod * p_inv,
                    v=v, p_in=p_in, gate=gate, bonus=_head_sums(r * k_mod * rk_ref[...]) * v)

    def front(seqs, w):
        chains = [(j, b, p) for j, b in enumerate(seqs) for p in range(HEADS // 2)]
        ids = range(len(chains))
        rs = [slice(j * t, (j + 1) * t) for j, _, _ in chains]
        ls = [slice(2 * HEAD_DIM * p, 2 * HEAD_DIM * (p + 1)) for _, _, p in chains]
        p_end = [w["p_in"][(j + 1) * t - 1:(j + 1) * t, ls[i]] for i, (j, _, _) in enumerate(chains)]
        ar = [jnp.concatenate([stack2(w["a_hat"][rs[i], ls[i]]), stack2(w["r_hat"][rs[i], ls[i]])],
                              axis=0).astype(BF16) for i in ids]
        bk2 = [jnp.concatenate([stack2(w["b_hat"][rs[i], ls[i]]), stack2(w["k_hat"][rs[i], ls[i]])], axis=0)
               for i in ids]
        bk = [bk2[i].astype(BF16) for i in ids]
        btkt = [(bk2[i] * p_end[i]).astype(BF16) for i in ids]
        v2 = [stack2(w["v"][rs[i], ls[i]]).astype(BF16) for i in ids]
        g = [_dot_nt(ar[i], bk[i]) for i in ids]
        st = [state_ref[b, p] for _, b, p in chains]
        ars = [_dot_nt(ar[i], st[i].astype(BF16)) for i in ids]
        l_ab = [jnp.where(strict, g[i][0:2 * t, 0:2 * t], 0.0) for i in ids]
        lm = [jnp.concatenate([jnp.where(strict, g[i][0:2 * t, 2 * t:4 * t], 0.0),
                               jnp.where(incl, g[i][2 * t:4 * t, 2 * t:4 * t], 0.0)], axis=0)
              for i in ids]
        m_rb = [jnp.where(incl, g[i][2 * t:4 * t, 0:2 * t], 0.0).astype(BF16) for i in ids]
        lv = [_mm(lm[i], v2[i]) for i in ids]
        return dict(chains=chains, rs=rs, ls=ls, p_end=p_end, btkt=btkt, v2=v2, st=st, ars=ars,
                    l_ab=l_ab, m_rb=m_rb, lv=lv)

    def inverse(f):
        w_inv = [eye + l for l in f["l_ab"]]
        l_pow = f["l_ab"]
        for _ in range(int(math.log2(t)) - 1):
            l_pow = [_mm(l, l) for l in l_pow]
            w_inv = [w + _mm(w, l) for w, l in zip(w_inv, l_pow)]
        return w_inv

    def back(w, f, w_inv):
        chains, rs, ls = f["chains"], f["rs"], f["ls"]
        ids = range(len(chains))
        z = [_mm(w_inv[i], f["ars"][i][0:2 * t] + f["lv"][i][0:2 * t]) for i in ids]
        y = [f["ars"][i][2 * t:4 * t] + f["lv"][i][2 * t:4 * t] + _mm(f["m_rb"][i], z[i]) for i in ids]
        for i, (_, b, p) in enumerate(chains):
            zv = jnp.concatenate([z[i].astype(BF16), f["v2"][i]], axis=0)
            state_ref[b, p] = f["st"][i] * f["p_end"][i] + _dot_tn(zv, f["btkt"][i])
        for i, (_, b, p) in enumerate(chains):
            mean = jnp.sum(y[i], axis=-1, keepdims=True) * (1.0 / HEAD_DIM)
            dev = jnp.where(own, y[i] - mean, 0.0)
            var = jnp.sum(dev * dev, axis=-1, keepdims=True) * (1.0 / HEAD_DIM)
            yn = dev * lax.rsqrt(var + GN_EPS)
            yn = yn[0:t] + yn[t:2 * t]
            out = ((yn * gng_ref[:, ls[i]] + gnb_ref[:, ls[i]] + w["bonus"][rs[i], ls[i]])
                   * w["gate"][rs[i], ls[i]])
            y_ref[b, :, ls[i]] = out.astype(BF16)

    seqs = list(range(nb))
    w = wide(seqs)
    f = front(seqs, w)
    back(w, f, inverse(f))


def _rwkv(rw, w2a2, g2, w0, a0, k_k, k_a, r_k, gn_g, gn_b):
    b, seq, _ = rw.shape
    nc = seq // CHUNK
    nb = RWKV_SEQS
    const = lambda i, c: (0, 0)
    vec = pl.BlockSpec((1, WIDTH), const)
    return pl.pallas_call(
        _rwkv_kernel,
        grid=(b // nb, nc),
        in_specs=[
            pl.BlockSpec((nb, CHUNK, RWKV_PROJ), lambda i, c: (i, c, 0)),
            pl.BlockSpec((DECAY_LORA + AAA_LORA, 2 * WIDTH), const),
            pl.BlockSpec((GATE_LORA, WIDTH), const),
            vec, vec, vec, vec, vec, vec, vec,
        ],
        out_specs=pl.BlockSpec((nb, CHUNK, WIDTH), lambda i, c: (i, c, 0)),
        out_shape=jax.ShapeDtypeStruct((b, seq, WIDTH), BF16),
        scratch_shapes=[pltpu.VMEM((nb, HEADS // 2, 2 * HEAD_DIM, 2 * HEAD_DIM), F32)],
        compiler_params=pltpu.CompilerParams(
            dimension_semantics=("arbitrary", "arbitrary"), vmem_limit_bytes=VMEM_LIMIT),
        name="rwkv7",
    )(rw, w2a2, g2, w0, a0, k_k, k_a, r_k, gn_g, gn_b)


def _merge_kernel(x_ref, att_ref, rwkv_ref, gate_ref, wa32_ref, wb32_ref, wo32_ref, g_ref,
                  wr_ref, br_ref, x1_ref, h2_ref, comb_ref, cnt_ref, wa_ref, wb_ref, wo_ref):
    @pl.when(pl.program_id(0) == 0)
    def _():
        wa_ref[...] = wa32_ref[...].astype(BF16)
        wb_ref[...] = wb32_ref[...].astype(BF16)
        wo_ref[...] = wo32_ref[...].astype(BF16)

    ga = gate_ref[:, 0:D_MODEL].astype(F32)
    gb = gate_ref[:, D_MODEL:2 * D_MODEL].astype(F32)
    merged = ga * _dot(att_ref[...], wa_ref[...]) + gb * _dot(rwkv_ref[...], wb_ref[...])
    x1 = x_ref[...] + _dot(merged.astype(BF16), wo_ref[...])
    x1_ref[...] = x1
    h2 = x1 * lax.rsqrt(jnp.mean(x1 * x1, axis=-1, keepdims=True) + RMS_EPS) * g_ref[...]
    h2_hi = h2.astype(BF16)
    h2_ref[...] = _pack_bf16(h2)
    h2_lo = (h2 - h2_hi.astype(F32)).astype(BF16)
    hw = _dot(h2_hi, wr_ref[...])
    lw = _dot(h2_lo, wr_ref[:, 0:ROUTER_LANES])
    logits = hw[:, 0:ROUTER_LANES] + (hw[:, ROUTER_LANES:2 * ROUTER_LANES] + lw) + br_ref[...]
    rec, counts = _route(logits)
    comb_ref[...] = rec
    cnt_ref[0] = jnp.broadcast_to(counts, (8, ROUTER_LANES)).astype(jnp.int32)


def _route(logits):
    lane_i = lax.broadcasted_iota(jnp.int32, logits.shape, 1)
    lane = lane_i.astype(F32)
    lane_group = (lane_i // EXPERTS_PER_GROUP).astype(F32)
    neg = jnp.finfo(F32).min
    big = float(ROUTER_LANES)

    def first_argmax(vals, mask):
        vm = jnp.where(mask, vals, neg)
        mx = jnp.max(vm, axis=-1, keepdims=True)
        idx = jnp.min(jnp.where(vm == mx, jnp.where(mask, lane, big), big), axis=-1, keepdims=True)
        return mx, idx

    is_group = (lane_i >= GROUP_LANE0) & (lane_i < GROUP_LANE0 + N_GROUPS)
    g_max, g_lane = first_argmax(logits, is_group)
    g_prob = 1.0 / jnp.sum(jnp.where(is_group, jnp.exp(logits - g_max), 0.0),
                           axis=-1, keepdims=True)
    g_idx = g_lane - GROUP_LANE0
    in_group = lane_group == g_idx
    e1, i1 = first_argmax(logits, in_group)
    e2, i2 = first_argmax(logits, in_group & (lane != i1))
    w2 = jnp.exp(e2 - e1)
    p1 = 1.0 / (1.0 + w2)
    p2 = w2 / (1.0 + w2)
    rec = jnp.where(lane_i == EXPERT1_LANE, i1, jnp.where(lane_i == EXPERT2_LANE, i2, 0.0))
    rec = jnp.where(lane_i == WEIGHT1_LANE, p1 * g_prob, jnp.where(lane_i == WEIGHT2_LANE, p2 * g_prob, rec))
    counts = jnp.sum(jnp.where(lane == i1, 1.0, 0.0) + jnp.where(lane == i2, 1.0, 0.0),
                     axis=0, keepdims=True)
    return rec, counts


def _merge(x2, att, rwkv, gates, wa, wb, wo, ln_g, wr, br):
    n = x2.shape[0]
    tm = MERGE_TILE
    row = lambda i: (i, 0)
    const = lambda i: (0, 0)
    return pl.pallas_call(
        _merge_kernel,
        grid=(n // tm,),
        in_specs=[
            pl.BlockSpec((tm, D_MODEL), row),
            pl.BlockSpec((tm, WIDTH), row),
            pl.BlockSpec((tm, WIDTH), row),
            pl.BlockSpec((tm, 2 * D_MODEL), row),
            pl.BlockSpec((WIDTH, D_MODEL), const),
            pl.BlockSpec((WIDTH, D_MODEL), const),
            pl.BlockSpec((D_MODEL, D_MODEL), const),
            pl.BlockSpec((1, D_MODEL), const),
            pl.BlockSpec((D_MODEL, 2 * ROUTER_LANES), const),
            pl.BlockSpec((1, ROUTER_LANES), const),
        ],
        out_specs=[
            pl.BlockSpec((tm, D_MODEL), row),
            pl.BlockSpec((tm, PACKED), row),
            pl.BlockSpec((tm, ROUTER_LANES), row),
            pl.BlockSpec((1, 8, ROUTER_LANES), lambda i: (i, 0, 0)),
        ],
        out_shape=[
            jax.ShapeDtypeStruct((n, D_MODEL), F32),
            jax.ShapeDtypeStruct((n, PACKED), jnp.uint32),
            jax.ShapeDtypeStruct((n, ROUTER_LANES), F32),
            jax.ShapeDtypeStruct((n // tm, 8, ROUTER_LANES), jnp.int32),
        ],
        scratch_shapes=[pltpu.VMEM((WIDTH, D_MODEL), BF16), pltpu.VMEM((WIDTH, D_MODEL), BF16),
                        pltpu.VMEM((D_MODEL, D_MODEL), BF16)],
        compiler_params=pltpu.CompilerParams(
            dimension_semantics=("arbitrary",), vmem_limit_bytes=VMEM_LIMIT),
        name="merge",
    )(x2, att, rwkv, gates, wa, wb, wo, ln_g, wr, br)


def _moe_plan_kernel(cnt_ref, base_ref, blk_ref):
    nt = cnt_ref.shape[0]
    cnt = cnt_ref[...].astype(F32)
    lane = lax.broadcasted_iota(jnp.int32, (8, ROUTER_LANES), 1)
    total = jnp.broadcast_to(jnp.sum(cnt, axis=0, keepdims=True), (8, ROUTER_LANES))
    padded = jnp.floor((total + (MOE_BLOCK - 1)) * (1.0 / MOE_BLOCK)) * MOE_BLOCK
    r = lax.broadcasted_iota(jnp.int32, (ROUTER_LANES, ROUTER_LANES), 0)
    c = lax.broadcasted_iota(jnp.int32, (ROUTER_LANES, ROUTER_LANES), 1)
    seg_start = _dot_hi(padded, jnp.where(r < c, 1.0, 0.0).astype(F32))
    tr = lax.broadcasted_iota(jnp.int32, (nt, nt), 0)
    tc = lax.broadcasted_iota(jnp.int32, (nt, nt), 1)
    tile_off = _dot_hi(jnp.where(tc < tr, 1.0, 0.0).astype(F32), cnt)
    base_ref[...] = seg_start[0:1, :] + tile_off
    seg_end = (seg_start + padded)[0:1, :]
    rows_total = jnp.sum(jnp.where(lane[0:1, :] < N_EXPERTS, padded[0:1, :], 0.0), axis=-1, keepdims=True)
    nblk = blk_ref.shape[0]
    blk_row = lax.broadcasted_iota(jnp.int32, (nblk, ROUTER_LANES), 0).astype(F32) * MOE_BLOCK
    blk_lane = lax.broadcasted_iota(jnp.int32, (nblk, ROUTER_LANES), 1)
    done = jnp.where((seg_end <= blk_row) & (blk_lane < N_EXPERTS), 1.0, 0.0)
    expert = jnp.minimum(jnp.sum(done, axis=-1, keepdims=True), N_EXPERTS - 1.0)
    valid = jnp.where(blk_row < rows_total, 1.0, 0.0)
    blk_ref[...] = jnp.where(blk_lane == 0, expert, jnp.where(blk_lane == 1, valid, 0.0)).astype(jnp.int32)


def _moe_plan(cnt, n_blocks):
    nt = cnt.shape[0]
    return pl.pallas_call(
        _moe_plan_kernel,
        out_shape=[jax.ShapeDtypeStruct((nt, ROUTER_LANES), F32),
                   jax.ShapeDtypeStruct((n_blocks, ROUTER_LANES), jnp.int32)],
        name="moe_plan",
    )(cnt)


def _moe_pos_kernel(comb_ref, base_ref, pos_ref):
    t = comb_ref.shape[0]
    comb = comb_ref[...]
    lane_i = lax.broadcasted_iota(jnp.int32, (t, ROUTER_LANES), 1)
    lane = lane_i.astype(F32)
    pick1 = jnp.where(lane == comb[:, EXPERT1_LANE:EXPERT1_LANE + 1], 1.0, 0.0)
    pick2 = jnp.where(lane == comb[:, EXPERT2_LANE:EXPERT2_LANE + 1], 1.0, 0.0)
    rows = lax.broadcasted_iota(jnp.int32, (t, t), 0)
    cols = lax.broadcasted_iota(jnp.int32, (t, t), 1)
    earlier = jnp.where(cols < rows, 1.0, 0.0).astype(BF16)
    before1 = _dot(earlier, pick1.astype(BF16))
    before2 = _dot(earlier, pick2.astype(BF16))
    base = base_ref[0]
    firsts = jnp.sum(pick1, axis=0, keepdims=True)
    pos1 = jnp.sum(pick1 * (base + before1), axis=-1, keepdims=True)
    pos2 = jnp.sum(pick2 * (base + firsts + before2), axis=-1, keepdims=True)
    both = jnp.where(lane_i == 0, pos1, jnp.where(lane_i == 1, pos2, 0.0))
    pos_ref[...] = jnp.transpose(both)[0:8, :].astype(jnp.int32)


def _moe_pos(comb, base):
    n = comb.shape[0]
    t = MERGE_TILE
    return pl.pallas_call(
        _moe_pos_kernel,
        grid=(n // t,),
        in_specs=[pl.BlockSpec((t, ROUTER_LANES), lambda i: (i, 0)),
                  pl.BlockSpec((1, 1, ROUTER_LANES), lambda i: (i, 0, 0))],
        out_specs=pl.BlockSpec((8, t), lambda i: (0, i)),
        out_shape=jax.ShapeDtypeStruct((8, n), jnp.int32),
        compiler_params=pltpu.CompilerParams(dimension_semantics=("arbitrary",)),
        name="moe_pos",
    )(comb, base.reshape(n // t, 1, ROUTER_LANES))


def _sc_mesh():
    return plsc.VectorSubcoreMesh(core_axis_name="core", subcore_axis_name="subcore")


def _sc_scratch():
    return [pltpu.VMEM((8, SC_WINDOW), jnp.int32), pltpu.VMEM((SC_WINDOW, PACKED), jnp.uint32)]


def _sc_scatter_rows(h, pos, n_rows):
    per_worker = h.shape[0] // (SC_WINDOW * SC_WORKERS)

    @pl.kernel(out_type=jax.ShapeDtypeStruct((n_rows, PACKED), jnp.uint32), mesh=_sc_mesh(),
               scratch_types=_sc_scratch())
    def scatter(h_hbm, pos_hbm, out_hbm, idx, buf):
        worker = lax.axis_index("core") * SC_SUBCORES + lax.axis_index("subcore")

        @pl.loop(0, per_worker)
        def _(b):
            start = (worker * per_worker + b) * SC_WINDOW
            pltpu.sync_copy(pos_hbm.at[:, pl.ds(start, SC_WINDOW)], idx)
            pltpu.sync_copy(h_hbm.at[pl.ds(start, SC_WINDOW)], buf)
            pltpu.sync_copy(buf, out_hbm.at[idx.at[0]])
            pltpu.sync_copy(buf, out_hbm.at[idx.at[1]])

    return scatter(h, pos)


def _sc_gather_rows(y, pos):
    n = pos.shape[1]
    per_worker = n // (SC_WINDOW * SC_WORKERS)
    out = jax.ShapeDtypeStruct((n, PACKED), jnp.uint32)

    @pl.kernel(out_type=(out, out), mesh=_sc_mesh(), scratch_types=_sc_scratch())
    def gather(y_hbm, pos_hbm, o1_hbm, o2_hbm, idx, buf):
        worker = lax.axis_index("core") * SC_SUBCORES + lax.axis_index("subcore")

        @pl.loop(0, per_worker)
        def _(b):
            start = (worker * per_worker + b) * SC_WINDOW
            pltpu.sync_copy(pos_hbm.at[:, pl.ds(start, SC_WINDOW)], idx)
            pltpu.sync_copy(y_hbm.at[idx.at[0]], buf)
            pltpu.sync_copy(buf, o1_hbm.at[pl.ds(start, SC_WINDOW)])
            pltpu.sync_copy(y_hbm.at[idx.at[1]], buf)
            pltpu.sync_copy(buf, o2_hbm.at[pl.ds(start, SC_WINDOW)])

    return gather(y, pos)


def _expert_mlp_kernel(expert_ref, valid_ref, hs_ref, wg32_ref, wu32_ref, wd32_ref, y_ref,
                       wg_ref, wu_ref, wd_ref):
    b = pl.program_id(0)

    @pl.when(valid_ref[b] == 0)
    def _():
        y_ref[...] = jnp.zeros(y_ref.shape, jnp.uint32)

    @pl.when(valid_ref[b] != 0)
    def _():
        @pl.when((b == 0) | (expert_ref[b] != expert_ref[jnp.maximum(b - 1, 0)]))
        def _():
            wg_ref[...] = wg32_ref[0].astype(BF16)
            wu_ref[...] = wu32_ref[0].astype(BF16)
            wd_ref[...] = wd32_ref[0].astype(BF16)

        half = D_MODEL // 2
        h_lo, h_hi = _unpack_bf16(hs_ref[...])
        h_lo = h_lo.astype(BF16)
        h_hi = h_hi.astype(BF16)
        hg = _dot(h_lo, wg_ref[0:half, :]) + _dot(h_hi, wg_ref[half:D_MODEL, :])
        hu = _dot(h_lo, wu_ref[0:half, :]) + _dot(h_hi, wu_ref[half:D_MODEL, :])
        act = (hg * _sigmoid(hg) * hu).astype(BF16)
        y_ref[...] = _pack_bf16(_dot(act, wd_ref[...]))


def _expert_mlp(blk_expert, blk_valid, hs, wg, wu, wd):
    rows = lambda b, expert, valid: (b, 0)
    by_expert = lambda b, expert, valid: (expert[b], 0, 0)
    grid_spec = pltpu.PrefetchScalarGridSpec(
        num_scalar_prefetch=2,
        grid=(hs.shape[0] // MOE_BLOCK,),
        in_specs=[
            pl.BlockSpec((MOE_BLOCK, PACKED), rows),
            pl.BlockSpec((1, D_MODEL, D_EXPERT), by_expert),
            pl.BlockSpec((1, D_MODEL, D_EXPERT), by_expert),
            pl.BlockSpec((1, D_EXPERT, D_MODEL), by_expert),
        ],
        out_specs=pl.BlockSpec((MOE_BLOCK, PACKED), rows),
        scratch_shapes=[pltpu.VMEM((D_MODEL, D_EXPERT), BF16), pltpu.VMEM((D_MODEL, D_EXPERT), BF16),
                        pltpu.VMEM((D_EXPERT, D_MODEL), BF16)],
    )
    return pl.pallas_call(
        _expert_mlp_kernel,
        grid_spec=grid_spec,
        out_shape=jax.ShapeDtypeStruct((hs.shape[0], PACKED), jnp.uint32),
        compiler_params=pltpu.CompilerParams(
            dimension_semantics=("arbitrary",), vmem_limit_bytes=VMEM_LIMIT),
        name="expert_mlp",
    )(blk_expert, blk_valid, hs, wg, wu, wd)


def _moe_out_kernel(x1_ref, y1_ref, y2_ref, comb_ref, lnf_ref, out_ref, *, final_norm):
    w1 = comb_ref[:, WEIGHT1_LANE:WEIGHT1_LANE + 1]
    w2 = comb_ref[:, WEIGHT2_LANE:WEIGHT2_LANE + 1]
    a_lo, a_hi = _unpack_bf16(y1_ref[...])
    b_lo, b_hi = _unpack_bf16(y2_ref[...])
    moe = jnp.concatenate([w1 * a_lo + w2 * b_lo, w1 * a_hi + w2 * b_hi], axis=1)
    y = x1_ref[...] + moe
    if final_norm:
        y = y * lax.rsqrt(jnp.mean(y * y, axis=-1, keepdims=True) + RMS_EPS) * lnf_ref[...]
    out_ref[...] = y


def _moe_out(x1, y1, y2, comb, lnf, final_norm):
    n = x1.shape[0]
    t = FINAL_TILE
    row = lambda i: (i, 0)
    return pl.pallas_call(
        functools.partial(_moe_out_kernel, final_norm=final_norm),
        grid=(n // t,),
        in_specs=[pl.BlockSpec((t, D_MODEL), row), pl.BlockSpec((t, PACKED), row),
                  pl.BlockSpec((t, PACKED), row), pl.BlockSpec((t, ROUTER_LANES), row),
                  pl.BlockSpec((1, D_MODEL), lambda i: (0, 0))],
        out_specs=pl.BlockSpec((t, D_MODEL), row),
        out_shape=jax.ShapeDtypeStruct((n, D_MODEL), F32),
        compiler_params=pltpu.CompilerParams(
            dimension_semantics=("arbitrary",), vmem_limit_bytes=VMEM_LIMIT),
        name="moe_out",
    )(x1, y1, y2, comb, lnf)


def _moe(counts, x1, h2p, comb, wg, wu, wd, lnf, final_norm):
    n = x1.shape[0]
    n_blocks = (2 * n) // MOE_BLOCK + N_EXPERTS
    base, blk = _moe_plan(counts[:, 0, :], n_blocks)
    pos = _moe_pos(comb, base)
    hs = _sc_scatter_rows(h2p, pos, n_blocks * MOE_BLOCK)
    ys = _expert_mlp(blk[:, 0], blk[:, 1], hs, wg, wu, wd)
    y1, y2 = _sc_gather_rows(ys, pos)
    return _moe_out(x1, y1, y2, comb, lnf, final_norm)


def kernel(x, ln_mix_g, w_in, att_rel_bias, rwkv_mu, rwkv_w0, rwkv_w2, rwkv_a0, rwkv_a2, rwkv_g2,
           rwkv_k_k, rwkv_k_a, rwkv_r_k, rwkv_gn_g, rwkv_gn_b, w_branch_att, w_branch_rwkv, w_out,
           ln_ffn_g, router_group_w, router_group_b, router_expert_w, router_expert_b,
           expert_w_gate, expert_w_up, expert_w_down, ln_final_g):
    bsz, seq, d = x.shape
    depth = w_in.shape[0]
    n = bsz * seq
    x2 = x.reshape(n, d)
    for l in range(depth):
        q, k, v, rw, gates = _in_proj(x2, ln_mix_g[l][None, :], w_in[l].astype(BF16),
                                      rwkv_mu[l][None, :], seq)
        bias = _rel_bias(att_rel_bias[l])
        att = _band_attn(q.reshape(bsz, seq, WIDTH), k.reshape(bsz, seq, WIDTH),
                         v.reshape(bsz, seq, WIDTH), bias)
        zeros = jnp.zeros((DECAY_LORA, WIDTH), F32)
        w2a2 = jnp.concatenate(
            [jnp.concatenate([rwkv_w2[l], zeros], axis=1),
             jnp.concatenate([zeros, rwkv_a2[l]], axis=1)], axis=0)
        rwkv = _rwkv(rw.reshape(bsz, seq, RWKV_PROJ), w2a2.astype(BF16), rwkv_g2[l].astype(BF16),
                     rwkv_w0[l][None, :], rwkv_a0[l][None, :], rwkv_k_k[l][None, :],
                     rwkv_k_a[l][None, :], rwkv_r_k[l].reshape(1, WIDTH),
                     rwkv_gn_g[l][None, :], rwkv_gn_b[l][None, :])
        wr = jnp.concatenate([router_expert_w[l], router_group_w[l]], axis=1)
        wr = jnp.pad(wr, ((0, 0), (0, ROUTER_LANES - wr.shape[1])))
        wr_hi = wr.astype(BF16)
        wr = jnp.concatenate([wr_hi, (wr - wr_hi.astype(F32)).astype(BF16)], axis=1)
        br = jnp.concatenate([router_expert_b[l], router_group_b[l]])
        br = jnp.pad(br, (0, ROUTER_LANES - br.shape[0]))[None, :]
        x1, h2, comb, counts = _merge(x2, att.reshape(n, WIDTH), rwkv.reshape(n, WIDTH), gates,
                                      w_branch_att[l], w_branch_rwkv[l], w_out[l],
                                      ln_ffn_g[l][None, :], wr, br)
        x2 = _moe(counts, x1, h2, comb, expert_w_gate[l], expert_w_up[l], expert_w_down[l],
                  ln_final_g[None, :], final_norm=(l == depth - 1))
    return x2.reshape(bsz, seq, d)
```

```python
import functools
import math

import jax
import jax.numpy as jnp
from jax import lax
from jax.experimental import pallas as pl
from jax.experimental.pallas import tpu as pltpu
from jax.experimental.pallas import tpu_sc as plsc

F32 = jnp.float32
BF16 = jnp.bfloat16
HIGHEST = lax.Precision.HIGHEST

D_MODEL = 1024
CHUNK = 64
HEADS = 8
HEAD_DIM = 64
WIDTH = HEADS * HEAD_DIM
LEFT_CHUNKS = 8
BAND = (LEFT_CHUNKS + 1) * CHUNK
REL_CLIP = 64
N_REL = 2 * REL_CLIP + 1
DECAY_LORA = 64
AAA_LORA = 64
GATE_LORA = 128
GN_EPS = 64e-5
RMS_EPS = 1e-6
ATT_PROJ = 3 * WIDTH
RWKV_PROJ = 3 * WIDTH + DECAY_LORA + AAA_LORA + GATE_LORA
D_IN = ATT_PROJ + RWKV_PROJ + 2 * D_MODEL
N_GROUPS = 4
EXPERTS_PER_GROUP = 8
N_EXPERTS = N_GROUPS * EXPERTS_PER_GROUP
D_EXPERT = 256
RWKV_SEQS = 4
RWKV_CHUNKS = 4
BIAS_KEYS = 192
ATT_CHUNKS = 4
ATT_GROUP = 4
ROUTER_LANES = 128
GROUP_LANE0 = N_EXPERTS
EXPERT1_LANE, EXPERT2_LANE, WEIGHT1_LANE, WEIGHT2_LANE = 126, 125, 124, 123
MERGE_TILE = 512
MOE_BLOCK = 1024
PACKED = D_MODEL // 2
SC_WINDOW = 128
SC_CORES, SC_SUBCORES = 2, 16
SC_WORKERS = SC_CORES * SC_SUBCORES
FINAL_TILE = 1024

V7X_VMEM_BYTES = 64 * 1024 * 1024
VMEM_LIMIT = V7X_VMEM_BYTES - 12 * 1024 * 1024


def _dot(a, b):
    return jnp.dot(a, b, preferred_element_type=F32)


def _dot_hi(a, b):
    return jnp.dot(a, b, preferred_element_type=F32, precision=HIGHEST)


def _dot_nt(a, b, precision=None):
    return lax.dot_general(a, b, (((1,), (1,)), ((), ())),
                           preferred_element_type=F32, precision=precision)


def _dot_tn(a, b, precision=None):
    return lax.dot_general(a, b, (((0,), (0,)), ((), ())),
                           preferred_element_type=F32, precision=precision)


def _sigmoid(x):
    return 1.0 / (1.0 + jnp.exp(-x))


def _pack_bf16(x):
    w = x.shape[1] // 2
    hi = pltpu.bitcast(x[:, :w].astype(BF16).astype(F32), jnp.uint32)
    lo = pltpu.bitcast(x[:, w:].astype(BF16).astype(F32), jnp.uint32)
    return hi | lax.shift_right_logical(lo, jnp.uint32(16))


def _unpack_bf16(p):
    hi = pltpu.bitcast(p & jnp.uint32(0xFFFF0000), F32)
    lo = pltpu.bitcast(lax.shift_left(p, jnp.uint32(16)), F32)
    return hi, lo


def _mm(a, b):
    return jnp.dot(a.astype(BF16), b.astype(BF16), preferred_element_type=F32)


def _head_sums(x):
    outs = []
    lane = lax.broadcasted_iota(jnp.int32, (x.shape[0], 2 * HEAD_DIM), 1)
    low = lane < HEAD_DIM
    for p in range(HEADS // 2):
        xp = x[:, 2 * HEAD_DIM * p:2 * HEAD_DIM * (p + 1)]
        s_lo = jnp.sum(jnp.where(low, xp, 0.0), axis=-1, keepdims=True)
        s_hi = jnp.sum(jnp.where(low, 0.0, xp), axis=-1, keepdims=True)
        outs.append(jnp.where(low, s_lo, s_hi))
    return jnp.concatenate(outs, axis=-1)


def _rel_bias_kernel(tab_ref, out_ref):
    rows = tab_ref.shape[1]
    n = lax.broadcasted_iota(jnp.int32, (rows, CHUNK * 128), 1)
    r = lax.broadcasted_iota(jnp.int32, (rows, CHUNK * 128), 0)
    q = n >> 7
    kk = n & 127
    idx = jnp.clip(CHUNK + q - kk, -REL_CLIP, REL_CLIP) + REL_CLIP
    pick = jnp.where(r == idx, 1.0, 0.0) - jnp.where(r == N_REL - 1, 1.0, 0.0)
    out_ref[...] = _dot_hi(tab_ref[...], pick.astype(F32))


def _rel_bias(rel_table):
    rows = 136
    tab = jnp.pad(rel_table.astype(F32), ((0, 0), (0, rows - N_REL)))
    tail = pl.pallas_call(
        _rel_bias_kernel,
        out_shape=jax.ShapeDtypeStruct((HEADS, CHUNK * 128), F32),
        name="rel_bias",
    )(tab)
    tail = tail.reshape(HEADS, CHUNK, 128)
    bias = jnp.concatenate([jnp.zeros((HEADS, CHUNK, BIAS_KEYS - 128), F32), tail], axis=-1)
    return bias.reshape(HEADS * CHUNK, BIAS_KEYS)


def _in_proj_kernel(x_ref, g_ref, w_ref, mu_ref, q_ref, k_ref, v_ref, rw_ref, gate_ref,
                    carry_ref, *, tiles_per_seq):
    i = pl.program_id(0)

    @pl.when(i == 0)
    def _():
        carry_ref[...] = jnp.zeros(carry_ref.shape, F32)

    x = x_ref[...]
    h = x * lax.rsqrt(jnp.mean(x * x, axis=-1, keepdims=True) + RMS_EPS) * g_ref[...]
    hb = h.astype(BF16)
    q_ref[...] = _dot(hb, w_ref[:, 0:WIDTH]).astype(BF16)
    k_ref[...] = _dot(hb, w_ref[:, WIDTH:2 * WIDTH]).astype(BF16)
    v_ref[...] = _dot(hb, w_ref[:, 2 * WIDTH:ATT_PROJ]).astype(BF16)
    rw = _dot(hb, w_ref[:, ATT_PROJ:ATT_PROJ + RWKV_PROJ])
    tm = rw.shape[0]
    first_prev = jnp.where(i % tiles_per_seq == 0, 0.0, carry_ref[0:1, :])
    rolled = pltpu.roll(rw, 1, axis=0)
    row = lax.broadcasted_iota(jnp.int32, rw.shape, 0)
    prev = jnp.where(row == 0, first_prev, rolled)
    carry_ref[0:1, :] = rw[tm - 1:tm, :]
    rw_ref[...] = rw + (prev - rw) * mu_ref[...]
    gate_ref[...] = _sigmoid(_dot(hb, w_ref[:, ATT_PROJ + RWKV_PROJ:D_IN])).astype(BF16)


def _in_proj(x2, ln_g, w_in_b, mu, seq):
    n = x2.shape[0]
    tm = 512
    row = lambda i: (i, 0)
    const = lambda i: (0, 0)
    return pl.pallas_call(
        functools.partial(_in_proj_kernel, tiles_per_seq=seq // tm),
        grid=(n // tm,),
        in_specs=[
            pl.BlockSpec((tm, D_MODEL), row),
            pl.BlockSpec((1, D_MODEL), const),
            pl.BlockSpec((D_MODEL, D_IN), const),
            pl.BlockSpec((1, RWKV_PROJ), const),
        ],
        out_specs=[
            pl.BlockSpec((tm, WIDTH), row),
            pl.BlockSpec((tm, WIDTH), row),
            pl.BlockSpec((tm, WIDTH), row),
            pl.BlockSpec((tm, RWKV_PROJ), row),
            pl.BlockSpec((tm, 2 * D_MODEL), row),
        ],
        out_shape=[
            jax.ShapeDtypeStruct((n, WIDTH), BF16),
            jax.ShapeDtypeStruct((n, WIDTH), BF16),
            jax.ShapeDtypeStruct((n, WIDTH), BF16),
            jax.ShapeDtypeStruct((n, RWKV_PROJ), F32),
            jax.ShapeDtypeStruct((n, 2 * D_MODEL), BF16),
        ],
        scratch_shapes=[pltpu.VMEM((8, RWKV_PROJ), F32)],
        compiler_params=pltpu.CompilerParams(
            dimension_semantics=("arbitrary",), vmem_limit_bytes=VMEM_LIMIT),
        name="in_proj",
    )(x2, ln_g, w_in_b, mu)


def _band_attn_kernel(q_ref, k_ref, v_ref, bias_ref, o_ref, kpad_ref, vpad_ref):
    seq = k_ref.shape[1]
    pad = LEFT_CHUNKS * CHUNK
    kpad_ref[0:pad, :] = jnp.zeros((pad, WIDTH), BF16)
    vpad_ref[0:pad, :] = jnp.zeros((pad, WIDTH), BF16)
    kpad_ref[pad:pad + seq, :] = k_ref[0]
    vpad_ref[pad:pad + seq, :] = v_ref[0]

    gw = ATT_GROUP * HEAD_DIM
    rows = ATT_GROUP * CHUNK
    r_head = lax.broadcasted_iota(jnp.int32, (rows, gw), 0) // CHUNK
    l_head = lax.broadcasted_iota(jnp.int32, (rows, gw), 1) // HEAD_DIM
    own = r_head == l_head
    kpos_lo = lax.broadcasted_iota(jnp.int32, (rows, BAND - BIAS_KEYS), 1)
    kpos_hi = lax.broadcasted_iota(jnp.int32, (rows, BIAS_KEYS), 1) + (BAND - BIAS_KEYS)
    neg = jnp.finfo(F32).min
    groups = range(HEADS // ATT_GROUP)
    lanes = [slice(g * gw, (g + 1) * gw) for g in groups]

    def chunk_pair(i, carry, masked):
        units = [(j, g) for j in range(ATT_CHUNKS) for g in groups]
        ids = range(len(units))
        starts = [pl.multiple_of((i * ATT_CHUNKS + j) * CHUNK, CHUNK) for j in range(ATT_CHUNKS)]
        kb = [kpad_ref[pl.ds(st, BAND), :] for st in starts]
        vb = [vpad_ref[pl.ds(st, BAND), :] for st in starts]
        q = [q_ref[0, pl.ds(st, CHUNK), :] * (HEAD_DIM ** -0.5) for st in starts]
        qrows = [jnp.where(own, jnp.concatenate([q[j][:, lanes[g]]] * ATT_GROUP, axis=0),
                           jnp.zeros((), BF16)) for j, g in units]
        s = [_dot_nt(qrows[u], kb[j][:, lanes[g]]) for u, (j, g) in enumerate(units)]
        s_lo = [s[u][:, 0:BAND - BIAS_KEYS] for u in ids]
        s_hi = [s[u][:, BAND - BIAS_KEYS:BAND] + bias_ref[g * rows:(g + 1) * rows, :]
                for u, (j, g) in enumerate(units)]
        if masked:
            first = [(LEFT_CHUNKS - (i * ATT_CHUNKS + j)) * CHUNK for j in range(ATT_CHUNKS)]
            s_lo = [jnp.where(kpos_lo >= first[j], s_lo[u], neg) for u, (j, g) in enumerate(units)]
            s_hi = [jnp.where(kpos_hi >= first[j], s_hi[u], neg) for u, (j, g) in enumerate(units)]
        m = [jnp.maximum(jnp.max(s_lo[u], axis=-1, keepdims=True),
                         jnp.max(s_hi[u], axis=-1, keepdims=True)) for u in ids]
        p_lo = [jnp.exp(s_lo[u] - m[u]) for u in ids]
        p_hi = [jnp.exp(s_hi[u] - m[u]) for u in ids]
        denom = [jnp.sum(p_lo[u], axis=-1, keepdims=True) + jnp.sum(p_hi[u], axis=-1, keepdims=True)
                 for u in ids]
        o_all = [(_dot(p_lo[u].astype(BF16), vb[j][0:BAND - BIAS_KEYS, lanes[g]])
                  + _dot(p_hi[u].astype(BF16), vb[j][BAND - BIAS_KEYS:BAND, lanes[g]])) / denom[u]
                 for u, (j, g) in enumerate(units)]
        for u, (j, g) in enumerate(units):
            o_own = jnp.where(own, o_all[u], 0.0)
            o = o_own[0:CHUNK]
            for h in range(1, ATT_GROUP):
                o = o + o_own[h * CHUNK:(h + 1) * CHUNK]
            o_ref[0, pl.ds(starts[j], CHUNK), lanes[g]] = o.astype(BF16)
        return carry

    n_trips = seq // (CHUNK * ATT_CHUNKS)
    n_masked = min(LEFT_CHUNKS // ATT_CHUNKS, n_trips)
    lax.fori_loop(0, n_masked, functools.partial(chunk_pair, masked=True), 0)
    lax.fori_loop(n_masked, n_trips, functools.partial(chunk_pair, masked=False), 0)


def _band_attn(q, k, v, bias):
    b, seq, _ = q.shape
    whole = pl.BlockSpec((1, seq, WIDTH), lambda i: (i, 0, 0))
    return pl.pallas_call(
        _band_attn_kernel,
        grid=(b,),
        in_specs=[whole, whole, whole, pl.BlockSpec((HEADS * CHUNK, BIAS_KEYS), lambda i: (0, 0))],
        out_specs=whole,
        out_shape=jax.ShapeDtypeStruct((b, seq, WIDTH), BF16),
        scratch_shapes=[pltpu.VMEM((seq + LEFT_CHUNKS * CHUNK, WIDTH), BF16),
                        pltpu.VMEM((seq + LEFT_CHUNKS * CHUNK, WIDTH), BF16)],
        compiler_params=pltpu.CompilerParams(
            dimension_semantics=("arbitrary",), vmem_limit_bytes=VMEM_LIMIT),
        name="band_attn",
    )(q, k, v, bias)


def _rwkv_kernel(rw_ref, w2a2_ref, g2_ref, w0_ref, a0_ref, kk_ref, ka_ref, rk_ref,
                 gng_ref, gnb_ref, y_ref, state_ref):
    c = pl.program_id(1)
    t = CHUNK
    nb = rw_ref.shape[0]

    @pl.when(c == 0)
    def _():
        state_ref[...] = jnp.zeros(state_ref.shape, F32)

    lane = lax.broadcasted_iota(jnp.int32, (t, 2 * HEAD_DIM), 1)
    low = lane < HEAD_DIM
    r2 = lax.broadcasted_iota(jnp.int32, (2 * t, 2 * HEAD_DIM), 0)
    c2 = lax.broadcasted_iota(jnp.int32, (2 * t, 2 * HEAD_DIM), 1)
    own = (r2 < t) == (c2 < HEAD_DIM)
    strict = (r2 & (t - 1)) > (c2 & (t - 1))
    incl = (r2 & (t - 1)) >= (c2 & (t - 1))
    eye = jnp.where(r2 == c2, 1.0, 0.0).astype(F32)

    def stack2(xp):
        return jnp.concatenate([jnp.where(low, xp, 0.0), jnp.where(low, 0.0, xp)], axis=0)

    def wide(seqs, start):
        rows = len(seqs) * t
        rw = rw_ref[seqs[0]:seqs[-1] + 1, pl.ds(start, t), :].reshape(rows, RWKV_PROJ)
        r = rw[:, 0:WIDTH]
        k = rw[:, WIDTH:2 * WIDTH]
        v = rw[:, 2 * WIDTH:3 * WIDTH]
        lora = rw[:, 3 * WIDTH:3 * WIDTH + DECAY_LORA + AAA_LORA]
        g_lo = rw[:, 3 * WIDTH + DECAY_LORA + AAA_LORA:RWKV_PROJ]
        lane128 = lax.broadcasted_iota(jnp.int32, lora.shape, 1)
        lora = jnp.where(lane128 < DECAY_LORA, jnp.tanh(lora), lora)
        wa = _mm(lora, w2a2_ref[...])
        log_decay = -math.exp(-0.5) * _sigmoid(w0_ref[...] + wa[:, 0:WIDTH])
        lr = _sigmoid(a0_ref[...] + wa[:, WIDTH:2 * WIDTH])
        gate = _mm(_sigmoid(g_lo), g2_ref[...])
        kk_raw = k * kk_ref[...]
        k_mod = k * (1.0 + (lr - 1.0) * ka_ref[...])
        row = lax.broadcasted_iota(jnp.int32, (rows, rows), 0)
        col = lax.broadcasted_iota(jnp.int32, (rows, rows), 1)
        tri = jnp.where((row >= col) & ((row // t) == (col // t)), 1.0, 0.0).astype(BF16)
        ld1 = log_decay.astype(BF16)
        rem = log_decay - ld1.astype(F32)
        ld2 = rem.astype(BF16)
        ld3 = (rem - ld2.astype(F32)).astype(BF16)
        parts = _dot(tri, jnp.concatenate([ld1, ld2, ld3], axis=1))
        logp = parts[:, 0:WIDTH] + parts[:, WIDTH:2 * WIDTH] + parts[:, 2 * WIDTH:3 * WIDTH]
        p_in = jnp.exp(logp)
        p_ex = jnp.exp(logp - log_decay)
        p_inv = jnp.exp(-logp)
        kk = kk_raw / jnp.maximum(jnp.sqrt(_head_sums(kk_raw * kk_raw)), 1e-12)
        return dict(a_hat=-kk * p_ex, r_hat=r * p_in, b_hat=kk * lr * p_inv, k_hat=k_mod * p_inv,
                    v=v, p_in=p_in, gate=gate, bonus=_head_sums(r * k_mod * rk_ref[...]) * v)

    def front(seqs, w):
        chains = [(j, b, p) for j, b in enumerate(seqs) for p in range(HEADS // 2)]
        ids = range(len(chains))
        rs = [slice(j * t, (j + 1) * t) for j, _, _ in chains]
        ls = [slice(2 * HEAD_DIM * p, 2 * HEAD_DIM * (p + 1)) for _, _, p in chains]
        p_end = [w["p_in"][(j + 1) * t - 1:(j + 1) * t, ls[i]] for i, (j, _, _) in enumerate(chains)]
        ar = [jnp.concatenate([stack2(w["a_hat"][rs[i], ls[i]]), stack2(w["r_hat"][rs[i], ls[i]])],
                              axis=0).astype(BF16) for i in ids]
        bk2 = [jnp.concatenate([stack2(w["b_hat"][rs[i], ls[i]]), stack2(w["k_hat"][rs[i], ls[i]])], axis=0)
               for i in ids]
        bk = [bk2[i].astype(BF16) for i in ids]
        btkt = [(bk2[i] * p_end[i]).astype(BF16) for i in ids]
        v2 = [stack2(w["v"][rs[i], ls[i]]).astype(BF16) for i in ids]
        g = [_dot_nt(ar[i], bk[i]) for i in ids]
        st = [state_ref[b, p] for _, b, p in chains]
        ars = [_dot_nt(ar[i], st[i].astype(BF16)) for i in ids]
        l_ab = [jnp.where(strict, g[i][0:2 * t, 0:2 * t], 0.0) for i in ids]
        lm = [jnp.concatenate([jnp.where(strict, g[i][0:2 * t, 2 * t:4 * t], 0.0),
                               jnp.where(incl, g[i][2 * t:4 * t, 2 * t:4 * t], 0.0)], axis=0)
              for i in ids]
        m_rb = [jnp.where(incl, g[i][2 * t:4 * t, 0:2 * t], 0.0).astype(BF16) for i in ids]
        lv = [_mm(lm[i], v2[i]) for i in ids]
        return dict(chains=chains, rs=rs, ls=ls, p_end=p_end, btkt=btkt, v2=v2, st=st, ars=ars,
                    l_ab=l_ab, m_rb=m_rb, lv=lv)

    def inverse(f):
        w_inv = [eye + l for l in f["l_ab"]]
        l_pow = f["l_ab"]
        for _ in range(int(math.log2(t)) - 1):
            l_pow = [_mm(l, l) for l in l_pow]
            w_inv = [w + _mm(w, l) for w, l in zip(w_inv, l_pow)]
        return w_inv

    def back(w, f, w_inv, start):
        chains, rs, ls = f["chains"], f["rs"], f["ls"]
        ids = range(len(chains))
        z = [_mm(w_inv[i], f["ars"][i][0:2 * t] + f["lv"][i][0:2 * t]) for i in ids]
        y = [f["ars"][i][2 * t:4 * t] + f["lv"][i][2 * t:4 * t] + _mm(f["m_rb"][i], z[i]) for i in ids]
        for i, (_, b, p) in enumerate(chains):
            zv = jnp.concatenate([z[i].astype(BF16), f["v2"][i]], axis=0)
            state_ref[b, p] = f["st"][i] * f["p_end"][i] + _dot_tn(zv, f["btkt"][i])
        for i, (_, b, p) in enumerate(chains):
            mean = jnp.sum(y[i], axis=-1, keepdims=True) * (1.0 / HEAD_DIM)
            dev = jnp.where(own, y[i] - mean, 0.0)
            var = jnp.sum(dev * dev, axis=-1, keepdims=True) * (1.0 / HEAD_DIM)
            yn = dev * lax.rsqrt(var + GN_EPS)
            yn = yn[0:t] + yn[t:2 * t]
            out = ((yn * gng_ref[:, ls[i]] + gnb_ref[:, ls[i]] + w["bonus"][rs[i], ls[i]])
                   * w["gate"][rs[i], ls[i]])
            y_ref[b, pl.ds(start, t), ls[i]] = out.astype(BF16)

    seqs = list(range(nb))

    def chunk(j, carry):
        start = pl.multiple_of(j * t, t)
        w = wide(seqs, start)
        f = front(seqs, w)
        back(w, f, inverse(f), start)
        return carry

    lax.fori_loop(0, rw_ref.shape[1] // t, chunk, 0)


def _rwkv(rw, w2a2, g2, w0, a0, k_k, k_a, r_k, gn_g, gn_b):
    b, seq, _ = rw.shape
    nc = seq // CHUNK
    nb = RWKV_SEQS
    const = lambda i, c: (0, 0)
    vec = pl.BlockSpec((1, WIDTH), const)
    return pl.pallas_call(
        _rwkv_kernel,
        grid=(b // nb, nc // RWKV_CHUNKS),
        in_specs=[
            pl.BlockSpec((nb, CHUNK * RWKV_CHUNKS, RWKV_PROJ), lambda i, c: (i, c, 0)),
            pl.BlockSpec((DECAY_LORA + AAA_LORA, 2 * WIDTH), const),
            pl.BlockSpec((GATE_LORA, WIDTH), const),
            vec, vec, vec, vec, vec, vec, vec,
        ],
        out_specs=pl.BlockSpec((nb, CHUNK * RWKV_CHUNKS, WIDTH), lambda i, c: (i, c, 0)),
        out_shape=jax.ShapeDtypeStruct((b, seq, WIDTH), BF16),
        scratch_shapes=[pltpu.VMEM((nb, HEADS // 2, 2 * HEAD_DIM, 2 * HEAD_DIM), F32)],
        compiler_params=pltpu.CompilerParams(
            dimension_semantics=("arbitrary", "arbitrary"), vmem_limit_bytes=VMEM_LIMIT),
        name="rwkv7",
    )(rw, w2a2, g2, w0, a0, k_k, k_a, r_k, gn_g, gn_b)


def _merge_kernel(x_ref, att_ref, rwkv_ref, gate_ref, wa32_ref, wb32_ref, wo32_ref, g_ref,
                  wr_ref, br_ref, x1_ref, h2_ref, comb_ref, cnt_ref, wa_ref, wb_ref, wo_ref):
    @pl.when(pl.program_id(0) == 0)
    def _():
        wa_ref[...] = wa32_ref[...].astype(BF16)
        wb_ref[...] = wb32_ref[...].astype(BF16)
        wo_ref[...] = wo32_ref[...].astype(BF16)

    ga = gate_ref[:, 0:D_MODEL].astype(F32)
    gb = gate_ref[:, D_MODEL:2 * D_MODEL].astype(F32)
    merged = ga * _dot(att_ref[...], wa_ref[...]) + gb * _dot(rwkv_ref[...], wb_ref[...])
    x1 = x_ref[...] + _dot(merged.astype(BF16), wo_ref[...])
    x1_ref[...] = x1
    h2 = x1 * lax.rsqrt(jnp.mean(x1 * x1, axis=-1, keepdims=True) + RMS_EPS) * g_ref[...]
    h2_hi = h2.astype(BF16)
    h2_ref[...] = _pack_bf16(h2)
    h2_lo = (h2 - h2_hi.astype(F32)).astype(BF16)
    hw = _dot(h2_hi, wr_ref[...])
    lw = _dot(h2_lo, wr_ref[:, 0:ROUTER_LANES])
    logits = hw[:, 0:ROUTER_LANES] + (hw[:, ROUTER_LANES:2 * ROUTER_LANES] + lw) + br_ref[...]
    rec, counts = _route(logits)
    comb_ref[...] = rec
    cnt_ref[0] = jnp.broadcast_to(counts, (8, ROUTER_LANES)).astype(jnp.int32)


def _route(logits):
    lane_i = lax.broadcasted_iota(jnp.int32, logits.shape, 1)
    lane = lane_i.astype(F32)
    lane_group = (lane_i // EXPERTS_PER_GROUP).astype(F32)
    neg = jnp.finfo(F32).min
    big = float(ROUTER_LANES)

    def first_argmax(vals, mask):
        vm = jnp.where(mask, vals, neg)
        mx = jnp.max(vm, axis=-1, keepdims=True)
        idx = jnp.min(jnp.where(vm == mx, jnp.where(mask, lane, big), big), axis=-1, keepdims=True)
        return mx, idx

    is_group = (lane_i >= GROUP_LANE0) & (lane_i < GROUP_LANE0 + N_GROUPS)
    g_max, g_lane = first_argmax(logits, is_group)
    g_prob = 1.0 / jnp.sum(jnp.where(is_group, jnp.exp(logits - g_max), 0.0),
                           axis=-1, keepdims=True)
    g_idx = g_lane - GROUP_LANE0
    in_group = lane_group == g_idx
    e1, i1 = first_argmax(logits, in_group)
    e2, i2 = first_argmax(logits, in_group & (lane != i1))
    w2 = jnp.exp(e2 - e1)
    p1 = 1.0 / (1.0 + w2)
    p2 = w2 / (1.0 + w2)
    rec = jnp.where(lane_i == EXPERT1_LANE, i1, jnp.where(lane_i == EXPERT2_LANE, i2, 0.0))
    rec = jnp.where(lane_i == WEIGHT1_LANE, p1 * g_prob, jnp.where(lane_i == WEIGHT2_LANE, p2 * g_prob, rec))
    counts = jnp.sum(jnp.where(lane == i1, 1.0, 0.0) + jnp.where(lane == i2, 1.0, 0.0),
                     axis=0, keepdims=True)
    return rec, counts


def _merge(x2, att, rwkv, gates, wa, wb, wo, ln_g, wr, br):
    n = x2.shape[0]
    tm = MERGE_TILE
    row = lambda i: (i, 0)
    const = lambda i: (0, 0)
    return pl.pallas_call(
        _merge_kernel,
        grid=(n // tm,),
        in_specs=[
            pl.BlockSpec((tm, D_MODEL), row),
            pl.BlockSpec((tm, WIDTH), row),
            pl.BlockSpec((tm, WIDTH), row),
            pl.BlockSpec((tm, 2 * D_MODEL), row),
            pl.BlockSpec((WIDTH, D_MODEL), const),
            pl.BlockSpec((WIDTH, D_MODEL), const),
            pl.BlockSpec((D_MODEL, D_MODEL), const),
            pl.BlockSpec((1, D_MODEL), const),
            pl.BlockSpec((D_MODEL, 2 * ROUTER_LANES), const),
            pl.BlockSpec((1, ROUTER_LANES), const),
        ],
        out_specs=[
            pl.BlockSpec((tm, D_MODEL), row),
            pl.BlockSpec((tm, PACKED), row),
            pl.BlockSpec((tm, ROUTER_LANES), row),
            pl.BlockSpec((1, 8, ROUTER_LANES), lambda i: (i, 0, 0)),
        ],
        out_shape=[
            jax.ShapeDtypeStruct((n, D_MODEL), F32),
            jax.ShapeDtypeStruct((n, PACKED), jnp.uint32),
            jax.ShapeDtypeStruct((n, ROUTER_LANES), F32),
            jax.ShapeDtypeStruct((n // tm, 8, ROUTER_LANES), jnp.int32),
        ],
        scratch_shapes=[pltpu.VMEM((WIDTH, D_MODEL), BF16), pltpu.VMEM((WIDTH, D_MODEL), BF16),
                        pltpu.VMEM((D_MODEL, D_MODEL), BF16)],
        compiler_params=pltpu.CompilerParams(
            dimension_semantics=("arbitrary",), vmem_limit_bytes=VMEM_LIMIT),
        name="merge",
    )(x2, att, rwkv, gates, wa, wb, wo, ln_g, wr, br)


def _moe_plan_kernel(cnt_ref, base_ref, blk_ref):
    nt = cnt_ref.shape[0]
    cnt = cnt_ref[...].astype(F32)
    lane = lax.broadcasted_iota(jnp.int32, (8, ROUTER_LANES), 1)
    total = jnp.broadcast_to(jnp.sum(cnt, axis=0, keepdims=True), (8, ROUTER_LANES))
    padded = jnp.floor((total + (MOE_BLOCK - 1)) * (1.0 / MOE_BLOCK)) * MOE_BLOCK
    r = lax.broadcasted_iota(jnp.int32, (ROUTER_LANES, ROUTER_LANES), 0)
    c = lax.broadcasted_iota(jnp.int32, (ROUTER_LANES, ROUTER_LANES), 1)
    seg_start = _dot_hi(padded, jnp.where(r < c, 1.0, 0.0).astype(F32))
    tr = lax.broadcasted_iota(jnp.int32, (nt, nt), 0)
    tc = lax.broadcasted_iota(jnp.int32, (nt, nt), 1)
    tile_off = _dot_hi(jnp.where(tc < tr, 1.0, 0.0).astype(F32), cnt)
    base_ref[...] = seg_start[0:1, :] + tile_off
    seg_end = (seg_start + padded)[0:1, :]
    rows_total = jnp.sum(jnp.where(lane[0:1, :] < N_EXPERTS, padded[0:1, :], 0.0), axis=-1, keepdims=True)
    nblk = blk_ref.shape[0]
    blk_row = lax.broadcasted_iota(jnp.int32, (nblk, ROUTER_LANES), 0).astype(F32) * MOE_BLOCK
    blk_lane = lax.broadcasted_iota(jnp.int32, (nblk, ROUTER_LANES), 1)
    done = jnp.where((seg_end <= blk_row) & (blk_lane < N_EXPERTS), 1.0, 0.0)
    expert = jnp.minimum(jnp.sum(done, axis=-1, keepdims=True), N_EXPERTS - 1.0)
    valid = jnp.where(blk_row < rows_total, 1.0, 0.0)
    blk_ref[...] = jnp.where(blk_lane == 0, expert, jnp.where(blk_lane == 1, valid, 0.0)).astype(jnp.int32)


def _moe_plan(cnt, n_blocks):
    nt = cnt.shape[0]
    return pl.pallas_call(
        _moe_plan_kernel,
        out_shape=[jax.ShapeDtypeStruct((nt, ROUTER_LANES), F32),
                   jax.ShapeDtypeStruct((n_blocks, ROUTER_LANES), jnp.int32)],
        name="moe_plan",
    )(cnt)


def _moe_pos_kernel(comb_ref, base_ref, pos_ref):
    t = comb_ref.shape[0]
    comb = comb_ref[...]
    lane_i = lax.broadcasted_iota(jnp.int32, (t, ROUTER_LANES), 1)
    lane = lane_i.astype(F32)
    pick1 = jnp.where(lane == comb[:, EXPERT1_LANE:EXPERT1_LANE + 1], 1.0, 0.0)
    pick2 = jnp.where(lane == comb[:, EXPERT2_LANE:EXPERT2_LANE + 1], 1.0, 0.0)
    rows = lax.broadcasted_iota(jnp.int32, (t, t), 0)
    cols = lax.broadcasted_iota(jnp.int32, (t, t), 1)
    earlier = jnp.where(cols < rows, 1.0, 0.0).astype(BF16)
    before1 = _dot(earlier, pick1.astype(BF16))
    before2 = _dot(earlier, pick2.astype(BF16))
    base = base_ref[0]
    firsts = jnp.sum(pick1, axis=0, keepdims=True)
    pos1 = jnp.sum(pick1 * (base + before1), axis=-1, keepdims=True)
    pos2 = jnp.sum(pick2 * (base + firsts + before2), axis=-1, keepdims=True)
    both = jnp.where(lane_i == 0, pos1, jnp.where(lane_i == 1, pos2, 0.0))
    pos_ref[...] = jnp.transpose(both)[0:8, :].astype(jnp.int32)


def _moe_pos(comb, base):
    n = comb.shape[0]
    t = MERGE_TILE
    return pl.pallas_call(
        _moe_pos_kernel,
        grid=(n // t,),
        in_specs=[pl.BlockSpec((t, ROUTER_LANES), lambda i: (i, 0)),
                  pl.BlockSpec((1, 1, ROUTER_LANES), lambda i: (i, 0, 0))],
        out_specs=pl.BlockSpec((8, t), lambda i: (0, i)),
        out_shape=jax.ShapeDtypeStruct((8, n), jnp.int32),
        compiler_params=pltpu.CompilerParams(dimension_semantics=("arbitrary",)),
        name="moe_pos",
    )(comb, base.reshape(n // t, 1, ROUTER_LANES))


def _sc_mesh():
    return plsc.VectorSubcoreMesh(core_axis_name="core", subcore_axis_name="subcore")


def _sc_scratch():
    return [pltpu.VMEM((8, SC_WINDOW), jnp.int32), pltpu.VMEM((SC_WINDOW, PACKED), jnp.uint32)]


def _sc_scatter_rows(h, pos, n_rows):
    per_worker = h.shape[0] // (SC_WINDOW * SC_WORKERS)

    @pl.kernel(out_type=jax.ShapeDtypeStruct((n_rows, PACKED), jnp.uint32), mesh=_sc_mesh(),
               scratch_types=_sc_scratch())
    def scatter(h_hbm, pos_hbm, out_hbm, idx, buf):
        worker = lax.axis_index("core") * SC_SUBCORES + lax.axis_index("subcore")

        @pl.loop(0, per_worker)
        def _(b):
            start = (worker * per_worker + b) * SC_WINDOW
            pltpu.sync_copy(pos_hbm.at[:, pl.ds(start, SC_WINDOW)], idx)
            pltpu.sync_copy(h_hbm.at[pl.ds(start, SC_WINDOW)], buf)
            pltpu.sync_copy(buf, out_hbm.at[idx.at[0]])
            pltpu.sync_copy(buf, out_hbm.at[idx.at[1]])

    return scatter(h, pos)


def _sc_gather_rows(y, pos):
    n = pos.shape[1]
    per_worker = n // (SC_WINDOW * SC_WORKERS)
    out = jax.ShapeDtypeStruct((n, PACKED), jnp.uint32)

    @pl.kernel(out_type=(out, out), mesh=_sc_mesh(), scratch_types=_sc_scratch())
    def gather(y_hbm, pos_hbm, o1_hbm, o2_hbm, idx, buf):
        worker = lax.axis_index("core") * SC_SUBCORES + lax.axis_index("subcore")

        @pl.loop(0, per_worker)
        def _(b):
            start = (worker * per_worker + b) * SC_WINDOW
            pltpu.sync_copy(pos_hbm.at[:, pl.ds(start, SC_WINDOW)], idx)
            pltpu.sync_copy(y_hbm.at[idx.at[0]], buf)
            pltpu.sync_copy(buf, o1_hbm.at[pl.ds(start, SC_WINDOW)])
            pltpu.sync_copy(y_hbm.at[idx.at[1]], buf)
            pltpu.sync_copy(buf, o2_hbm.at[pl.ds(start, SC_WINDOW)])

    return gather(y, pos)


def _expert_mlp_kernel(expert_ref, valid_ref, hs_ref, wg32_ref, wu32_ref, wd32_ref, y_ref,
                       wg_ref, wu_ref, wd_ref):
    b = pl.program_id(0)

    @pl.when(valid_ref[b] == 0)
    def _():
        y_ref[...] = jnp.zeros(y_ref.shape, jnp.uint32)

    @pl.when(valid_ref[b] != 0)
    def _():
        @pl.when((b == 0) | (expert_ref[b] != expert_ref[jnp.maximum(b - 1, 0)]))
        def _():
            wg_ref[...] = wg32_ref[0].astype(BF16)
            wu_ref[...] = wu32_ref[0].astype(BF16)
            wd_ref[...] = wd32_ref[0].astype(BF16)

        half = D_MODEL // 2
        h_lo, h_hi = _unpack_bf16(hs_ref[...])
        h_lo = h_lo.astype(BF16)
        h_hi = h_hi.astype(BF16)
        hg = _dot(h_lo, wg_ref[0:half, :]) + _dot(h_hi, wg_ref[half:D_MODEL, :])
        hu = _dot(h_lo, wu_ref[0:half, :]) + _dot(h_hi, wu_ref[half:D_MODEL, :])
        act = (hg * _sigmoid(hg) * hu).astype(BF16)
        y_ref[...] = _pack_bf16(_dot(act, wd_ref[...]))


def _expert_mlp(blk_expert, blk_valid, hs, wg, wu, wd):
    rows = lambda b, expert, valid: (b, 0)
    by_expert = lambda b, expert, valid: (expert[b], 0, 0)
    grid_spec = pltpu.PrefetchScalarGridSpec(
        num_scalar_prefetch=2,
        grid=(hs.shape[0] // MOE_BLOCK,),
        in_specs=[
            pl.BlockSpec((MOE_BLOCK, PACKED), rows),
            pl.BlockSpec((1, D_MODEL, D_EXPERT), by_expert),
            pl.BlockSpec((1, D_MODEL, D_EXPERT), by_expert),
            pl.BlockSpec((1, D_EXPERT, D_MODEL), by_expert),
        ],
        out_specs=pl.BlockSpec((MOE_BLOCK, PACKED), rows),
        scratch_shapes=[pltpu.VMEM((D_MODEL, D_EXPERT), BF16), pltpu.VMEM((D_MODEL, D_EXPERT), BF16),
                        pltpu.VMEM((D_EXPERT, D_MODEL), BF16)],
    )
    return pl.pallas_call(
        _expert_mlp_kernel,
        grid_spec=grid_spec,
        out_shape=jax.ShapeDtypeStruct((hs.shape[0], PACKED), jnp.uint32),
        compiler_params=pltpu.CompilerParams(
            dimension_semantics=("arbitrary",), vmem_limit_bytes=VMEM_LIMIT),
        name="expert_mlp",
    )(blk_expert, blk_valid, hs, wg, wu, wd)


def _moe_out_kernel(x1_ref, y1_ref, y2_ref, comb_ref, lnf_ref, out_ref, *, final_norm):
    w1 = comb_ref[:, WEIGHT1_LANE:WEIGHT1_LANE + 1]
    w2 = comb_ref[:, WEIGHT2_LANE:WEIGHT2_LANE + 1]
    a_lo, a_hi = _unpack_bf16(y1_ref[...])
    b_lo, b_hi = _unpack_bf16(y2_ref[...])
    moe = jnp.concatenate([w1 * a_lo + w2 * b_lo, w1 * a_hi + w2 * b_hi], axis=1)
    y = x1_ref[...] + moe
    if final_norm:
        y = y * lax.rsqrt(jnp.mean(y * y, axis=-1, keepdims=True) + RMS_EPS) * lnf_ref[...]
    out_ref[...] = y


def _moe_out(x1, y1, y2, comb, lnf, final_norm):
    n = x1.shape[0]
    t = FINAL_TILE
    row = lambda i: (i, 0)
    return pl.pallas_call(
        functools.partial(_moe_out_kernel, final_norm=final_norm),
        grid=(n // t,),
        in_specs=[pl.BlockSpec((t, D_MODEL), row), pl.BlockSpec((t, PACKED), row),
                  pl.BlockSpec((t, PACKED), row), pl.BlockSpec((t, ROUTER_LANES), row),
                  pl.BlockSpec((1, D_MODEL), lambda i: (0, 0))],
        out_specs=pl.BlockSpec((t, D_MODEL), row),
        out_shape=jax.ShapeDtypeStruct((n, D_MODEL), F32),
        compiler_params=pltpu.CompilerParams(
            dimension_semantics=("arbitrary",), vmem_limit_bytes=VMEM_LIMIT),
        name="moe_out",
    )(x1, y1, y2, comb, lnf)


def _moe(counts, x1, h2p, comb, wg, wu, wd, lnf, final_norm):
    n = x1.shape[0]
    n_blocks = (2 * n) // MOE_BLOCK + N_EXPERTS
    base, blk = _moe_plan(counts[:, 0, :], n_blocks)
    pos = _moe_pos(comb, base)
    hs = _sc_scatter_rows(h2p, pos, n_blocks * MOE_BLOCK)
    ys = _expert_mlp(blk[:, 0], blk[:, 1], hs, wg, wu, wd)
    y1, y2 = _sc_gather_rows(ys, pos)
    return _moe_out(x1, y1, y2, comb, lnf, final_norm)


def kernel(x, ln_mix_g, w_in, att_rel_bias, rwkv_mu, rwkv_w0, rwkv_w2, rwkv_a0, rwkv_a2, rwkv_g2,
           rwkv_k_k, rwkv_k_a, rwkv_r_k, rwkv_gn_g, rwkv_gn_b, w_branch_att, w_branch_rwkv, w_out,
           ln_ffn_g, router_group_w, router_group_b, router_expert_w, router_expert_b,
           expert_w_gate, expert_w_up, expert_w_down, ln_final_g):
    bsz, seq, d = x.shape
    depth = w_in.shape[0]
    n = bsz * seq
    x2 = x.reshape(n, d)
    for l in range(depth):
        q, k, v, rw, gates = _in_proj(x2, ln_mix_g[l][None, :], w_in[l].astype(BF16),
                                      rwkv_mu[l][None, :], seq)
        bias = _rel_bias(att_rel_bias[l])
        att = _band_attn(q.reshape(bsz, seq, WIDTH), k.reshape(bsz, seq, WIDTH),
                         v.reshape(bsz, seq, WIDTH), bias)
        zeros = jnp.zeros((DECAY_LORA, WIDTH), F32)
        w2a2 = jnp.concatenate(
            [jnp.concatenate([rwkv_w2[l], zeros], axis=1),
             jnp.concatenate([zeros, rwkv_a2[l]], axis=1)], axis=0)
        rwkv = _rwkv(rw.reshape(bsz, seq, RWKV_PROJ), w2a2.astype(BF16), rwkv_g2[l].astype(BF16),
                     rwkv_w0[l][None, :], rwkv_a0[l][None, :], rwkv_k_k[l][None, :],
                     rwkv_k_a[l][None, :], rwkv_r_k[l].reshape(1, WIDTH),
                     rwkv_gn_g[l][None, :], rwkv_gn_b[l][None, :])
        wr = jnp.concatenate([router_expert_w[l], router_group_w[l]], axis=1)
        wr = jnp.pad(wr, ((0, 0), (0, ROUTER_LANES - wr.shape[1])))
        wr_hi = wr.astype(BF16)
        wr = jnp.concatenate([wr_hi, (wr - wr_hi.astype(F32)).astype(BF16)], axis=1)
        br = jnp.concatenate([router_expert_b[l], router_group_b[l]])
        br = jnp.pad(br, (0, ROUTER_LANES - br.shape[0]))[None, :]
        x1, h2, comb, counts = _merge(x2, att.reshape(n, WIDTH), rwkv.reshape(n, WIDTH), gates,
                                      w_branch_att[l], w_branch_rwkv[l], w_out[l],
                                      ln_ffn_g[l][None, :], wr, br)
        x2 = _moe(counts, x1, h2, comb, expert_w_gate[l], expert_w_up[l], expert_w_down[l],
                  ln_final_g[None, :], final_norm=(l == depth - 1))
    return x2.reshape(bsz, seq, d)
```

```python
import functools
import math

import jax
import jax.numpy as jnp
from jax import lax
from jax.experimental import pallas as pl
from jax.experimental.pallas import tpu as pltpu
from jax.experimental.pallas import tpu_sc as plsc

F32 = jnp.float32
BF16 = jnp.bfloat16
HIGHEST = lax.Precision.HIGHEST

D_MODEL = 1024
CHUNK = 64
HEADS = 8
HEAD_DIM = 64
WIDTH = HEADS * HEAD_DIM
LEFT_CHUNKS = 8
BAND = (LEFT_CHUNKS + 1) * CHUNK
REL_CLIP = 64
N_REL = 2 * REL_CLIP + 1
DECAY_LORA = 64
AAA_LORA = 64
GATE_LORA = 128
GN_EPS = 64e-5
RMS_EPS = 1e-6
ATT_PROJ = 3 * WIDTH
RWKV_PROJ = 3 * WIDTH + DECAY_LORA + AAA_LORA + GATE_LORA
D_IN = ATT_PROJ + RWKV_PROJ + 2 * D_MODEL
N_GROUPS = 4
EXPERTS_PER_GROUP = 8
N_EXPERTS = N_GROUPS * EXPERTS_PER_GROUP
D_EXPERT = 256
RWKV_SEQS = 4
RWKV_CHUNKS = 4
BIAS_KEYS = 192
ATT_CHUNKS = 4
ATT_GROUP = 4
ROUTER_LANES = 128
GROUP_LANE0 = N_EXPERTS
EXPERT1_LANE, EXPERT2_LANE, WEIGHT1_LANE, WEIGHT2_LANE = 126, 125, 124, 123
MERGE_TILE = 512
MOE_BLOCK = 1024
PACKED = D_MODEL // 2
SC_WINDOW = 128
SC_CORES, SC_SUBCORES = 2, 16
SC_WORKERS = SC_CORES * SC_SUBCORES
FINAL_TILE = 1024

V7X_VMEM_BYTES = 64 * 1024 * 1024
VMEM_LIMIT = V7X_VMEM_BYTES - 12 * 1024 * 1024


def _dot(a, b):
    return jnp.dot(a, b, preferred_element_type=F32)


def _dot_hi(a, b):
    return jnp.dot(a, b, preferred_element_type=F32, precision=HIGHEST)


def _dot_nt(a, b, precision=None):
    return lax.dot_general(a, b, (((1,), (1,)), ((), ())),
                           preferred_element_type=F32, precision=precision)


def _dot_tn(a, b, precision=None):
    return lax.dot_general(a, b, (((0,), (0,)), ((), ())),
                           preferred_element_type=F32, precision=precision)


def _sigmoid(x):
    return 1.0 / (1.0 + jnp.exp(-x))


def _pack_bf16(x):
    w = x.shape[1] // 2
    hi = pltpu.bitcast(x[:, :w].astype(BF16).astype(F32), jnp.uint32)
    lo = pltpu.bitcast(x[:, w:].astype(BF16).astype(F32), jnp.uint32)
    return hi | lax.shift_right_logical(lo, jnp.uint32(16))


def _unpack_bf16(p):
    hi = pltpu.bitcast(p & jnp.uint32(0xFFFF0000), F32)
    lo = pltpu.bitcast(lax.shift_left(p, jnp.uint32(16)), F32)
    return hi, lo


def _mm(a, b):
    return jnp.dot(a.astype(BF16), b.astype(BF16), preferred_element_type=F32)


def _head_sums(x):
    outs = []
    lane = lax.broadcasted_iota(jnp.int32, (x.shape[0], 2 * HEAD_DIM), 1)
    low = lane < HEAD_DIM
    for p in range(HEADS // 2):
        xp = x[:, 2 * HEAD_DIM * p:2 * HEAD_DIM * (p + 1)]
        s_lo = jnp.sum(jnp.where(low, xp, 0.0), axis=-1, keepdims=True)
        s_hi = jnp.sum(jnp.where(low, 0.0, xp), axis=-1, keepdims=True)
        outs.append(jnp.where(low, s_lo, s_hi))
    return jnp.concatenate(outs, axis=-1)


def _rel_bias_kernel(tab_ref, out_ref):
    rows = tab_ref.shape[1]
    n = lax.broadcasted_iota(jnp.int32, (rows, CHUNK * 128), 1)
    r = lax.broadcasted_iota(jnp.int32, (rows, CHUNK * 128), 0)
    q = n >> 7
    kk = n & 127
    idx = jnp.clip(CHUNK + q - kk, -REL_CLIP, REL_CLIP) + REL_CLIP
    pick = jnp.where(r == idx, 1.0, 0.0) - jnp.where(r == N_REL - 1, 1.0, 0.0)
    out_ref[...] = _dot_hi(tab_ref[...], pick.astype(F32))


def _rel_bias(rel_table):
    rows = 136
    tab = jnp.pad(rel_table.astype(F32), ((0, 0), (0, rows - N_REL)))
    tail = pl.pallas_call(
        _rel_bias_kernel,
        out_shape=jax.ShapeDtypeStruct((HEADS, CHUNK * 128), F32),
        name="rel_bias",
    )(tab)
    tail = tail.reshape(HEADS, CHUNK, 128)
    bias = jnp.concatenate([jnp.zeros((HEADS, CHUNK, BIAS_KEYS - 128), F32), tail], axis=-1)
    return bias.reshape(HEADS * CHUNK, BIAS_KEYS)


def _in_proj_kernel(x_ref, g_ref, w_ref, mu_ref, q_ref, k_ref, v_ref, rw_ref, gate_ref,
                    carry_ref, *, tiles_per_seq):
    i = pl.program_id(0)

    @pl.when(i == 0)
    def _():
        carry_ref[...] = jnp.zeros(carry_ref.shape, F32)

    x = x_ref[...]
    h = x * lax.rsqrt(jnp.mean(x * x, axis=-1, keepdims=True) + RMS_EPS) * g_ref[...]
    hb = h.astype(BF16)
    q_ref[...] = _dot(hb, w_ref[:, 0:WIDTH]).astype(BF16)
    k_ref[...] = _dot(hb, w_ref[:, WIDTH:2 * WIDTH]).astype(BF16)
    v_ref[...] = _dot(hb, w_ref[:, 2 * WIDTH:ATT_PROJ]).astype(BF16)
    rw = _dot(hb, w_ref[:, ATT_PROJ:ATT_PROJ + RWKV_PROJ])
    tm = rw.shape[0]
    first_prev = jnp.where(i % tiles_per_seq == 0, 0.0, carry_ref[0:1, :])
    rolled = pltpu.roll(rw, 1, axis=0)
    row = lax.broadcasted_iota(jnp.int32, rw.shape, 0)
    prev = jnp.where(row == 0, first_prev, rolled)
    carry_ref[0:1, :] = rw[tm - 1:tm, :]
    rw_ref[...] = rw + (prev - rw) * mu_ref[...]
    gate_ref[...] = _sigmoid(_dot(hb, w_ref[:, ATT_PROJ + RWKV_PROJ:D_IN])).astype(BF16)


def _in_proj(x2, ln_g, w_in_b, mu, seq):
    n = x2.shape[0]
    tm = 512
    row = lambda i: (i, 0)
    const = lambda i: (0, 0)
    return pl.pallas_call(
        functools.partial(_in_proj_kernel, tiles_per_seq=seq // tm),
        grid=(n // tm,),
        in_specs=[
            pl.BlockSpec((tm, D_MODEL), row),
            pl.BlockSpec((1, D_MODEL), const),
            pl.BlockSpec((D_MODEL, D_IN), const),
            pl.BlockSpec((1, RWKV_PROJ), const),
        ],
        out_specs=[
            pl.BlockSpec((tm, WIDTH), row),
            pl.BlockSpec((tm, WIDTH), row),
            pl.BlockSpec((tm, WIDTH), row),
            pl.BlockSpec((tm, RWKV_PROJ), row),
            pl.BlockSpec((tm, 2 * D_MODEL), row),
        ],
        out_shape=[
            jax.ShapeDtypeStruct((n, WIDTH), BF16),
            jax.ShapeDtypeStruct((n, WIDTH), BF16),
            jax.ShapeDtypeStruct((n, WIDTH), BF16),
            jax.ShapeDtypeStruct((n, RWKV_PROJ), F32),
            jax.ShapeDtypeStruct((n, 2 * D_MODEL), BF16),
        ],
        scratch_shapes=[pltpu.VMEM((8, RWKV_PROJ), F32)],
        compiler_params=pltpu.CompilerParams(
            dimension_semantics=("arbitrary",), vmem_limit_bytes=VMEM_LIMIT),
        name="in_proj",
    )(x2, ln_g, w_in_b, mu)


def _band_attn_kernel(q_ref, k_ref, v_ref, bias_ref, o_ref, kpad_ref, vpad_ref):
    seq = k_ref.shape[1]
    pad = LEFT_CHUNKS * CHUNK
    kpad_ref[0:pad, :] = jnp.zeros((pad, WIDTH), BF16)
    vpad_ref[0:pad, :] = jnp.zeros((pad, WIDTH), BF16)
    kpad_ref[pad:pad + seq, :] = k_ref[0]
    vpad_ref[pad:pad + seq, :] = v_ref[0]

    gw = ATT_GROUP * HEAD_DIM
    rows = ATT_GROUP * CHUNK
    r_head = lax.broadcasted_iota(jnp.int32, (rows, gw), 0) // CHUNK
    l_head = lax.broadcasted_iota(jnp.int32, (rows, gw), 1) // HEAD_DIM
    own = r_head == l_head
    kpos_lo = lax.broadcasted_iota(jnp.int32, (rows, BAND - BIAS_KEYS), 1)
    kpos_hi = lax.broadcasted_iota(jnp.int32, (rows, BIAS_KEYS), 1) + (BAND - BIAS_KEYS)
    neg = jnp.finfo(F32).min
    groups = range(HEADS // ATT_GROUP)
    lanes = [slice(g * gw, (g + 1) * gw) for g in groups]

    def chunk_pair(i, carry, masked):
        units = [(j, g) for j in range(ATT_CHUNKS) for g in groups]
        ids = range(len(units))
        starts = [pl.multiple_of((i * ATT_CHUNKS + j) * CHUNK, CHUNK) for j in range(ATT_CHUNKS)]
        kb = [kpad_ref[pl.ds(st, BAND), :] for st in starts]
        vb = [vpad_ref[pl.ds(st, BAND), :] for st in starts]
        q = [q_ref[0, pl.ds(st, CHUNK), :] * (HEAD_DIM ** -0.5) for st in starts]
        qrows = [jnp.where(own, jnp.concatenate([q[j][:, lanes[g]]] * ATT_GROUP, axis=0),
                           jnp.zeros((), BF16)) for j, g in units]
        s = [_dot_nt(qrows[u], kb[j][:, lanes[g]]) for u, (j, g) in enumerate(units)]
        s_lo = [s[u][:, 0:BAND - BIAS_KEYS] for u in ids]
        s_hi = [s[u][:, BAND - BIAS_KEYS:BAND] + bias_ref[g * rows:(g + 1) * rows, :]
                for u, (j, g) in enumerate(units)]
        if masked:
            first = [(LEFT_CHUNKS - (i * ATT_CHUNKS + j)) * CHUNK for j in range(ATT_CHUNKS)]
            s_lo = [jnp.where(kpos_lo >= first[j], s_lo[u], neg) for u, (j, g) in enumerate(units)]
            s_hi = [jnp.where(kpos_hi >= first[j], s_hi[u], neg) for u, (j, g) in enumerate(units)]
        m = [jnp.maximum(jnp.max(s_lo[u], axis=-1, keepdims=True),
                         jnp.max(s_hi[u], axis=-1, keepdims=True)) for u in ids]
        p_lo = [jnp.exp(s_lo[u] - m[u]) for u in ids]
        p_hi = [jnp.exp(s_hi[u] - m[u]) for u in ids]
        denom = [jnp.sum(p_lo[u], axis=-1, keepdims=True) + jnp.sum(p_hi[u], axis=-1, keepdims=True)
                 for u in ids]
        o_all = [(_dot(p_lo[u].astype(BF16), vb[j][0:BAND - BIAS_KEYS, lanes[g]])
                  + _dot(p_hi[u].astype(BF16), vb[j][BAND - BIAS_KEYS:BAND, lanes[g]])) / denom[u]
                 for u, (j, g) in enumerate(units)]
        for u, (j, g) in enumerate(units):
            o_own = jnp.where(own, o_all[u], 0.0)
            o = o_own[0:CHUNK]
            for h in range(1, ATT_GROUP):
                o = o + o_own[h * CHUNK:(h + 1) * CHUNK]
            o_ref[0, pl.ds(starts[j], CHUNK), lanes[g]] = o.astype(BF16)
        return carry

    n_trips = seq // (CHUNK * ATT_CHUNKS)
    n_masked = min(LEFT_CHUNKS // ATT_CHUNKS, n_trips)
    lax.fori_loop(0, n_masked, functools.partial(chunk_pair, masked=True), 0)
    lax.fori_loop(n_masked, n_trips, functools.partial(chunk_pair, masked=False), 0)


def _band_attn(q, k, v, bias):
    b, seq, _ = q.shape
    whole = pl.BlockSpec((1, seq, WIDTH), lambda i: (i, 0, 0))
    return pl.pallas_call(
        _band_attn_kernel,
        grid=(b,),
        in_specs=[whole, whole, whole, pl.BlockSpec((HEADS * CHUNK, BIAS_KEYS), lambda i: (0, 0))],
        out_specs=whole,
        out_shape=jax.ShapeDtypeStruct((b, seq, WIDTH), BF16),
        scratch_shapes=[pltpu.VMEM((seq + LEFT_CHUNKS * CHUNK, WIDTH), BF16),
                        pltpu.VMEM((seq + LEFT_CHUNKS * CHUNK, WIDTH), BF16)],
        compiler_params=pltpu.CompilerParams(
            dimension_semantics=("arbitrary",), vmem_limit_bytes=VMEM_LIMIT),
        name="band_attn",
    )(q, k, v, bias)


def _rwkv_kernel(rw_ref, w2a2_ref, g2_ref, w0_ref, a0_ref, kk_ref, ka_ref, rk_ref,
                 gng_ref, gnb_ref, y_ref, state_ref):
    c = pl.program_id(1)
    t = CHUNK
    nb = rw_ref.shape[0]

    @pl.when(c == 0)
    def _():
        state_ref[...] = jnp.zeros(state_ref.shape, F32)

    lane = lax.broadcasted_iota(jnp.int32, (t, 2 * HEAD_DIM), 1)
    low = lane < HEAD_DIM
    r2 = lax.broadcasted_iota(jnp.int32, (2 * t, 2 * HEAD_DIM), 0)
    c2 = lax.broadcasted_iota(jnp.int32, (2 * t, 2 * HEAD_DIM), 1)
    own = (r2 < t) == (c2 < HEAD_DIM)
    strict = (r2 & (t - 1)) > (c2 & (t - 1))
    incl = (r2 & (t - 1)) >= (c2 & (t - 1))
    eye = jnp.where(r2 == c2, 1.0, 0.0).astype(F32)

    def stack2(xp):
        return jnp.concatenate([jnp.where(low, xp, 0.0), jnp.where(low, 0.0, xp)], axis=0)

    def wide(seqs, start):
        rows = len(seqs) * t
        rw = rw_ref[seqs[0]:seqs[-1] + 1, pl.ds(start, t), :].reshape(rows, RWKV_PROJ)
        r = rw[:, 0:WIDTH]
        k = rw[:, WIDTH:2 * WIDTH]
        v = rw[:, 2 * WIDTH:3 * WIDTH]
        lora = rw[:, 3 * WIDTH:3 * WIDTH + DECAY_LORA + AAA_LORA]
        g_lo = rw[:, 3 * WIDTH + DECAY_LORA + AAA_LORA:RWKV_PROJ]
        lane128 = lax.broadcasted_iota(jnp.int32, lora.shape, 1)
        lora = jnp.where(lane128 < DECAY_LORA, jnp.tanh(lora), lora)
        wa = _mm(lora, w2a2_ref[...])
        log_decay = -math.exp(-0.5) * _sigmoid(w0_ref[...] + wa[:, 0:WIDTH])
        lr = _sigmoid(a0_ref[...] + wa[:, WIDTH:2 * WIDTH])
        gate = _mm(_sigmoid(g_lo), g2_ref[...])
        kk_raw = k * kk_ref[...]
        k_mod = k * (1.0 + (lr - 1.0) * ka_ref[...])
        row = lax.broadcasted_iota(jnp.int32, (rows, rows), 0)
        col = lax.broadcasted_iota(jnp.int32, (rows, rows), 1)
        tri = jnp.where((row >= col) & ((row // t) == (col // t)), 1.0, 0.0).astype(BF16)
        ld1 = log_decay.astype(BF16)
        rem = log_decay - ld1.astype(F32)
        ld2 = rem.astype(BF16)
        ld3 = (rem - ld2.astype(F32)).astype(BF16)
        parts = _dot(tri, jnp.concatenate([ld1, ld2, ld3], axis=1))
        logp = parts[:, 0:WIDTH] + parts[:, WIDTH:2 * WIDTH] + parts[:, 2 * WIDTH:3 * WIDTH]
        p_in = jnp.exp(logp)
        p_ex = jnp.exp(logp - log_decay)
        p_inv = jnp.exp(-logp)
        kk = kk_raw / jnp.maximum(jnp.sqrt(_head_sums(kk_raw * kk_raw)), 1e-12)
        return dict(a_hat=-kk * p_ex, r_hat=r * p_in, b_hat=kk * lr * p_inv, k_hat=k_mod * p_inv,
                    v=v, p_in=p_in, gate=gate, bonus=_head_sums(r * k_mod * rk_ref[...]) * v)

    def front(seqs, w):
        chains = [(j, b, p) for j, b in enumerate(seqs) for p in range(HEADS // 2)]
        ids = range(len(chains))
        rs = [slice(j * t, (j + 1) * t) for j, _, _ in chains]
        ls = [slice(2 * HEAD_DIM * p, 2 * HEAD_DIM * (p + 1)) for _, _, p in chains]
        p_end = [w["p_in"][(j + 1) * t - 1:(j + 1) * t, ls[i]] for i, (j, _, _) in enumerate(chains)]
        ar = [jnp.concatenate([stack2(w["a_hat"][rs[i], ls[i]]), stack2(w["r_hat"][rs[i], ls[i]])],
                              axis=0).astype(BF16) for i in ids]
        bk2 = [jnp.concatenate([stack2(w["b_hat"][rs[i], ls[i]]), stack2(w["k_hat"][rs[i], ls[i]])], axis=0)
               for i in ids]
        bk = [bk2[i].astype(BF16) for i in ids]
        btkt = [(bk2[i] * p_end[i]).astype(BF16) for i in ids]
        v2 = [stack2(w["v"][rs[i], ls[i]]).astype(BF16) for i in ids]
        g = [_dot_nt(ar[i], bk[i]) for i in ids]
        st = [state_ref[b, p] for _, b, p in chains]
        ars = [_dot_nt(ar[i], st[i].astype(BF16)) for i in ids]
        l_ab = [jnp.where(strict, g[i][0:2 * t, 0:2 * t], 0.0) for i in ids]
        lm = [jnp.concatenate([jnp.where(strict, g[i][0:2 * t, 2 * t:4 * t], 0.0),
                               jnp.where(incl, g[i][2 * t:4 * t, 2 * t:4 * t], 0.0)], axis=0)
              for i in ids]
        m_rb = [jnp.where(incl, g[i][2 * t:4 * t, 0:2 * t], 0.0).astype(BF16) for i in ids]
        lv = [_mm(lm[i], v2[i]) for i in ids]
        return dict(chains=chains, rs=rs, ls=ls, p_end=p_end, btkt=btkt, v2=v2, st=st, ars=ars,
                    l_ab=l_ab, m_rb=m_rb, lv=lv)

    def inverse(f):
        w_inv = [eye + l for l in f["l_ab"]]
        l_pow = f["l_ab"]
        for _ in range(int(math.log2(t)) - 1):
            l_pow = [_mm(l, l) for l in l_pow]
            w_inv = [w + _mm(w, l) for w, l in zip(w_inv, l_pow)]
        return w_inv

    def back(w, f, w_inv, start):
        chains, rs, ls = f["chains"], f["rs"], f["ls"]
        ids = range(len(chains))
        z = [_mm(w_inv[i], f["ars"][i][0:2 * t] + f["lv"][i][0:2 * t]) for i in ids]
        y = [f["ars"][i][2 * t:4 * t] + f["lv"][i][2 * t:4 * t] + _mm(f["m_rb"][i], z[i]) for i in ids]
        for i, (_, b, p) in enumerate(chains):
            zv = jnp.concatenate([z[i].astype(BF16), f["v2"][i]], axis=0)
            state_ref[b, p] = f["st"][i] * f["p_end"][i] + _dot_tn(zv, f["btkt"][i])
        for i, (_, b, p) in enumerate(chains):
            mean = jnp.sum(y[i], axis=-1, keepdims=True) * (1.0 / HEAD_DIM)
            dev = jnp.where(own, y[i] - mean, 0.0)
            var = jnp.sum(dev * dev, axis=-1, keepdims=True) * (1.0 / HEAD_DIM)
            yn = dev * lax.rsqrt(var + GN_EPS)
            yn = yn[0:t] + yn[t:2 * t]
            out = ((yn * gng_ref[:, ls[i]] + gnb_ref[:, ls[i]] + w["bonus"][rs[i], ls[i]])
                   * w["gate"][rs[i], ls[i]])
            y_ref[b, pl.ds(start, t), ls[i]] = out.astype(BF16)

    seqs = list(range(nb))

    def chunk(j, carry):
        start = pl.multiple_of(j * t, t)
        w = wide(seqs, start)
        f = front(seqs, w)
        back(w, f, inverse(f), start)
        return carry

    lax.fori_loop(0, rw_ref.shape[1] // t, chunk, 0)


def _rwkv(rw, w2a2, g2, w0, a0, k_k, k_a, r_k, gn_g, gn_b):
    b, seq, _ = rw.shape
    nc = seq // CHUNK
    nb = RWKV_SEQS
    const = lambda i, c: (0, 0)
    vec = pl.BlockSpec((1, WIDTH), const)
    return pl.pallas_call(
        _rwkv_kernel,
        grid=(b // nb, nc // RWKV_CHUNKS),
        in_specs=[
            pl.BlockSpec((nb, CHUNK * RWKV_CHUNKS, RWKV_PROJ), lambda i, c: (i, c, 0)),
            pl.BlockSpec((DECAY_LORA + AAA_LORA, 2 * WIDTH), const),
            pl.BlockSpec((GATE_LORA, WIDTH), const),
            vec, vec, vec, vec, vec, vec, vec,
        ],
        out_specs=pl.BlockSpec((nb, CHUNK * RWKV_CHUNKS, WIDTH), lambda i, c: (i, c, 0)),
        out_shape=jax.ShapeDtypeStruct((b, seq, WIDTH), BF16),
        scratch_shapes=[pltpu.VMEM((nb, HEADS // 2, 2 * HEAD_DIM, 2 * HEAD_DIM), F32)],
        compiler_params=pltpu.CompilerParams(
            dimension_semantics=("arbitrary", "arbitrary"), vmem_limit_bytes=VMEM_LIMIT),
        name="rwkv7",
    )(rw, w2a2, g2, w0, a0, k_k, k_a, r_k, gn_g, gn_b)


def _merge_kernel(x_ref, att_ref, rwkv_ref, gate_ref, wa32_ref, wb32_ref, wo32_ref, g_ref,
                  wr_ref, br_ref, x1_ref, h2_ref, comb_ref, cnt_ref, wa_ref, wb_ref, wo_ref):
    @pl.when(pl.program_id(0) == 0)
    def _():
        wa_ref[...] = wa32_ref[...].astype(BF16)
        wb_ref[...] = wb32_ref[...].astype(BF16)
        wo_ref[...] = wo32_ref[...].astype(BF16)

    ga = gate_ref[:, 0:D_MODEL].astype(F32)
    gb = gate_ref[:, D_MODEL:2 * D_MODEL].astype(F32)
    merged = ga * _dot(att_ref[...], wa_ref[...]) + gb * _dot(rwkv_ref[...], wb_ref[...])
    x1 = x_ref[...] + _dot(merged.astype(BF16), wo_ref[...])
    x1_ref[...] = x1
    h2 = x1 * lax.rsqrt(jnp.mean(x1 * x1, axis=-1, keepdims=True) + RMS_EPS) * g_ref[...]
    h2_hi = h2.astype(BF16)
    h2_ref[...] = _pack_bf16(h2)
    h2_lo = (h2 - h2_hi.astype(F32)).astype(BF16)
    hw = _dot(h2_hi, wr_ref[...])
    lw = _dot(h2_lo, wr_ref[:, 0:ROUTER_LANES])
    logits = hw[:, 0:ROUTER_LANES] + (hw[:, ROUTER_LANES:2 * ROUTER_LANES] + lw) + br_ref[...]
    rec, counts = _route(logits)
    comb_ref[...] = rec
    cnt_ref[0] = jnp.broadcast_to(counts, (8, ROUTER_LANES)).astype(jnp.int32)


def _route(logits):
    lane_i = lax.broadcasted_iota(jnp.int32, logits.shape, 1)
    lane = lane_i.astype(F32)
    lane_group = (lane_i // EXPERTS_PER_GROUP).astype(F32)
    neg = jnp.finfo(F32).min
    big = float(ROUTER_LANES)

    def first_argmax(vals, mask):
        vm = jnp.where(mask, vals, neg)
        mx = jnp.max(vm, axis=-1, keepdims=True)
        idx = jnp.min(jnp.where(vm == mx, jnp.where(mask, lane, big), big), axis=-1, keepdims=True)
        return mx, idx

    is_group = (lane_i >= GROUP_LANE0) & (lane_i < GROUP_LANE0 + N_GROUPS)
    g_max, g_lane = first_argmax(logits, is_group)
    g_prob = 1.0 / jnp.sum(jnp.where(is_group, jnp.exp(logits - g_max), 0.0),
                           axis=-1, keepdims=True)
    g_idx = g_lane - GROUP_LANE0
    in_group = lane_group == g_idx
    e1, i1 = first_argmax(logits, in_group)
    e2, i2 = first_argmax(logits, in_group & (lane != i1))
    w2 = jnp.exp(e2 - e1)
    p1 = 1.0 / (1.0 + w2)
    p2 = w2 / (1.0 + w2)
    rec = jnp.where(lane_i == EXPERT1_LANE, i1, jnp.where(lane_i == EXPERT2_LANE, i2, 0.0))
    rec = jnp.where(lane_i == WEIGHT1_LANE, p1 * g_prob, jnp.where(lane_i == WEIGHT2_LANE, p2 * g_prob, rec))
    counts = jnp.sum(jnp.where(lane == i1, 1.0, 0.0) + jnp.where(lane == i2, 1.0, 0.0),
                     axis=0, keepdims=True)
    return rec, counts


def _merge(x2, att, rwkv, gates, wa, wb, wo, ln_g, wr, br):
    n = x2.shape[0]
    tm = MERGE_TILE
    row = lambda i: (i, 0)
    const = lambda i: (0, 0)
    return pl.pallas_call(
        _merge_kernel,
        grid=(n // tm,),
        in_specs=[
            pl.BlockSpec((tm, D_MODEL), row),
            pl.BlockSpec((tm, WIDTH), row),
            pl.BlockSpec((tm, WIDTH), row),
            pl.BlockSpec((tm, 2 * D_MODEL), row),
            pl.BlockSpec((WIDTH, D_MODEL), const),
            pl.BlockSpec((WIDTH, D_MODEL), const),
            pl.BlockSpec((D_MODEL, D_MODEL), const),
            pl.BlockSpec((1, D_MODEL), const),
            pl.BlockSpec((D_MODEL, 2 * ROUTER_LANES), const),
            pl.BlockSpec((1, ROUTER_LANES), const),
        ],
        out_specs=[
            pl.BlockSpec((tm, D_MODEL), row),
            pl.BlockSpec((tm, PACKED), row),
            pl.BlockSpec((tm, ROUTER_LANES), row),
            pl.BlockSpec((1, 8, ROUTER_LANES), lambda i: (i, 0, 0)),
        ],
        out_shape=[
            jax.ShapeDtypeStruct((n, D_MODEL), F32),
            jax.ShapeDtypeStruct((n, PACKED), jnp.uint32),
            jax.ShapeDtypeStruct((n, ROUTER_LANES), F32),
            jax.ShapeDtypeStruct((n // tm, 8, ROUTER_LANES), jnp.int32),
        ],
        scratch_shapes=[pltpu.VMEM((WIDTH, D_MODEL), BF16), pltpu.VMEM((WIDTH, D_MODEL), BF16),
                        pltpu.VMEM((D_MODEL, D_MODEL), BF16)],
        compiler_params=pltpu.CompilerParams(
            dimension_semantics=("arbitrary",), vmem_limit_bytes=VMEM_LIMIT),
        name="merge",
    )(x2, att, rwkv, gates, wa, wb, wo, ln_g, wr, br)


def _moe_plan_kernel(cnt_ref, base_ref, blk_ref):
    nt = cnt_ref.shape[0]
    cnt = cnt_ref[...].astype(F32)
    lane = lax.broadcasted_iota(jnp.int32, (8, ROUTER_LANES), 1)
    total = jnp.broadcast_to(jnp.sum(cnt, axis=0, keepdims=True), (8, ROUTER_LANES))
    padded = jnp.floor((total + (MOE_BLOCK - 1)) * (1.0 / MOE_BLOCK)) * MOE_BLOCK
    r = lax.broadcasted_iota(jnp.int32, (ROUTER_LANES, ROUTER_LANES), 0)
    c = lax.broadcasted_iota(jnp.int32, (ROUTER_LANES, ROUTER_LANES), 1)
    seg_start = _dot_hi(padded, jnp.where(r < c, 1.0, 0.0).astype(F32))
    tr = lax.broadcasted_iota(jnp.int32, (nt, nt), 0)
    tc = lax.broadcasted_iota(jnp.int32, (nt, nt), 1)
    tile_off = _dot_hi(jnp.where(tc < tr, 1.0, 0.0).astype(F32), cnt)
    base_ref[...] = seg_start[0:1, :] + tile_off
    seg_end = (seg_start + padded)[0:1, :]
    rows_total = jnp.sum(jnp.where(lane[0:1, :] < N_EXPERTS, padded[0:1, :], 0.0), axis=-1, keepdims=True)
    nblk = blk_ref.shape[0]
    blk_row = lax.broadcasted_iota(jnp.int32, (nblk, ROUTER_LANES), 0).astype(F32) * MOE_BLOCK
    blk_lane = lax.broadcasted_iota(jnp.int32, (nblk, ROUTER_LANES), 1)
    done = jnp.where((seg_end <= blk_row) & (blk_lane < N_EXPERTS), 1.0, 0.0)
    expert = jnp.minimum(jnp.sum(done, axis=-1, keepdims=True), N_EXPERTS - 1.0)
    valid = jnp.where(blk_row < rows_total, 1.0, 0.0)
    blk_idx = lax.broadcasted_iota(jnp.int32, (nblk, ROUTER_LANES), 0).astype(F32)
    source = jnp.minimum(blk_idx, rows_total * (1.0 / MOE_BLOCK) - 1.0)
    table = jnp.where(blk_lane == 0, expert, jnp.where(blk_lane == 1, valid,
                                                       jnp.where(blk_lane == 2, source, 0.0)))
    blk_ref[...] = table.astype(jnp.int32)


def _moe_plan(cnt, n_blocks):
    nt = cnt.shape[0]
    return pl.pallas_call(
        _moe_plan_kernel,
        out_shape=[jax.ShapeDtypeStruct((nt, ROUTER_LANES), F32),
                   jax.ShapeDtypeStruct((n_blocks, ROUTER_LANES), jnp.int32)],
        name="moe_plan",
    )(cnt)


def _moe_pos_kernel(comb_ref, base_ref, pos_ref):
    t = comb_ref.shape[0]
    comb = comb_ref[...]
    lane_i = lax.broadcasted_iota(jnp.int32, (t, ROUTER_LANES), 1)
    lane = lane_i.astype(F32)
    pick1 = jnp.where(lane == comb[:, EXPERT1_LANE:EXPERT1_LANE + 1], 1.0, 0.0)
    pick2 = jnp.where(lane == comb[:, EXPERT2_LANE:EXPERT2_LANE + 1], 1.0, 0.0)
    rows = lax.broadcasted_iota(jnp.int32, (t, t), 0)
    cols = lax.broadcasted_iota(jnp.int32, (t, t), 1)
    earlier = jnp.where(cols < rows, 1.0, 0.0).astype(BF16)
    before1 = _dot(earlier, pick1.astype(BF16))
    before2 = _dot(earlier, pick2.astype(BF16))
    base = base_ref[0]
    firsts = jnp.sum(pick1, axis=0, keepdims=True)
    pos1 = jnp.sum(pick1 * (base + before1), axis=-1, keepdims=True)
    pos2 = jnp.sum(pick2 * (base + firsts + before2), axis=-1, keepdims=True)
    both = jnp.where(lane_i == 0, pos1, jnp.where(lane_i == 1, pos2, 0.0))
    pos_ref[...] = jnp.transpose(both)[0:8, :].astype(jnp.int32)


def _moe_pos(comb, base):
    n = comb.shape[0]
    t = MERGE_TILE
    return pl.pallas_call(
        _moe_pos_kernel,
        grid=(n // t,),
        in_specs=[pl.BlockSpec((t, ROUTER_LANES), lambda i: (i, 0)),
                  pl.BlockSpec((1, 1, ROUTER_LANES), lambda i: (i, 0, 0))],
        out_specs=pl.BlockSpec((8, t), lambda i: (0, i)),
        out_shape=jax.ShapeDtypeStruct((8, n), jnp.int32),
        compiler_params=pltpu.CompilerParams(dimension_semantics=("arbitrary",)),
        name="moe_pos",
    )(comb, base.reshape(n // t, 1, ROUTER_LANES))


def _sc_mesh():
    return plsc.VectorSubcoreMesh(core_axis_name="core", subcore_axis_name="subcore")


def _sc_scratch():
    return [pltpu.VMEM((8, SC_WINDOW), jnp.int32), pltpu.VMEM((SC_WINDOW, PACKED), jnp.uint32)]


def _sc_scatter_rows(h, pos, n_rows):
    per_worker = h.shape[0] // (SC_WINDOW * SC_WORKERS)

    @pl.kernel(out_type=jax.ShapeDtypeStruct((n_rows, PACKED), jnp.uint32), mesh=_sc_mesh(),
               scratch_types=_sc_scratch())
    def scatter(h_hbm, pos_hbm, out_hbm, idx, buf):
        worker = lax.axis_index("core") * SC_SUBCORES + lax.axis_index("subcore")

        @pl.loop(0, per_worker)
        def _(b):
            start = (worker * per_worker + b) * SC_WINDOW
            pltpu.sync_copy(pos_hbm.at[:, pl.ds(start, SC_WINDOW)], idx)
            pltpu.sync_copy(h_hbm.at[pl.ds(start, SC_WINDOW)], buf)
            pltpu.sync_copy(buf, out_hbm.at[idx.at[0]])
            pltpu.sync_copy(buf, out_hbm.at[idx.at[1]])

    return scatter(h, pos)


def _sc_gather_rows(y, pos):
    n = pos.shape[1]
    per_worker = n // (SC_WINDOW * SC_WORKERS)
    out = jax.ShapeDtypeStruct((n, PACKED), jnp.uint32)

    @pl.kernel(out_type=(out, out), mesh=_sc_mesh(), scratch_types=_sc_scratch())
    def gather(y_hbm, pos_hbm, o1_hbm, o2_hbm, idx, buf):
        worker = lax.axis_index("core") * SC_SUBCORES + lax.axis_index("subcore")

        @pl.loop(0, per_worker)
        def _(b):
            start = (worker * per_worker + b) * SC_WINDOW
            pltpu.sync_copy(pos_hbm.at[:, pl.ds(start, SC_WINDOW)], idx)
            pltpu.sync_copy(y_hbm.at[idx.at[0]], buf)
            pltpu.sync_copy(buf, o1_hbm.at[pl.ds(start, SC_WINDOW)])
            pltpu.sync_copy(y_hbm.at[idx.at[1]], buf)
            pltpu.sync_copy(buf, o2_hbm.at[pl.ds(start, SC_WINDOW)])

    return gather(y, pos)


def _expert_mlp_kernel(expert_ref, valid_ref, source_ref, hs_ref, wg32_ref, wu32_ref, wd32_ref, y_ref,
                       wg_ref, wu_ref, wd_ref):
    b = pl.program_id(0)

    @pl.when(valid_ref[b] != 0)
    def _():
        @pl.when((b == 0) | (expert_ref[b] != expert_ref[jnp.maximum(b - 1, 0)]))
        def _():
            wg_ref[...] = wg32_ref[0].astype(BF16)
            wu_ref[...] = wu32_ref[0].astype(BF16)
            wd_ref[...] = wd32_ref[0].astype(BF16)

        half = D_MODEL // 2
        h_lo, h_hi = _unpack_bf16(hs_ref[...])
        h_lo = h_lo.astype(BF16)
        h_hi = h_hi.astype(BF16)
        hg = _dot(h_lo, wg_ref[0:half, :]) + _dot(h_hi, wg_ref[half:D_MODEL, :])
        hu = _dot(h_lo, wu_ref[0:half, :]) + _dot(h_hi, wu_ref[half:D_MODEL, :])
        act = (hg * _sigmoid(hg) * hu).astype(BF16)
        y_ref[...] = _pack_bf16(_dot(act, wd_ref[...]))


def _expert_mlp(blk_expert, blk_valid, blk_source, hs, wg, wu, wd):
    rows = lambda b, expert, valid, source: (source[b], 0)
    by_expert = lambda b, expert, valid, source: (expert[source[b]], 0, 0)
    grid_spec = pltpu.PrefetchScalarGridSpec(
        num_scalar_prefetch=3,
        grid=(hs.shape[0] // MOE_BLOCK,),
        in_specs=[
            pl.BlockSpec((MOE_BLOCK, PACKED), rows),
            pl.BlockSpec((1, D_MODEL, D_EXPERT), by_expert),
            pl.BlockSpec((1, D_MODEL, D_EXPERT), by_expert),
            pl.BlockSpec((1, D_EXPERT, D_MODEL), by_expert),
        ],
        out_specs=pl.BlockSpec((MOE_BLOCK, PACKED), rows),
        scratch_shapes=[pltpu.VMEM((D_MODEL, D_EXPERT), BF16), pltpu.VMEM((D_MODEL, D_EXPERT), BF16),
                        pltpu.VMEM((D_EXPERT, D_MODEL), BF16)],
    )
    return pl.pallas_call(
        _expert_mlp_kernel,
        grid_spec=grid_spec,
        out_shape=jax.ShapeDtypeStruct((hs.shape[0], PACKED), jnp.uint32),
        compiler_params=pltpu.CompilerParams(
            dimension_semantics=("arbitrary",), vmem_limit_bytes=VMEM_LIMIT),
        name="expert_mlp",
    )(blk_expert, blk_valid, blk_source, hs, wg, wu, wd)


def _moe_out_kernel(x1_ref, y1_ref, y2_ref, comb_ref, lnf_ref, out_ref, *, final_norm):
    w1 = comb_ref[:, WEIGHT1_LANE:WEIGHT1_LANE + 1]
    w2 = comb_ref[:, WEIGHT2_LANE:WEIGHT2_LANE + 1]
    a_lo, a_hi = _unpack_bf16(y1_ref[...])
    b_lo, b_hi = _unpack_bf16(y2_ref[...])
    moe = jnp.concatenate([w1 * a_lo + w2 * b_lo, w1 * a_hi + w2 * b_hi], axis=1)
    y = x1_ref[...] + moe
    if final_norm:
        y = y * lax.rsqrt(jnp.mean(y * y, axis=-1, keepdims=True) + RMS_EPS) * lnf_ref[...]
    out_ref[...] = y


def _moe_out(x1, y1, y2, comb, lnf, final_norm):
    n = x1.shape[0]
    t = FINAL_TILE
    row = lambda i: (i, 0)
    return pl.pallas_call(
        functools.partial(_moe_out_kernel, final_norm=final_norm),
        grid=(n // t,),
        in_specs=[pl.BlockSpec((t, D_MODEL), row), pl.BlockSpec((t, PACKED), row),
                  pl.BlockSpec((t, PACKED), row), pl.BlockSpec((t, ROUTER_LANES), row),
                  pl.BlockSpec((1, D_MODEL), lambda i: (0, 0))],
        out_specs=pl.BlockSpec((t, D_MODEL), row),
        out_shape=jax.ShapeDtypeStruct((n, D_MODEL), F32),
        compiler_params=pltpu.CompilerParams(
            dimension_semantics=("arbitrary",), vmem_limit_bytes=VMEM_LIMIT),
        name="moe_out",
    )(x1, y1, y2, comb, lnf)


def _moe(counts, x1, h2p, comb, wg, wu, wd, lnf, final_norm):
    n = x1.shape[0]
    n_blocks = (2 * n) // MOE_BLOCK + N_EXPERTS
    base, blk = _moe_plan(counts[:, 0, :], n_blocks)
    pos = _moe_pos(comb, base)
    hs = _sc_scatter_rows(h2p, pos, n_blocks * MOE_BLOCK)
    ys = _expert_mlp(blk[:, 0], blk[:, 1], blk[:, 2], hs, wg, wu, wd)
    y1, y2 = _sc_gather_rows(ys, pos)
    return _moe_out(x1, y1, y2, comb, lnf, final_norm)


def kernel(x, ln_mix_g, w_in, att_rel_bias, rwkv_mu, rwkv_w0, rwkv_w2, rwkv_a0, rwkv_a2, rwkv_g2,
           rwkv_k_k, rwkv_k_a, rwkv_r_k, rwkv_gn_g, rwkv_gn_b, w_branch_att, w_branch_rwkv, w_out,
           ln_ffn_g, router_group_w, router_group_b, router_expert_w, router_expert_b,
           expert_w_gate, expert_w_up, expert_w_down, ln_final_g):
    bsz, seq, d = x.shape
    depth = w_in.shape[0]
    n = bsz * seq
    x2 = x.reshape(n, d)
    for l in range(depth):
        q, k, v, rw, gates = _in_proj(x2, ln_mix_g[l][None, :], w_in[l].astype(BF16),
                                      rwkv_mu[l][None, :], seq)
        bias = _rel_bias(att_rel_bias[l])
        att = _band_attn(q.reshape(bsz, seq, WIDTH), k.reshape(bsz, seq, WIDTH),
                         v.reshape(bsz, seq, WIDTH), bias)
        zeros = jnp.zeros((DECAY_LORA, WIDTH), F32)
        w2a2 = jnp.concatenate(
            [jnp.concatenate([rwkv_w2[l], zeros], axis=1),
             jnp.concatenate([zeros, rwkv_a2[l]], axis=1)], axis=0)
        rwkv = _rwkv(rw.reshape(bsz, seq, RWKV_PROJ), w2a2.astype(BF16), rwkv_g2[l].astype(BF16),
                     rwkv_w0[l][None, :], rwkv_a0[l][None, :], rwkv_k_k[l][None, :],
                     rwkv_k_a[l][None, :], rwkv_r_k[l].reshape(1, WIDTH),
                     rwkv_gn_g[l][None, :], rwkv_gn_b[l][None, :])
        wr = jnp.concatenate([router_expert_w[l], router_group_w[l]], axis=1)
        wr = jnp.pad(wr, ((0, 0), (0, ROUTER_LANES - wr.shape[1])))
        wr_hi = wr.astype(BF16)
        wr = jnp.concatenate([wr_hi, (wr - wr_hi.astype(F32)).astype(BF16)], axis=1)
        br = jnp.concatenate([router_expert_b[l], router_group_b[l]])
        br = jnp.pad(br, (0, ROUTER_LANES - br.shape[0]))[None, :]
        x1, h2, comb, counts = _merge(x2, att.reshape(n, WIDTH), rwkv.reshape(n, WIDTH), gates,
                                      w_branch_att[l], w_branch_rwkv[l], w_out[l],
                                      ln_ffn_g[l][None, :], wr, br)
        x2 = _moe(counts, x1, h2, comb, expert_w_gate[l], expert_w_up[l], expert_w_down[l],
                  ln_final_g[None, :], final_norm=(l == depth - 1))
    return x2.reshape(bsz, seq, d)
```

```python
import functools
import math

import jax
import jax.numpy as jnp
from jax import lax
from jax.experimental import pallas as pl
from jax.experimental.pallas import tpu as pltpu
from jax.experimental.pallas import tpu_sc as plsc

F32 = jnp.float32
BF16 = jnp.bfloat16
HIGHEST = lax.Precision.HIGHEST

D_MODEL = 1024
CHUNK = 64
HEADS = 8
HEAD_DIM = 64
WIDTH = HEADS * HEAD_DIM
LEFT_CHUNKS = 8
BAND = (LEFT_CHUNKS + 1) * CHUNK
REL_CLIP = 64
N_REL = 2 * REL_CLIP + 1
DECAY_LORA = 64
AAA_LORA = 64
GATE_LORA = 128
GN_EPS = 64e-5
RMS_EPS = 1e-6
ATT_PROJ = 3 * WIDTH
RWKV_PROJ = 3 * WIDTH + DECAY_LORA + AAA_LORA + GATE_LORA
D_IN = ATT_PROJ + RWKV_PROJ + 2 * D_MODEL
N_GROUPS = 4
EXPERTS_PER_GROUP = 8
N_EXPERTS = N_GROUPS * EXPERTS_PER_GROUP
D_EXPERT = 256
RWKV_SEQS = 4
RWKV_CHUNKS = 4
BIAS_KEYS = 192
ATT_CHUNKS = 4
ATT_GROUP = 4
ROUTER_LANES = 128
GROUP_LANE0 = N_EXPERTS
EXPERT1_LANE, EXPERT2_LANE, WEIGHT1_LANE, WEIGHT2_LANE = 126, 125, 124, 123
MERGE_TILE = 512
MOE_BLOCK = 1024
PACKED = D_MODEL // 2
SC_WINDOW = 128
SC_CORES, SC_SUBCORES = 2, 16
SC_WORKERS = SC_CORES * SC_SUBCORES
FINAL_TILE = 1024

V7X_VMEM_BYTES = 64 * 1024 * 1024
VMEM_LIMIT = V7X_VMEM_BYTES - 12 * 1024 * 1024


def _dot(a, b):
    return jnp.dot(a, b, preferred_element_type=F32)


def _dot_hi(a, b):
    return jnp.dot(a, b, preferred_element_type=F32, precision=HIGHEST)


def _dot_nt(a, b, precision=None):
    return lax.dot_general(a, b, (((1,), (1,)), ((), ())),
                           preferred_element_type=F32, precision=precision)


def _dot_tn(a, b, precision=None):
    return lax.dot_general(a, b, (((0,), (0,)), ((), ())),
                           preferred_element_type=F32, precision=precision)


def _sigmoid(x):
    return 1.0 / (1.0 + jnp.exp(-x))


def _pack_bf16(x):
    w = x.shape[1] // 2
    hi = pltpu.bitcast(x[:, :w].astype(BF16).astype(F32), jnp.uint32)
    lo = pltpu.bitcast(x[:, w:].astype(BF16).astype(F32), jnp.uint32)
    return hi | lax.shift_right_logical(lo, jnp.uint32(16))


def _unpack_bf16(p):
    hi = pltpu.bitcast(p & jnp.uint32(0xFFFF0000), F32)
    lo = pltpu.bitcast(lax.shift_left(p, jnp.uint32(16)), F32)
    return hi, lo


def _mm(a, b):
    return jnp.dot(a.astype(BF16), b.astype(BF16), preferred_element_type=F32)


def _head_sums(x):
    outs = []
    lane = lax.broadcasted_iota(jnp.int32, (x.shape[0], 2 * HEAD_DIM), 1)
    low = lane < HEAD_DIM
    for p in range(HEADS // 2):
        xp = x[:, 2 * HEAD_DIM * p:2 * HEAD_DIM * (p + 1)]
        s_lo = jnp.sum(jnp.where(low, xp, 0.0), axis=-1, keepdims=True)
        s_hi = jnp.sum(jnp.where(low, 0.0, xp), axis=-1, keepdims=True)
        outs.append(jnp.where(low, s_lo, s_hi))
    return jnp.concatenate(outs, axis=-1)


def _rel_bias_kernel(tab_ref, out_ref):
    rows = tab_ref.shape[1]
    n = lax.broadcasted_iota(jnp.int32, (rows, CHUNK * 128), 1)
    r = lax.broadcasted_iota(jnp.int32, (rows, CHUNK * 128), 0)
    q = n >> 7
    kk = n & 127
    idx = jnp.clip(CHUNK + q - kk, -REL_CLIP, REL_CLIP) + REL_CLIP
    pick = jnp.where(r == idx, 1.0, 0.0) - jnp.where(r == N_REL - 1, 1.0, 0.0)
    out_ref[...] = _dot_hi(tab_ref[...], pick.astype(F32))


def _rel_bias(rel_table):
    rows = 136
    tab = jnp.pad(rel_table.astype(F32), ((0, 0), (0, rows - N_REL)))
    tail = pl.pallas_call(
        _rel_bias_kernel,
        out_shape=jax.ShapeDtypeStruct((HEADS, CHUNK * 128), F32),
        name="rel_bias",
    )(tab)
    tail = tail.reshape(HEADS, CHUNK, 128)
    bias = jnp.concatenate([jnp.zeros((HEADS, CHUNK, BIAS_KEYS - 128), F32), tail], axis=-1)
    return bias.reshape(HEADS * CHUNK, BIAS_KEYS)


def _in_proj_kernel(x_ref, g_ref, w_ref, mu_ref, q_ref, k_ref, v_ref, rw_ref, gate_ref,
                    carry_ref, *, tiles_per_seq):
    i = pl.program_id(0)

    @pl.when(i == 0)
    def _():
        carry_ref[...] = jnp.zeros(carry_ref.shape, F32)

    x = x_ref[...]
    h = x * lax.rsqrt(jnp.mean(x * x, axis=-1, keepdims=True) + RMS_EPS) * g_ref[...]
    hb = h.astype(BF16)
    q_ref[...] = _dot(hb, w_ref[:, 0:WIDTH]).astype(BF16)
    k_ref[...] = _dot(hb, w_ref[:, WIDTH:2 * WIDTH]).astype(BF16)
    v_ref[...] = _dot(hb, w_ref[:, 2 * WIDTH:ATT_PROJ]).astype(BF16)
    rw = _dot(hb, w_ref[:, ATT_PROJ:ATT_PROJ + RWKV_PROJ])
    tm = rw.shape[0]
    first_prev = jnp.where(i % tiles_per_seq == 0, 0.0, carry_ref[0:1, :])
    rolled = pltpu.roll(rw, 1, axis=0)
    row = lax.broadcasted_iota(jnp.int32, rw.shape, 0)
    prev = jnp.where(row == 0, first_prev, rolled)
    carry_ref[0:1, :] = rw[tm - 1:tm, :]
    rw_ref[...] = rw + (prev - rw) * mu_ref[...]
    gate_ref[...] = _sigmoid(_dot(hb, w_ref[:, ATT_PROJ + RWKV_PROJ:D_IN])).astype(BF16)


def _in_proj(x2, ln_g, w_in_b, mu, seq):
    n = x2.shape[0]
    tm = 1024
    assert seq % tm == 0, "a projection tile must not straddle two sequences (token shift carry)"
    row = lambda i: (i, 0)
    const = lambda i: (0, 0)
    return pl.pallas_call(
        functools.partial(_in_proj_kernel, tiles_per_seq=seq // tm),
        grid=(n // tm,),
        in_specs=[
            pl.BlockSpec((tm, D_MODEL), row),
            pl.BlockSpec((1, D_MODEL), const),
            pl.BlockSpec((D_MODEL, D_IN), const, pipeline_mode=pl.Buffered(1)),
            pl.BlockSpec((1, RWKV_PROJ), const),
        ],
        out_specs=[
            pl.BlockSpec((tm, WIDTH), row),
            pl.BlockSpec((tm, WIDTH), row),
            pl.BlockSpec((tm, WIDTH), row),
            pl.BlockSpec((tm, RWKV_PROJ), row),
            pl.BlockSpec((tm, 2 * D_MODEL), row),
        ],
        out_shape=[
            jax.ShapeDtypeStruct((n, WIDTH), BF16),
            jax.ShapeDtypeStruct((n, WIDTH), BF16),
            jax.ShapeDtypeStruct((n, WIDTH), BF16),
            jax.ShapeDtypeStruct((n, RWKV_PROJ), F32),
            jax.ShapeDtypeStruct((n, 2 * D_MODEL), BF16),
        ],
        scratch_shapes=[pltpu.VMEM((8, RWKV_PROJ), F32)],
        compiler_params=pltpu.CompilerParams(
            dimension_semantics=("arbitrary",), vmem_limit_bytes=VMEM_LIMIT),
        name="in_proj",
    )(x2, ln_g, w_in_b, mu)


def _band_attn_kernel(q_ref, k_ref, v_ref, bias_ref, o_ref, kpad_ref, vpad_ref):
    seq = k_ref.shape[1]
    pad = LEFT_CHUNKS * CHUNK
    kpad_ref[0:pad, :] = jnp.zeros((pad, WIDTH), BF16)
    vpad_ref[0:pad, :] = jnp.zeros((pad, WIDTH), BF16)
    kpad_ref[pad:pad + seq, :] = k_ref[0]
    vpad_ref[pad:pad + seq, :] = v_ref[0]

    gw = ATT_GROUP * HEAD_DIM
    rows = ATT_GROUP * CHUNK
    r_head = lax.broadcasted_iota(jnp.int32, (rows, gw), 0) // CHUNK
    l_head = lax.broadcasted_iota(jnp.int32, (rows, gw), 1) // HEAD_DIM
    own = r_head == l_head
    kpos_lo = lax.broadcasted_iota(jnp.int32, (rows, BAND - BIAS_KEYS), 1)
    kpos_hi = lax.broadcasted_iota(jnp.int32, (rows, BIAS_KEYS), 1) + (BAND - BIAS_KEYS)
    neg = jnp.finfo(F32).min
    groups = range(HEADS // ATT_GROUP)
    lanes = [slice(g * gw, (g + 1) * gw) for g in groups]

    def chunk_pair(i, carry, masked):
        units = [(j, g) for j in range(ATT_CHUNKS) for g in groups]
        ids = range(len(units))
        starts = [pl.multiple_of((i * ATT_CHUNKS + j) * CHUNK, CHUNK) for j in range(ATT_CHUNKS)]
        kb = [kpad_ref[pl.ds(st, BAND), :] for st in starts]
        vb = [vpad_ref[pl.ds(st, BAND), :] for st in starts]
        q = [q_ref[0, pl.ds(st, CHUNK), :] * (HEAD_DIM ** -0.5) for st in starts]
        qrows = [jnp.where(own, jnp.concatenate([q[j][:, lanes[g]]] * ATT_GROUP, axis=0),
                           jnp.zeros((), BF16)) for j, g in units]
        s = [_dot_nt(qrows[u], kb[j][:, lanes[g]]) for u, (j, g) in enumerate(units)]
        s_lo = [s[u][:, 0:BAND - BIAS_KEYS] for u in ids]
        s_hi = [s[u][:, BAND - BIAS_KEYS:BAND] + bias_ref[g * rows:(g + 1) * rows, :]
                for u, (j, g) in enumerate(units)]
        if masked:
            first = [(LEFT_CHUNKS - (i * ATT_CHUNKS + j)) * CHUNK for j in range(ATT_CHUNKS)]
            s_lo = [jnp.where(kpos_lo >= first[j], s_lo[u], neg) for u, (j, g) in enumerate(units)]
            s_hi = [jnp.where(kpos_hi >= first[j], s_hi[u], neg) for u, (j, g) in enumerate(units)]
        m = [jnp.maximum(jnp.max(s_lo[u], axis=-1, keepdims=True),
                         jnp.max(s_hi[u], axis=-1, keepdims=True)) for u in ids]
        p_lo = [jnp.exp(s_lo[u] - m[u]) for u in ids]
        p_hi = [jnp.exp(s_hi[u] - m[u]) for u in ids]
        denom = [jnp.sum(p_lo[u], axis=-1, keepdims=True) + jnp.sum(p_hi[u], axis=-1, keepdims=True)
                 for u in ids]
        o_all = [(_dot(p_lo[u].astype(BF16), vb[j][0:BAND - BIAS_KEYS, lanes[g]])
                  + _dot(p_hi[u].astype(BF16), vb[j][BAND - BIAS_KEYS:BAND, lanes[g]])) / denom[u]
                 for u, (j, g) in enumerate(units)]
        for u, (j, g) in enumerate(units):
            o_own = jnp.where(own, o_all[u], 0.0)
            o = o_own[0:CHUNK]
            for h in range(1, ATT_GROUP):
                o = o + o_own[h * CHUNK:(h + 1) * CHUNK]
            o_ref[0, pl.ds(starts[j], CHUNK), lanes[g]] = o.astype(BF16)
        return carry

    n_trips = seq // (CHUNK * ATT_CHUNKS)
    n_masked = min(LEFT_CHUNKS // ATT_CHUNKS, n_trips)
    lax.fori_loop(0, n_masked, functools.partial(chunk_pair, masked=True), 0)
    lax.fori_loop(n_masked, n_trips, functools.partial(chunk_pair, masked=False), 0)


def _band_attn(q, k, v, bias):
    b, seq, _ = q.shape
    whole = pl.BlockSpec((1, seq, WIDTH), lambda i: (i, 0, 0))
    return pl.pallas_call(
        _band_attn_kernel,
        grid=(b,),
        in_specs=[whole, whole, whole, pl.BlockSpec((HEADS * CHUNK, BIAS_KEYS), lambda i: (0, 0))],
        out_specs=whole,
        out_shape=jax.ShapeDtypeStruct((b, seq, WIDTH), BF16),
        scratch_shapes=[pltpu.VMEM((seq + LEFT_CHUNKS * CHUNK, WIDTH), BF16),
                        pltpu.VMEM((seq + LEFT_CHUNKS * CHUNK, WIDTH), BF16)],
        compiler_params=pltpu.CompilerParams(
            dimension_semantics=("arbitrary",), vmem_limit_bytes=VMEM_LIMIT),
        name="band_attn",
    )(q, k, v, bias)


def _rwkv_kernel(rw_ref, w2a2_ref, g2_ref, w0_ref, a0_ref, kk_ref, ka_ref, rk_ref,
                 gng_ref, gnb_ref, y_ref, state_ref):
    c = pl.program_id(1)
    t = CHUNK
    nb = rw_ref.shape[0]

    @pl.when(c == 0)
    def _():
        state_ref[...] = jnp.zeros(state_ref.shape, F32)

    lane = lax.broadcasted_iota(jnp.int32, (t, 2 * HEAD_DIM), 1)
    low = lane < HEAD_DIM
    r2 = lax.broadcasted_iota(jnp.int32, (2 * t, 2 * HEAD_DIM), 0)
    c2 = lax.broadcasted_iota(jnp.int32, (2 * t, 2 * HEAD_DIM), 1)
    own = (r2 < t) == (c2 < HEAD_DIM)
    strict = (r2 & (t - 1)) > (c2 & (t - 1))
    incl = (r2 & (t - 1)) >= (c2 & (t - 1))
    eye = jnp.where(r2 == c2, 1.0, 0.0).astype(F32)

    def stack2(xp):
        return jnp.concatenate([jnp.where(low, xp, 0.0), jnp.where(low, 0.0, xp)], axis=0)

    def wide(seqs, start):
        rows = len(seqs) * t
        rw = rw_ref[seqs[0]:seqs[-1] + 1, pl.ds(start, t), :].reshape(rows, RWKV_PROJ)
        r = rw[:, 0:WIDTH]
        k = rw[:, WIDTH:2 * WIDTH]
        v = rw[:, 2 * WIDTH:3 * WIDTH]
        lora = rw[:, 3 * WIDTH:3 * WIDTH + DECAY_LORA + AAA_LORA]
        g_lo = rw[:, 3 * WIDTH + DECAY_LORA + AAA_LORA:RWKV_PROJ]
        lane128 = lax.broadcasted_iota(jnp.int32, lora.shape, 1)
        lora = jnp.where(lane128 < DECAY_LORA, jnp.tanh(lora), lora)
        wa = _mm(lora, w2a2_ref[...])
        log_decay = -math.exp(-0.5) * _sigmoid(w0_ref[...] + wa[:, 0:WIDTH])
        lr = _sigmoid(a0_ref[...] + wa[:, WIDTH:2 * WIDTH])
        gate = _mm(_sigmoid(g_lo), g2_ref[...])
        kk_raw = k * kk_ref[...]
        k_mod = k * (1.0 + (lr - 1.0) * ka_ref[...])
        row = lax.broadcasted_iota(jnp.int32, (rows, rows), 0)
        col = lax.broadcasted_iota(jnp.int32, (rows, rows), 1)
        tri = jnp.where((row >= col) & ((row // t) == (col // t)), 1.0, 0.0).astype(BF16)
        ld1 = log_decay.astype(BF16)
        rem = log_decay - ld1.astype(F32)
        ld2 = rem.astype(BF16)
        ld3 = (rem - ld2.astype(F32)).astype(BF16)
        parts = _dot(tri, jnp.concatenate([ld1, ld2, ld3], axis=1))
        logp = parts[:, 0:WIDTH] + parts[:, WIDTH:2 * WIDTH] + parts[:, 2 * WIDTH:3 * WIDTH]
        p_in = jnp.exp(logp)
        p_ex = jnp.exp(logp - log_decay)
        p_inv = jnp.exp(-logp)
        kk = kk_raw / jnp.maximum(jnp.sqrt(_head_sums(kk_raw * kk_raw)), 1e-12)
        return dict(a_hat=-kk * p_ex, r_hat=r * p_in, b_hat=kk * lr * p_inv, k_hat=k_mod * p_inv,
                    v=v, p_in=p_in, gate=gate, bonus=_head_sums(r * k_mod * rk_ref[...]) * v)

    def front(seqs, w):
        chains = [(j, b, p) for j, b in enumerate(seqs) for p in range(HEADS // 2)]
        ids = range(len(chains))
        rs = [slice(j * t, (j + 1) * t) for j, _, _ in chains]
        ls = [slice(2 * HEAD_DIM * p, 2 * HEAD_DIM * (p + 1)) for _, _, p in chains]
        p_end = [w["p_in"][(j + 1) * t - 1:(j + 1) * t, ls[i]] for i, (j, _, _) in enumerate(chains)]
        ar = [jnp.concatenate([stack2(w["a_hat"][rs[i], ls[i]]), stack2(w["r_hat"][rs[i], ls[i]])],
                              axis=0).astype(BF16) for i in ids]
        bk2 = [jnp.concatenate([stack2(w["b_hat"][rs[i], ls[i]]), stack2(w["k_hat"][rs[i], ls[i]])], axis=0)
               for i in ids]
        bk = [bk2[i].astype(BF16) for i in ids]
        btkt = [(bk2[i] * p_end[i]).astype(BF16) for i in ids]
        v2 = [stack2(w["v"][rs[i], ls[i]]).astype(BF16) for i in ids]
        g = [_dot_nt(ar[i], bk[i]) for i in ids]
        st = [state_ref[b, p] for _, b, p in chains]
        ars = [_dot_nt(ar[i], st[i].astype(BF16)) for i in ids]
        l_ab = [jnp.where(strict, g[i][0:2 * t, 0:2 * t], 0.0) for i in ids]
        lm = [jnp.concatenate([jnp.where(strict, g[i][0:2 * t, 2 * t:4 * t], 0.0),
                               jnp.where(incl, g[i][2 * t:4 * t, 2 * t:4 * t], 0.0)], axis=0)
              for i in ids]
        m_rb = [jnp.where(incl, g[i][2 * t:4 * t, 0:2 * t], 0.0).astype(BF16) for i in ids]
        lv = [_mm(lm[i], v2[i]) for i in ids]
        return dict(chains=chains, rs=rs, ls=ls, p_end=p_end, btkt=btkt, v2=v2, st=st, ars=ars,
                    l_ab=l_ab, m_rb=m_rb, lv=lv)

    def inverse(f):
        w_inv = [eye + l for l in f["l_ab"]]
        l_pow = f["l_ab"]
        for _ in range(int(math.log2(t)) - 1):
            l_pow = [_mm(l, l) for l in l_pow]
            w_inv = [w + _mm(w, l) for w, l in zip(w_inv, l_pow)]
        return w_inv

    def back(w, f, w_inv, start):
        chains, rs, ls = f["chains"], f["rs"], f["ls"]
        ids = range(len(chains))
        z = [_mm(w_inv[i], f["ars"][i][0:2 * t] + f["lv"][i][0:2 * t]) for i in ids]
        y = [f["ars"][i][2 * t:4 * t] + f["lv"][i][2 * t:4 * t] + _mm(f["m_rb"][i], z[i]) for i in ids]
        for i, (_, b, p) in enumerate(chains):
            zv = jnp.concatenate([z[i].astype(BF16), f["v2"][i]], axis=0)
            state_ref[b, p] = f["st"][i] * f["p_end"][i] + _dot_tn(zv, f["btkt"][i])
        for i, (_, b, p) in enumerate(chains):
            mean = jnp.sum(y[i], axis=-1, keepdims=True) * (1.0 / HEAD_DIM)
            dev = jnp.where(own, y[i] - mean, 0.0)
            var = jnp.sum(dev * dev, axis=-1, keepdims=True) * (1.0 / HEAD_DIM)
            yn = dev * lax.rsqrt(var + GN_EPS)
            yn = yn[0:t] + yn[t:2 * t]
            out = ((yn * gng_ref[:, ls[i]] + gnb_ref[:, ls[i]] + w["bonus"][rs[i], ls[i]])
                   * w["gate"][rs[i], ls[i]])
            y_ref[b, pl.ds(start, t), ls[i]] = out.astype(BF16)

    seqs = list(range(nb))

    def chunk(j, carry):
        start = pl.multiple_of(j * t, t)
        w = wide(seqs, start)
        f = front(seqs, w)
        back(w, f, inverse(f), start)
        return carry

    lax.fori_loop(0, rw_ref.shape[1] // t, chunk, 0)


def _rwkv(rw, w2a2, g2, w0, a0, k_k, k_a, r_k, gn_g, gn_b):
    b, seq, _ = rw.shape
    nc = seq // CHUNK
    nb = RWKV_SEQS
    const = lambda i, c: (0, 0)
    vec = pl.BlockSpec((1, WIDTH), const)
    return pl.pallas_call(
        _rwkv_kernel,
        grid=(b // nb, nc // RWKV_CHUNKS),
        in_specs=[
            pl.BlockSpec((nb, CHUNK * RWKV_CHUNKS, RWKV_PROJ), lambda i, c: (i, c, 0)),
            pl.BlockSpec((DECAY_LORA + AAA_LORA, 2 * WIDTH), const),
            pl.BlockSpec((GATE_LORA, WIDTH), const),
            vec, vec, vec, vec, vec, vec, vec,
        ],
        out_specs=pl.BlockSpec((nb, CHUNK * RWKV_CHUNKS, WIDTH), lambda i, c: (i, c, 0)),
        out_shape=jax.ShapeDtypeStruct((b, seq, WIDTH), BF16),
        scratch_shapes=[pltpu.VMEM((nb, HEADS // 2, 2 * HEAD_DIM, 2 * HEAD_DIM), F32)],
        compiler_params=pltpu.CompilerParams(
            dimension_semantics=("arbitrary", "arbitrary"), vmem_limit_bytes=VMEM_LIMIT),
        name="rwkv7",
    )(rw, w2a2, g2, w0, a0, k_k, k_a, r_k, gn_g, gn_b)


def _merge_kernel(x_ref, att_ref, rwkv_ref, gate_ref, wa32_ref, wb32_ref, wo32_ref, g_ref,
                  wr_ref, br_ref, x1_ref, h2_ref, comb_ref, cnt_ref, wa_ref, wb_ref, wo_ref):
    @pl.when(pl.program_id(0) == 0)
    def _():
        wa_ref[...] = wa32_ref[...].astype(BF16)
        wb_ref[...] = wb32_ref[...].astype(BF16)
        wo_ref[...] = wo32_ref[...].astype(BF16)

    ga = gate_ref[:, 0:D_MODEL].astype(F32)
    gb = gate_ref[:, D_MODEL:2 * D_MODEL].astype(F32)
    merged = ga * _dot(att_ref[...], wa_ref[...]) + gb * _dot(rwkv_ref[...], wb_ref[...])
    x1 = x_ref[...] + _dot(merged.astype(BF16), wo_ref[...])
    x1_ref[...] = x1
    h2 = x1 * lax.rsqrt(jnp.mean(x1 * x1, axis=-1, keepdims=True) + RMS_EPS) * g_ref[...]
    h2_hi = h2.astype(BF16)
    h2_ref[...] = _pack_bf16(h2)
    h2_lo = (h2 - h2_hi.astype(F32)).astype(BF16)
    hw = _dot(h2_hi, wr_ref[...])
    lw = _dot(h2_lo, wr_ref[:, 0:ROUTER_LANES])
    logits = hw[:, 0:ROUTER_LANES] + (hw[:, ROUTER_LANES:2 * ROUTER_LANES] + lw) + br_ref[...]
    rec, counts = _route(logits)
    comb_ref[...] = rec
    cnt_ref[0] = jnp.broadcast_to(counts, (8, ROUTER_LANES)).astype(jnp.int32)


def _route(logits):
    lane_i = lax.broadcasted_iota(jnp.int32, logits.shape, 1)
    lane = lane_i.astype(F32)
    lane_group = (lane_i // EXPERTS_PER_GROUP).astype(F32)
    neg = jnp.finfo(F32).min
    big = float(ROUTER_LANES)

    def first_argmax(vals, mask):
        vm = jnp.where(mask, vals, neg)
        mx = jnp.max(vm, axis=-1, keepdims=True)
        idx = jnp.min(jnp.where(vm == mx, jnp.where(mask, lane, big), big), axis=-1, keepdims=True)
        return mx, idx

    is_group = (lane_i >= GROUP_LANE0) & (lane_i < GROUP_LANE0 + N_GROUPS)
    g_max, g_lane = first_argmax(logits, is_group)
    g_prob = 1.0 / jnp.sum(jnp.where(is_group, jnp.exp(logits - g_max), 0.0),
                           axis=-1, keepdims=True)
    g_idx = g_lane - GROUP_LANE0
    in_group = lane_group == g_idx
    e1, i1 = first_argmax(logits, in_group)
    e2, i2 = first_argmax(logits, in_group & (lane != i1))
    w2 = jnp.exp(e2 - e1)
    p1 = 1.0 / (1.0 + w2)
    p2 = w2 / (1.0 + w2)
    rec = jnp.where(lane_i == EXPERT1_LANE, i1, jnp.where(lane_i == EXPERT2_LANE, i2, 0.0))
    rec = jnp.where(lane_i == WEIGHT1_LANE, p1 * g_prob, jnp.where(lane_i == WEIGHT2_LANE, p2 * g_prob, rec))
    counts = jnp.sum(jnp.where(lane == i1, 1.0, 0.0) + jnp.where(lane == i2, 1.0, 0.0),
                     axis=0, keepdims=True)
    return rec, counts


def _merge(x2, att, rwkv, gates, wa, wb, wo, ln_g, wr, br):
    n = x2.shape[0]
    tm = MERGE_TILE
    row = lambda i: (i, 0)
    const = lambda i: (0, 0)
    return pl.pallas_call(
        _merge_kernel,
        grid=(n // tm,),
        in_specs=[
            pl.BlockSpec((tm, D_MODEL), row),
            pl.BlockSpec((tm, WIDTH), row),
            pl.BlockSpec((tm, WIDTH), row),
            pl.BlockSpec((tm, 2 * D_MODEL), row),
            pl.BlockSpec((WIDTH, D_MODEL), const),
            pl.BlockSpec((WIDTH, D_MODEL), const),
            pl.BlockSpec((D_MODEL, D_MODEL), const),
            pl.BlockSpec((1, D_MODEL), const),
            pl.BlockSpec((D_MODEL, 2 * ROUTER_LANES), const),
            pl.BlockSpec((1, ROUTER_LANES), const),
        ],
        out_specs=[
            pl.BlockSpec((tm, D_MODEL), row),
            pl.BlockSpec((tm, PACKED), row),
            pl.BlockSpec((tm, ROUTER_LANES), row),
            pl.BlockSpec((1, 8, ROUTER_LANES), lambda i: (i, 0, 0)),
        ],
        out_shape=[
            jax.ShapeDtypeStruct((n, D_MODEL), F32),
            jax.ShapeDtypeStruct((n, PACKED), jnp.uint32),
            jax.ShapeDtypeStruct((n, ROUTER_LANES), F32),
            jax.ShapeDtypeStruct((n // tm, 8, ROUTER_LANES), jnp.int32),
        ],
        scratch_shapes=[pltpu.VMEM((WIDTH, D_MODEL), BF16), pltpu.VMEM((WIDTH, D_MODEL), BF16),
                        pltpu.VMEM((D_MODEL, D_MODEL), BF16)],
        compiler_params=pltpu.CompilerParams(
            dimension_semantics=("arbitrary",), vmem_limit_bytes=VMEM_LIMIT),
        name="merge",
    )(x2, att, rwkv, gates, wa, wb, wo, ln_g, wr, br)


def _moe_plan_kernel(cnt_ref, base_ref, blk_ref):
    nt = cnt_ref.shape[0]
    cnt = cnt_ref[...].astype(F32)
    lane = lax.broadcasted_iota(jnp.int32, (8, ROUTER_LANES), 1)
    total = jnp.broadcast_to(jnp.sum(cnt, axis=0, keepdims=True), (8, ROUTER_LANES))
    padded = jnp.floor((total + (MOE_BLOCK - 1)) * (1.0 / MOE_BLOCK)) * MOE_BLOCK
    r = lax.broadcasted_iota(jnp.int32, (ROUTER_LANES, ROUTER_LANES), 0)
    c = lax.broadcasted_iota(jnp.int32, (ROUTER_LANES, ROUTER_LANES), 1)
    seg_start = _dot_hi(padded, jnp.where(r < c, 1.0, 0.0).astype(F32))
    tr = lax.broadcasted_iota(jnp.int32, (nt, nt), 0)
    tc = lax.broadcasted_iota(jnp.int32, (nt, nt), 1)
    tile_off = _dot_hi(jnp.where(tc < tr, 1.0, 0.0).astype(F32), cnt)
    base_ref[...] = seg_start[0:1, :] + tile_off
    seg_end = (seg_start + padded)[0:1, :]
    rows_total = jnp.sum(jnp.where(lane[0:1, :] < N_EXPERTS, padded[0:1, :], 0.0), axis=-1, keepdims=True)
    nblk = blk_ref.shape[0]
    blk_row = lax.broadcasted_iota(jnp.int32, (nblk, ROUTER_LANES), 0).astype(F32) * MOE_BLOCK
    blk_lane = lax.broadcasted_iota(jnp.int32, (nblk, ROUTER_LANES), 1)
    done = jnp.where((seg_end <= blk_row) & (blk_lane < N_EXPERTS), 1.0, 0.0)
    expert = jnp.minimum(jnp.sum(done, axis=-1, keepdims=True), N_EXPERTS - 1.0)
    valid = jnp.where(blk_row < rows_total, 1.0, 0.0)
    blk_idx = lax.broadcasted_iota(jnp.int32, (nblk, ROUTER_LANES), 0).astype(F32)
    source = jnp.minimum(blk_idx, rows_total * (1.0 / MOE_BLOCK) - 1.0)
    table = jnp.where(blk_lane == 0, expert, jnp.where(blk_lane == 1, valid,
                                                       jnp.where(blk_lane == 2, source, 0.0)))
    blk_ref[...] = table.astype(jnp.int32)


def _moe_plan(cnt, n_blocks):
    nt = cnt.shape[0]
    return pl.pallas_call(
        _moe_plan_kernel,
        out_shape=[jax.ShapeDtypeStruct((nt, ROUTER_LANES), F32),
                   jax.ShapeDtypeStruct((n_blocks, ROUTER_LANES), jnp.int32)],
        name="moe_plan",
    )(cnt)


def _moe_pos_kernel(comb_ref, base_ref, pos_ref, earlier_ref):
    t = comb_ref.shape[0]

    @pl.when(pl.program_id(0) == 0)
    def _():
        rows = lax.broadcasted_iota(jnp.int32, (t, t), 0)
        cols = lax.broadcasted_iota(jnp.int32, (t, t), 1)
        earlier_ref[...] = jnp.where(cols < rows, 1.0, 0.0).astype(BF16)

    comb = comb_ref[...]
    lane_i = lax.broadcasted_iota(jnp.int32, (t, ROUTER_LANES), 1)
    lane = lane_i.astype(F32)
    pick1 = jnp.where(lane == comb[:, EXPERT1_LANE:EXPERT1_LANE + 1], 1.0, 0.0)
    pick2 = jnp.where(lane == comb[:, EXPERT2_LANE:EXPERT2_LANE + 1], 1.0, 0.0)
    earlier = earlier_ref[...]
    before1 = _dot(earlier, pick1.astype(BF16))
    before2 = _dot(earlier, pick2.astype(BF16))
    base = base_ref[0]
    firsts = jnp.sum(pick1, axis=0, keepdims=True)
    pos1 = jnp.sum(pick1 * (base + before1), axis=-1, keepdims=True)
    pos2 = jnp.sum(pick2 * (base + firsts + before2), axis=-1, keepdims=True)
    both = jnp.where(lane_i == 0, pos1, jnp.where(lane_i == 1, pos2, 0.0))
    pos_ref[...] = jnp.transpose(both)[0:8, :].astype(jnp.int32)


def _moe_pos(comb, base):
    n = comb.shape[0]
    t = MERGE_TILE
    return pl.pallas_call(
        _moe_pos_kernel,
        grid=(n // t,),
        in_specs=[pl.BlockSpec((t, ROUTER_LANES), lambda i: (i, 0)),
                  pl.BlockSpec((1, 1, ROUTER_LANES), lambda i: (i, 0, 0))],
        out_specs=pl.BlockSpec((8, t), lambda i: (0, i)),
        out_shape=jax.ShapeDtypeStruct((8, n), jnp.int32),
        scratch_shapes=[pltpu.VMEM((t, t), BF16)],
        compiler_params=pltpu.CompilerParams(dimension_semantics=("arbitrary",)),
        name="moe_pos",
    )(comb, base.reshape(n // t, 1, ROUTER_LANES))


def _sc_mesh():
    return plsc.VectorSubcoreMesh(core_axis_name="core", subcore_axis_name="subcore")


def _sc_scratch():
    return [pltpu.VMEM((8, SC_WINDOW), jnp.int32), pltpu.VMEM((SC_WINDOW, PACKED), jnp.uint32)]


def _sc_scatter_rows(h, pos, n_rows):
    per_worker = h.shape[0] // (SC_WINDOW * SC_WORKERS)

    @pl.kernel(out_type=jax.ShapeDtypeStruct((n_rows, PACKED), jnp.uint32), mesh=_sc_mesh(),
               scratch_types=_sc_scratch())
    def scatter(h_hbm, pos_hbm, out_hbm, idx, buf):
        worker = lax.axis_index("core") * SC_SUBCORES + lax.axis_index("subcore")

        @pl.loop(0, per_worker)
        def _(b):
            start = (worker * per_worker + b) * SC_WINDOW
            pltpu.sync_copy(pos_hbm.at[:, pl.ds(start, SC_WINDOW)], idx)
            pltpu.sync_copy(h_hbm.at[pl.ds(start, SC_WINDOW)], buf)
            pltpu.sync_copy(buf, out_hbm.at[idx.at[0]])
            pltpu.sync_copy(buf, out_hbm.at[idx.at[1]])

    return scatter(h, pos)


def _sc_gather_rows(y, pos):
    n = pos.shape[1]
    per_worker = n // (SC_WINDOW * SC_WORKERS)
    out = jax.ShapeDtypeStruct((n, PACKED), jnp.uint32)

    @pl.kernel(out_type=(out, out), mesh=_sc_mesh(), scratch_types=_sc_scratch())
    def gather(y_hbm, pos_hbm, o1_hbm, o2_hbm, idx, buf):
        worker = lax.axis_index("core") * SC_SUBCORES + lax.axis_index("subcore")

        @pl.loop(0, per_worker)
        def _(b):
            start = (worker * per_worker + b) * SC_WINDOW
            pltpu.sync_copy(pos_hbm.at[:, pl.ds(start, SC_WINDOW)], idx)
            pltpu.sync_copy(y_hbm.at[idx.at[0]], buf)
            pltpu.sync_copy(buf, o1_hbm.at[pl.ds(start, SC_WINDOW)])
            pltpu.sync_copy(y_hbm.at[idx.at[1]], buf)
            pltpu.sync_copy(buf, o2_hbm.at[pl.ds(start, SC_WINDOW)])

    return gather(y, pos)


def _expert_mlp_kernel(expert_ref, valid_ref, source_ref, hs_ref, wg32_ref, wu32_ref, wd32_ref, y_ref,
                       wg_ref, wu_ref, wd_ref):
    b = pl.program_id(0)

    @pl.when(valid_ref[b] != 0)
    def _():
        @pl.when((b == 0) | (expert_ref[b] != expert_ref[jnp.maximum(b - 1, 0)]))
        def _():
            wg_ref[...] = wg32_ref[0].astype(BF16)
            wu_ref[...] = wu32_ref[0].astype(BF16)
            wd_ref[...] = wd32_ref[0].astype(BF16)

        half = D_MODEL // 2
        h_lo, h_hi = _unpack_bf16(hs_ref[...])
        h_lo = h_lo.astype(BF16)
        h_hi = h_hi.astype(BF16)
        hg = _dot(h_lo, wg_ref[0:half, :]) + _dot(h_hi, wg_ref[half:D_MODEL, :])
        hu = _dot(h_lo, wu_ref[0:half, :]) + _dot(h_hi, wu_ref[half:D_MODEL, :])
        act = (hg * _sigmoid(hg) * hu).astype(BF16)
        y_ref[...] = _pack_bf16(_dot(act, wd_ref[...]))


def _expert_mlp(blk_expert, blk_valid, blk_source, hs, wg, wu, wd):
    rows = lambda b, expert, valid, source: (source[b], 0)
    by_expert = lambda b, expert, valid, source: (expert[source[b]], 0, 0)
    grid_spec = pltpu.PrefetchScalarGridSpec(
        num_scalar_prefetch=3,
        grid=(hs.shape[0] // MOE_BLOCK,),
        in_specs=[
            pl.BlockSpec((MOE_BLOCK, PACKED), rows),
            pl.BlockSpec((1, D_MODEL, D_EXPERT), by_expert),
            pl.BlockSpec((1, D_MODEL, D_EXPERT), by_expert),
            pl.BlockSpec((1, D_EXPERT, D_MODEL), by_expert),
        ],
        out_specs=pl.BlockSpec((MOE_BLOCK, PACKED), rows),
        scratch_shapes=[pltpu.VMEM((D_MODEL, D_EXPERT), BF16), pltpu.VMEM((D_MODEL, D_EXPERT), BF16),
                        pltpu.VMEM((D_EXPERT, D_MODEL), BF16)],
    )
    return pl.pallas_call(
        _expert_mlp_kernel,
        grid_spec=grid_spec,
        out_shape=jax.ShapeDtypeStruct((hs.shape[0], PACKED), jnp.uint32),
        compiler_params=pltpu.CompilerParams(
            dimension_semantics=("arbitrary",), vmem_limit_bytes=VMEM_LIMIT),
        name="expert_mlp",
    )(blk_expert, blk_valid, blk_source, hs, wg, wu, wd)


def _moe_out_kernel(x1_ref, y1_ref, y2_ref, comb_ref, lnf_ref, out_ref, *, final_norm):
    w1 = comb_ref[:, WEIGHT1_LANE:WEIGHT1_LANE + 1]
    w2 = comb_ref[:, WEIGHT2_LANE:WEIGHT2_LANE + 1]
    a_lo, a_hi = _unpack_bf16(y1_ref[...])
    b_lo, b_hi = _unpack_bf16(y2_ref[...])
    moe = jnp.concatenate([w1 * a_lo + w2 * b_lo, w1 * a_hi + w2 * b_hi], axis=1)
    y = x1_ref[...] + moe
    if final_norm:
        y = y * lax.rsqrt(jnp.mean(y * y, axis=-1, keepdims=True) + RMS_EPS) * lnf_ref[...]
    out_ref[...] = y


def _moe_out(x1, y1, y2, comb, lnf, final_norm):
    n = x1.shape[0]
    t = FINAL_TILE
    row = lambda i: (i, 0)
    return pl.pallas_call(
        functools.partial(_moe_out_kernel, final_norm=final_norm),
        grid=(n // t,),
        in_specs=[pl.BlockSpec((t, D_MODEL), row), pl.BlockSpec((t, PACKED), row),
                  pl.BlockSpec((t, PACKED), row), pl.BlockSpec((t, ROUTER_LANES), row),
                  pl.BlockSpec((1, D_MODEL), lambda i: (0, 0))],
        out_specs=pl.BlockSpec((t, D_MODEL), row),
        out_shape=jax.ShapeDtypeStruct((n, D_MODEL), F32),
        compiler_params=pltpu.CompilerParams(
            dimension_semantics=("arbitrary",), vmem_limit_bytes=VMEM_LIMIT),
        name="moe_out",
    )(x1, y1, y2, comb, lnf)


def _moe(counts, x1, h2p, comb, wg, wu, wd, lnf, final_norm):
    n = x1.shape[0]
    n_blocks = (2 * n) // MOE_BLOCK + N_EXPERTS
    base, blk = _moe_plan(counts[:, 0, :], n_blocks)
    pos = _moe_pos(comb, base)
    hs = _sc_scatter_rows(h2p, pos, n_blocks * MOE_BLOCK)
    ys = _expert_mlp(blk[:, 0], blk[:, 1], blk[:, 2], hs, wg, wu, wd)
    y1, y2 = _sc_gather_rows(ys, pos)
    return _moe_out(x1, y1, y2, comb, lnf, final_norm)


def kernel(x, ln_mix_g, w_in, att_rel_bias, rwkv_mu, rwkv_w0, rwkv_w2, rwkv_a0, rwkv_a2, rwkv_g2,
           rwkv_k_k, rwkv_k_a, rwkv_r_k, rwkv_gn_g, rwkv_gn_b, w_branch_att, w_branch_rwkv, w_out,
           ln_ffn_g, router_group_w, router_group_b, router_expert_w, router_expert_b,
           expert_w_gate, expert_w_up, expert_w_down, ln_final_g):
    bsz, seq, d = x.shape
    depth = w_in.shape[0]
    n = bsz * seq
    x2 = x.reshape(n, d)
    for l in range(depth):
        q, k, v, rw, gates = _in_proj(x2, ln_mix_g[l][None, :], w_in[l].astype(BF16),
                                      rwkv_mu[l][None, :], seq)
        bias = _rel_bias(att_rel_bias[l])
        att = _band_attn(q.reshape(bsz, seq, WIDTH), k.reshape(bsz, seq, WIDTH),
                         v.reshape(bsz, seq, WIDTH), bias)
        zeros = jnp.zeros((DECAY_LORA, WIDTH), F32)
        w2a2 = jnp.concatenate(
            [jnp.concatenate([rwkv_w2[l], zeros], axis=1),
             jnp.concatenate([zeros, rwkv_a2[l]], axis=1)], axis=0)
        rwkv = _rwkv(rw.reshape(bsz, seq, RWKV_PROJ), w2a2.astype(BF16), rwkv_g2[l].astype(BF16),
                     rwkv_w0[l][None, :], rwkv_a0[l][None, :], rwkv_k_k[l][None, :],
                     rwkv_k_a[l][None, :], rwkv_r_k[l].reshape(1, WIDTH),
                     rwkv_gn_g[l][None, :], rwkv_gn_b[l][None, :])
        wr = jnp.concatenate([router_expert_w[l], router_group_w[l]], axis=1)
        wr = jnp.pad(wr, ((0, 0), (0, ROUTER_LANES - wr.shape[1])))
        wr_hi = wr.astype(BF16)
        wr = jnp.concatenate([wr_hi, (wr - wr_hi.astype(F32)).astype(BF16)], axis=1)
        br = jnp.concatenate([router_expert_b[l], router_group_b[l]])
        br = jnp.pad(br, (0, ROUTER_LANES - br.shape[0]))[None, :]
        x1, h2, comb, counts = _merge(x2, att.reshape(n, WIDTH), rwkv.reshape(n, WIDTH), gates,
                                      w_branch_att[l], w_branch_rwkv[l], w_out[l],
                                      ln_ffn_g[l][None, :], wr, br)
        x2 = _moe(counts, x1, h2, comb, expert_w_gate[l], expert_w_up[l], expert_w_down[l],
                  ln_final_g[None, :], final_norm=(l == depth - 1))
    return x2.reshape(bsz, seq, d)
```

```python
import functools
import math

import jax
import jax.numpy as jnp
from jax import lax
from jax.experimental import pallas as pl
from jax.experimental.pallas import tpu as pltpu
from jax.experimental.pallas import tpu_sc as plsc

F32 = jnp.float32
BF16 = jnp.bfloat16
HIGHEST = lax.Precision.HIGHEST

D_MODEL = 1024
CHUNK = 64
HEADS = 8
HEAD_DIM = 64
WIDTH = HEADS * HEAD_DIM
LEFT_CHUNKS = 8
BAND = (LEFT_CHUNKS + 1) * CHUNK
REL_CLIP = 64
N_REL = 2 * REL_CLIP + 1
DECAY_LORA = 64
AAA_LORA = 64
GATE_LORA = 128
GN_EPS = 64e-5
RMS_EPS = 1e-6
ATT_PROJ = 3 * WIDTH
RWKV_PROJ = 3 * WIDTH + DECAY_LORA + AAA_LORA + GATE_LORA
D_IN = ATT_PROJ + RWKV_PROJ + 2 * D_MODEL
N_GROUPS = 4
EXPERTS_PER_GROUP = 8
N_EXPERTS = N_GROUPS * EXPERTS_PER_GROUP
D_EXPERT = 256
RWKV_SEQS = 4
RWKV_CHUNKS = 4
BIAS_KEYS = 192
ATT_CHUNKS = 8
ATT_GROUP = 4
ROUTER_LANES = 128
GROUP_LANE0 = N_EXPERTS
EXPERT1_LANE, EXPERT2_LANE, WEIGHT1_LANE, WEIGHT2_LANE = 126, 125, 124, 123
MERGE_TILE = 512
MOE_BLOCK = 1024
PACKED = D_MODEL // 2
SC_WINDOW = 128
SC_CORES, SC_SUBCORES = 2, 16
SC_WORKERS = SC_CORES * SC_SUBCORES
FINAL_TILE = 1024

V7X_VMEM_BYTES = 64 * 1024 * 1024
VMEM_LIMIT = V7X_VMEM_BYTES - 12 * 1024 * 1024


def _dot(a, b):
    return jnp.dot(a, b, preferred_element_type=F32)


def _dot_hi(a, b):
    return jnp.dot(a, b, preferred_element_type=F32, precision=HIGHEST)


def _dot_nt(a, b, precision=None):
    return lax.dot_general(a, b, (((1,), (1,)), ((), ())),
                           preferred_element_type=F32, precision=precision)


def _dot_tn(a, b, precision=None):
    return lax.dot_general(a, b, (((0,), (0,)), ((), ())),
                           preferred_element_type=F32, precision=precision)


def _sigmoid(x):
    return 1.0 / (1.0 + jnp.exp(-x))


def _pack_bf16(x):
    w = x.shape[1] // 2
    hi = pltpu.bitcast(x[:, :w].astype(BF16).astype(F32), jnp.uint32)
    lo = pltpu.bitcast(x[:, w:].astype(BF16).astype(F32), jnp.uint32)
    return hi | lax.shift_right_logical(lo, jnp.uint32(16))


def _unpack_bf16(p):
    hi = pltpu.bitcast(p & jnp.uint32(0xFFFF0000), F32)
    lo = pltpu.bitcast(lax.shift_left(p, jnp.uint32(16)), F32)
    return hi, lo


def _mm(a, b):
    return jnp.dot(a.astype(BF16), b.astype(BF16), preferred_element_type=F32)


def _head_sums(x):
    outs = []
    lane = lax.broadcasted_iota(jnp.int32, (x.shape[0], 2 * HEAD_DIM), 1)
    low = lane < HEAD_DIM
    for p in range(HEADS // 2):
        xp = x[:, 2 * HEAD_DIM * p:2 * HEAD_DIM * (p + 1)]
        s_lo = jnp.sum(jnp.where(low, xp, 0.0), axis=-1, keepdims=True)
        s_hi = jnp.sum(jnp.where(low, 0.0, xp), axis=-1, keepdims=True)
        outs.append(jnp.where(low, s_lo, s_hi))
    return jnp.concatenate(outs, axis=-1)


def _rel_bias_kernel(tab_ref, out_ref):
    rows = tab_ref.shape[1]
    n = lax.broadcasted_iota(jnp.int32, (rows, CHUNK * 128), 1)
    r = lax.broadcasted_iota(jnp.int32, (rows, CHUNK * 128), 0)
    q = n >> 7
    kk = n & 127
    idx = jnp.clip(CHUNK + q - kk, -REL_CLIP, REL_CLIP) + REL_CLIP
    pick = jnp.where(r == idx, 1.0, 0.0) - jnp.where(r == N_REL - 1, 1.0, 0.0)
    out_ref[...] = _dot_hi(tab_ref[...], pick.astype(F32))


def _rel_bias(rel_table):
    rows = 136
    tab = jnp.pad(rel_table.astype(F32), ((0, 0), (0, rows - N_REL)))
    tail = pl.pallas_call(
        _rel_bias_kernel,
        out_shape=jax.ShapeDtypeStruct((HEADS, CHUNK * 128), F32),
        name="rel_bias",
    )(tab)
    tail = tail.reshape(HEADS, CHUNK, 128)
    bias = jnp.concatenate([jnp.zeros((HEADS, CHUNK, BIAS_KEYS - 128), F32), tail], axis=-1)
    return bias.reshape(HEADS * CHUNK, BIAS_KEYS)


def _in_proj_kernel(x_ref, g_ref, w_ref, mu_ref, q_ref, k_ref, v_ref, rw_ref, gate_ref,
                    carry_ref, *, tiles_per_seq):
    i = pl.program_id(0)

    @pl.when(i == 0)
    def _():
        carry_ref[...] = jnp.zeros(carry_ref.shape, F32)

    x = x_ref[...]
    h = x * lax.rsqrt(jnp.mean(x * x, axis=-1, keepdims=True) + RMS_EPS) * g_ref[...]
    hb = h.astype(BF16)
    q_ref[...] = _dot(hb, w_ref[:, 0:WIDTH]).astype(BF16)
    k_ref[...] = _dot(hb, w_ref[:, WIDTH:2 * WIDTH]).astype(BF16)
    v_ref[...] = _dot(hb, w_ref[:, 2 * WIDTH:ATT_PROJ]).astype(BF16)
    rw = _dot(hb, w_ref[:, ATT_PROJ:ATT_PROJ + RWKV_PROJ])
    tm = rw.shape[0]
    first_prev = jnp.where(i % tiles_per_seq == 0, 0.0, carry_ref[0:1, :])
    rolled = pltpu.roll(rw, 1, axis=0)
    row = lax.broadcasted_iota(jnp.int32, rw.shape, 0)
    prev = jnp.where(row == 0, first_prev, rolled)
    carry_ref[0:1, :] = rw[tm - 1:tm, :]
    rw_ref[...] = rw + (prev - rw) * mu_ref[...]
    gate_ref[...] = _sigmoid(_dot(hb, w_ref[:, ATT_PROJ + RWKV_PROJ:D_IN])).astype(BF16)


def _in_proj(x2, ln_g, w_in_b, mu, seq):
    n = x2.shape[0]
    tm = 1024
    assert seq % tm == 0, "a projection tile must not straddle two sequences (token shift carry)"
    row = lambda i: (i, 0)
    const = lambda i: (0, 0)
    return pl.pallas_call(
        functools.partial(_in_proj_kernel, tiles_per_seq=seq // tm),
        grid=(n // tm,),
        in_specs=[
            pl.BlockSpec((tm, D_MODEL), row),
            pl.BlockSpec((1, D_MODEL), const),
            pl.BlockSpec((D_MODEL, D_IN), const, pipeline_mode=pl.Buffered(1)),
            pl.BlockSpec((1, RWKV_PROJ), const),
        ],
        out_specs=[
            pl.BlockSpec((tm, WIDTH), row),
            pl.BlockSpec((tm, WIDTH), row),
            pl.BlockSpec((tm, WIDTH), row),
            pl.BlockSpec((tm, RWKV_PROJ), row),
            pl.BlockSpec((tm, 2 * D_MODEL), row),
        ],
        out_shape=[
            jax.ShapeDtypeStruct((n, WIDTH), BF16),
            jax.ShapeDtypeStruct((n, WIDTH), BF16),
            jax.ShapeDtypeStruct((n, WIDTH), BF16),
            jax.ShapeDtypeStruct((n, RWKV_PROJ), F32),
            jax.ShapeDtypeStruct((n, 2 * D_MODEL), BF16),
        ],
        scratch_shapes=[pltpu.VMEM((8, RWKV_PROJ), F32)],
        compiler_params=pltpu.CompilerParams(
            dimension_semantics=("arbitrary",), vmem_limit_bytes=VMEM_LIMIT),
        name="in_proj",
    )(x2, ln_g, w_in_b, mu)


def _band_attn_kernel(q_ref, k_ref, v_ref, bias_ref, o_ref, kpad_ref, vpad_ref):
    seq = k_ref.shape[1]
    pad = LEFT_CHUNKS * CHUNK
    kpad_ref[0:pad, :] = jnp.zeros((pad, WIDTH), BF16)
    vpad_ref[0:pad, :] = jnp.zeros((pad, WIDTH), BF16)
    kpad_ref[pad:pad + seq, :] = k_ref[0]
    vpad_ref[pad:pad + seq, :] = v_ref[0]

    gw = ATT_GROUP * HEAD_DIM
    rows = ATT_GROUP * CHUNK
    r_head = lax.broadcasted_iota(jnp.int32, (rows, gw), 0) // CHUNK
    l_head = lax.broadcasted_iota(jnp.int32, (rows, gw), 1) // HEAD_DIM
    own = r_head == l_head
    kpos_lo = lax.broadcasted_iota(jnp.int32, (rows, BAND - BIAS_KEYS), 1)
    kpos_hi = lax.broadcasted_iota(jnp.int32, (rows, BIAS_KEYS), 1) + (BAND - BIAS_KEYS)
    neg = jnp.finfo(F32).min
    groups = range(HEADS // ATT_GROUP)
    lanes = [slice(g * gw, (g + 1) * gw) for g in groups]

    def chunk_pair(i, carry, masked):
        units = [(j, g) for j in range(ATT_CHUNKS) for g in groups]
        ids = range(len(units))
        starts = [pl.multiple_of((i * ATT_CHUNKS + j) * CHUNK, CHUNK) for j in range(ATT_CHUNKS)]
        kb = [kpad_ref[pl.ds(st, BAND), :] for st in starts]
        vb = [vpad_ref[pl.ds(st, BAND), :] for st in starts]
        q = [q_ref[0, pl.ds(st, CHUNK), :] * (HEAD_DIM ** -0.5) for st in starts]
        qrows = [jnp.where(own, jnp.concatenate([q[j][:, lanes[g]]] * ATT_GROUP, axis=0),
                           jnp.zeros((), BF16)) for j, g in units]
        s = [_dot_nt(qrows[u], kb[j][:, lanes[g]]) for u, (j, g) in enumerate(units)]
        s_lo = [s[u][:, 0:BAND - BIAS_KEYS] for u in ids]
        s_hi = [s[u][:, BAND - BIAS_KEYS:BAND] + bias_ref[g * rows:(g + 1) * rows, :]
                for u, (j, g) in enumerate(units)]
        if masked:
            first = [(LEFT_CHUNKS - (i * ATT_CHUNKS + j)) * CHUNK for j in range(ATT_CHUNKS)]
            s_lo = [jnp.where(kpos_lo >= first[j], s_lo[u], neg) for u, (j, g) in enumerate(units)]
            s_hi = [jnp.where(kpos_hi >= first[j], s_hi[u], neg) for u, (j, g) in enumerate(units)]
        m = [jnp.maximum(jnp.max(s_lo[u], axis=-1, keepdims=True),
                         jnp.max(s_hi[u], axis=-1, keepdims=True)) for u in ids]
        p_lo = [jnp.exp(s_lo[u] - m[u]) for u in ids]
        p_hi = [jnp.exp(s_hi[u] - m[u]) for u in ids]
        denom = [jnp.sum(p_lo[u], axis=-1, keepdims=True) + jnp.sum(p_hi[u], axis=-1, keepdims=True)
                 for u in ids]
        o_all = [(_dot(p_lo[u].astype(BF16), vb[j][0:BAND - BIAS_KEYS, lanes[g]])
                  + _dot(p_hi[u].astype(BF16), vb[j][BAND - BIAS_KEYS:BAND, lanes[g]])) / denom[u]
                 for u, (j, g) in enumerate(units)]
        for u, (j, g) in enumerate(units):
            o_own = jnp.where(own, o_all[u], 0.0)
            o = o_own[0:CHUNK]
            for h in range(1, ATT_GROUP):
                o = o + o_own[h * CHUNK:(h + 1) * CHUNK]
            o_ref[0, pl.ds(starts[j], CHUNK), lanes[g]] = o.astype(BF16)
        return carry

    n_trips = seq // (CHUNK * ATT_CHUNKS)
    n_masked = min(LEFT_CHUNKS // ATT_CHUNKS, n_trips)
    lax.fori_loop(0, n_masked, functools.partial(chunk_pair, masked=True), 0)
    lax.fori_loop(n_masked, n_trips, functools.partial(chunk_pair, masked=False), 0)


def _band_attn(q, k, v, bias):
    b, seq, _ = q.shape
    whole = pl.BlockSpec((1, seq, WIDTH), lambda i: (i, 0, 0))
    return pl.pallas_call(
        _band_attn_kernel,
        grid=(b,),
        in_specs=[whole, whole, whole, pl.BlockSpec((HEADS * CHUNK, BIAS_KEYS), lambda i: (0, 0))],
        out_specs=whole,
        out_shape=jax.ShapeDtypeStruct((b, seq, WIDTH), BF16),
        scratch_shapes=[pltpu.VMEM((seq + LEFT_CHUNKS * CHUNK, WIDTH), BF16),
                        pltpu.VMEM((seq + LEFT_CHUNKS * CHUNK, WIDTH), BF16)],
        compiler_params=pltpu.CompilerParams(
            dimension_semantics=("arbitrary",), vmem_limit_bytes=VMEM_LIMIT),
        name="band_attn",
    )(q, k, v, bias)


def _rwkv_kernel(rw_ref, w2a2_ref, g2_ref, w0_ref, a0_ref, kk_ref, ka_ref, rk_ref,
                 gng_ref, gnb_ref, y_ref, state_ref):
    c = pl.program_id(1)
    t = CHUNK
    nb = rw_ref.shape[0]

    @pl.when(c == 0)
    def _():
        state_ref[...] = jnp.zeros(state_ref.shape, F32)

    lane = lax.broadcasted_iota(jnp.int32, (t, 2 * HEAD_DIM), 1)
    low = lane < HEAD_DIM
    r2 = lax.broadcasted_iota(jnp.int32, (2 * t, 2 * HEAD_DIM), 0)
    c2 = lax.broadcasted_iota(jnp.int32, (2 * t, 2 * HEAD_DIM), 1)
    own = (r2 < t) == (c2 < HEAD_DIM)
    strict = (r2 & (t - 1)) > (c2 & (t - 1))
    incl = (r2 & (t - 1)) >= (c2 & (t - 1))
    eye = jnp.where(r2 == c2, 1.0, 0.0).astype(F32)

    def stack2(xp):
        return jnp.concatenate([jnp.where(low, xp, 0.0), jnp.where(low, 0.0, xp)], axis=0)

    def wide(seqs, start):
        rows = len(seqs) * t
        rw = rw_ref[seqs[0]:seqs[-1] + 1, pl.ds(start, t), :].reshape(rows, RWKV_PROJ)
        r = rw[:, 0:WIDTH]
        k = rw[:, WIDTH:2 * WIDTH]
        v = rw[:, 2 * WIDTH:3 * WIDTH]
        lora = rw[:, 3 * WIDTH:3 * WIDTH + DECAY_LORA + AAA_LORA]
        g_lo = rw[:, 3 * WIDTH + DECAY_LORA + AAA_LORA:RWKV_PROJ]
        lane128 = lax.broadcasted_iota(jnp.int32, lora.shape, 1)
        lora = jnp.where(lane128 < DECAY_LORA, jnp.tanh(lora), lora)
        wa = _mm(lora, w2a2_ref[...])
        log_decay = -math.exp(-0.5) * _sigmoid(w0_ref[...] + wa[:, 0:WIDTH])
        lr = _sigmoid(a0_ref[...] + wa[:, WIDTH:2 * WIDTH])
        gate = _mm(_sigmoid(g_lo), g2_ref[...])
        kk_raw = k * kk_ref[...]
        k_mod = k * (1.0 + (lr - 1.0) * ka_ref[...])
        row = lax.broadcasted_iota(jnp.int32, (rows, rows), 0)
        col = lax.broadcasted_iota(jnp.int32, (rows, rows), 1)
        tri = jnp.where((row >= col) & ((row // t) == (col // t)), 1.0, 0.0).astype(BF16)
        ld1 = log_decay.astype(BF16)
        rem = log_decay - ld1.astype(F32)
        ld2 = rem.astype(BF16)
        ld3 = (rem - ld2.astype(F32)).astype(BF16)
        parts = _dot(tri, jnp.concatenate([ld1, ld2, ld3], axis=1))
        logp = parts[:, 0:WIDTH] + parts[:, WIDTH:2 * WIDTH] + parts[:, 2 * WIDTH:3 * WIDTH]
        p_in = jnp.exp(logp)
        p_ex = jnp.exp(logp - log_decay)
        p_inv = jnp.exp(-logp)
        kk = kk_raw / jnp.maximum(jnp.sqrt(_head_sums(kk_raw * kk_raw)), 1e-12)
        return dict(a_hat=-kk * p_ex, r_hat=r * p_in, b_hat=kk * lr * p_inv, k_hat=k_mod * p_inv,
                    v=v, p_in=p_in, gate=gate, bonus=_head_sums(r * k_mod * rk_ref[...]) * v)

    def front(seqs, w):
        chains = [(j, b, p) for j, b in enumerate(seqs) for p in range(HEADS // 2)]
        ids = range(len(chains))
        rs = [slice(j * t, (j + 1) * t) for j, _, _ in chains]
        ls = [slice(2 * HEAD_DIM * p, 2 * HEAD_DIM * (p + 1)) for _, _, p in chains]
        p_end = [w["p_in"][(j + 1) * t - 1:(j + 1) * t, ls[i]] for i, (j, _, _) in enumerate(chains)]
        ar = [jnp.concatenate([stack2(w["a_hat"][rs[i], ls[i]]), stack2(w["r_hat"][rs[i], ls[i]])],
                              axis=0).astype(BF16) for i in ids]
        bk2 = [jnp.concatenate([stack2(w["b_hat"][rs[i], ls[i]]), stack2(w["k_hat"][rs[i], ls[i]])], axis=0)
               for i in ids]
        bk = [bk2[i].astype(BF16) for i in ids]
        btkt = [(bk2[i] * p_end[i]).astype(BF16) for i in ids]
        v2 = [stack2(w["v"][rs[i], ls[i]]).astype(BF16) for i in ids]
        g = [_dot_nt(ar[i], bk[i]) for i in ids]
        st = [state_ref[b, p] for _, b, p in chains]
        ars = [_dot_nt(ar[i], st[i].astype(BF16)) for i in ids]
        l_ab = [jnp.where(strict, g[i][0:2 * t, 0:2 * t], 0.0) for i in ids]
        lm = [jnp.concatenate([jnp.where(strict, g[i][0:2 * t, 2 * t:4 * t], 0.0),
                               jnp.where(incl, g[i][2 * t:4 * t, 2 * t:4 * t], 0.0)], axis=0)
              for i in ids]
        m_rb = [jnp.where(incl, g[i][2 * t:4 * t, 0:2 * t], 0.0).astype(BF16) for i in ids]
        lv = [_mm(lm[i], v2[i]) for i in ids]
        return dict(chains=chains, rs=rs, ls=ls, p_end=p_end, btkt=btkt, v2=v2, st=st, ars=ars,
                    l_ab=l_ab, m_rb=m_rb, lv=lv)

    def inverse(f):
        w_inv = [eye + l for l in f["l_ab"]]
        l_pow = f["l_ab"]
        for _ in range(int(math.log2(t)) - 1):
            l_pow = [_mm(l, l) for l in l_pow]
            w_inv = [w + _mm(w, l) for w, l in zip(w_inv, l_pow)]
        return w_inv

    def back(w, f, w_inv, start):
        chains, rs, ls = f["chains"], f["rs"], f["ls"]
        ids = range(len(chains))
        z = [_mm(w_inv[i], f["ars"][i][0:2 * t] + f["lv"][i][0:2 * t]) for i in ids]
        y = [f["ars"][i][2 * t:4 * t] + f["lv"][i][2 * t:4 * t] + _mm(f["m_rb"][i], z[i]) for i in ids]
        for i, (_, b, p) in enumerate(chains):
            zv = jnp.concatenate([z[i].astype(BF16), f["v2"][i]], axis=0)
            state_ref[b, p] = f["st"][i] * f["p_end"][i] + _dot_tn(zv, f["btkt"][i])
        for i, (_, b, p) in enumerate(chains):
            mean = jnp.sum(y[i], axis=-1, keepdims=True) * (1.0 / HEAD_DIM)
            dev = jnp.where(own, y[i] - mean, 0.0)
            var = jnp.sum(dev * dev, axis=-1, keepdims=True) * (1.0 / HEAD_DIM)
            yn = dev * lax.rsqrt(var + GN_EPS)
            yn = yn[0:t] + yn[t:2 * t]
            out = ((yn * gng_ref[:, ls[i]] + gnb_ref[:, ls[i]] + w["bonus"][rs[i], ls[i]])
                   * w["gate"][rs[i], ls[i]])
            y_ref[b, pl.ds(start, t), ls[i]] = out.astype(BF16)

    seqs = list(range(nb))

    def chunk(j, carry):
        start = pl.multiple_of(j * t, t)
        w = wide(seqs, start)
        f = front(seqs, w)
        back(w, f, inverse(f), start)
        return carry

    lax.fori_loop(0, rw_ref.shape[1] // t, chunk, 0)


def _rwkv(rw, w2a2, g2, w0, a0, k_k, k_a, r_k, gn_g, gn_b):
    b, seq, _ = rw.shape
    nc = seq // CHUNK
    nb = RWKV_SEQS
    const = lambda i, c: (0, 0)
    vec = pl.BlockSpec((1, WIDTH), const)
    return pl.pallas_call(
        _rwkv_kernel,
        grid=(b // nb, nc // RWKV_CHUNKS),
        in_specs=[
            pl.BlockSpec((nb, CHUNK * RWKV_CHUNKS, RWKV_PROJ), lambda i, c: (i, c, 0)),
            pl.BlockSpec((DECAY_LORA + AAA_LORA, 2 * WIDTH), const),
            pl.BlockSpec((GATE_LORA, WIDTH), const),
            vec, vec, vec, vec, vec, vec, vec,
        ],
        out_specs=pl.BlockSpec((nb, CHUNK * RWKV_CHUNKS, WIDTH), lambda i, c: (i, c, 0)),
        out_shape=jax.ShapeDtypeStruct((b, seq, WIDTH), BF16),
        scratch_shapes=[pltpu.VMEM((nb, HEADS // 2, 2 * HEAD_DIM, 2 * HEAD_DIM), F32)],
        compiler_params=pltpu.CompilerParams(
            dimension_semantics=("arbitrary", "arbitrary"), vmem_limit_bytes=VMEM_LIMIT),
        name="rwkv7",
    )(rw, w2a2, g2, w0, a0, k_k, k_a, r_k, gn_g, gn_b)


def _merge_kernel(x_ref, att_ref, rwkv_ref, gate_ref, wa32_ref, wb32_ref, wo32_ref, g_ref,
                  wr_ref, br_ref, x1_ref, h2_ref, comb_ref, cnt_ref, wa_ref, wb_ref, wo_ref):
    @pl.when(pl.program_id(0) == 0)
    def _():
        wa_ref[...] = wa32_ref[...].astype(BF16)
        wb_ref[...] = wb32_ref[...].astype(BF16)
        wo_ref[...] = wo32_ref[...].astype(BF16)

    ga = gate_ref[:, 0:D_MODEL].astype(F32)
    gb = gate_ref[:, D_MODEL:2 * D_MODEL].astype(F32)
    merged = ga * _dot(att_ref[...], wa_ref[...]) + gb * _dot(rwkv_ref[...], wb_ref[...])
    x1 = x_ref[...] + _dot(merged.astype(BF16), wo_ref[...])
    x1_ref[...] = x1
    h2 = x1 * lax.rsqrt(jnp.mean(x1 * x1, axis=-1, keepdims=True) + RMS_EPS) * g_ref[...]
    h2_hi = h2.astype(BF16)
    h2_ref[...] = _pack_bf16(h2)
    h2_lo = (h2 - h2_hi.astype(F32)).astype(BF16)
    hw = _dot(h2_hi, wr_ref[...])
    lw = _dot(h2_lo, wr_ref[:, 0:ROUTER_LANES])
    logits = hw[:, 0:ROUTER_LANES] + (hw[:, ROUTER_LANES:2 * ROUTER_LANES] + lw) + br_ref[...]
    rec, counts = _route(logits)
    comb_ref[...] = rec
    cnt_ref[0] = jnp.broadcast_to(counts, (8, ROUTER_LANES)).astype(jnp.int32)


def _route(logits):
    lane_i = lax.broadcasted_iota(jnp.int32, logits.shape, 1)
    lane = lane_i.astype(F32)
    lane_group = (lane_i // EXPERTS_PER_GROUP).astype(F32)
    neg = jnp.finfo(F32).min
    big = float(ROUTER_LANES)

    def first_argmax(vals, mask):
        vm = jnp.where(mask, vals, neg)
        mx = jnp.max(vm, axis=-1, keepdims=True)
        idx = jnp.min(jnp.where(vm == mx, jnp.where(mask, lane, big), big), axis=-1, keepdims=True)
        return mx, idx

    is_group = (lane_i >= GROUP_LANE0) & (lane_i < GROUP_LANE0 + N_GROUPS)
    g_max, g_lane = first_argmax(logits, is_group)
    g_prob = 1.0 / jnp.sum(jnp.where(is_group, jnp.exp(logits - g_max), 0.0),
                           axis=-1, keepdims=True)
    g_idx = g_lane - GROUP_LANE0
    in_group = lane_group == g_idx
    e1, i1 = first_argmax(logits, in_group)
    e2, i2 = first_argmax(logits, in_group & (lane != i1))
    w2 = jnp.exp(e2 - e1)
    p1 = 1.0 / (1.0 + w2)
    p2 = w2 / (1.0 + w2)
    rec = jnp.where(lane_i == EXPERT1_LANE, i1, jnp.where(lane_i == EXPERT2_LANE, i2, 0.0))
    rec = jnp.where(lane_i == WEIGHT1_LANE, p1 * g_prob, jnp.where(lane_i == WEIGHT2_LANE, p2 * g_prob, rec))
    counts = jnp.sum(jnp.where(lane == i1, 1.0, 0.0) + jnp.where(lane == i2, 1.0, 0.0),
                     axis=0, keepdims=True)
    return rec, counts


def _merge(x2, att, rwkv, gates, wa, wb, wo, ln_g, wr, br):
    n = x2.shape[0]
    tm = MERGE_TILE
    row = lambda i: (i, 0)
    const = lambda i: (0, 0)
    return pl.pallas_call(
        _merge_kernel,
        grid=(n // tm,),
        in_specs=[
            pl.BlockSpec((tm, D_MODEL), row),
            pl.BlockSpec((tm, WIDTH), row),
            pl.BlockSpec((tm, WIDTH), row),
            pl.BlockSpec((tm, 2 * D_MODEL), row),
            pl.BlockSpec((WIDTH, D_MODEL), const),
            pl.BlockSpec((WIDTH, D_MODEL), const),
            pl.BlockSpec((D_MODEL, D_MODEL), const),
            pl.BlockSpec((1, D_MODEL), const),
            pl.BlockSpec((D_MODEL, 2 * ROUTER_LANES), const),
            pl.BlockSpec((1, ROUTER_LANES), const),
        ],
        out_specs=[
            pl.BlockSpec((tm, D_MODEL), row),
            pl.BlockSpec((tm, PACKED), row),
            pl.BlockSpec((tm, ROUTER_LANES), row),
            pl.BlockSpec((1, 8, ROUTER_LANES), lambda i: (i, 0, 0)),
        ],
        out_shape=[
            jax.ShapeDtypeStruct((n, D_MODEL), F32),
            jax.ShapeDtypeStruct((n, PACKED), jnp.uint32),
            jax.ShapeDtypeStruct((n, ROUTER_LANES), F32),
            jax.ShapeDtypeStruct((n // tm, 8, ROUTER_LANES), jnp.int32),
        ],
        scratch_shapes=[pltpu.VMEM((WIDTH, D_MODEL), BF16), pltpu.VMEM((WIDTH, D_MODEL), BF16),
                        pltpu.VMEM((D_MODEL, D_MODEL), BF16)],
        compiler_params=pltpu.CompilerParams(
            dimension_semantics=("arbitrary",), vmem_limit_bytes=VMEM_LIMIT),
        name="merge",
    )(x2, att, rwkv, gates, wa, wb, wo, ln_g, wr, br)


def _moe_plan_kernel(cnt_ref, base_ref, blk_ref):
    nt = cnt_ref.shape[0]
    cnt = cnt_ref[...].astype(F32)
    lane = lax.broadcasted_iota(jnp.int32, (8, ROUTER_LANES), 1)
    total = jnp.broadcast_to(jnp.sum(cnt, axis=0, keepdims=True), (8, ROUTER_LANES))
    padded = jnp.floor((total + (MOE_BLOCK - 1)) * (1.0 / MOE_BLOCK)) * MOE_BLOCK
    r = lax.broadcasted_iota(jnp.int32, (ROUTER_LANES, ROUTER_LANES), 0)
    c = lax.broadcasted_iota(jnp.int32, (ROUTER_LANES, ROUTER_LANES), 1)
    seg_start = _dot_hi(padded, jnp.where(r < c, 1.0, 0.0).astype(F32))
    tr = lax.broadcasted_iota(jnp.int32, (nt, nt), 0)
    tc = lax.broadcasted_iota(jnp.int32, (nt, nt), 1)
    tile_off = _dot_hi(jnp.where(tc < tr, 1.0, 0.0).astype(F32), cnt)
    base_ref[...] = seg_start[0:1, :] + tile_off
    seg_end = (seg_start + padded)[0:1, :]
    rows_total = jnp.sum(jnp.where(lane[0:1, :] < N_EXPERTS, padded[0:1, :], 0.0), axis=-1, keepdims=True)
    nblk = blk_ref.shape[0]
    blk_row = lax.broadcasted_iota(jnp.int32, (nblk, ROUTER_LANES), 0).astype(F32) * MOE_BLOCK
    blk_lane = lax.broadcasted_iota(jnp.int32, (nblk, ROUTER_LANES), 1)
    done = jnp.where((seg_end <= blk_row) & (blk_lane < N_EXPERTS), 1.0, 0.0)
    expert = jnp.minimum(jnp.sum(done, axis=-1, keepdims=True), N_EXPERTS - 1.0)
    valid = jnp.where(blk_row < rows_total, 1.0, 0.0)
    blk_idx = lax.broadcasted_iota(jnp.int32, (nblk, ROUTER_LANES), 0).astype(F32)
    source = jnp.minimum(blk_idx, rows_total * (1.0 / MOE_BLOCK) - 1.0)
    table = jnp.where(blk_lane == 0, expert, jnp.where(blk_lane == 1, valid,
                                                       jnp.where(blk_lane == 2, source, 0.0)))
    blk_ref[...] = table.astype(jnp.int32)


def _moe_plan(cnt, n_blocks):
    nt = cnt.shape[0]
    return pl.pallas_call(
        _moe_plan_kernel,
        out_shape=[jax.ShapeDtypeStruct((nt, ROUTER_LANES), F32),
                   jax.ShapeDtypeStruct((n_blocks, ROUTER_LANES), jnp.int32)],
        name="moe_plan",
    )(cnt)


def _moe_pos_kernel(comb_ref, base_ref, pos_ref):
    t = comb_ref.shape[0]
    comb = comb_ref[...]
    lane_i = lax.broadcasted_iota(jnp.int32, (t, ROUTER_LANES), 1)
    lane = lane_i.astype(F32)
    pick1 = jnp.where(lane == comb[:, EXPERT1_LANE:EXPERT1_LANE + 1], 1.0, 0.0)
    pick2 = jnp.where(lane == comb[:, EXPERT2_LANE:EXPERT2_LANE + 1], 1.0, 0.0)
    rows = lax.broadcasted_iota(jnp.int32, (t, t), 0)
    cols = lax.broadcasted_iota(jnp.int32, (t, t), 1)
    earlier = jnp.where(cols < rows, 1.0, 0.0).astype(BF16)
    before1 = _dot(earlier, pick1.astype(BF16))
    before2 = _dot(earlier, pick2.astype(BF16))
    base = base_ref[0]
    firsts = jnp.sum(pick1, axis=0, keepdims=True)
    pos1 = jnp.sum(pick1 * (base + before1), axis=-1, keepdims=True)
    pos2 = jnp.sum(pick2 * (base + firsts + before2), axis=-1, keepdims=True)
    both = jnp.where(lane_i == 0, pos1, jnp.where(lane_i == 1, pos2, 0.0))
    pos_ref[...] = jnp.transpose(both)[0:8, :].astype(jnp.int32)


def _moe_pos(comb, base):
    n = comb.shape[0]
    t = MERGE_TILE
    return pl.pallas_call(
        _moe_pos_kernel,
        grid=(n // t,),
        in_specs=[pl.BlockSpec((t, ROUTER_LANES), lambda i: (i, 0)),
                  pl.BlockSpec((1, 1, ROUTER_LANES), lambda i: (i, 0, 0))],
        out_specs=pl.BlockSpec((8, t), lambda i: (0, i)),
        out_shape=jax.ShapeDtypeStruct((8, n), jnp.int32),
        compiler_params=pltpu.CompilerParams(dimension_semantics=("arbitrary",)),
        name="moe_pos",
    )(comb, base.reshape(n // t, 1, ROUTER_LANES))


def _sc_mesh():
    return plsc.VectorSubcoreMesh(core_axis_name="core", subcore_axis_name="subcore")


def _sc_scratch():
    return [pltpu.VMEM((8, SC_WINDOW), jnp.int32), pltpu.VMEM((SC_WINDOW, PACKED), jnp.uint32)]


def _sc_scatter_rows(h, pos, n_rows):
    per_worker = h.shape[0] // (SC_WINDOW * SC_WORKERS)

    @pl.kernel(out_type=jax.ShapeDtypeStruct((n_rows, PACKED), jnp.uint32), mesh=_sc_mesh(),
               scratch_types=_sc_scratch())
    def scatter(h_hbm, pos_hbm, out_hbm, idx, buf):
        worker = lax.axis_index("core") * SC_SUBCORES + lax.axis_index("subcore")

        @pl.loop(0, per_worker)
        def _(b):
            start = (worker * per_worker + b) * SC_WINDOW
            pltpu.sync_copy(pos_hbm.at[:, pl.ds(start, SC_WINDOW)], idx)
            pltpu.sync_copy(h_hbm.at[pl.ds(start, SC_WINDOW)], buf)
            pltpu.sync_copy(buf, out_hbm.at[idx.at[0]])
            pltpu.sync_copy(buf, out_hbm.at[idx.at[1]])

    return scatter(h, pos)


def _sc_gather_rows(y, pos):
    n = pos.shape[1]
    per_worker = n // (SC_WINDOW * SC_WORKERS)
    out = jax.ShapeDtypeStruct((n, PACKED), jnp.uint32)

    @pl.kernel(out_type=(out, out), mesh=_sc_mesh(), scratch_types=_sc_scratch())
    def gather(y_hbm, pos_hbm, o1_hbm, o2_hbm, idx, buf):
        worker = lax.axis_index("core") * SC_SUBCORES + lax.axis_index("subcore")

        @pl.loop(0, per_worker)
        def _(b):
            start = (worker * per_worker + b) * SC_WINDOW
            pltpu.sync_copy(pos_hbm.at[:, pl.ds(start, SC_WINDOW)], idx)
            pltpu.sync_copy(y_hbm.at[idx.at[0]], buf)
            pltpu.sync_copy(buf, o1_hbm.at[pl.ds(start, SC_WINDOW)])
            pltpu.sync_copy(y_hbm.at[idx.at[1]], buf)
            pltpu.sync_copy(buf, o2_hbm.at[pl.ds(start, SC_WINDOW)])

    return gather(y, pos)


def _expert_mlp_kernel(expert_ref, valid_ref, source_ref, hs_ref, wg32_ref, wu32_ref, wd32_ref, y_ref,
                       wg_ref, wu_ref, wd_ref):
    b = pl.program_id(0)

    @pl.when(valid_ref[b] != 0)
    def _():
        @pl.when((b == 0) | (expert_ref[b] != expert_ref[jnp.maximum(b - 1, 0)]))
        def _():
            wg_ref[...] = wg32_ref[0].astype(BF16)
            wu_ref[...] = wu32_ref[0].astype(BF16)
            wd_ref[...] = wd32_ref[0].astype(BF16)

        half = D_MODEL // 2
        h_lo, h_hi = _unpack_bf16(hs_ref[...])
        h_lo = h_lo.astype(BF16)
        h_hi = h_hi.astype(BF16)
        hg = _dot(h_lo, wg_ref[0:half, :]) + _dot(h_hi, wg_ref[half:D_MODEL, :])
        hu = _dot(h_lo, wu_ref[0:half, :]) + _dot(h_hi, wu_ref[half:D_MODEL, :])
        act = (hg * _sigmoid(hg) * hu).astype(BF16)
        y_ref[...] = _pack_bf16(_dot(act, wd_ref[...]))


def _expert_mlp(blk_expert, blk_valid, blk_source, hs, wg, wu, wd):
    rows = lambda b, expert, valid, source: (source[b], 0)
    by_expert = lambda b, expert, valid, source: (expert[source[b]], 0, 0)
    grid_spec = pltpu.PrefetchScalarGridSpec(
        num_scalar_prefetch=3,
        grid=(hs.shape[0] // MOE_BLOCK,),
        in_specs=[
            pl.BlockSpec((MOE_BLOCK, PACKED), rows),
            pl.BlockSpec((1, D_MODEL, D_EXPERT), by_expert),
            pl.BlockSpec((1, D_MODEL, D_EXPERT), by_expert),
            pl.BlockSpec((1, D_EXPERT, D_MODEL), by_expert),
        ],
        out_specs=pl.BlockSpec((MOE_BLOCK, PACKED), rows),
        scratch_shapes=[pltpu.VMEM((D_MODEL, D_EXPERT), BF16), pltpu.VMEM((D_MODEL, D_EXPERT), BF16),
                        pltpu.VMEM((D_EXPERT, D_MODEL), BF16)],
    )
    return pl.pallas_call(
        _expert_mlp_kernel,
        grid_spec=grid_spec,
        out_shape=jax.ShapeDtypeStruct((hs.shape[0], PACKED), jnp.uint32),
        compiler_params=pltpu.CompilerParams(
            dimension_semantics=("arbitrary",), vmem_limit_bytes=VMEM_LIMIT),
        name="expert_mlp",
    )(blk_expert, blk_valid, blk_source, hs, wg, wu, wd)


def _moe_out_kernel(x1_ref, y1_ref, y2_ref, comb_ref, lnf_ref, out_ref, *, final_norm):
    w1 = comb_ref[:, WEIGHT1_LANE:WEIGHT1_LANE + 1]
    w2 = comb_ref[:, WEIGHT2_LANE:WEIGHT2_LANE + 1]
    a_lo, a_hi = _unpack_bf16(y1_ref[...])
    b_lo, b_hi = _unpack_bf16(y2_ref[...])
    moe = jnp.concatenate([w1 * a_lo + w2 * b_lo, w1 * a_hi + w2 * b_hi], axis=1)
    y = x1_ref[...] + moe
    if final_norm:
        y = y * lax.rsqrt(jnp.mean(y * y, axis=-1, keepdims=True) + RMS_EPS) * lnf_ref[...]
    out_ref[...] = y


def _moe_out(x1, y1, y2, comb, lnf, final_norm):
    n = x1.shape[0]
    t = FINAL_TILE
    row = lambda i: (i, 0)
    return pl.pallas_call(
        functools.partial(_moe_out_kernel, final_norm=final_norm),
        grid=(n // t,),
        in_specs=[pl.BlockSpec((t, D_MODEL), row), pl.BlockSpec((t, PACKED), row),
                  pl.BlockSpec((t, PACKED), row), pl.BlockSpec((t, ROUTER_LANES), row),
                  pl.BlockSpec((1, D_MODEL), lambda i: (0, 0))],
        out_specs=pl.BlockSpec((t, D_MODEL), row),
        out_shape=jax.ShapeDtypeStruct((n, D_MODEL), F32),
        compiler_params=pltpu.CompilerParams(
            dimension_semantics=("arbitrary",), vmem_limit_bytes=VMEM_LIMIT),
        name="moe_out",
    )(x1, y1, y2, comb, lnf)


def _moe(counts, x1, h2p, comb, wg, wu, wd, lnf, final_norm):
    n = x1.shape[0]
    n_blocks = (2 * n) // MOE_BLOCK + N_EXPERTS
    base, blk = _moe_plan(counts[:, 0, :], n_blocks)
    pos = _moe_pos(comb, base)
    hs = _sc_scatter_rows(h2p, pos, n_blocks * MOE_BLOCK)
    ys = _expert_mlp(blk[:, 0], blk[:, 1], blk[:, 2], hs, wg, wu, wd)
    y1, y2 = _sc_gather_rows(ys, pos)
    return _moe_out(x1, y1, y2, comb, lnf, final_norm)


def kernel(x, ln_mix_g, w_in, att_rel_bias, rwkv_mu, rwkv_w0, rwkv_w2, rwkv_a0, rwkv_a2, rwkv_g2,
           rwkv_k_k, rwkv_k_a, rwkv_r_k, rwkv_gn_g, rwkv_gn_b, w_branch_att, w_branch_rwkv, w_out,
           ln_ffn_g, router_group_w, router_group_b, router_expert_w, router_expert_b,
           expert_w_gate, expert_w_up, expert_w_down, ln_final_g):
    bsz, seq, d = x.shape
    depth = w_in.shape[0]
    n = bsz * seq
    x2 = x.reshape(n, d)
    for l in range(depth):
        q, k, v, rw, gates = _in_proj(x2, ln_mix_g[l][None, :], w_in[l].astype(BF16),
                                      rwkv_mu[l][None, :], seq)
        bias = _rel_bias(att_rel_bias[l])
        att = _band_attn(q.reshape(bsz, seq, WIDTH), k.reshape(bsz, seq, WIDTH),
                         v.reshape(bsz, seq, WIDTH), bias)
        zeros = jnp.zeros((DECAY_LORA, WIDTH), F32)
        w2a2 = jnp.concatenate(
            [jnp.concatenate([rwkv_w2[l], zeros], axis=1),
             jnp.concatenate([zeros, rwkv_a2[l]], axis=1)], axis=0)
        rwkv = _rwkv(rw.reshape(bsz, seq, RWKV_PROJ), w2a2.astype(BF16), rwkv_g2[l].astype(BF16),
                     rwkv_w0[l][None, :], rwkv_a0[l][None, :], rwkv_k_k[l][None, :],
                     rwkv_k_a[l][None, :], rwkv_r_k[l].reshape(1, WIDTH),
                     rwkv_gn_g[l][None, :], rwkv_gn_b[l][None, :])
        wr = jnp.concatenate([router_expert_w[l], router_group_w[l]], axis=1)
        wr = jnp.pad(wr, ((0, 0), (0, ROUTER_LANES - wr.shape[1])))
        wr_hi = wr.astype(BF16)
        wr = jnp.concatenate([wr_hi, (wr - wr_hi.astype(F32)).astype(BF16)], axis=1)
        br = jnp.concatenate([router_expert_b[l], router_group_b[l]])
        br = jnp.pad(br, (0, ROUTER_LANES - br.shape[0]))[None, :]
        x1, h2, comb, counts = _merge(x2, att.reshape(n, WIDTH), rwkv.reshape(n, WIDTH), gates,
                                      w_branch_att[l], w_branch_rwkv[l], w_out[l],
                                      ln_ffn_g[l][None, :], wr, br)
        x2 = _moe(counts, x1, h2, comb, expert_w_gate[l], expert_w_up[l], expert_w_down[l],
                  ln_final_g[None, :], final_norm=(l == depth - 1))
    return x2.reshape(bsz, seq, d)
```

```python
import functools
import math

import jax
import jax.numpy as jnp
from jax import lax
from jax.experimental import pallas as pl
from jax.experimental.pallas import tpu as pltpu
from jax.experimental.pallas import tpu_sc as plsc

F32 = jnp.float32
BF16 = jnp.bfloat16
HIGHEST = lax.Precision.HIGHEST

D_MODEL = 1024
CHUNK = 64
HEADS = 8
HEAD_DIM = 64
WIDTH = HEADS * HEAD_DIM
LEFT_CHUNKS = 8
BAND = (LEFT_CHUNKS + 1) * CHUNK
REL_CLIP = 64
N_REL = 2 * REL_CLIP + 1
DECAY_LORA = 64
AAA_LORA = 64
GATE_LORA = 128
GN_EPS = 64e-5
RMS_EPS = 1e-6
ATT_PROJ = 3 * WIDTH
RWKV_PROJ = 3 * WIDTH + DECAY_LORA + AAA_LORA + GATE_LORA
D_IN = ATT_PROJ + RWKV_PROJ + 2 * D_MODEL
N_GROUPS = 4
EXPERTS_PER_GROUP = 8
N_EXPERTS = N_GROUPS * EXPERTS_PER_GROUP
D_EXPERT = 256
RWKV_SEQS = 4
RWKV_CHUNKS = 4
BIAS_KEYS = 192
ATT_CHUNKS = 8
ATT_GROUP = 4
ROUTER_LANES = 128
GROUP_LANE0 = N_EXPERTS
EXPERT1_LANE, EXPERT2_LANE, WEIGHT1_LANE, WEIGHT2_LANE = 126, 125, 124, 123
MERGE_TILE = 512
MOE_BLOCK = 1024
PACKED = D_MODEL // 2
SC_WINDOW = 128
SC_CORES, SC_SUBCORES = 2, 16
SC_WORKERS = SC_CORES * SC_SUBCORES
FINAL_TILE = 1024

V7X_VMEM_BYTES = 64 * 1024 * 1024
VMEM_LIMIT = V7X_VMEM_BYTES - 12 * 1024 * 1024


def _dot(a, b):
    return jnp.dot(a, b, preferred_element_type=F32)


def _dot_hi(a, b):
    return jnp.dot(a, b, preferred_element_type=F32, precision=HIGHEST)


def _dot_nt(a, b, precision=None):
    return lax.dot_general(a, b, (((1,), (1,)), ((), ())),
                           preferred_element_type=F32, precision=precision)


def _dot_tn(a, b, precision=None):
    return lax.dot_general(a, b, (((0,), (0,)), ((), ())),
                           preferred_element_type=F32, precision=precision)


def _sigmoid(x):
    return 1.0 / (1.0 + jnp.exp(-x))


def _pack_bf16(x):
    w = x.shape[1] // 2
    hi = pltpu.bitcast(x[:, :w].astype(BF16).astype(F32), jnp.uint32)
    lo = pltpu.bitcast(x[:, w:].astype(BF16).astype(F32), jnp.uint32)
    return hi | lax.shift_right_logical(lo, jnp.uint32(16))


def _unpack_bf16(p):
    hi = pltpu.bitcast(p & jnp.uint32(0xFFFF0000), F32)
    lo = pltpu.bitcast(lax.shift_left(p, jnp.uint32(16)), F32)
    return hi, lo


def _mm(a, b):
    return jnp.dot(a.astype(BF16), b.astype(BF16), preferred_element_type=F32)


def _head_sums(x):
    outs = []
    lane = lax.broadcasted_iota(jnp.int32, (x.shape[0], 2 * HEAD_DIM), 1)
    low = lane < HEAD_DIM
    for p in range(HEADS // 2):
        xp = x[:, 2 * HEAD_DIM * p:2 * HEAD_DIM * (p + 1)]
        s_lo = jnp.sum(jnp.where(low, xp, 0.0), axis=-1, keepdims=True)
        s_hi = jnp.sum(jnp.where(low, 0.0, xp), axis=-1, keepdims=True)
        outs.append(jnp.where(low, s_lo, s_hi))
    return jnp.concatenate(outs, axis=-1)


def _rel_bias_kernel(tab_ref, out_ref):
    rows = tab_ref.shape[1]
    n = lax.broadcasted_iota(jnp.int32, (rows, CHUNK * 128), 1)
    r = lax.broadcasted_iota(jnp.int32, (rows, CHUNK * 128), 0)
    q = n >> 7
    kk = n & 127
    idx = jnp.clip(CHUNK + q - kk, -REL_CLIP, REL_CLIP) + REL_CLIP
    pick = jnp.where(r == idx, 1.0, 0.0) - jnp.where(r == N_REL - 1, 1.0, 0.0)
    out_ref[...] = _dot_hi(tab_ref[...], pick.astype(F32))


def _rel_bias(rel_table):
    rows = 136
    tab = jnp.pad(rel_table.astype(F32), ((0, 0), (0, rows - N_REL)))
    tail = pl.pallas_call(
        _rel_bias_kernel,
        out_shape=jax.ShapeDtypeStruct((HEADS, CHUNK * 128), F32),
        name="rel_bias",
    )(tab)
    tail = tail.reshape(HEADS, CHUNK, 128)
    bias = jnp.concatenate([jnp.zeros((HEADS, CHUNK, BIAS_KEYS - 128), F32), tail], axis=-1)
    return bias.reshape(HEADS * CHUNK, BIAS_KEYS)


def _in_proj_kernel(x_ref, g_ref, w_ref, mu_ref, q_ref, k_ref, v_ref, rw_ref, gate_ref,
                    carry_ref, *, tiles_per_seq):
    i = pl.program_id(0)

    @pl.when(i == 0)
    def _():
        carry_ref[...] = jnp.zeros(carry_ref.shape, F32)

    x = x_ref[...]
    h = x * lax.rsqrt(jnp.mean(x * x, axis=-1, keepdims=True) + RMS_EPS) * g_ref[...]
    hb = h.astype(BF16)
    q_ref[...] = _dot(hb, w_ref[:, 0:WIDTH]).astype(BF16)
    k_ref[...] = _dot(hb, w_ref[:, WIDTH:2 * WIDTH]).astype(BF16)
    v_ref[...] = _dot(hb, w_ref[:, 2 * WIDTH:ATT_PROJ]).astype(BF16)
    rw = _dot(hb, w_ref[:, ATT_PROJ:ATT_PROJ + RWKV_PROJ])
    tm = rw.shape[0]
    first_prev = jnp.where(i % tiles_per_seq == 0, 0.0, carry_ref[0:1, :])
    rolled = pltpu.roll(rw, 1, axis=0)
    row = lax.broadcasted_iota(jnp.int32, rw.shape, 0)
    prev = jnp.where(row == 0, first_prev, rolled)
    carry_ref[0:1, :] = rw[tm - 1:tm, :]
    rw_ref[...] = rw + (prev - rw) * mu_ref[...]
    gate_ref[...] = _sigmoid(_dot(hb, w_ref[:, ATT_PROJ + RWKV_PROJ:D_IN])).astype(BF16)


def _in_proj(x2, ln_g, w_in_b, mu, seq):
    n = x2.shape[0]
    tm = 1024
    assert seq % tm == 0, "a projection tile must not straddle two sequences (token shift carry)"
    row = lambda i: (i, 0)
    const = lambda i: (0, 0)
    return pl.pallas_call(
        functools.partial(_in_proj_kernel, tiles_per_seq=seq // tm),
        grid=(n // tm,),
        in_specs=[
            pl.BlockSpec((tm, D_MODEL), row),
            pl.BlockSpec((1, D_MODEL), const),
            pl.BlockSpec((D_MODEL, D_IN), const, pipeline_mode=pl.Buffered(1)),
            pl.BlockSpec((1, RWKV_PROJ), const),
        ],
        out_specs=[
            pl.BlockSpec((tm, WIDTH), row),
            pl.BlockSpec((tm, WIDTH), row),
            pl.BlockSpec((tm, WIDTH), row),
            pl.BlockSpec((tm, RWKV_PROJ), row),
            pl.BlockSpec((tm, 2 * D_MODEL), row),
        ],
        out_shape=[
            jax.ShapeDtypeStruct((n, WIDTH), BF16),
            jax.ShapeDtypeStruct((n, WIDTH), BF16),
            jax.ShapeDtypeStruct((n, WIDTH), BF16),
            jax.ShapeDtypeStruct((n, RWKV_PROJ), F32),
            jax.ShapeDtypeStruct((n, 2 * D_MODEL), BF16),
        ],
        scratch_shapes=[pltpu.VMEM((8, RWKV_PROJ), F32)],
        compiler_params=pltpu.CompilerParams(
            dimension_semantics=("arbitrary",), vmem_limit_bytes=VMEM_LIMIT),
        name="in_proj",
    )(x2, ln_g, w_in_b, mu)


def _band_attn_kernel(q_ref, k_ref, v_ref, bias_ref, o_ref, kpad_ref, vpad_ref):
    seq = k_ref.shape[1]
    pad = LEFT_CHUNKS * CHUNK
    kpad_ref[0:pad, :] = jnp.zeros((pad, WIDTH), BF16)
    vpad_ref[0:pad, :] = jnp.zeros((pad, WIDTH), BF16)
    kpad_ref[pad:pad + seq, :] = k_ref[0]
    vpad_ref[pad:pad + seq, :] = v_ref[0]

    gw = ATT_GROUP * HEAD_DIM
    rows = ATT_GROUP * CHUNK
    r_head = lax.broadcasted_iota(jnp.int32, (rows, gw), 0) // CHUNK
    l_head = lax.broadcasted_iota(jnp.int32, (rows, gw), 1) // HEAD_DIM
    own = r_head == l_head
    kpos_lo = lax.broadcasted_iota(jnp.int32, (rows, BAND - BIAS_KEYS), 1)
    kpos_hi = lax.broadcasted_iota(jnp.int32, (rows, BIAS_KEYS), 1) + (BAND - BIAS_KEYS)
    neg = jnp.finfo(F32).min
    groups = range(HEADS // ATT_GROUP)
    lanes = [slice(g * gw, (g + 1) * gw) for g in groups]

    def chunk_pair(i, carry, masked):
        units = [(j, g) for j in range(ATT_CHUNKS) for g in groups]
        ids = range(len(units))
        starts = [pl.multiple_of((i * ATT_CHUNKS + j) * CHUNK, CHUNK) for j in range(ATT_CHUNKS)]
        kb = [kpad_ref[pl.ds(st, BAND), :] for st in starts]
        vb = [vpad_ref[pl.ds(st, BAND), :] for st in starts]
        q = [q_ref[0, pl.ds(st, CHUNK), :] * (HEAD_DIM ** -0.5) for st in starts]
        qrows = [jnp.where(own, jnp.concatenate([q[j][:, lanes[g]]] * ATT_GROUP, axis=0),
                           jnp.zeros((), BF16)) for j, g in units]
        s = [_dot_nt(qrows[u], kb[j][:, lanes[g]]) for u, (j, g) in enumerate(units)]
        s_lo = [s[u][:, 0:BAND - BIAS_KEYS] for u in ids]
        s_hi = [s[u][:, BAND - BIAS_KEYS:BAND] + bias_ref[g * rows:(g + 1) * rows, :]
                for u, (j, g) in enumerate(units)]
        if masked:
            first = [(LEFT_CHUNKS - (i * ATT_CHUNKS + j)) * CHUNK for j in range(ATT_CHUNKS)]
            s_lo = [jnp.where(kpos_lo >= first[j], s_lo[u], neg) for u, (j, g) in enumerate(units)]
            s_hi = [jnp.where(kpos_hi >= first[j], s_hi[u], neg) for u, (j, g) in enumerate(units)]
        m = [jnp.maximum(jnp.max(s_lo[u], axis=-1, keepdims=True),
                         jnp.max(s_hi[u], axis=-1, keepdims=True)) for u in ids]
        p_lo = [jnp.exp(s_lo[u] - m[u]) for u in ids]
        p_hi = [jnp.exp(s_hi[u] - m[u]) for u in ids]
        denom = [jnp.sum(p_lo[u], axis=-1, keepdims=True) + jnp.sum(p_hi[u], axis=-1, keepdims=True)
                 for u in ids]
        o_all = [(_dot(p_lo[u].astype(BF16), vb[j][0:BAND - BIAS_KEYS, lanes[g]])
                  + _dot(p_hi[u].astype(BF16), vb[j][BAND - BIAS_KEYS:BAND, lanes[g]])) / denom[u]
                 for u, (j, g) in enumerate(units)]
        for u, (j, g) in enumerate(units):
            o_own = jnp.where(own, o_all[u], 0.0)
            o = o_own[0:CHUNK]
            for h in range(1, ATT_GROUP):
                o = o + o_own[h * CHUNK:(h + 1) * CHUNK]
            o_ref[0, pl.ds(starts[j], CHUNK), lanes[g]] = o.astype(BF16)
        return carry

    n_trips = seq // (CHUNK * ATT_CHUNKS)
    n_masked = min(LEFT_CHUNKS // ATT_CHUNKS, n_trips)
    lax.fori_loop(0, n_masked, functools.partial(chunk_pair, masked=True), 0)
    lax.fori_loop(n_masked, n_trips, functools.partial(chunk_pair, masked=False), 0)


def _band_attn(q, k, v, bias):
    b, seq, _ = q.shape
    whole = pl.BlockSpec((1, seq, WIDTH), lambda i: (i, 0, 0))
    return pl.pallas_call(
        _band_attn_kernel,
        grid=(b,),
        in_specs=[whole, whole, whole, pl.BlockSpec((HEADS * CHUNK, BIAS_KEYS), lambda i: (0, 0))],
        out_specs=whole,
        out_shape=jax.ShapeDtypeStruct((b, seq, WIDTH), BF16),
        scratch_shapes=[pltpu.VMEM((seq + LEFT_CHUNKS * CHUNK, WIDTH), BF16),
                        pltpu.VMEM((seq + LEFT_CHUNKS * CHUNK, WIDTH), BF16)],
        compiler_params=pltpu.CompilerParams(
            dimension_semantics=("arbitrary",), vmem_limit_bytes=VMEM_LIMIT),
        name="band_attn",
    )(q, k, v, bias)


def _rwkv_kernel(rw_ref, w2a2_ref, g2_ref, w0_ref, a0_ref, kk_ref, ka_ref, rk_ref,
                 gng_ref, gnb_ref, y_ref, state_ref):
    c = pl.program_id(1)
    t = CHUNK
    nb = rw_ref.shape[0]

    @pl.when(c == 0)
    def _():
        state_ref[...] = jnp.zeros(state_ref.shape, F32)

    lane = lax.broadcasted_iota(jnp.int32, (t, 2 * HEAD_DIM), 1)
    low = lane < HEAD_DIM
    r2 = lax.broadcasted_iota(jnp.int32, (2 * t, 2 * HEAD_DIM), 0)
    c2 = lax.broadcasted_iota(jnp.int32, (2 * t, 2 * HEAD_DIM), 1)
    own = (r2 < t) == (c2 < HEAD_DIM)
    strict = (r2 & (t - 1)) > (c2 & (t - 1))
    incl = (r2 & (t - 1)) >= (c2 & (t - 1))
    eye = jnp.where(r2 == c2, 1.0, 0.0).astype(F32)

    def stack2(xp):
        return jnp.concatenate([jnp.where(low, xp, 0.0), jnp.where(low, 0.0, xp)], axis=0)

    def wide(seqs, start):
        rows = len(seqs) * t
        rw = rw_ref[seqs[0]:seqs[-1] + 1, pl.ds(start, t), :].reshape(rows, RWKV_PROJ)
        r = rw[:, 0:WIDTH]
        k = rw[:, WIDTH:2 * WIDTH]
        v = rw[:, 2 * WIDTH:3 * WIDTH]
        lora = rw[:, 3 * WIDTH:3 * WIDTH + DECAY_LORA + AAA_LORA]
        g_lo = rw[:, 3 * WIDTH + DECAY_LORA + AAA_LORA:RWKV_PROJ]
        lane128 = lax.broadcasted_iota(jnp.int32, lora.shape, 1)
        lora = jnp.where(lane128 < DECAY_LORA, jnp.tanh(lora), lora)
        wa = _mm(lora, w2a2_ref[...])
        log_decay = -math.exp(-0.5) * _sigmoid(w0_ref[...] + wa[:, 0:WIDTH])
        lr = _sigmoid(a0_ref[...] + wa[:, WIDTH:2 * WIDTH])
        gate = _mm(_sigmoid(g_lo), g2_ref[...])
        kk_raw = k * kk_ref[...]
        k_mod = k * (1.0 + (lr - 1.0) * ka_ref[...])
        row = lax.broadcasted_iota(jnp.int32, (rows, rows), 0)
        col = lax.broadcasted_iota(jnp.int32, (rows, rows), 1)
        tri = jnp.where((row >= col) & ((row // t) == (col // t)), 1.0, 0.0).astype(BF16)
        ld1 = log_decay.astype(BF16)
        rem = log_decay - ld1.astype(F32)
        ld2 = rem.astype(BF16)
        ld3 = (rem - ld2.astype(F32)).astype(BF16)
        parts = _dot(tri, jnp.concatenate([ld1, ld2, ld3], axis=1))
        logp = parts[:, 0:WIDTH] + parts[:, WIDTH:2 * WIDTH] + parts[:, 2 * WIDTH:3 * WIDTH]
        p_in = jnp.exp(logp)
        p_ex = jnp.exp(logp - log_decay)
        p_inv = jnp.exp(-logp)
        kk = kk_raw / jnp.maximum(jnp.sqrt(_head_sums(kk_raw * kk_raw)), 1e-12)
        return dict(a_hat=-kk * p_ex, r_hat=r * p_in, b_hat=kk * lr * p_inv, k_hat=k_mod * p_inv,
                    v=v, p_in=p_in, gate=gate, bonus=_head_sums(r * k_mod * rk_ref[...]) * v)

    def front(seqs, w):
        chains = [(j, b, p) for j, b in enumerate(seqs) for p in range(HEADS // 2)]
        ids = range(len(chains))
        rs = [slice(j * t, (j + 1) * t) for j, _, _ in chains]
        ls = [slice(2 * HEAD_DIM * p, 2 * HEAD_DIM * (p + 1)) for _, _, p in chains]
        p_end = [w["p_in"][(j + 1) * t - 1:(j + 1) * t, ls[i]] for i, (j, _, _) in enumerate(chains)]
        ar = [jnp.concatenate([stack2(w["a_hat"][rs[i], ls[i]]), stack2(w["r_hat"][rs[i], ls[i]])],
                              axis=0).astype(BF16) for i in ids]
        bk2 = [jnp.concatenate([stack2(w["b_hat"][rs[i], ls[i]]), stack2(w["k_hat"][rs[i], ls[i]])], axis=0)
               for i in ids]
        bk = [bk2[i].astype(BF16) for i in ids]
        btkt = [(bk2[i] * p_end[i]).astype(BF16) for i in ids]
        v2 = [stack2(w["v"][rs[i], ls[i]]).astype(BF16) for i in ids]
        g = [_dot_nt(ar[i], bk[i]) for i in ids]
        st = [state_ref[b, p] for _, b, p in chains]
        ars = [_dot_nt(ar[i], st[i].astype(BF16)) for i in ids]
        l_ab = [jnp.where(strict, g[i][0:2 * t, 0:2 * t], 0.0) for i in ids]
        lm = [jnp.concatenate([jnp.where(strict, g[i][0:2 * t, 2 * t:4 * t], 0.0),
                               jnp.where(incl, g[i][2 * t:4 * t, 2 * t:4 * t], 0.0)], axis=0)
              for i in ids]
        m_rb = [jnp.where(incl, g[i][2 * t:4 * t, 0:2 * t], 0.0).astype(BF16) for i in ids]
        lv = [_mm(lm[i], v2[i]) for i in ids]
        return dict(chains=chains, rs=rs, ls=ls, p_end=p_end, btkt=btkt, v2=v2, st=st, ars=ars,
                    l_ab=l_ab, m_rb=m_rb, lv=lv)

    def inverse(f):
        w_inv = [eye + l for l in f["l_ab"]]
        l_pow = f["l_ab"]
        for _ in range(int(math.log2(t)) - 1):
            l_pow = [_mm(l, l) for l in l_pow]
            w_inv = [w + _mm(w, l) for w, l in zip(w_inv, l_pow)]
        return w_inv

    def back(w, f, w_inv, start):
        chains, rs, ls = f["chains"], f["rs"], f["ls"]
        ids = range(len(chains))
        z = [_mm(w_inv[i], f["ars"][i][0:2 * t] + f["lv"][i][0:2 * t]) for i in ids]
        y = [f["ars"][i][2 * t:4 * t] + f["lv"][i][2 * t:4 * t] + _mm(f["m_rb"][i], z[i]) for i in ids]
        for i, (_, b, p) in enumerate(chains):
            zv = jnp.concatenate([z[i].astype(BF16), f["v2"][i]], axis=0)
            state_ref[b, p] = f["st"][i] * f["p_end"][i] + _dot_tn(zv, f["btkt"][i])
        for i, (_, b, p) in enumerate(chains):
            mean = jnp.sum(y[i], axis=-1, keepdims=True) * (1.0 / HEAD_DIM)
            dev = jnp.where(own, y[i] - mean, 0.0)
            var = jnp.sum(dev * dev, axis=-1, keepdims=True) * (1.0 / HEAD_DIM)
            yn = dev * lax.rsqrt(var + GN_EPS)
            yn = yn[0:t] + yn[t:2 * t]
            out = ((yn * gng_ref[:, ls[i]] + gnb_ref[:, ls[i]] + w["bonus"][rs[i], ls[i]])
                   * w["gate"][rs[i], ls[i]])
            y_ref[b, pl.ds(start, t), ls[i]] = out.astype(BF16)

    seqs = list(range(nb))

    def chunk(j, carry):
        start = pl.multiple_of(j * t, t)
        w = wide(seqs, start)
        f = front(seqs, w)
        back(w, f, inverse(f), start)
        return carry

    lax.fori_loop(0, rw_ref.shape[1] // t, chunk, 0)


def _rwkv(rw, w2a2, g2, w0, a0, k_k, k_a, r_k, gn_g, gn_b):
    b, seq, _ = rw.shape
    nc = seq // CHUNK
    nb = RWKV_SEQS
    const = lambda i, c: (0, 0)
    vec = pl.BlockSpec((1, WIDTH), const)
    return pl.pallas_call(
        _rwkv_kernel,
        grid=(b // nb, nc // RWKV_CHUNKS),
        in_specs=[
            pl.BlockSpec((nb, CHUNK * RWKV_CHUNKS, RWKV_PROJ), lambda i, c: (i, c, 0)),
            pl.BlockSpec((DECAY_LORA + AAA_LORA, 2 * WIDTH), const),
            pl.BlockSpec((GATE_LORA, WIDTH), const),
            vec, vec, vec, vec, vec, vec, vec,
        ],
        out_specs=pl.BlockSpec((nb, CHUNK * RWKV_CHUNKS, WIDTH), lambda i, c: (i, c, 0)),
        out_shape=jax.ShapeDtypeStruct((b, seq, WIDTH), BF16),
        scratch_shapes=[pltpu.VMEM((nb, HEADS // 2, 2 * HEAD_DIM, 2 * HEAD_DIM), F32)],
        compiler_params=pltpu.CompilerParams(
            dimension_semantics=("arbitrary", "arbitrary"), vmem_limit_bytes=VMEM_LIMIT),
        name="rwkv7",
    )(rw, w2a2, g2, w0, a0, k_k, k_a, r_k, gn_g, gn_b)


def _merge_kernel(x_ref, att_ref, rwkv_ref, gate_ref, wa32_ref, wb32_ref, wo32_ref, g_ref,
                  wr_ref, br_ref, x1_ref, h2_ref, comb_ref, cnt_ref, wa_ref, wb_ref, wo_ref,
                  hi_ref, lo_ref):
    @pl.when(pl.program_id(0) == 0)
    def _():
        wa_ref[...] = wa32_ref[...].astype(BF16)
        wb_ref[...] = wb32_ref[...].astype(BF16)
        wo_ref[...] = wo32_ref[...].astype(BF16)
        hi_ref[...] = jnp.zeros(hi_ref.shape, BF16)
        lo_ref[...] = jnp.zeros(lo_ref.shape, BF16)

    hw = _dot(hi_ref[...], wr_ref[...])
    lw = _dot(lo_ref[...], wr_ref[:, 0:ROUTER_LANES])
    da = _dot(att_ref[...], wa_ref[...])
    db = _dot(rwkv_ref[...], wb_ref[...])
    logits = hw[:, 0:ROUTER_LANES] + (hw[:, ROUTER_LANES:2 * ROUTER_LANES] + lw) + br_ref[...]
    rec, counts = _route(logits)
    comb_ref[...] = rec
    cnt_ref[0] = jnp.broadcast_to(counts, (8, ROUTER_LANES)).astype(jnp.int32)

    ga = gate_ref[:, 0:D_MODEL].astype(F32)
    gb = gate_ref[:, D_MODEL:2 * D_MODEL].astype(F32)
    merged = ga * da + gb * db
    x1 = x_ref[...] + _dot(merged.astype(BF16), wo_ref[...])
    x1_ref[...] = x1
    h2 = x1 * lax.rsqrt(jnp.mean(x1 * x1, axis=-1, keepdims=True) + RMS_EPS) * g_ref[...]
    h2_hi = h2.astype(BF16)
    h2_ref[...] = _pack_bf16(h2)
    hi_ref[...] = h2_hi
    lo_ref[...] = (h2 - h2_hi.astype(F32)).astype(BF16)


def _route(logits):
    lane_i = lax.broadcasted_iota(jnp.int32, logits.shape, 1)
    lane = lane_i.astype(F32)
    lane_group = (lane_i // EXPERTS_PER_GROUP).astype(F32)
    neg = jnp.finfo(F32).min
    big = float(ROUTER_LANES)

    def first_argmax(vals, mask):
        vm = jnp.where(mask, vals, neg)
        mx = jnp.max(vm, axis=-1, keepdims=True)
        idx = jnp.min(jnp.where(vm == mx, jnp.where(mask, lane, big), big), axis=-1, keepdims=True)
        return mx, idx

    is_group = (lane_i >= GROUP_LANE0) & (lane_i < GROUP_LANE0 + N_GROUPS)
    g_max, g_lane = first_argmax(logits, is_group)
    g_prob = 1.0 / jnp.sum(jnp.where(is_group, jnp.exp(logits - g_max), 0.0),
                           axis=-1, keepdims=True)
    g_idx = g_lane - GROUP_LANE0
    in_group = lane_group == g_idx
    e1, i1 = first_argmax(logits, in_group)
    e2, i2 = first_argmax(logits, in_group & (lane != i1))
    w2 = jnp.exp(e2 - e1)
    p1 = 1.0 / (1.0 + w2)
    p2 = w2 / (1.0 + w2)
    rec = jnp.where(lane_i == EXPERT1_LANE, i1, jnp.where(lane_i == EXPERT2_LANE, i2, 0.0))
    rec = jnp.where(lane_i == WEIGHT1_LANE, p1 * g_prob, jnp.where(lane_i == WEIGHT2_LANE, p2 * g_prob, rec))
    counts = jnp.sum(jnp.where(lane == i1, 1.0, 0.0) + jnp.where(lane == i2, 1.0, 0.0),
                     axis=0, keepdims=True)
    return rec, counts


def _merge(x2, att, rwkv, gates, wa, wb, wo, ln_g, wr, br):
    n = x2.shape[0]
    tm = MERGE_TILE
    tiles = n // tm
    row = lambda i: (jnp.minimum(i, tiles - 1), 0)
    routed = lambda i: (jnp.maximum(i - 1, 0), 0)
    const = lambda i: (0, 0)
    return pl.pallas_call(
        _merge_kernel,
        grid=(tiles + 1,),
        in_specs=[
            pl.BlockSpec((tm, D_MODEL), row),
            pl.BlockSpec((tm, WIDTH), row),
            pl.BlockSpec((tm, WIDTH), row),
            pl.BlockSpec((tm, 2 * D_MODEL), row),
            pl.BlockSpec((WIDTH, D_MODEL), const),
            pl.BlockSpec((WIDTH, D_MODEL), const),
            pl.BlockSpec((D_MODEL, D_MODEL), const),
            pl.BlockSpec((1, D_MODEL), const),
            pl.BlockSpec((D_MODEL, 2 * ROUTER_LANES), const),
            pl.BlockSpec((1, ROUTER_LANES), const),
        ],
        out_specs=[
            pl.BlockSpec((tm, D_MODEL), row),
            pl.BlockSpec((tm, PACKED), row),
            pl.BlockSpec((tm, ROUTER_LANES), routed),
            pl.BlockSpec((1, 8, ROUTER_LANES), lambda i: (jnp.maximum(i - 1, 0), 0, 0)),
        ],
        out_shape=[
            jax.ShapeDtypeStruct((n, D_MODEL), F32),
            jax.ShapeDtypeStruct((n, PACKED), jnp.uint32),
            jax.ShapeDtypeStruct((n, ROUTER_LANES), F32),
            jax.ShapeDtypeStruct((tiles, 8, ROUTER_LANES), jnp.int32),
        ],
        scratch_shapes=[pltpu.VMEM((WIDTH, D_MODEL), BF16), pltpu.VMEM((WIDTH, D_MODEL), BF16),
                        pltpu.VMEM((D_MODEL, D_MODEL), BF16),
                        pltpu.VMEM((tm, D_MODEL), BF16), pltpu.VMEM((tm, D_MODEL), BF16)],
        compiler_params=pltpu.CompilerParams(
            dimension_semantics=("arbitrary",), vmem_limit_bytes=VMEM_LIMIT),
        name="merge",
    )(x2, att, rwkv, gates, wa, wb, wo, ln_g, wr, br)


def _moe_plan_kernel(cnt_ref, base_ref, blk_ref):
    nt = cnt_ref.shape[0]
    cnt = cnt_ref[...].astype(F32)
    lane = lax.broadcasted_iota(jnp.int32, (8, ROUTER_LANES), 1)
    total = jnp.broadcast_to(jnp.sum(cnt, axis=0, keepdims=True), (8, ROUTER_LANES))
    padded = jnp.floor((total + (MOE_BLOCK - 1)) * (1.0 / MOE_BLOCK)) * MOE_BLOCK
    r = lax.broadcasted_iota(jnp.int32, (ROUTER_LANES, ROUTER_LANES), 0)
    c = lax.broadcasted_iota(jnp.int32, (ROUTER_LANES, ROUTER_LANES), 1)
    seg_start = _dot_hi(padded, jnp.where(r < c, 1.0, 0.0).astype(F32))
    tr = lax.broadcasted_iota(jnp.int32, (nt, nt), 0)
    tc = lax.broadcasted_iota(jnp.int32, (nt, nt), 1)
    tile_off = _dot_hi(jnp.where(tc < tr, 1.0, 0.0).astype(F32), cnt)
    base_ref[...] = seg_start[0:1, :] + tile_off
    seg_end = (seg_start + padded)[0:1, :]
    rows_total = jnp.sum(jnp.where(lane[0:1, :] < N_EXPERTS, padded[0:1, :], 0.0), axis=-1, keepdims=True)
    nblk = blk_ref.shape[0]
    blk_row = lax.broadcasted_iota(jnp.int32, (nblk, ROUTER_LANES), 0).astype(F32) * MOE_BLOCK
    blk_lane = lax.broadcasted_iota(jnp.int32, (nblk, ROUTER_LANES), 1)
    done = jnp.where((seg_end <= blk_row) & (blk_lane < N_EXPERTS), 1.0, 0.0)
    expert = jnp.minimum(jnp.sum(done, axis=-1, keepdims=True), N_EXPERTS - 1.0)
    valid = jnp.where(blk_row < rows_total, 1.0, 0.0)
    blk_idx = lax.broadcasted_iota(jnp.int32, (nblk, ROUTER_LANES), 0).astype(F32)
    source = jnp.minimum(blk_idx, rows_total * (1.0 / MOE_BLOCK) - 1.0)
    table = jnp.where(blk_lane == 0, expert, jnp.where(blk_lane == 1, valid,
                                                       jnp.where(blk_lane == 2, source, 0.0)))
    blk_ref[...] = table.astype(jnp.int32)


def _moe_plan(cnt, n_blocks):
    nt = cnt.shape[0]
    return pl.pallas_call(
        _moe_plan_kernel,
        out_shape=[jax.ShapeDtypeStruct((nt, ROUTER_LANES), F32),
                   jax.ShapeDtypeStruct((n_blocks, ROUTER_LANES), jnp.int32)],
        name="moe_plan",
    )(cnt)


def _moe_pos_kernel(comb_ref, base_ref, pos_ref):
    t = comb_ref.shape[0]
    comb = comb_ref[...]
    lane_i = lax.broadcasted_iota(jnp.int32, (t, ROUTER_LANES), 1)
    lane = lane_i.astype(F32)
    pick1 = jnp.where(lane == comb[:, EXPERT1_LANE:EXPERT1_LANE + 1], 1.0, 0.0)
    pick2 = jnp.where(lane == comb[:, EXPERT2_LANE:EXPERT2_LANE + 1], 1.0, 0.0)
    rows = lax.broadcasted_iota(jnp.int32, (t, t), 0)
    cols = lax.broadcasted_iota(jnp.int32, (t, t), 1)
    earlier = jnp.where(cols < rows, 1.0, 0.0).astype(BF16)
    before1 = _dot(earlier, pick1.astype(BF16))
    before2 = _dot(earlier, pick2.astype(BF16))
    base = base_ref[0]
    firsts = jnp.sum(pick1, axis=0, keepdims=True)
    pos1 = jnp.sum(pick1 * (base + before1), axis=-1, keepdims=True)
    pos2 = jnp.sum(pick2 * (base + firsts + before2), axis=-1, keepdims=True)
    both = jnp.where(lane_i == 0, pos1, jnp.where(lane_i == 1, pos2, 0.0))
    pos_ref[...] = jnp.transpose(both)[0:8, :].astype(jnp.int32)


def _moe_pos(comb, base):
    n = comb.shape[0]
    t = MERGE_TILE
    return pl.pallas_call(
        _moe_pos_kernel,
        grid=(n // t,),
        in_specs=[pl.BlockSpec((t, ROUTER_LANES), lambda i: (i, 0)),
                  pl.BlockSpec((1, 1, ROUTER_LANES), lambda i: (i, 0, 0))],
        out_specs=pl.BlockSpec((8, t), lambda i: (0, i)),
        out_shape=jax.ShapeDtypeStruct((8, n), jnp.int32),
        compiler_params=pltpu.CompilerParams(dimension_semantics=("arbitrary",)),
        name="moe_pos",
    )(comb, base.reshape(n // t, 1, ROUTER_LANES))


def _sc_mesh():
    return plsc.VectorSubcoreMesh(core_axis_name="core", subcore_axis_name="subcore")


def _sc_scratch():
    return [pltpu.VMEM((8, SC_WINDOW), jnp.int32), pltpu.VMEM((SC_WINDOW, PACKED), jnp.uint32)]


def _sc_scatter_rows(h, pos, n_rows):
    per_worker = h.shape[0] // (SC_WINDOW * SC_WORKERS)

    @pl.kernel(out_type=jax.ShapeDtypeStruct((n_rows, PACKED), jnp.uint32), mesh=_sc_mesh(),
               scratch_types=_sc_scratch())
    def scatter(h_hbm, pos_hbm, out_hbm, idx, buf):
        worker = lax.axis_index("core") * SC_SUBCORES + lax.axis_index("subcore")

        @pl.loop(0, per_worker)
        def _(b):
            start = (worker * per_worker + b) * SC_WINDOW
            pltpu.sync_copy(pos_hbm.at[:, pl.ds(start, SC_WINDOW)], idx)
            pltpu.sync_copy(h_hbm.at[pl.ds(start, SC_WINDOW)], buf)
            pltpu.sync_copy(buf, out_hbm.at[idx.at[0]])
            pltpu.sync_copy(buf, out_hbm.at[idx.at[1]])

    return scatter(h, pos)


def _sc_gather_rows(y, pos):
    n = pos.shape[1]
    per_worker = n // (SC_WINDOW * SC_WORKERS)
    out = jax.ShapeDtypeStruct((n, PACKED), jnp.uint32)

    @pl.kernel(out_type=(out, out), mesh=_sc_mesh(), scratch_types=_sc_scratch())
    def gather(y_hbm, pos_hbm, o1_hbm, o2_hbm, idx, buf):
        worker = lax.axis_index("core") * SC_SUBCORES + lax.axis_index("subcore")

        @pl.loop(0, per_worker)
        def _(b):
            start = (worker * per_worker + b) * SC_WINDOW
            pltpu.sync_copy(pos_hbm.at[:, pl.ds(start, SC_WINDOW)], idx)
            pltpu.sync_copy(y_hbm.at[idx.at[0]], buf)
            pltpu.sync_copy(buf, o1_hbm.at[pl.ds(start, SC_WINDOW)])
            pltpu.sync_copy(y_hbm.at[idx.at[1]], buf)
            pltpu.sync_copy(buf, o2_hbm.at[pl.ds(start, SC_WINDOW)])

    return gather(y, pos)


def _expert_mlp_kernel(expert_ref, valid_ref, source_ref, hs_ref, wg32_ref, wu32_ref, wd32_ref, y_ref,
                       wg_ref, wu_ref, wd_ref):
    b = pl.program_id(0)

    @pl.when(valid_ref[b] != 0)
    def _():
        @pl.when((b == 0) | (expert_ref[b] != expert_ref[jnp.maximum(b - 1, 0)]))
        def _():
            wg_ref[...] = wg32_ref[0].astype(BF16)
            wu_ref[...] = wu32_ref[0].astype(BF16)
            wd_ref[...] = wd32_ref[0].astype(BF16)

        half = D_MODEL // 2
        h_lo, h_hi = _unpack_bf16(hs_ref[...])
        h_lo = h_lo.astype(BF16)
        h_hi = h_hi.astype(BF16)
        hg = _dot(h_lo, wg_ref[0:half, :]) + _dot(h_hi, wg_ref[half:D_MODEL, :])
        hu = _dot(h_lo, wu_ref[0:half, :]) + _dot(h_hi, wu_ref[half:D_MODEL, :])
        act = (hg * _sigmoid(hg) * hu).astype(BF16)
        y_ref[...] = _pack_bf16(_dot(act, wd_ref[...]))


def _expert_mlp(blk_expert, blk_valid, blk_source, hs, wg, wu, wd):
    rows = lambda b, expert, valid, source: (source[b], 0)
    by_expert = lambda b, expert, valid, source: (expert[source[b]], 0, 0)
    grid_spec = pltpu.PrefetchScalarGridSpec(
        num_scalar_prefetch=3,
        grid=(hs.shape[0] // MOE_BLOCK,),
        in_specs=[
            pl.BlockSpec((MOE_BLOCK, PACKED), rows),
            pl.BlockSpec((1, D_MODEL, D_EXPERT), by_expert),
            pl.BlockSpec((1, D_MODEL, D_EXPERT), by_expert),
            pl.BlockSpec((1, D_EXPERT, D_MODEL), by_expert),
        ],
        out_specs=pl.BlockSpec((MOE_BLOCK, PACKED), rows),
        scratch_shapes=[pltpu.VMEM((D_MODEL, D_EXPERT), BF16), pltpu.VMEM((D_MODEL, D_EXPERT), BF16),
                        pltpu.VMEM((D_EXPERT, D_MODEL), BF16)],
    )
    return pl.pallas_call(
        _expert_mlp_kernel,
        grid_spec=grid_spec,
        out_shape=jax.ShapeDtypeStruct((hs.shape[0], PACKED), jnp.uint32),
        compiler_params=pltpu.CompilerParams(
            dimension_semantics=("arbitrary",), vmem_limit_bytes=VMEM_LIMIT),
        name="expert_mlp",
    )(blk_expert, blk_valid, blk_source, hs, wg, wu, wd)


def _moe_out_kernel(x1_ref, y1_ref, y2_ref, comb_ref, lnf_ref, out_ref, *, final_norm):
    w1 = comb_ref[:, WEIGHT1_LANE:WEIGHT1_LANE + 1]
    w2 = comb_ref[:, WEIGHT2_LANE:WEIGHT2_LANE + 1]
    a_lo, a_hi = _unpack_bf16(y1_ref[...])
    b_lo, b_hi = _unpack_bf16(y2_ref[...])
    moe = jnp.concatenate([w1 * a_lo + w2 * b_lo, w1 * a_hi + w2 * b_hi], axis=1)
    y = x1_ref[...] + moe
    if final_norm:
        y = y * lax.rsqrt(jnp.mean(y * y, axis=-1, keepdims=True) + RMS_EPS) * lnf_ref[...]
    out_ref[...] = y


def _moe_out(x1, y1, y2, comb, lnf, final_norm):
    n = x1.shape[0]
    t = FINAL_TILE
    row = lambda i: (i, 0)
    return pl.pallas_call(
        functools.partial(_moe_out_kernel, final_norm=final_norm),
        grid=(n // t,),
        in_specs=[pl.BlockSpec((t, D_MODEL), row), pl.BlockSpec((t, PACKED), row),
                  pl.BlockSpec((t, PACKED), row), pl.BlockSpec((t, ROUTER_LANES), row),
                  pl.BlockSpec((1, D_MODEL), lambda i: (0, 0))],
        out_specs=pl.BlockSpec((t, D_MODEL), row),
        out_shape=jax.ShapeDtypeStruct((n, D_MODEL), F32),
        compiler_params=pltpu.CompilerParams(
            dimension_semantics=("arbitrary",), vmem_limit_bytes=VMEM_LIMIT),
        name="moe_out",
    )(x1, y1, y2, comb, lnf)


def _moe(counts, x1, h2p, comb, wg, wu, wd, lnf, final_norm):
    n = x1.shape[0]
    n_blocks = (2 * n) // MOE_BLOCK + N_EXPERTS
    base, blk = _moe_plan(counts[:, 0, :], n_blocks)
    pos = _moe_pos(comb, base)
    hs = _sc_scatter_rows(h2p, pos, n_blocks * MOE_BLOCK)
    ys = _expert_mlp(blk[:, 0], blk[:, 1], blk[:, 2], hs, wg, wu, wd)
    y1, y2 = _sc_gather_rows(ys, pos)
    return _moe_out(x1, y1, y2, comb, lnf, final_norm)


def kernel(x, ln_mix_g, w_in, att_rel_bias, rwkv_mu, rwkv_w0, rwkv_w2, rwkv_a0, rwkv_a2, rwkv_g2,
           rwkv_k_k, rwkv_k_a, rwkv_r_k, rwkv_gn_g, rwkv_gn_b, w_branch_att, w_branch_rwkv, w_out,
           ln_ffn_g, router_group_w, router_group_b, router_expert_w, router_expert_b,
           expert_w_gate, expert_w_up, expert_w_down, ln_final_g):
    bsz, seq, d = x.shape
    depth = w_in.shape[0]
    n = bsz * seq
    x2 = x.reshape(n, d)
    for l in range(depth):
        q, k, v, rw, gates = _in_proj(x2, ln_mix_g[l][None, :], w_in[l].astype(BF16),
                                      rwkv_mu[l][None, :], seq)
        bias = _rel_bias(att_rel_bias[l])
        att = _band_attn(q.reshape(bsz, seq, WIDTH), k.reshape(bsz, seq, WIDTH),
                         v.reshape(bsz, seq, WIDTH), bias)
        zeros = jnp.zeros((DECAY_LORA, WIDTH), F32)
        w2a2 = jnp.concatenate(
            [jnp.concatenate([rwkv_w2[l], zeros], axis=1),
             jnp.concatenate([zeros, rwkv_a2[l]], axis=1)], axis=0)
        rwkv = _rwkv(rw.reshape(bsz, seq, RWKV_PROJ), w2a2.astype(BF16), rwkv_g2[l].astype(BF16),
                     rwkv_w0[l][None, :], rwkv_a0[l][None, :], rwkv_k_k[l][None, :],
                     rwkv_k_a[l][None, :], rwkv_r_k[l].reshape(1, WIDTH),
                     rwkv_gn_g[l][None, :], rwkv_gn_b[l][None, :])
        wr = jnp.concatenate([router_expert_w[l], router_group_w[l]], axis=1)
        wr = jnp.pad(wr, ((0, 0), (0, ROUTER_LANES - wr.shape[1])))
        wr_hi = wr.astype(BF16)
        wr = jnp.concatenate([wr_hi, (wr - wr_hi.astype(F32)).astype(BF16)], axis=1)
        br = jnp.concatenate([router_expert_b[l], router_group_b[l]])
        br = jnp.pad(br, (0, ROUTER_LANES - br.shape[0]))[None, :]
        x1, h2, comb, counts = _merge(x2, att.reshape(n, WIDTH), rwkv.reshape(n, WIDTH), gates,
                                      w_branch_att[l], w_branch_rwkv[l], w_out[l],
                                      ln_ffn_g[l][None, :], wr, br)
        x2 = _moe(counts, x1, h2, comb, expert_w_gate[l], expert_w_up[l], expert_w_down[l],
                  ln_final_g[None, :], final_norm=(l == depth - 1))
    return x2.reshape(bsz, seq, d)
```

```python
import functools
import math

import jax
import jax.numpy as jnp
from jax import lax
from jax.experimental import pallas as pl
from jax.experimental.pallas import tpu as pltpu
from jax.experimental.pallas import tpu_sc as plsc

F32 = jnp.float32
BF16 = jnp.bfloat16
HIGHEST = lax.Precision.HIGHEST

D_MODEL = 1024
CHUNK = 64
HEADS = 8
HEAD_DIM = 64
WIDTH = HEADS * HEAD_DIM
LEFT_CHUNKS = 8
BAND = (LEFT_CHUNKS + 1) * CHUNK
REL_CLIP = 64
N_REL = 2 * REL_CLIP + 1
DECAY_LORA = 64
AAA_LORA = 64
GATE_LORA = 128
GN_EPS = 64e-5
RMS_EPS = 1e-6
ATT_PROJ = 3 * WIDTH
RWKV_PROJ = 3 * WIDTH + DECAY_LORA + AAA_LORA + GATE_LORA
D_IN = ATT_PROJ + RWKV_PROJ + 2 * D_MODEL
N_GROUPS = 4
EXPERTS_PER_GROUP = 8
N_EXPERTS = N_GROUPS * EXPERTS_PER_GROUP
D_EXPERT = 256
RWKV_SEQS = 4
RWKV_CHUNKS = 4
BIAS_KEYS = 192
ATT_CHUNKS = 8
ATT_GROUP = 4
ROUTER_LANES = 128
GROUP_LANE0 = N_EXPERTS
EXPERT1_LANE, EXPERT2_LANE, WEIGHT1_LANE, WEIGHT2_LANE = 126, 125, 124, 123
MERGE_TILE = 512
MOE_BLOCK = 1024
PACKED = D_MODEL // 2
SC_WINDOW = 128
SC_CORES, SC_SUBCORES = 2, 16
SC_WORKERS = SC_CORES * SC_SUBCORES
FINAL_TILE = 1024

V7X_VMEM_BYTES = 64 * 1024 * 1024
VMEM_LIMIT = V7X_VMEM_BYTES - 12 * 1024 * 1024


def _dot(a, b):
    return jnp.dot(a, b, preferred_element_type=F32)


def _dot_hi(a, b):
    return jnp.dot(a, b, preferred_element_type=F32, precision=HIGHEST)


def _dot_nt(a, b, precision=None):
    return lax.dot_general(a, b, (((1,), (1,)), ((), ())),
                           preferred_element_type=F32, precision=precision)


def _dot_tn(a, b, precision=None):
    return lax.dot_general(a, b, (((0,), (0,)), ((), ())),
                           preferred_element_type=F32, precision=precision)


def _sigmoid(x):
    return 1.0 / (1.0 + jnp.exp(-x))


def _pack_bf16(x):
    w = x.shape[1] // 2
    hi = pltpu.bitcast(x[:, :w].astype(BF16).astype(F32), jnp.uint32)
    lo = pltpu.bitcast(x[:, w:].astype(BF16).astype(F32), jnp.uint32)
    return hi | lax.shift_right_logical(lo, jnp.uint32(16))


def _unpack_bf16(p):
    hi = pltpu.bitcast(p & jnp.uint32(0xFFFF0000), F32)
    lo = pltpu.bitcast(lax.shift_left(p, jnp.uint32(16)), F32)
    return hi, lo


def _mm(a, b):
    return jnp.dot(a.astype(BF16), b.astype(BF16), preferred_element_type=F32)


def _head_sums(x):
    outs = []
    lane = lax.broadcasted_iota(jnp.int32, (x.shape[0], 2 * HEAD_DIM), 1)
    low = lane < HEAD_DIM
    for p in range(HEADS // 2):
        xp = x[:, 2 * HEAD_DIM * p:2 * HEAD_DIM * (p + 1)]
        s_lo = jnp.sum(jnp.where(low, xp, 0.0), axis=-1, keepdims=True)
        s_hi = jnp.sum(jnp.where(low, 0.0, xp), axis=-1, keepdims=True)
        outs.append(jnp.where(low, s_lo, s_hi))
    return jnp.concatenate(outs, axis=-1)


def _rel_bias_kernel(tab_ref, out_ref):
    rows = tab_ref.shape[1]
    n = lax.broadcasted_iota(jnp.int32, (rows, CHUNK * 128), 1)
    r = lax.broadcasted_iota(jnp.int32, (rows, CHUNK * 128), 0)
    q = n >> 7
    kk = n & 127
    idx = jnp.clip(CHUNK + q - kk, -REL_CLIP, REL_CLIP) + REL_CLIP
    pick = jnp.where(r == idx, 1.0, 0.0) - jnp.where(r == N_REL - 1, 1.0, 0.0)
    out_ref[...] = _dot_hi(tab_ref[...], pick.astype(F32))


def _rel_bias(rel_table):
    rows = 136
    tab = jnp.pad(rel_table.astype(F32), ((0, 0), (0, rows - N_REL)))
    tail = pl.pallas_call(
        _rel_bias_kernel,
        out_shape=jax.ShapeDtypeStruct((HEADS, CHUNK * 128), F32),
        name="rel_bias",
    )(tab)
    tail = tail.reshape(HEADS, CHUNK, 128)
    bias = jnp.concatenate([jnp.zeros((HEADS, CHUNK, BIAS_KEYS - 128), F32), tail], axis=-1)
    return bias.reshape(HEADS * CHUNK, BIAS_KEYS)


def _in_proj_kernel(x_ref, g_ref, w_ref, mu_ref, q_ref, k_ref, v_ref, rw_ref, gate_ref,
                    carry_ref, *, tiles_per_seq):
    i = pl.program_id(0)

    @pl.when(i == 0)
    def _():
        carry_ref[...] = jnp.zeros(carry_ref.shape, F32)

    x = x_ref[...]
    h = x * lax.rsqrt(jnp.mean(x * x, axis=-1, keepdims=True) + RMS_EPS) * g_ref[...]
    hb = h.astype(BF16)
    q_ref[...] = _dot(hb, w_ref[:, 0:WIDTH]).astype(BF16)
    k_ref[...] = _dot(hb, w_ref[:, WIDTH:2 * WIDTH]).astype(BF16)
    v_ref[...] = _dot(hb, w_ref[:, 2 * WIDTH:ATT_PROJ]).astype(BF16)
    rw = _dot(hb, w_ref[:, ATT_PROJ:ATT_PROJ + RWKV_PROJ])
    tm = rw.shape[0]
    first_prev = jnp.where(i % tiles_per_seq == 0, 0.0, carry_ref[0:1, :])
    rolled = pltpu.roll(rw, 1, axis=0)
    row = lax.broadcasted_iota(jnp.int32, rw.shape, 0)
    prev = jnp.where(row == 0, first_prev, rolled)
    carry_ref[0:1, :] = rw[tm - 1:tm, :]
    rw_ref[...] = rw + (prev - rw) * mu_ref[...]
    gate_ref[...] = _sigmoid(_dot(hb, w_ref[:, ATT_PROJ + RWKV_PROJ:D_IN])).astype(BF16)


def _in_proj(x2, ln_g, w_in_b, mu, seq):
    n = x2.shape[0]
    tm = 1024
    assert seq % tm == 0, "a projection tile must not straddle two sequences (token shift carry)"
    row = lambda i: (i, 0)
    const = lambda i: (0, 0)
    return pl.pallas_call(
        functools.partial(_in_proj_kernel, tiles_per_seq=seq // tm),
        grid=(n // tm,),
        in_specs=[
            pl.BlockSpec((tm, D_MODEL), row),
            pl.BlockSpec((1, D_MODEL), const),
            pl.BlockSpec((D_MODEL, D_IN), const, pipeline_mode=pl.Buffered(1)),
            pl.BlockSpec((1, RWKV_PROJ), const),
        ],
        out_specs=[
            pl.BlockSpec((tm, WIDTH), row),
            pl.BlockSpec((tm, WIDTH), row),
            pl.BlockSpec((tm, WIDTH), row),
            pl.BlockSpec((tm, RWKV_PROJ), row),
            pl.BlockSpec((tm, 2 * D_MODEL), row),
        ],
        out_shape=[
            jax.ShapeDtypeStruct((n, WIDTH), BF16),
            jax.ShapeDtypeStruct((n, WIDTH), BF16),
            jax.ShapeDtypeStruct((n, WIDTH), BF16),
            jax.ShapeDtypeStruct((n, RWKV_PROJ), F32),
            jax.ShapeDtypeStruct((n, 2 * D_MODEL), BF16),
        ],
        scratch_shapes=[pltpu.VMEM((8, RWKV_PROJ), F32)],
        compiler_params=pltpu.CompilerParams(
            dimension_semantics=("arbitrary",), vmem_limit_bytes=VMEM_LIMIT),
        name="in_proj",
    )(x2, ln_g, w_in_b, mu)


def _band_attn_kernel(q_ref, k_ref, v_ref, bias_ref, o_ref, kpad_ref, vpad_ref):
    seq = k_ref.shape[1]
    pad = LEFT_CHUNKS * CHUNK
    kpad_ref[0:pad, :] = jnp.zeros((pad, WIDTH), BF16)
    vpad_ref[0:pad, :] = jnp.zeros((pad, WIDTH), BF16)
    kpad_ref[pad:pad + seq, :] = k_ref[0]
    vpad_ref[pad:pad + seq, :] = v_ref[0]

    gw = ATT_GROUP * HEAD_DIM
    rows = ATT_GROUP * CHUNK
    r_head = lax.broadcasted_iota(jnp.int32, (rows, gw), 0) // CHUNK
    l_head = lax.broadcasted_iota(jnp.int32, (rows, gw), 1) // HEAD_DIM
    own = r_head == l_head
    kpos_lo = lax.broadcasted_iota(jnp.int32, (rows, BAND - BIAS_KEYS), 1)
    kpos_hi = lax.broadcasted_iota(jnp.int32, (rows, BIAS_KEYS), 1) + (BAND - BIAS_KEYS)
    neg = jnp.finfo(F32).min
    groups = range(HEADS // ATT_GROUP)
    lanes = [slice(g * gw, (g + 1) * gw) for g in groups]

    def chunk_pair(i, carry, masked):
        units = [(j, g) for j in range(ATT_CHUNKS) for g in groups]
        ids = range(len(units))
        starts = [pl.multiple_of((i * ATT_CHUNKS + j) * CHUNK, CHUNK) for j in range(ATT_CHUNKS)]
        kb = [kpad_ref[pl.ds(st, BAND), :] for st in starts]
        vb = [vpad_ref[pl.ds(st, BAND), :] for st in starts]
        q = [q_ref[0, pl.ds(st, CHUNK), :] * (HEAD_DIM ** -0.5) for st in starts]
        qrows = [jnp.where(own, jnp.concatenate([q[j][:, lanes[g]]] * ATT_GROUP, axis=0),
                           jnp.zeros((), BF16)) for j, g in units]
        s = [_dot_nt(qrows[u], kb[j][:, lanes[g]]) for u, (j, g) in enumerate(units)]
        s_lo = [s[u][:, 0:BAND - BIAS_KEYS] for u in ids]
        s_hi = [s[u][:, BAND - BIAS_KEYS:BAND] + bias_ref[g * rows:(g + 1) * rows, :]
                for u, (j, g) in enumerate(units)]
        if masked:
            first = [(LEFT_CHUNKS - (i * ATT_CHUNKS + j)) * CHUNK for j in range(ATT_CHUNKS)]
            s_lo = [jnp.where(kpos_lo >= first[j], s_lo[u], neg) for u, (j, g) in enumerate(units)]
            s_hi = [jnp.where(kpos_hi >= first[j], s_hi[u], neg) for u, (j, g) in enumerate(units)]
        m = [jnp.maximum(jnp.max(s_lo[u], axis=-1, keepdims=True),
                         jnp.max(s_hi[u], axis=-1, keepdims=True)) for u in ids]
        p_lo = [jnp.exp(s_lo[u] - m[u]) for u in ids]
        p_hi = [jnp.exp(s_hi[u] - m[u]) for u in ids]
        denom = [jnp.sum(p_lo[u], axis=-1, keepdims=True) + jnp.sum(p_hi[u], axis=-1, keepdims=True)
                 for u in ids]
        o_all = [(_dot(p_lo[u].astype(BF16), vb[j][0:BAND - BIAS_KEYS, lanes[g]])
                  + _dot(p_hi[u].astype(BF16), vb[j][BAND - BIAS_KEYS:BAND, lanes[g]])) / denom[u]
                 for u, (j, g) in enumerate(units)]
        for u, (j, g) in enumerate(units):
            o_own = jnp.where(own, o_all[u], 0.0)
            o = o_own[0:CHUNK]
            for h in range(1, ATT_GROUP):
                o = o + o_own[h * CHUNK:(h + 1) * CHUNK]
            o_ref[0, pl.ds(starts[j], CHUNK), lanes[g]] = o.astype(BF16)
        return carry

    n_trips = seq // (CHUNK * ATT_CHUNKS)
    n_masked = min(LEFT_CHUNKS // ATT_CHUNKS, n_trips)
    lax.fori_loop(0, n_masked, functools.partial(chunk_pair, masked=True), 0)
    lax.fori_loop(n_masked, n_trips, functools.partial(chunk_pair, masked=False), 0)


def _band_attn(q, k, v, bias):
    b, seq, _ = q.shape
    whole = pl.BlockSpec((1, seq, WIDTH), lambda i: (i, 0, 0))
    return pl.pallas_call(
        _band_attn_kernel,
        grid=(b,),
        in_specs=[whole, whole, whole, pl.BlockSpec((HEADS * CHUNK, BIAS_KEYS), lambda i: (0, 0))],
        out_specs=whole,
        out_shape=jax.ShapeDtypeStruct((b, seq, WIDTH), BF16),
        scratch_shapes=[pltpu.VMEM((seq + LEFT_CHUNKS * CHUNK, WIDTH), BF16),
                        pltpu.VMEM((seq + LEFT_CHUNKS * CHUNK, WIDTH), BF16)],
        compiler_params=pltpu.CompilerParams(
            dimension_semantics=("arbitrary",), vmem_limit_bytes=VMEM_LIMIT),
        name="band_attn",
    )(q, k, v, bias)


def _rwkv_kernel(rw_ref, w2a2_ref, g2_ref, w0_ref, a0_ref, kk_ref, ka_ref, rk_ref,
                 gng_ref, gnb_ref, y_ref, state_ref):
    c = pl.program_id(1)
    t = CHUNK
    nb = rw_ref.shape[0]

    @pl.when(c == 0)
    def _():
        state_ref[...] = jnp.zeros(state_ref.shape, F32)

    lane = lax.broadcasted_iota(jnp.int32, (t, 2 * HEAD_DIM), 1)
    low = lane < HEAD_DIM
    r2 = lax.broadcasted_iota(jnp.int32, (2 * t, 2 * HEAD_DIM), 0)
    c2 = lax.broadcasted_iota(jnp.int32, (2 * t, 2 * HEAD_DIM), 1)
    own = (r2 < t) == (c2 < HEAD_DIM)
    strict = (r2 & (t - 1)) > (c2 & (t - 1))
    incl = (r2 & (t - 1)) >= (c2 & (t - 1))
    eye = jnp.where(r2 == c2, 1.0, 0.0).astype(F32)

    def stack2(xp):
        return jnp.concatenate([jnp.where(low, xp, 0.0), jnp.where(low, 0.0, xp)], axis=0)

    def wide(seqs, start):
        rows = len(seqs) * t
        rw = rw_ref[seqs[0]:seqs[-1] + 1, pl.ds(start, t), :].reshape(rows, RWKV_PROJ)
        r = rw[:, 0:WIDTH]
        k = rw[:, WIDTH:2 * WIDTH]
        v = rw[:, 2 * WIDTH:3 * WIDTH]
        lora = rw[:, 3 * WIDTH:3 * WIDTH + DECAY_LORA + AAA_LORA]
        g_lo = rw[:, 3 * WIDTH + DECAY_LORA + AAA_LORA:RWKV_PROJ]
        lane128 = lax.broadcasted_iota(jnp.int32, lora.shape, 1)
        lora = jnp.where(lane128 < DECAY_LORA, jnp.tanh(lora), lora)
        wa = _mm(lora, w2a2_ref[...])
        log_decay = -math.exp(-0.5) * _sigmoid(w0_ref[...] + wa[:, 0:WIDTH])
        lr = _sigmoid(a0_ref[...] + wa[:, WIDTH:2 * WIDTH])
        gate = _mm(_sigmoid(g_lo), g2_ref[...])
        kk_raw = k * kk_ref[...]
        k_mod = k * (1.0 + (lr - 1.0) * ka_ref[...])
        row = lax.broadcasted_iota(jnp.int32, (rows, rows), 0)
        col = lax.broadcasted_iota(jnp.int32, (rows, rows), 1)
        tri = jnp.where((row >= col) & ((row // t) == (col // t)), 1.0, 0.0).astype(BF16)
        ld1 = log_decay.astype(BF16)
        rem = log_decay - ld1.astype(F32)
        ld2 = rem.astype(BF16)
        ld3 = (rem - ld2.astype(F32)).astype(BF16)
        parts = _dot(tri, jnp.concatenate([ld1, ld2, ld3], axis=1))
        logp = parts[:, 0:WIDTH] + parts[:, WIDTH:2 * WIDTH] + parts[:, 2 * WIDTH:3 * WIDTH]
        p_in = jnp.exp(logp)
        p_ex = jnp.exp(logp - log_decay)
        p_inv = jnp.exp(-logp)
        kk = kk_raw / jnp.maximum(jnp.sqrt(_head_sums(kk_raw * kk_raw)), 1e-12)
        return dict(a_hat=-kk * p_ex, r_hat=r * p_in, b_hat=kk * lr * p_inv, k_hat=k_mod * p_inv,
                    v=v, p_in=p_in, gate=gate, bonus=_head_sums(r * k_mod * rk_ref[...]) * v)

    def front(seqs, w):
        chains = [(j, b, p) for j, b in enumerate(seqs) for p in range(HEADS // 2)]
        ids = range(len(chains))
        rs = [slice(j * t, (j + 1) * t) for j, _, _ in chains]
        ls = [slice(2 * HEAD_DIM * p, 2 * HEAD_DIM * (p + 1)) for _, _, p in chains]
        p_end = [w["p_in"][(j + 1) * t - 1:(j + 1) * t, ls[i]] for i, (j, _, _) in enumerate(chains)]
        ar = [jnp.concatenate([stack2(w["a_hat"][rs[i], ls[i]]), stack2(w["r_hat"][rs[i], ls[i]])],
                              axis=0).astype(BF16) for i in ids]
        bk2 = [jnp.concatenate([stack2(w["b_hat"][rs[i], ls[i]]), stack2(w["k_hat"][rs[i], ls[i]])], axis=0)
               for i in ids]
        bk = [bk2[i].astype(BF16) for i in ids]
        btkt = [(bk2[i] * p_end[i]).astype(BF16) for i in ids]
        v2 = [stack2(w["v"][rs[i], ls[i]]).astype(BF16) for i in ids]
        g = [_dot_nt(ar[i], bk[i]) for i in ids]
        st = [state_ref[b, p] for _, b, p in chains]
        ars = [_dot_nt(ar[i], st[i].astype(BF16)) for i in ids]
        l_ab = [jnp.where(strict, g[i][0:2 * t, 0:2 * t], 0.0) for i in ids]
        lm = [jnp.concatenate([jnp.where(strict, g[i][0:2 * t, 2 * t:4 * t], 0.0),
                               jnp.where(incl, g[i][2 * t:4 * t, 2 * t:4 * t], 0.0)], axis=0)
              for i in ids]
        m_rb = [jnp.where(incl, g[i][2 * t:4 * t, 0:2 * t], 0.0).astype(BF16) for i in ids]
        lv = [_mm(lm[i], v2[i]) for i in ids]
        return dict(chains=chains, rs=rs, ls=ls, p_end=p_end, btkt=btkt, v2=v2, st=st, ars=ars,
                    l_ab=l_ab, m_rb=m_rb, lv=lv)

    def inverse(f):
        w_inv = [eye + l for l in f["l_ab"]]
        l_pow = f["l_ab"]
        for _ in range(int(math.log2(t)) - 1):
            l_pow = [_mm(l, l) for l in l_pow]
            w_inv = [w + _mm(w, l) for w, l in zip(w_inv, l_pow)]
        return w_inv

    def back(w, f, w_inv, start):
        chains, rs, ls = f["chains"], f["rs"], f["ls"]
        ids = range(len(chains))
        z = [_mm(w_inv[i], f["ars"][i][0:2 * t] + f["lv"][i][0:2 * t]) for i in ids]
        y = [f["ars"][i][2 * t:4 * t] + f["lv"][i][2 * t:4 * t] + _mm(f["m_rb"][i], z[i]) for i in ids]
        for i, (_, b, p) in enumerate(chains):
            zv = jnp.concatenate([z[i].astype(BF16), f["v2"][i]], axis=0)
            state_ref[b, p] = f["st"][i] * f["p_end"][i] + _dot_tn(zv, f["btkt"][i])
        for i, (_, b, p) in enumerate(chains):
            mean = jnp.sum(y[i], axis=-1, keepdims=True) * (1.0 / HEAD_DIM)
            dev = jnp.where(own, y[i] - mean, 0.0)
            var = jnp.sum(dev * dev, axis=-1, keepdims=True) * (1.0 / HEAD_DIM)
            yn = dev * lax.rsqrt(var + GN_EPS)
            yn = yn[0:t] + yn[t:2 * t]
            out = ((yn * gng_ref[:, ls[i]] + gnb_ref[:, ls[i]] + w["bonus"][rs[i], ls[i]])
                   * w["gate"][rs[i], ls[i]])
            y_ref[b, pl.ds(start, t), ls[i]] = out.astype(BF16)

    seqs = list(range(nb))

    def chunk(j, carry):
        start = pl.multiple_of(j * t, t)
        w = wide(seqs, start)
        f = front(seqs, w)
        back(w, f, inverse(f), start)
        return carry

    lax.fori_loop(0, rw_ref.shape[1] // t, chunk, 0)


def _rwkv(rw, w2a2, g2, w0, a0, k_k, k_a, r_k, gn_g, gn_b):
    b, seq, _ = rw.shape
    nc = seq // CHUNK
    nb = RWKV_SEQS
    const = lambda i, c: (0, 0)
    vec = pl.BlockSpec((1, WIDTH), const)
    return pl.pallas_call(
        _rwkv_kernel,
        grid=(b // nb, nc // RWKV_CHUNKS),
        in_specs=[
            pl.BlockSpec((nb, CHUNK * RWKV_CHUNKS, RWKV_PROJ), lambda i, c: (i, c, 0)),
            pl.BlockSpec((DECAY_LORA + AAA_LORA, 2 * WIDTH), const),
            pl.BlockSpec((GATE_LORA, WIDTH), const),
            vec, vec, vec, vec, vec, vec, vec,
        ],
        out_specs=pl.BlockSpec((nb, CHUNK * RWKV_CHUNKS, WIDTH), lambda i, c: (i, c, 0)),
        out_shape=jax.ShapeDtypeStruct((b, seq, WIDTH), BF16),
        scratch_shapes=[pltpu.VMEM((nb, HEADS // 2, 2 * HEAD_DIM, 2 * HEAD_DIM), F32)],
        compiler_params=pltpu.CompilerParams(
            dimension_semantics=("arbitrary", "arbitrary"), vmem_limit_bytes=VMEM_LIMIT),
        name="rwkv7",
    )(rw, w2a2, g2, w0, a0, k_k, k_a, r_k, gn_g, gn_b)


def _merge_kernel(x_ref, att_ref, rwkv_ref, gate_ref, wa32_ref, wb32_ref, wo32_ref, g_ref,
                  wr_ref, br_ref, x1_ref, h2_ref, comb_ref, cnt_ref, wa_ref, wb_ref, wo_ref):
    @pl.when(pl.program_id(0) == 0)
    def _():
        wa_ref[...] = wa32_ref[...].astype(BF16)
        wb_ref[...] = wb32_ref[...].astype(BF16)
        wo_ref[...] = wo32_ref[...].astype(BF16)

    ga = gate_ref[:, 0:D_MODEL].astype(F32)
    gb = gate_ref[:, D_MODEL:2 * D_MODEL].astype(F32)
    merged = ga * _dot(att_ref[...], wa_ref[...]) + gb * _dot(rwkv_ref[...], wb_ref[...])
    x1 = x_ref[...] + _dot(merged.astype(BF16), wo_ref[...])
    x1_ref[...] = x1
    h2 = x1 * lax.rsqrt(jnp.mean(x1 * x1, axis=-1, keepdims=True) + RMS_EPS) * g_ref[...]
    h2_hi = h2.astype(BF16)
    h2_ref[...] = _pack_bf16(h2)
    h2_lo = (h2 - h2_hi.astype(F32)).astype(BF16)
    hw = _dot(h2_hi, wr_ref[...])
    lw = _dot(h2_lo, wr_ref[:, 0:ROUTER_LANES])
    logits = hw[:, 0:ROUTER_LANES] + (hw[:, ROUTER_LANES:2 * ROUTER_LANES] + lw) + br_ref[...]
    rec, counts = _route(logits)
    comb_ref[...] = rec
    cnt_ref[0] = jnp.broadcast_to(counts, (8, ROUTER_LANES)).astype(jnp.int32)


def _route(logits):
    lane_i = lax.broadcasted_iota(jnp.int32, logits.shape, 1)
    lane = lane_i.astype(F32)
    lane_group = (lane_i // EXPERTS_PER_GROUP).astype(F32)
    neg = jnp.finfo(F32).min
    big = float(ROUTER_LANES)

    def first_argmax(vals, mask):
        vm = jnp.where(mask, vals, neg)
        mx = jnp.max(vm, axis=-1, keepdims=True)
        idx = jnp.min(jnp.where(vm == mx, jnp.where(mask, lane, big), big), axis=-1, keepdims=True)
        return mx, idx

    is_group = (lane_i >= GROUP_LANE0) & (lane_i < GROUP_LANE0 + N_GROUPS)
    g_max, g_lane = first_argmax(logits, is_group)
    g_prob = 1.0 / jnp.sum(jnp.where(is_group, jnp.exp(logits - g_max), 0.0),
                           axis=-1, keepdims=True)
    g_idx = g_lane - GROUP_LANE0
    in_group = lane_group == g_idx
    e1, i1 = first_argmax(logits, in_group)
    e2, i2 = first_argmax(logits, in_group & (lane != i1))
    w2 = jnp.exp(e2 - e1)
    p1 = 1.0 / (1.0 + w2)
    p2 = w2 / (1.0 + w2)
    rec = jnp.where(lane_i == EXPERT1_LANE, i1, jnp.where(lane_i == EXPERT2_LANE, i2, 0.0))
    rec = jnp.where(lane_i == WEIGHT1_LANE, p1 * g_prob, jnp.where(lane_i == WEIGHT2_LANE, p2 * g_prob, rec))
    counts = jnp.sum(jnp.where(lane == i1, 1.0, 0.0) + jnp.where(lane == i2, 1.0, 0.0),
                     axis=0, keepdims=True)
    return rec, counts


def _merge(x2, att, rwkv, gates, wa, wb, wo, ln_g, wr, br):
    n = x2.shape[0]
    tm = MERGE_TILE
    row = lambda i: (i, 0)
    const = lambda i: (0, 0)
    return pl.pallas_call(
        _merge_kernel,
        grid=(n // tm,),
        in_specs=[
            pl.BlockSpec((tm, D_MODEL), row),
            pl.BlockSpec((tm, WIDTH), row),
            pl.BlockSpec((tm, WIDTH), row),
            pl.BlockSpec((tm, 2 * D_MODEL), row),
            pl.BlockSpec((WIDTH, D_MODEL), const),
            pl.BlockSpec((WIDTH, D_MODEL), const),
            pl.BlockSpec((D_MODEL, D_MODEL), const),
            pl.BlockSpec((1, D_MODEL), const),
            pl.BlockSpec((D_MODEL, 2 * ROUTER_LANES), const),
            pl.BlockSpec((1, ROUTER_LANES), const),
        ],
        out_specs=[
            pl.BlockSpec((tm, D_MODEL), row),
            pl.BlockSpec((tm, PACKED), row),
            pl.BlockSpec((tm, ROUTER_LANES), row),
            pl.BlockSpec((1, 8, ROUTER_LANES), lambda i: (i, 0, 0)),
        ],
        out_shape=[
            jax.ShapeDtypeStruct((n, D_MODEL), F32),
            jax.ShapeDtypeStruct((n, PACKED), jnp.uint32),
            jax.ShapeDtypeStruct((n, ROUTER_LANES), F32),
            jax.ShapeDtypeStruct((n // tm, 8, ROUTER_LANES), jnp.int32),
        ],
        scratch_shapes=[pltpu.VMEM((WIDTH, D_MODEL), BF16), pltpu.VMEM((WIDTH, D_MODEL), BF16),
                        pltpu.VMEM((D_MODEL, D_MODEL), BF16)],
        compiler_params=pltpu.CompilerParams(
            dimension_semantics=("arbitrary",), vmem_limit_bytes=VMEM_LIMIT),
        name="merge",
    )(x2, att, rwkv, gates, wa, wb, wo, ln_g, wr, br)


def _moe_plan_kernel(cnt_ref, base_ref, blk_ref):
    nt = cnt_ref.shape[0]
    cnt = cnt_ref[...].astype(F32)
    lane = lax.broadcasted_iota(jnp.int32, (8, ROUTER_LANES), 1)
    total = jnp.broadcast_to(jnp.sum(cnt, axis=0, keepdims=True), (8, ROUTER_LANES))
    padded = jnp.floor((total + (MOE_BLOCK - 1)) * (1.0 / MOE_BLOCK)) * MOE_BLOCK
    r = lax.broadcasted_iota(jnp.int32, (ROUTER_LANES, ROUTER_LANES), 0)
    c = lax.broadcasted_iota(jnp.int32, (ROUTER_LANES, ROUTER_LANES), 1)
    seg_start = _dot_hi(padded, jnp.where(r < c, 1.0, 0.0).astype(F32))
    tr = lax.broadcasted_iota(jnp.int32, (nt, nt), 0)
    tc = lax.broadcasted_iota(jnp.int32, (nt, nt), 1)
    tile_off = _dot_hi(jnp.where(tc < tr, 1.0, 0.0).astype(F32), cnt)
    base_ref[...] = seg_start[0:1, :] + tile_off
    seg_end = (seg_start + padded)[0:1, :]
    rows_total = jnp.sum(jnp.where(lane[0:1, :] < N_EXPERTS, padded[0:1, :], 0.0), axis=-1, keepdims=True)
    nblk = blk_ref.shape[0]
    blk_row = lax.broadcasted_iota(jnp.int32, (nblk, ROUTER_LANES), 0).astype(F32) * MOE_BLOCK
    blk_lane = lax.broadcasted_iota(jnp.int32, (nblk, ROUTER_LANES), 1)
    done = jnp.where((seg_end <= blk_row) & (blk_lane < N_EXPERTS), 1.0, 0.0)
    expert = jnp.minimum(jnp.sum(done, axis=-1, keepdims=True), N_EXPERTS - 1.0)
    valid = jnp.where(blk_row < rows_total, 1.0, 0.0)
    blk_idx = lax.broadcasted_iota(jnp.int32, (nblk, ROUTER_LANES), 0).astype(F32)
    source = jnp.minimum(blk_idx, rows_total * (1.0 / MOE_BLOCK) - 1.0)
    has_rows = jnp.where((padded > 0.0) & (lane < N_EXPERTS), 1.0, 0.0)
    before = _dot_hi(has_rows, jnp.where(r < c, 1.0, 0.0).astype(F32))[0:1, :]
    lane_f = blk_lane.astype(F32)
    ordinal = jnp.sum(jnp.where(lane_f == expert, before, 0.0), axis=-1, keepdims=True)
    later = (lane_f > expert) & (has_rows[0:1, :] > 0.0)
    nxt = jnp.min(jnp.where(later, lane_f, float(N_EXPERTS)), axis=-1, keepdims=True)
    table = jnp.where(blk_lane == 0, expert, jnp.where(blk_lane == 1, valid, jnp.where(blk_lane == 2, source,
                      jnp.where(blk_lane == 3, nxt, jnp.where(blk_lane == 4, ordinal, 0.0)))))
    blk_ref[...] = table.astype(jnp.int32)


def _moe_plan(cnt, n_blocks):
    nt = cnt.shape[0]
    return pl.pallas_call(
        _moe_plan_kernel,
        out_shape=[jax.ShapeDtypeStruct((nt, ROUTER_LANES), F32),
                   jax.ShapeDtypeStruct((n_blocks, ROUTER_LANES), jnp.int32)],
        name="moe_plan",
    )(cnt)


def _moe_pos_kernel(comb_ref, base_ref, pos_ref):
    t = comb_ref.shape[0]
    comb = comb_ref[...]
    lane_i = lax.broadcasted_iota(jnp.int32, (t, ROUTER_LANES), 1)
    lane = lane_i.astype(F32)
    pick1 = jnp.where(lane == comb[:, EXPERT1_LANE:EXPERT1_LANE + 1], 1.0, 0.0)
    pick2 = jnp.where(lane == comb[:, EXPERT2_LANE:EXPERT2_LANE + 1], 1.0, 0.0)
    rows = lax.broadcasted_iota(jnp.int32, (t, t), 0)
    cols = lax.broadcasted_iota(jnp.int32, (t, t), 1)
    earlier = jnp.where(cols < rows, 1.0, 0.0).astype(BF16)
    before1 = _dot(earlier, pick1.astype(BF16))
    before2 = _dot(earlier, pick2.astype(BF16))
    base = base_ref[0]
    firsts = jnp.sum(pick1, axis=0, keepdims=True)
    pos1 = jnp.sum(pick1 * (base + before1), axis=-1, keepdims=True)
    pos2 = jnp.sum(pick2 * (base + firsts + before2), axis=-1, keepdims=True)
    both = jnp.where(lane_i == 0, pos1, jnp.where(lane_i == 1, pos2, 0.0))
    pos_ref[...] = jnp.transpose(both)[0:8, :].astype(jnp.int32)


def _moe_pos(comb, base):
    n = comb.shape[0]
    t = MERGE_TILE
    return pl.pallas_call(
        _moe_pos_kernel,
        grid=(n // t,),
        in_specs=[pl.BlockSpec((t, ROUTER_LANES), lambda i: (i, 0)),
                  pl.BlockSpec((1, 1, ROUTER_LANES), lambda i: (i, 0, 0))],
        out_specs=pl.BlockSpec((8, t), lambda i: (0, i)),
        out_shape=jax.ShapeDtypeStruct((8, n), jnp.int32),
        compiler_params=pltpu.CompilerParams(dimension_semantics=("arbitrary",)),
        name="moe_pos",
    )(comb, base.reshape(n // t, 1, ROUTER_LANES))


def _sc_mesh():
    return plsc.VectorSubcoreMesh(core_axis_name="core", subcore_axis_name="subcore")


def _sc_scratch():
    return [pltpu.VMEM((8, SC_WINDOW), jnp.int32), pltpu.VMEM((SC_WINDOW, PACKED), jnp.uint32)]


def _sc_scatter_rows(h, pos, n_rows):
    per_worker = h.shape[0] // (SC_WINDOW * SC_WORKERS)

    @pl.kernel(out_type=jax.ShapeDtypeStruct((n_rows, PACKED), jnp.uint32), mesh=_sc_mesh(),
               scratch_types=_sc_scratch())
    def scatter(h_hbm, pos_hbm, out_hbm, idx, buf):
        worker = lax.axis_index("core") * SC_SUBCORES + lax.axis_index("subcore")

        @pl.loop(0, per_worker)
        def _(b):
            start = (worker * per_worker + b) * SC_WINDOW
            pltpu.sync_copy(pos_hbm.at[:, pl.ds(start, SC_WINDOW)], idx)
            pltpu.sync_copy(h_hbm.at[pl.ds(start, SC_WINDOW)], buf)
            pltpu.sync_copy(buf, out_hbm.at[idx.at[0]])
            pltpu.sync_copy(buf, out_hbm.at[idx.at[1]])

    return scatter(h, pos)


def _sc_gather_rows(y, pos):
    n = pos.shape[1]
    per_worker = n // (SC_WINDOW * SC_WORKERS)
    out = jax.ShapeDtypeStruct((n, PACKED), jnp.uint32)

    @pl.kernel(out_type=(out, out), mesh=_sc_mesh(), scratch_types=_sc_scratch())
    def gather(y_hbm, pos_hbm, o1_hbm, o2_hbm, idx, buf):
        worker = lax.axis_index("core") * SC_SUBCORES + lax.axis_index("subcore")

        @pl.loop(0, per_worker)
        def _(b):
            start = (worker * per_worker + b) * SC_WINDOW
            pltpu.sync_copy(pos_hbm.at[:, pl.ds(start, SC_WINDOW)], idx)
            pltpu.sync_copy(y_hbm.at[idx.at[0]], buf)
            pltpu.sync_copy(buf, o1_hbm.at[pl.ds(start, SC_WINDOW)])
            pltpu.sync_copy(y_hbm.at[idx.at[1]], buf)
            pltpu.sync_copy(buf, o2_hbm.at[pl.ds(start, SC_WINDOW)])

    return gather(y, pos)


def _expert_mlp_kernel(expert_ref, valid_ref, source_ref, next_ref, ordinal_ref, hs_ref,
                       wg_hbm, wu_hbm, wd_hbm, y_ref, wg32_ref, wu32_ref, wd32_ref,
                       wg_ref, wu_ref, wd_ref, sem):
    b = pl.program_id(0)

    def weight_copies(e, slot):
        return [pltpu.make_async_copy(src.at[e], dst.at[slot], sem.at[slot])
                for src, dst in ((wg_hbm, wg32_ref), (wu_hbm, wu32_ref), (wd_hbm, wd32_ref))]

    @pl.when(valid_ref[b] != 0)
    def _():
        @pl.when((b == 0) | (expert_ref[b] != expert_ref[jnp.maximum(b - 1, 0)]))
        def _():
            slot = ordinal_ref[b] % 2

            @pl.when(b == 0)
            def _():
                for cp in weight_copies(expert_ref[0], slot):
                    cp.start()

            for cp in weight_copies(expert_ref[b], slot):
                cp.wait()

            @pl.when(next_ref[b] < N_EXPERTS)
            def _():
                for cp in weight_copies(next_ref[b], 1 - slot):
                    cp.start()

            wg_ref[...] = wg32_ref[slot].astype(BF16)
            wu_ref[...] = wu32_ref[slot].astype(BF16)
            wd_ref[...] = wd32_ref[slot].astype(BF16)

        half = D_MODEL // 2
        h_lo, h_hi = _unpack_bf16(hs_ref[...])
        h_lo = h_lo.astype(BF16)
        h_hi = h_hi.astype(BF16)
        hg = _dot(h_lo, wg_ref[0:half, :]) + _dot(h_hi, wg_ref[half:D_MODEL, :])
        hu = _dot(h_lo, wu_ref[0:half, :]) + _dot(h_hi, wu_ref[half:D_MODEL, :])
        act = (hg * _sigmoid(hg) * hu).astype(BF16)
        y_ref[...] = _pack_bf16(_dot(act, wd_ref[...]))


def _expert_mlp(blk, hs, wg, wu, wd):
    rows = lambda b, expert, valid, source, nxt, ordinal: (source[b], 0)
    grid_spec = pltpu.PrefetchScalarGridSpec(
        num_scalar_prefetch=5,
        grid=(hs.shape[0] // MOE_BLOCK,),
        in_specs=[
            pl.BlockSpec((MOE_BLOCK, PACKED), rows),
            pl.BlockSpec(memory_space=pl.ANY),
            pl.BlockSpec(memory_space=pl.ANY),
            pl.BlockSpec(memory_space=pl.ANY),
        ],
        out_specs=pl.BlockSpec((MOE_BLOCK, PACKED), rows),
        scratch_shapes=[pltpu.VMEM((2, D_MODEL, D_EXPERT), F32), pltpu.VMEM((2, D_MODEL, D_EXPERT), F32),
                        pltpu.VMEM((2, D_EXPERT, D_MODEL), F32),
                        pltpu.VMEM((D_MODEL, D_EXPERT), BF16), pltpu.VMEM((D_MODEL, D_EXPERT), BF16),
                        pltpu.VMEM((D_EXPERT, D_MODEL), BF16),
                        pltpu.SemaphoreType.DMA((2,))],
    )
    return pl.pallas_call(
        _expert_mlp_kernel,
        grid_spec=grid_spec,
        out_shape=jax.ShapeDtypeStruct((hs.shape[0], PACKED), jnp.uint32),
        compiler_params=pltpu.CompilerParams(
            dimension_semantics=("arbitrary",), vmem_limit_bytes=VMEM_LIMIT),
        name="expert_mlp",
    )(blk[:, 0], blk[:, 1], blk[:, 2], blk[:, 3], blk[:, 4], hs, wg, wu, wd)


def _moe_out_kernel(x1_ref, y1_ref, y2_ref, comb_ref, lnf_ref, out_ref, *, final_norm):
    w1 = comb_ref[:, WEIGHT1_LANE:WEIGHT1_LANE + 1]
    w2 = comb_ref[:, WEIGHT2_LANE:WEIGHT2_LANE + 1]
    a_lo, a_hi = _unpack_bf16(y1_ref[...])
    b_lo, b_hi = _unpack_bf16(y2_ref[...])
    moe = jnp.concatenate([w1 * a_lo + w2 * b_lo, w1 * a_hi + w2 * b_hi], axis=1)
    y = x1_ref[...] + moe
    if final_norm:
        y = y * lax.rsqrt(jnp.mean(y * y, axis=-1, keepdims=True) + RMS_EPS) * lnf_ref[...]
    out_ref[...] = y


def _moe_out(x1, y1, y2, comb, lnf, final_norm):
    n = x1.shape[0]
    t = FINAL_TILE
    row = lambda i: (i, 0)
    return pl.pallas_call(
        functools.partial(_moe_out_kernel, final_norm=final_norm),
        grid=(n // t,),
        in_specs=[pl.BlockSpec((t, D_MODEL), row), pl.BlockSpec((t, PACKED), row),
                  pl.BlockSpec((t, PACKED), row), pl.BlockSpec((t, ROUTER_LANES), row),
                  pl.BlockSpec((1, D_MODEL), lambda i: (0, 0))],
        out_specs=pl.BlockSpec((t, D_MODEL), row),
        out_shape=jax.ShapeDtypeStruct((n, D_MODEL), F32),
        compiler_params=pltpu.CompilerParams(
            dimension_semantics=("arbitrary",), vmem_limit_bytes=VMEM_LIMIT),
        name="moe_out",
    )(x1, y1, y2, comb, lnf)


def _moe(counts, x1, h2p, comb, wg, wu, wd, lnf, final_norm):
    n = x1.shape[0]
    n_blocks = (2 * n) // MOE_BLOCK + N_EXPERTS
    base, blk = _moe_plan(counts[:, 0, :], n_blocks)
    pos = _moe_pos(comb, base)
    hs = _sc_scatter_rows(h2p, pos, n_blocks * MOE_BLOCK)
    ys = _expert_mlp(blk, hs, wg, wu, wd)
    y1, y2 = _sc_gather_rows(ys, pos)
    return _moe_out(x1, y1, y2, comb, lnf, final_norm)


def kernel(x, ln_mix_g, w_in, att_rel_bias, rwkv_mu, rwkv_w0, rwkv_w2, rwkv_a0, rwkv_a2, rwkv_g2,
           rwkv_k_k, rwkv_k_a, rwkv_r_k, rwkv_gn_g, rwkv_gn_b, w_branch_att, w_branch_rwkv, w_out,
           ln_ffn_g, router_group_w, router_group_b, router_expert_w, router_expert_b,
           expert_w_gate, expert_w_up, expert_w_down, ln_final_g):
    bsz, seq, d = x.shape
    depth = w_in.shape[0]
    n = bsz * seq
    x2 = x.reshape(n, d)
    for l in range(depth):
        q, k, v, rw, gates = _in_proj(x2, ln_mix_g[l][None, :], w_in[l].astype(BF16),
                                      rwkv_mu[l][None, :], seq)
        bias = _rel_bias(att_rel_bias[l])
        att = _band_attn(q.reshape(bsz, seq, WIDTH), k.reshape(bsz, seq, WIDTH),
                         v.reshape(bsz, seq, WIDTH), bias)
        zeros = jnp.zeros((DECAY_LORA, WIDTH), F32)
        w2a2 = jnp.concatenate(
            [jnp.concatenate([rwkv_w2[l], zeros], axis=1),
             jnp.concatenate([zeros, rwkv_a2[l]], axis=1)], axis=0)
        rwkv = _rwkv(rw.reshape(bsz, seq, RWKV_PROJ), w2a2.astype(BF16), rwkv_g2[l].astype(BF16),
                     rwkv_w0[l][None, :], rwkv_a0[l][None, :], rwkv_k_k[l][None, :],
                     rwkv_k_a[l][None, :], rwkv_r_k[l].reshape(1, WIDTH),
                     rwkv_gn_g[l][None, :], rwkv_gn_b[l][None, :])
        wr = jnp.concatenate([router_expert_w[l], router_group_w[l]], axis=1)
        wr = jnp.pad(wr, ((0, 0), (0, ROUTER_LANES - wr.shape[1])))
        wr_hi = wr.astype(BF16)
        wr = jnp.concatenate([wr_hi, (wr - wr_hi.astype(F32)).astype(BF16)], axis=1)
        br = jnp.concatenate([router_expert_b[l], router_group_b[l]])
        br = jnp.pad(br, (0, ROUTER_LANES - br.shape[0]))[None, :]
        x1, h2, comb, counts = _merge(x2, att.reshape(n, WIDTH), rwkv.reshape(n, WIDTH), gates,
                                      w_branch_att[l], w_branch_rwkv[l], w_out[l],
                                      ln_ffn_g[l][None, :], wr, br)
        x2 = _moe(counts, x1, h2, comb, expert_w_gate[l], expert_w_up[l], expert_w_down[l],
                  ln_final_g[None, :], final_norm=(l == depth - 1))
    return x2.reshape(bsz, seq, d)
```

```python
import functools
import math

import jax
import jax.numpy as jnp
from jax import lax
from jax.experimental import pallas as pl
from jax.experimental.pallas import tpu as pltpu
from jax.experimental.pallas import tpu_sc as plsc

F32 = jnp.float32
BF16 = jnp.bfloat16
HIGHEST = lax.Precision.HIGHEST

D_MODEL = 1024
CHUNK = 64
HEADS = 8
HEAD_DIM = 64
WIDTH = HEADS * HEAD_DIM
LEFT_CHUNKS = 8
BAND = (LEFT_CHUNKS + 1) * CHUNK
REL_CLIP = 64
N_REL = 2 * REL_CLIP + 1
DECAY_LORA = 64
AAA_LORA = 64
GATE_LORA = 128
GN_EPS = 64e-5
RMS_EPS = 1e-6
ATT_PROJ = 3 * WIDTH
RWKV_PROJ = 3 * WIDTH + DECAY_LORA + AAA_LORA + GATE_LORA
D_IN = ATT_PROJ + RWKV_PROJ + 2 * D_MODEL
N_GROUPS = 4
EXPERTS_PER_GROUP = 8
N_EXPERTS = N_GROUPS * EXPERTS_PER_GROUP
D_EXPERT = 256
RWKV_SEQS = 4
RWKV_CHUNKS = 4
BIAS_KEYS = 192
ATT_CHUNKS = 8
ATT_GROUP = 4
ROUTER_LANES = 128
GROUP_LANE0 = N_EXPERTS
EXPERT1_LANE, EXPERT2_LANE, WEIGHT1_LANE, WEIGHT2_LANE = 126, 125, 124, 123
MERGE_TILE = 512
MOE_BLOCK = 1024
PACKED = D_MODEL // 2
SC_WINDOW = 128
SC_CORES, SC_SUBCORES = 2, 16
SC_WORKERS = SC_CORES * SC_SUBCORES
FINAL_TILE = 1024
ROW_RING = 3

V7X_VMEM_BYTES = 64 * 1024 * 1024
VMEM_LIMIT = V7X_VMEM_BYTES - 12 * 1024 * 1024


def _dot(a, b):
    return jnp.dot(a, b, preferred_element_type=F32)


def _dot_hi(a, b):
    return jnp.dot(a, b, preferred_element_type=F32, precision=HIGHEST)


def _dot_nt(a, b, precision=None):
    return lax.dot_general(a, b, (((1,), (1,)), ((), ())),
                           preferred_element_type=F32, precision=precision)


def _dot_tn(a, b, precision=None):
    return lax.dot_general(a, b, (((0,), (0,)), ((), ())),
                           preferred_element_type=F32, precision=precision)


def _sigmoid(x):
    return 1.0 / (1.0 + jnp.exp(-x))


def _pack_bf16(x):
    w = x.shape[1] // 2
    hi = pltpu.bitcast(x[:, :w].astype(BF16).astype(F32), jnp.uint32)
    lo = pltpu.bitcast(x[:, w:].astype(BF16).astype(F32), jnp.uint32)
    return hi | lax.shift_right_logical(lo, jnp.uint32(16))


def _unpack_bf16(p):
    hi = pltpu.bitcast(p & jnp.uint32(0xFFFF0000), F32)
    lo = pltpu.bitcast(lax.shift_left(p, jnp.uint32(16)), F32)
    return hi, lo


def _mm(a, b):
    return jnp.dot(a.astype(BF16), b.astype(BF16), preferred_element_type=F32)


def _head_sums(x):
    outs = []
    lane = lax.broadcasted_iota(jnp.int32, (x.shape[0], 2 * HEAD_DIM), 1)
    low = lane < HEAD_DIM
    for p in range(HEADS // 2):
        xp = x[:, 2 * HEAD_DIM * p:2 * HEAD_DIM * (p + 1)]
        s_lo = jnp.sum(jnp.where(low, xp, 0.0), axis=-1, keepdims=True)
        s_hi = jnp.sum(jnp.where(low, 0.0, xp), axis=-1, keepdims=True)
        outs.append(jnp.where(low, s_lo, s_hi))
    return jnp.concatenate(outs, axis=-1)


def _rel_bias_kernel(tab_ref, out_ref):
    rows = tab_ref.shape[1]
    n = lax.broadcasted_iota(jnp.int32, (rows, CHUNK * 128), 1)
    r = lax.broadcasted_iota(jnp.int32, (rows, CHUNK * 128), 0)
    q = n >> 7
    kk = n & 127
    idx = jnp.clip(CHUNK + q - kk, -REL_CLIP, REL_CLIP) + REL_CLIP
    pick = jnp.where(r == idx, 1.0, 0.0) - jnp.where(r == N_REL - 1, 1.0, 0.0)
    out_ref[...] = _dot_hi(tab_ref[...], pick.astype(F32))


def _rel_bias(rel_table):
    rows = 136
    tab = jnp.pad(rel_table.astype(F32), ((0, 0), (0, rows - N_REL)))
    tail = pl.pallas_call(
        _rel_bias_kernel,
        out_shape=jax.ShapeDtypeStruct((HEADS, CHUNK * 128), F32),
        name="rel_bias",
    )(tab)
    tail = tail.reshape(HEADS, CHUNK, 128)
    bias = jnp.concatenate([jnp.zeros((HEADS, CHUNK, BIAS_KEYS - 128), F32), tail], axis=-1)
    return bias.reshape(HEADS * CHUNK, BIAS_KEYS)


def _in_proj_kernel(x_ref, g_ref, w_ref, mu_ref, q_ref, k_ref, v_ref, rw_ref, gate_ref,
                    carry_ref, *, tiles_per_seq):
    i = pl.program_id(0)

    @pl.when(i == 0)
    def _():
        carry_ref[...] = jnp.zeros(carry_ref.shape, F32)

    x = x_ref[...]
    h = x * lax.rsqrt(jnp.mean(x * x, axis=-1, keepdims=True) + RMS_EPS) * g_ref[...]
    hb = h.astype(BF16)
    q_ref[...] = _dot(hb, w_ref[:, 0:WIDTH]).astype(BF16)
    k_ref[...] = _dot(hb, w_ref[:, WIDTH:2 * WIDTH]).astype(BF16)
    v_ref[...] = _dot(hb, w_ref[:, 2 * WIDTH:ATT_PROJ]).astype(BF16)
    rw = _dot(hb, w_ref[:, ATT_PROJ:ATT_PROJ + RWKV_PROJ])
    tm = rw.shape[0]
    first_prev = jnp.where(i % tiles_per_seq == 0, 0.0, carry_ref[0:1, :])
    rolled = pltpu.roll(rw, 1, axis=0)
    row = lax.broadcasted_iota(jnp.int32, rw.shape, 0)
    prev = jnp.where(row == 0, first_prev, rolled)
    carry_ref[0:1, :] = rw[tm - 1:tm, :]
    rw_ref[...] = rw + (prev - rw) * mu_ref[...]
    gate_ref[...] = _sigmoid(_dot(hb, w_ref[:, ATT_PROJ + RWKV_PROJ:D_IN])).astype(BF16)


def _in_proj(x2, ln_g, w_in_b, mu, seq):
    n = x2.shape[0]
    tm = 1024
    assert seq % tm == 0, "a projection tile must not straddle two sequences (token shift carry)"
    row = lambda i: (i, 0)
    const = lambda i: (0, 0)
    return pl.pallas_call(
        functools.partial(_in_proj_kernel, tiles_per_seq=seq // tm),
        grid=(n // tm,),
        in_specs=[
            pl.BlockSpec((tm, D_MODEL), row),
            pl.BlockSpec((1, D_MODEL), const),
            pl.BlockSpec((D_MODEL, D_IN), const, pipeline_mode=pl.Buffered(1)),
            pl.BlockSpec((1, RWKV_PROJ), const),
        ],
        out_specs=[
            pl.BlockSpec((tm, WIDTH), row),
            pl.BlockSpec((tm, WIDTH), row),
            pl.BlockSpec((tm, WIDTH), row),
            pl.BlockSpec((tm, RWKV_PROJ), row),
            pl.BlockSpec((tm, 2 * D_MODEL), row),
        ],
        out_shape=[
            jax.ShapeDtypeStruct((n, WIDTH), BF16),
            jax.ShapeDtypeStruct((n, WIDTH), BF16),
            jax.ShapeDtypeStruct((n, WIDTH), BF16),
            jax.ShapeDtypeStruct((n, RWKV_PROJ), F32),
            jax.ShapeDtypeStruct((n, 2 * D_MODEL), BF16),
        ],
        scratch_shapes=[pltpu.VMEM((8, RWKV_PROJ), F32)],
        compiler_params=pltpu.CompilerParams(
            dimension_semantics=("arbitrary",), vmem_limit_bytes=VMEM_LIMIT),
        name="in_proj",
    )(x2, ln_g, w_in_b, mu)


def _band_attn_kernel(q_ref, k_ref, v_ref, bias_ref, o_ref, kpad_ref, vpad_ref):
    seq = k_ref.shape[1]
    pad = LEFT_CHUNKS * CHUNK
    kpad_ref[0:pad, :] = jnp.zeros((pad, WIDTH), BF16)
    vpad_ref[0:pad, :] = jnp.zeros((pad, WIDTH), BF16)
    kpad_ref[pad:pad + seq, :] = k_ref[0]
    vpad_ref[pad:pad + seq, :] = v_ref[0]

    gw = ATT_GROUP * HEAD_DIM
    rows = ATT_GROUP * CHUNK
    r_head = lax.broadcasted_iota(jnp.int32, (rows, gw), 0) // CHUNK
    l_head = lax.broadcasted_iota(jnp.int32, (rows, gw), 1) // HEAD_DIM
    own = r_head == l_head
    kpos_lo = lax.broadcasted_iota(jnp.int32, (rows, BAND - BIAS_KEYS), 1)
    kpos_hi = lax.broadcasted_iota(jnp.int32, (rows, BIAS_KEYS), 1) + (BAND - BIAS_KEYS)
    neg = jnp.finfo(F32).min
    groups = range(HEADS // ATT_GROUP)
    lanes = [slice(g * gw, (g + 1) * gw) for g in groups]

    def chunk_pair(i, carry, masked):
        units = [(j, g) for j in range(ATT_CHUNKS) for g in groups]
        ids = range(len(units))
        starts = [pl.multiple_of((i * ATT_CHUNKS + j) * CHUNK, CHUNK) for j in range(ATT_CHUNKS)]
        kb = [kpad_ref[pl.ds(st, BAND), :] for st in starts]
        vb = [vpad_ref[pl.ds(st, BAND), :] for st in starts]
        q = [q_ref[0, pl.ds(st, CHUNK), :] * (HEAD_DIM ** -0.5) for st in starts]
        qrows = [jnp.where(own, jnp.concatenate([q[j][:, lanes[g]]] * ATT_GROUP, axis=0),
                           jnp.zeros((), BF16)) for j, g in units]
        s = [_dot_nt(qrows[u], kb[j][:, lanes[g]]) for u, (j, g) in enumerate(units)]
        s_lo = [s[u][:, 0:BAND - BIAS_KEYS] for u in ids]
        s_hi = [s[u][:, BAND - BIAS_KEYS:BAND] + bias_ref[g * rows:(g + 1) * rows, :]
                for u, (j, g) in enumerate(units)]
        if masked:
            first = [(LEFT_CHUNKS - (i * ATT_CHUNKS + j)) * CHUNK for j in range(ATT_CHUNKS)]
            s_lo = [jnp.where(kpos_lo >= first[j], s_lo[u], neg) for u, (j, g) in enumerate(units)]
            s_hi = [jnp.where(kpos_hi >= first[j], s_hi[u], neg) for u, (j, g) in enumerate(units)]
        m = [jnp.maximum(jnp.max(s_lo[u], axis=-1, keepdims=True),
                         jnp.max(s_hi[u], axis=-1, keepdims=True)) for u in ids]
        p_lo = [jnp.exp(s_lo[u] - m[u]) for u in ids]
        p_hi = [jnp.exp(s_hi[u] - m[u]) for u in ids]
        denom = [jnp.sum(p_lo[u], axis=-1, keepdims=True) + jnp.sum(p_hi[u], axis=-1, keepdims=True)
                 for u in ids]
        o_all = [(_dot(p_lo[u].astype(BF16), vb[j][0:BAND - BIAS_KEYS, lanes[g]])
                  + _dot(p_hi[u].astype(BF16), vb[j][BAND - BIAS_KEYS:BAND, lanes[g]])) / denom[u]
                 for u, (j, g) in enumerate(units)]
        for u, (j, g) in enumerate(units):
            o_own = jnp.where(own, o_all[u], 0.0)
            o = o_own[0:CHUNK]
            for h in range(1, ATT_GROUP):
                o = o + o_own[h * CHUNK:(h + 1) * CHUNK]
            o_ref[0, pl.ds(starts[j], CHUNK), lanes[g]] = o.astype(BF16)
        return carry

    n_trips = seq // (CHUNK * ATT_CHUNKS)
    n_masked = min(LEFT_CHUNKS // ATT_CHUNKS, n_trips)
    lax.fori_loop(0, n_masked, functools.partial(chunk_pair, masked=True), 0)
    lax.fori_loop(n_masked, n_trips, functools.partial(chunk_pair, masked=False), 0)


def _band_attn(q, k, v, bias):
    b, seq, _ = q.shape
    whole = pl.BlockSpec((1, seq, WIDTH), lambda i: (i, 0, 0))
    return pl.pallas_call(
        _band_attn_kernel,
        grid=(b,),
        in_specs=[whole, whole, whole, pl.BlockSpec((HEADS * CHUNK, BIAS_KEYS), lambda i: (0, 0))],
        out_specs=whole,
        out_shape=jax.ShapeDtypeStruct((b, seq, WIDTH), BF16),
        scratch_shapes=[pltpu.VMEM((seq + LEFT_CHUNKS * CHUNK, WIDTH), BF16),
                        pltpu.VMEM((seq + LEFT_CHUNKS * CHUNK, WIDTH), BF16)],
        compiler_params=pltpu.CompilerParams(
            dimension_semantics=("arbitrary",), vmem_limit_bytes=VMEM_LIMIT),
        name="band_attn",
    )(q, k, v, bias)


def _rwkv_kernel(rw_ref, w2a2_ref, g2_ref, w0_ref, a0_ref, kk_ref, ka_ref, rk_ref,
                 gng_ref, gnb_ref, y_ref, state_ref):
    c = pl.program_id(1)
    t = CHUNK
    nb = rw_ref.shape[0]

    @pl.when(c == 0)
    def _():
        state_ref[...] = jnp.zeros(state_ref.shape, F32)

    lane = lax.broadcasted_iota(jnp.int32, (t, 2 * HEAD_DIM), 1)
    low = lane < HEAD_DIM
    r2 = lax.broadcasted_iota(jnp.int32, (2 * t, 2 * HEAD_DIM), 0)
    c2 = lax.broadcasted_iota(jnp.int32, (2 * t, 2 * HEAD_DIM), 1)
    own = (r2 < t) == (c2 < HEAD_DIM)
    strict = (r2 & (t - 1)) > (c2 & (t - 1))
    incl = (r2 & (t - 1)) >= (c2 & (t - 1))
    eye = jnp.where(r2 == c2, 1.0, 0.0).astype(F32)

    def stack2(xp):
        return jnp.concatenate([jnp.where(low, xp, 0.0), jnp.where(low, 0.0, xp)], axis=0)

    def wide(seqs, start):
        rows = len(seqs) * t
        rw = rw_ref[seqs[0]:seqs[-1] + 1, pl.ds(start, t), :].reshape(rows, RWKV_PROJ)
        r = rw[:, 0:WIDTH]
        k = rw[:, WIDTH:2 * WIDTH]
        v = rw[:, 2 * WIDTH:3 * WIDTH]
        lora = rw[:, 3 * WIDTH:3 * WIDTH + DECAY_LORA + AAA_LORA]
        g_lo = rw[:, 3 * WIDTH + DECAY_LORA + AAA_LORA:RWKV_PROJ]
        lane128 = lax.broadcasted_iota(jnp.int32, lora.shape, 1)
        lora = jnp.where(lane128 < DECAY_LORA, jnp.tanh(lora), lora)
        wa = _mm(lora, w2a2_ref[...])
        log_decay = -math.exp(-0.5) * _sigmoid(w0_ref[...] + wa[:, 0:WIDTH])
        lr = _sigmoid(a0_ref[...] + wa[:, WIDTH:2 * WIDTH])
        gate = _mm(_sigmoid(g_lo), g2_ref[...])
        kk_raw = k * kk_ref[...]
        k_mod = k * (1.0 + (lr - 1.0) * ka_ref[...])
        row = lax.broadcasted_iota(jnp.int32, (rows, rows), 0)
        col = lax.broadcasted_iota(jnp.int32, (rows, rows), 1)
        tri = jnp.where((row >= col) & ((row // t) == (col // t)), 1.0, 0.0).astype(BF16)
        ld1 = log_decay.astype(BF16)
        rem = log_decay - ld1.astype(F32)
        ld2 = rem.astype(BF16)
        ld3 = (rem - ld2.astype(F32)).astype(BF16)
        parts = _dot(tri, jnp.concatenate([ld1, ld2, ld3], axis=1))
        logp = parts[:, 0:WIDTH] + parts[:, WIDTH:2 * WIDTH] + parts[:, 2 * WIDTH:3 * WIDTH]
        p_in = jnp.exp(logp)
        p_ex = jnp.exp(logp - log_decay)
        p_inv = jnp.exp(-logp)
        kk = kk_raw / jnp.maximum(jnp.sqrt(_head_sums(kk_raw * kk_raw)), 1e-12)
        return dict(a_hat=-kk * p_ex, r_hat=r * p_in, b_hat=kk * lr * p_inv, k_hat=k_mod * p_inv,
                    v=v, p_in=p_in, gate=gate, bonus=_head_sums(r * k_mod * rk_ref[...]) * v)

    def front(seqs, w):
        chains = [(j, b, p) for j, b in enumerate(seqs) for p in range(HEADS // 2)]
        ids = range(len(chains))
        rs = [slice(j * t, (j + 1) * t) for j, _, _ in chains]
        ls = [slice(2 * HEAD_DIM * p, 2 * HEAD_DIM * (p + 1)) for _, _, p in chains]
        p_end = [w["p_in"][(j + 1) * t - 1:(j + 1) * t, ls[i]] for i, (j, _, _) in enumerate(chains)]
        ar = [jnp.concatenate([stack2(w["a_hat"][rs[i], ls[i]]), stack2(w["r_hat"][rs[i], ls[i]])],
                              axis=0).astype(BF16) for i in ids]
        bk2 = [jnp.concatenate([stack2(w["b_hat"][rs[i], ls[i]]), stack2(w["k_hat"][rs[i], ls[i]])], axis=0)
               for i in ids]
        bk = [bk2[i].astype(BF16) for i in ids]
        btkt = [(bk2[i] * p_end[i]).astype(BF16) for i in ids]
        v2 = [stack2(w["v"][rs[i], ls[i]]).astype(BF16) for i in ids]
        g = [_dot_nt(ar[i], bk[i]) for i in ids]
        st = [state_ref[b, p] for _, b, p in chains]
        ars = [_dot_nt(ar[i], st[i].astype(BF16)) for i in ids]
        l_ab = [jnp.where(strict, g[i][0:2 * t, 0:2 * t], 0.0) for i in ids]
        lm = [jnp.concatenate([jnp.where(strict, g[i][0:2 * t, 2 * t:4 * t], 0.0),
                               jnp.where(incl, g[i][2 * t:4 * t, 2 * t:4 * t], 0.0)], axis=0)
              for i in ids]
        m_rb = [jnp.where(incl, g[i][2 * t:4 * t, 0:2 * t], 0.0).astype(BF16) for i in ids]
        lv = [_mm(lm[i], v2[i]) for i in ids]
        return dict(chains=chains, rs=rs, ls=ls, p_end=p_end, btkt=btkt, v2=v2, st=st, ars=ars,
                    l_ab=l_ab, m_rb=m_rb, lv=lv)

    def inverse(f):
        w_inv = [eye + l for l in f["l_ab"]]
        l_pow = f["l_ab"]
        for _ in range(int(math.log2(t)) - 1):
            l_pow = [_mm(l, l) for l in l_pow]
            w_inv = [w + _mm(w, l) for w, l in zip(w_inv, l_pow)]
        return w_inv

    def back(w, f, w_inv, start):
        chains, rs, ls = f["chains"], f["rs"], f["ls"]
        ids = range(len(chains))
        z = [_mm(w_inv[i], f["ars"][i][0:2 * t] + f["lv"][i][0:2 * t]) for i in ids]
        y = [f["ars"][i][2 * t:4 * t] + f["lv"][i][2 * t:4 * t] + _mm(f["m_rb"][i], z[i]) for i in ids]
        for i, (_, b, p) in enumerate(chains):
            zv = jnp.concatenate([z[i].astype(BF16), f["v2"][i]], axis=0)
            state_ref[b, p] = f["st"][i] * f["p_end"][i] + _dot_tn(zv, f["btkt"][i])
        for i, (_, b, p) in enumerate(chains):
            mean = jnp.sum(y[i], axis=-1, keepdims=True) * (1.0 / HEAD_DIM)
            dev = jnp.where(own, y[i] - mean, 0.0)
            var = jnp.sum(dev * dev, axis=-1, keepdims=True) * (1.0 / HEAD_DIM)
            yn = dev * lax.rsqrt(var + GN_EPS)
            yn = yn[0:t] + yn[t:2 * t]
            out = ((yn * gng_ref[:, ls[i]] + gnb_ref[:, ls[i]] + w["bonus"][rs[i], ls[i]])
                   * w["gate"][rs[i], ls[i]])
            y_ref[b, pl.ds(start, t), ls[i]] = out.astype(BF16)

    seqs = list(range(nb))

    def chunk(j, carry):
        start = pl.multiple_of(j * t, t)
        w = wide(seqs, start)
        f = front(seqs, w)
        back(w, f, inverse(f), start)
        return carry

    lax.fori_loop(0, rw_ref.shape[1] // t, chunk, 0)


def _rwkv(rw, w2a2, g2, w0, a0, k_k, k_a, r_k, gn_g, gn_b):
    b, seq, _ = rw.shape
    nc = seq // CHUNK
    nb = RWKV_SEQS
    const = lambda i, c: (0, 0)
    vec = pl.BlockSpec((1, WIDTH), const)
    return pl.pallas_call(
        _rwkv_kernel,
        grid=(b // nb, nc // RWKV_CHUNKS),
        in_specs=[
            pl.BlockSpec((nb, CHUNK * RWKV_CHUNKS, RWKV_PROJ), lambda i, c: (i, c, 0)),
            pl.BlockSpec((DECAY_LORA + AAA_LORA, 2 * WIDTH), const),
            pl.BlockSpec((GATE_LORA, WIDTH), const),
            vec, vec, vec, vec, vec, vec, vec,
        ],
        out_specs=pl.BlockSpec((nb, CHUNK * RWKV_CHUNKS, WIDTH), lambda i, c: (i, c, 0)),
        out_shape=jax.ShapeDtypeStruct((b, seq, WIDTH), BF16),
        scratch_shapes=[pltpu.VMEM((nb, HEADS // 2, 2 * HEAD_DIM, 2 * HEAD_DIM), F32)],
        compiler_params=pltpu.CompilerParams(
            dimension_semantics=("arbitrary", "arbitrary"), vmem_limit_bytes=VMEM_LIMIT),
        name="rwkv7",
    )(rw, w2a2, g2, w0, a0, k_k, k_a, r_k, gn_g, gn_b)


def _merge_kernel(x_ref, att_ref, rwkv_ref, gate_ref, wa32_ref, wb32_ref, wo32_ref, g_ref,
                  wr_ref, br_ref, x1_ref, h2_ref, comb_ref, cnt_ref, wa_ref, wb_ref, wo_ref):
    @pl.when(pl.program_id(0) == 0)
    def _():
        wa_ref[...] = wa32_ref[...].astype(BF16)
        wb_ref[...] = wb32_ref[...].astype(BF16)
        wo_ref[...] = wo32_ref[...].astype(BF16)

    ga = gate_ref[:, 0:D_MODEL].astype(F32)
    gb = gate_ref[:, D_MODEL:2 * D_MODEL].astype(F32)
    merged = ga * _dot(att_ref[...], wa_ref[...]) + gb * _dot(rwkv_ref[...], wb_ref[...])
    x1 = x_ref[...] + _dot(merged.astype(BF16), wo_ref[...])
    x1_ref[...] = x1
    h2 = x1 * lax.rsqrt(jnp.mean(x1 * x1, axis=-1, keepdims=True) + RMS_EPS) * g_ref[...]
    h2_hi = h2.astype(BF16)
    h2_ref[...] = _pack_bf16(h2)
    h2_lo = (h2 - h2_hi.astype(F32)).astype(BF16)
    hw = _dot(h2_hi, wr_ref[...])
    lw = _dot(h2_lo, wr_ref[:, 0:ROUTER_LANES])
    logits = hw[:, 0:ROUTER_LANES] + (hw[:, ROUTER_LANES:2 * ROUTER_LANES] + lw) + br_ref[...]
    rec, counts = _route(logits)
    comb_ref[...] = rec
    cnt_ref[0] = jnp.broadcast_to(counts, (8, ROUTER_LANES)).astype(jnp.int32)


def _route(logits):
    lane_i = lax.broadcasted_iota(jnp.int32, logits.shape, 1)
    lane = lane_i.astype(F32)
    lane_group = (lane_i // EXPERTS_PER_GROUP).astype(F32)
    neg = jnp.finfo(F32).min
    big = float(ROUTER_LANES)

    def first_argmax(vals, mask):
        vm = jnp.where(mask, vals, neg)
        mx = jnp.max(vm, axis=-1, keepdims=True)
        idx = jnp.min(jnp.where(vm == mx, jnp.where(mask, lane, big), big), axis=-1, keepdims=True)
        return mx, idx

    is_group = (lane_i >= GROUP_LANE0) & (lane_i < GROUP_LANE0 + N_GROUPS)
    g_max, g_lane = first_argmax(logits, is_group)
    g_prob = 1.0 / jnp.sum(jnp.where(is_group, jnp.exp(logits - g_max), 0.0),
                           axis=-1, keepdims=True)
    g_idx = g_lane - GROUP_LANE0
    in_group = lane_group == g_idx
    e1, i1 = first_argmax(logits, in_group)
    e2, i2 = first_argmax(logits, in_group & (lane != i1))
    w2 = jnp.exp(e2 - e1)
    p1 = 1.0 / (1.0 + w2)
    p2 = w2 / (1.0 + w2)
    rec = jnp.where(lane_i == EXPERT1_LANE, i1, jnp.where(lane_i == EXPERT2_LANE, i2, 0.0))
    rec = jnp.where(lane_i == WEIGHT1_LANE, p1 * g_prob, jnp.where(lane_i == WEIGHT2_LANE, p2 * g_prob, rec))
    counts = jnp.sum(jnp.where(lane == i1, 1.0, 0.0) + jnp.where(lane == i2, 1.0, 0.0),
                     axis=0, keepdims=True)
    return rec, counts


def _merge(x2, att, rwkv, gates, wa, wb, wo, ln_g, wr, br):
    n = x2.shape[0]
    tm = MERGE_TILE
    row = lambda i: (i, 0)
    const = lambda i: (0, 0)
    return pl.pallas_call(
        _merge_kernel,
        grid=(n // tm,),
        in_specs=[
            pl.BlockSpec((tm, D_MODEL), row),
            pl.BlockSpec((tm, WIDTH), row),
            pl.BlockSpec((tm, WIDTH), row),
            pl.BlockSpec((tm, 2 * D_MODEL), row),
            pl.BlockSpec((WIDTH, D_MODEL), const),
            pl.BlockSpec((WIDTH, D_MODEL), const),
            pl.BlockSpec((D_MODEL, D_MODEL), const),
            pl.BlockSpec((1, D_MODEL), const),
            pl.BlockSpec((D_MODEL, 2 * ROUTER_LANES), const),
            pl.BlockSpec((1, ROUTER_LANES), const),
        ],
        out_specs=[
            pl.BlockSpec((tm, D_MODEL), row),
            pl.BlockSpec((tm, PACKED), row),
            pl.BlockSpec((tm, ROUTER_LANES), row),
            pl.BlockSpec((1, 8, ROUTER_LANES), lambda i: (i, 0, 0)),
        ],
        out_shape=[
            jax.ShapeDtypeStruct((n, D_MODEL), F32),
            jax.ShapeDtypeStruct((n, PACKED), jnp.uint32),
            jax.ShapeDtypeStruct((n, ROUTER_LANES), F32),
            jax.ShapeDtypeStruct((n // tm, 8, ROUTER_LANES), jnp.int32),
        ],
        scratch_shapes=[pltpu.VMEM((WIDTH, D_MODEL), BF16), pltpu.VMEM((WIDTH, D_MODEL), BF16),
                        pltpu.VMEM((D_MODEL, D_MODEL), BF16)],
        compiler_params=pltpu.CompilerParams(
            dimension_semantics=("arbitrary",), vmem_limit_bytes=VMEM_LIMIT),
        name="merge",
    )(x2, att, rwkv, gates, wa, wb, wo, ln_g, wr, br)


def _moe_plan_kernel(cnt_ref, base_ref, blk_ref):
    nt = cnt_ref.shape[0]
    cnt = cnt_ref[...].astype(F32)
    lane = lax.broadcasted_iota(jnp.int32, (8, ROUTER_LANES), 1)
    total = jnp.broadcast_to(jnp.sum(cnt, axis=0, keepdims=True), (8, ROUTER_LANES))
    padded = jnp.floor((total + (MOE_BLOCK - 1)) * (1.0 / MOE_BLOCK)) * MOE_BLOCK
    r = lax.broadcasted_iota(jnp.int32, (ROUTER_LANES, ROUTER_LANES), 0)
    c = lax.broadcasted_iota(jnp.int32, (ROUTER_LANES, ROUTER_LANES), 1)
    seg_start = _dot_hi(padded, jnp.where(r < c, 1.0, 0.0).astype(F32))
    tr = lax.broadcasted_iota(jnp.int32, (nt, nt), 0)
    tc = lax.broadcasted_iota(jnp.int32, (nt, nt), 1)
    tile_off = _dot_hi(jnp.where(tc < tr, 1.0, 0.0).astype(F32), cnt)
    base_ref[...] = seg_start[0:1, :] + tile_off
    seg_end = (seg_start + padded)[0:1, :]
    rows_total = jnp.sum(jnp.where(lane[0:1, :] < N_EXPERTS, padded[0:1, :], 0.0), axis=-1, keepdims=True)
    nblk = blk_ref.shape[0]
    blk_row = lax.broadcasted_iota(jnp.int32, (nblk, ROUTER_LANES), 0).astype(F32) * MOE_BLOCK
    blk_lane = lax.broadcasted_iota(jnp.int32, (nblk, ROUTER_LANES), 1)
    done = jnp.where((seg_end <= blk_row) & (blk_lane < N_EXPERTS), 1.0, 0.0)
    expert = jnp.minimum(jnp.sum(done, axis=-1, keepdims=True), N_EXPERTS - 1.0)
    valid = jnp.where(blk_row < rows_total, 1.0, 0.0)
    blk_idx = lax.broadcasted_iota(jnp.int32, (nblk, ROUTER_LANES), 0).astype(F32)
    source = jnp.minimum(blk_idx, rows_total * (1.0 / MOE_BLOCK) - 1.0)
    has_rows = jnp.where((padded > 0.0) & (lane < N_EXPERTS), 1.0, 0.0)
    before = _dot_hi(has_rows, jnp.where(r < c, 1.0, 0.0).astype(F32))[0:1, :]
    lane_f = blk_lane.astype(F32)
    ordinal = jnp.sum(jnp.where(lane_f == expert, before, 0.0), axis=-1, keepdims=True)
    later = (lane_f > expert) & (has_rows[0:1, :] > 0.0)
    nxt = jnp.min(jnp.where(later, lane_f, float(N_EXPERTS)), axis=-1, keepdims=True)
    table = jnp.where(blk_lane == 0, expert, jnp.where(blk_lane == 1, valid, jnp.where(blk_lane == 2, source,
                      jnp.where(blk_lane == 3, nxt, jnp.where(blk_lane == 4, ordinal, 0.0)))))
    blk_ref[...] = table.astype(jnp.int32)


def _moe_plan(cnt, n_blocks):
    nt = cnt.shape[0]
    return pl.pallas_call(
        _moe_plan_kernel,
        out_shape=[jax.ShapeDtypeStruct((nt, ROUTER_LANES), F32),
                   jax.ShapeDtypeStruct((n_blocks, ROUTER_LANES), jnp.int32)],
        name="moe_plan",
    )(cnt)


def _moe_pos_kernel(comb_ref, base_ref, pos_ref):
    t = comb_ref.shape[0]
    comb = comb_ref[...]
    lane_i = lax.broadcasted_iota(jnp.int32, (t, ROUTER_LANES), 1)
    lane = lane_i.astype(F32)
    pick1 = jnp.where(lane == comb[:, EXPERT1_LANE:EXPERT1_LANE + 1], 1.0, 0.0)
    pick2 = jnp.where(lane == comb[:, EXPERT2_LANE:EXPERT2_LANE + 1], 1.0, 0.0)
    rows = lax.broadcasted_iota(jnp.int32, (t, t), 0)
    cols = lax.broadcasted_iota(jnp.int32, (t, t), 1)
    earlier = jnp.where(cols < rows, 1.0, 0.0).astype(BF16)
    before1 = _dot(earlier, pick1.astype(BF16))
    before2 = _dot(earlier, pick2.astype(BF16))
    base = base_ref[0]
    firsts = jnp.sum(pick1, axis=0, keepdims=True)
    pos1 = jnp.sum(pick1 * (base + before1), axis=-1, keepdims=True)
    pos2 = jnp.sum(pick2 * (base + firsts + before2), axis=-1, keepdims=True)
    both = jnp.where(lane_i == 0, pos1, jnp.where(lane_i == 1, pos2, 0.0))
    pos_ref[...] = jnp.transpose(both)[0:8, :].astype(jnp.int32)


def _moe_pos(comb, base):
    n = comb.shape[0]
    t = MERGE_TILE
    return pl.pallas_call(
        _moe_pos_kernel,
        grid=(n // t,),
        in_specs=[pl.BlockSpec((t, ROUTER_LANES), lambda i: (i, 0)),
                  pl.BlockSpec((1, 1, ROUTER_LANES), lambda i: (i, 0, 0))],
        out_specs=pl.BlockSpec((8, t), lambda i: (0, i)),
        out_shape=jax.ShapeDtypeStruct((8, n), jnp.int32),
        compiler_params=pltpu.CompilerParams(dimension_semantics=("arbitrary",)),
        name="moe_pos",
    )(comb, base.reshape(n // t, 1, ROUTER_LANES))


def _sc_mesh():
    return plsc.VectorSubcoreMesh(core_axis_name="core", subcore_axis_name="subcore")


def _sc_scratch():
    return [pltpu.VMEM((8, SC_WINDOW), jnp.int32), pltpu.VMEM((SC_WINDOW, PACKED), jnp.uint32)]


def _sc_scatter_rows(h, pos, n_rows):
    per_worker = h.shape[0] // (SC_WINDOW * SC_WORKERS)

    @pl.kernel(out_type=jax.ShapeDtypeStruct((n_rows, PACKED), jnp.uint32), mesh=_sc_mesh(),
               scratch_types=_sc_scratch())
    def scatter(h_hbm, pos_hbm, out_hbm, idx, buf):
        worker = lax.axis_index("core") * SC_SUBCORES + lax.axis_index("subcore")

        @pl.loop(0, per_worker)
        def _(b):
            start = (worker * per_worker + b) * SC_WINDOW
            pltpu.sync_copy(pos_hbm.at[:, pl.ds(start, SC_WINDOW)], idx)
            pltpu.sync_copy(h_hbm.at[pl.ds(start, SC_WINDOW)], buf)
            pltpu.sync_copy(buf, out_hbm.at[idx.at[0]])
            pltpu.sync_copy(buf, out_hbm.at[idx.at[1]])

    return scatter(h, pos)


def _sc_gather_rows(y, pos):
    n = pos.shape[1]
    per_worker = n // (SC_WINDOW * SC_WORKERS)
    out = jax.ShapeDtypeStruct((n, PACKED), jnp.uint32)

    @pl.kernel(out_type=(out, out), mesh=_sc_mesh(), scratch_types=_sc_scratch())
    def gather(y_hbm, pos_hbm, o1_hbm, o2_hbm, idx, buf):
        worker = lax.axis_index("core") * SC_SUBCORES + lax.axis_index("subcore")

        @pl.loop(0, per_worker)
        def _(b):
            start = (worker * per_worker + b) * SC_WINDOW
            pltpu.sync_copy(pos_hbm.at[:, pl.ds(start, SC_WINDOW)], idx)
            pltpu.sync_copy(y_hbm.at[idx.at[0]], buf)
            pltpu.sync_copy(buf, o1_hbm.at[pl.ds(start, SC_WINDOW)])
            pltpu.sync_copy(y_hbm.at[idx.at[1]], buf)
            pltpu.sync_copy(buf, o2_hbm.at[pl.ds(start, SC_WINDOW)])

    return gather(y, pos)


def _expert_mlp_kernel(expert_ref, valid_ref, source_ref, next_ref, ordinal_ref, hs_hbm,
                       wg_hbm, wu_hbm, wd_hbm, y_ref, wg32_ref, wu32_ref, wd32_ref,
                       wg_ref, wu_ref, wd_ref, sem, rows_ref, rows_sem):
    b = pl.program_id(0)
    n_blocks = pl.num_programs(0)

    def rows_copy(blk):
        slot = blk % ROW_RING
        return pltpu.make_async_copy(hs_hbm.at[pl.ds(pl.multiple_of(blk * MOE_BLOCK, MOE_BLOCK), MOE_BLOCK)],
                                     rows_ref.at[slot], rows_sem.at[slot])

    def weight_copies(e, slot):
        return [pltpu.make_async_copy(src.at[e], dst.at[slot], sem.at[slot])
                for src, dst in ((wg_hbm, wg32_ref), (wu_hbm, wu32_ref), (wd_hbm, wd32_ref))]

    @pl.when(valid_ref[b] != 0)
    def _():
        @pl.when(b == 0)
        def _():
            rows_copy(0).start()
            for ahead in range(1, ROW_RING - 1):
                @pl.when(valid_ref[jnp.minimum(ahead, n_blocks - 1)] != 0)
                def _():
                    rows_copy(ahead).start()

        far = b + ROW_RING - 1

        @pl.when((far < n_blocks) & (valid_ref[jnp.minimum(far, n_blocks - 1)] != 0))
        def _():
            rows_copy(far).start()

        rows_copy(b).wait()

        @pl.when((b == 0) | (expert_ref[b] != expert_ref[jnp.maximum(b - 1, 0)]))
        def _():
            slot = ordinal_ref[b] % 2

            @pl.when(b == 0)
            def _():
                for cp in weight_copies(expert_ref[0], slot):
                    cp.start()

            for cp in weight_copies(expert_ref[b], slot):
                cp.wait()

            @pl.when(next_ref[b] < N_EXPERTS)
            def _():
                for cp in weight_copies(next_ref[b], 1 - slot):
                    cp.start()

            wg_ref[...] = wg32_ref[slot].astype(BF16)
            wu_ref[...] = wu32_ref[slot].astype(BF16)
            wd_ref[...] = wd32_ref[slot].astype(BF16)

        half = D_MODEL // 2
        h_lo, h_hi = _unpack_bf16(rows_ref[b % ROW_RING])
        h_lo = h_lo.astype(BF16)
        h_hi = h_hi.astype(BF16)
        hg = _dot(h_lo, wg_ref[0:half, :]) + _dot(h_hi, wg_ref[half:D_MODEL, :])
        hu = _dot(h_lo, wu_ref[0:half, :]) + _dot(h_hi, wu_ref[half:D_MODEL, :])
        act = (hg * _sigmoid(hg) * hu).astype(BF16)
        y_ref[...] = _pack_bf16(_dot(act, wd_ref[...]))


def _expert_mlp(blk, hs, wg, wu, wd):
    rows = lambda b, expert, valid, source, nxt, ordinal: (source[b], 0)
    grid_spec = pltpu.PrefetchScalarGridSpec(
        num_scalar_prefetch=5,
        grid=(hs.shape[0] // MOE_BLOCK,),
        in_specs=[pl.BlockSpec(memory_space=pl.ANY)] * 4,
        out_specs=pl.BlockSpec((MOE_BLOCK, PACKED), rows),
        scratch_shapes=[pltpu.VMEM((2, D_MODEL, D_EXPERT), F32), pltpu.VMEM((2, D_MODEL, D_EXPERT), F32),
                        pltpu.VMEM((2, D_EXPERT, D_MODEL), F32),
                        pltpu.VMEM((D_MODEL, D_EXPERT), BF16), pltpu.VMEM((D_MODEL, D_EXPERT), BF16),
                        pltpu.VMEM((D_EXPERT, D_MODEL), BF16),
                        pltpu.SemaphoreType.DMA((2,)),
                        pltpu.VMEM((ROW_RING, MOE_BLOCK, PACKED), jnp.uint32),
                        pltpu.SemaphoreType.DMA((ROW_RING,))],
    )
    return pl.pallas_call(
        _expert_mlp_kernel,
        grid_spec=grid_spec,
        out_shape=jax.ShapeDtypeStruct((hs.shape[0], PACKED), jnp.uint32),
        compiler_params=pltpu.CompilerParams(
            dimension_semantics=("arbitrary",), vmem_limit_bytes=VMEM_LIMIT),
        name="expert_mlp",
    )(blk[:, 0], blk[:, 1], blk[:, 2], blk[:, 3], blk[:, 4], hs, wg, wu, wd)


def _moe_out_kernel(x1_ref, y1_ref, y2_ref, comb_ref, lnf_ref, out_ref, *, final_norm):
    w1 = comb_ref[:, WEIGHT1_LANE:WEIGHT1_LANE + 1]
    w2 = comb_ref[:, WEIGHT2_LANE:WEIGHT2_LANE + 1]
    a_lo, a_hi = _unpack_bf16(y1_ref[...])
    b_lo, b_hi = _unpack_bf16(y2_ref[...])
    moe = jnp.concatenate([w1 * a_lo + w2 * b_lo, w1 * a_hi + w2 * b_hi], axis=1)
    y = x1_ref[...] + moe
    if final_norm:
        y = y * lax.rsqrt(jnp.mean(y * y, axis=-1, keepdims=True) + RMS_EPS) * lnf_ref[...]
    out_ref[...] = y


def _moe_out(x1, y1, y2, comb, lnf, final_norm):
    n = x1.shape[0]
    t = FINAL_TILE
    row = lambda i: (i, 0)
    return pl.pallas_call(
        functools.partial(_moe_out_kernel, final_norm=final_norm),
        grid=(n // t,),
        in_specs=[pl.BlockSpec((t, D_MODEL), row), pl.BlockSpec((t, PACKED), row),
                  pl.BlockSpec((t, PACKED), row), pl.BlockSpec((t, ROUTER_LANES), row),
                  pl.BlockSpec((1, D_MODEL), lambda i: (0, 0))],
        out_specs=pl.BlockSpec((t, D_MODEL), row),
        out_shape=jax.ShapeDtypeStruct((n, D_MODEL), F32),
        compiler_params=pltpu.CompilerParams(
            dimension_semantics=("arbitrary",), vmem_limit_bytes=VMEM_LIMIT),
        name="moe_out",
    )(x1, y1, y2, comb, lnf)


def _moe(counts, x1, h2p, comb, wg, wu, wd, lnf, final_norm):
    n = x1.shape[0]
    n_blocks = (2 * n) // MOE_BLOCK + N_EXPERTS
    base, blk = _moe_plan(counts[:, 0, :], n_blocks)
    pos = _moe_pos(comb, base)
    hs = _sc_scatter_rows(h2p, pos, n_blocks * MOE_BLOCK)
    ys = _expert_mlp(blk, hs, wg, wu, wd)
    y1, y2 = _sc_gather_rows(ys, pos)
    return _moe_out(x1, y1, y2, comb, lnf, final_norm)


def kernel(x, ln_mix_g, w_in, att_rel_bias, rwkv_mu, rwkv_w0, rwkv_w2, rwkv_a0, rwkv_a2, rwkv_g2,
           rwkv_k_k, rwkv_k_a, rwkv_r_k, rwkv_gn_g, rwkv_gn_b, w_branch_att, w_branch_rwkv, w_out,
           ln_ffn_g, router_group_w, router_group_b, router_expert_w, router_expert_b,
           expert_w_gate, expert_w_up, expert_w_down, ln_final_g):
    bsz, seq, d = x.shape
    depth = w_in.shape[0]
    n = bsz * seq
    x2 = x.reshape(n, d)
    for l in range(depth):
        q, k, v, rw, gates = _in_proj(x2, ln_mix_g[l][None, :], w_in[l].astype(BF16),
                                      rwkv_mu[l][None, :], seq)
        bias = _rel_bias(att_rel_bias[l])
        att = _band_attn(q.reshape(bsz, seq, WIDTH), k.reshape(bsz, seq, WIDTH),
                         v.reshape(bsz, seq, WIDTH), bias)
        zeros = jnp.zeros((DECAY_LORA, WIDTH), F32)
        w2a2 = jnp.concatenate(
            [jnp.concatenate([rwkv_w2[l], zeros], axis=1),
             jnp.concatenate([zeros, rwkv_a2[l]], axis=1)], axis=0)
        rwkv = _rwkv(rw.reshape(bsz, seq, RWKV_PROJ), w2a2.astype(BF16), rwkv_g2[l].astype(BF16),
                     rwkv_w0[l][None, :], rwkv_a0[l][None, :], rwkv_k_k[l][None, :],
                     rwkv_k_a[l][None, :], rwkv_r_k[l].reshape(1, WIDTH),
                     rwkv_gn_g[l][None, :], rwkv_gn_b[l][None, :])
        wr = jnp.concatenate([router_expert_w[l], router_group_w[l]], axis=1)
        wr = jnp.pad(wr, ((0, 0), (0, ROUTER_LANES - wr.shape[1])))
        wr_hi = wr.astype(BF16)
        wr = jnp.concatenate([wr_hi, (wr - wr_hi.astype(F32)).astype(BF16)], axis=1)
        br = jnp.concatenate([router_expert_b[l], router_group_b[l]])
        br = jnp.pad(br, (0, ROUTER_LANES - br.shape[0]))[None, :]
        x1, h2, comb, counts = _merge(x2, att.reshape(n, WIDTH), rwkv.reshape(n, WIDTH), gates,
                                      w_branch_att[l], w_branch_rwkv[l], w_out[l],
                                      ln_ffn_g[l][None, :], wr, br)
        x2 = _moe(counts, x1, h2, comb, expert_w_gate[l], expert_w_up[l], expert_w_down[l],
                  ln_final_g[None, :], final_norm=(l == depth - 1))
    return x2.reshape(bsz, seq, d)
```

```python
import functools
import math

import jax
import jax.numpy as jnp
from jax import lax
from jax.experimental import pallas as pl
from jax.experimental.pallas import tpu as pltpu
from jax.experimental.pallas import tpu_sc as plsc

F32 = jnp.float32
BF16 = jnp.bfloat16
HIGHEST = lax.Precision.HIGHEST

D_MODEL = 1024
CHUNK = 64
HEADS = 8
HEAD_DIM = 64
WIDTH = HEADS * HEAD_DIM
LEFT_CHUNKS = 8
BAND = (LEFT_CHUNKS + 1) * CHUNK
REL_CLIP = 64
N_REL = 2 * REL_CLIP + 1
DECAY_LORA = 64
AAA_LORA = 64
GATE_LORA = 128
GN_EPS = 64e-5
RMS_EPS = 1e-6
ATT_PROJ = 3 * WIDTH
RWKV_PROJ = 3 * WIDTH + DECAY_LORA + AAA_LORA + GATE_LORA
D_IN = ATT_PROJ + RWKV_PROJ + 2 * D_MODEL
W_IN_CHUNK = 384
N_GROUPS = 4
EXPERTS_PER_GROUP = 8
N_EXPERTS = N_GROUPS * EXPERTS_PER_GROUP
D_EXPERT = 256
RWKV_SEQS = 4
RWKV_CHUNKS = 4
BIAS_KEYS = 192
ATT_CHUNKS = 8
ATT_GROUP = 4
ROUTER_LANES = 128
GROUP_LANE0 = N_EXPERTS
EXPERT1_LANE, EXPERT2_LANE, WEIGHT1_LANE, WEIGHT2_LANE = 126, 125, 124, 123
MERGE_TILE = 512
MOE_BLOCK = 1024
PACKED = D_MODEL // 2
SC_WINDOW = 128
SC_CORES, SC_SUBCORES = 2, 16
SC_WORKERS = SC_CORES * SC_SUBCORES
FINAL_TILE = 1024
ROW_RING = 3

V7X_VMEM_BYTES = 64 * 1024 * 1024
VMEM_LIMIT = V7X_VMEM_BYTES - 12 * 1024 * 1024


def _dot(a, b):
    return jnp.dot(a, b, preferred_element_type=F32)


def _dot_hi(a, b):
    return jnp.dot(a, b, preferred_element_type=F32, precision=HIGHEST)


def _dot_nt(a, b, precision=None):
    return lax.dot_general(a, b, (((1,), (1,)), ((), ())),
                           preferred_element_type=F32, precision=precision)


def _dot_tn(a, b, precision=None):
    return lax.dot_general(a, b, (((0,), (0,)), ((), ())),
                           preferred_element_type=F32, precision=precision)


def _sigmoid(x):
    return 1.0 / (1.0 + jnp.exp(-x))


def _pack_bf16(x):
    w = x.shape[1] // 2
    hi = pltpu.bitcast(x[:, :w].astype(BF16).astype(F32), jnp.uint32)
    lo = pltpu.bitcast(x[:, w:].astype(BF16).astype(F32), jnp.uint32)
    return hi | lax.shift_right_logical(lo, jnp.uint32(16))


def _unpack_bf16(p):
    hi = pltpu.bitcast(p & jnp.uint32(0xFFFF0000), F32)
    lo = pltpu.bitcast(lax.shift_left(p, jnp.uint32(16)), F32)
    return hi, lo


def _mm(a, b):
    return jnp.dot(a.astype(BF16), b.astype(BF16), preferred_element_type=F32)


def _head_sums(x):
    outs = []
    lane = lax.broadcasted_iota(jnp.int32, (x.shape[0], 2 * HEAD_DIM), 1)
    low = lane < HEAD_DIM
    for p in range(HEADS // 2):
        xp = x[:, 2 * HEAD_DIM * p:2 * HEAD_DIM * (p + 1)]
        s_lo = jnp.sum(jnp.where(low, xp, 0.0), axis=-1, keepdims=True)
        s_hi = jnp.sum(jnp.where(low, 0.0, xp), axis=-1, keepdims=True)
        outs.append(jnp.where(low, s_lo, s_hi))
    return jnp.concatenate(outs, axis=-1)


def _rel_bias_kernel(tab_ref, out_ref):
    rows = tab_ref.shape[1]
    n = lax.broadcasted_iota(jnp.int32, (rows, CHUNK * 128), 1)
    r = lax.broadcasted_iota(jnp.int32, (rows, CHUNK * 128), 0)
    q = n >> 7
    kk = n & 127
    idx = jnp.clip(CHUNK + q - kk, -REL_CLIP, REL_CLIP) + REL_CLIP
    pick = jnp.where(r == idx, 1.0, 0.0) - jnp.where(r == N_REL - 1, 1.0, 0.0)
    out_ref[...] = _dot_hi(tab_ref[...], pick.astype(F32))


def _rel_bias(rel_table):
    rows = 136
    tab = jnp.pad(rel_table.astype(F32), ((0, 0), (0, rows - N_REL)))
    tail = pl.pallas_call(
        _rel_bias_kernel,
        out_shape=jax.ShapeDtypeStruct((HEADS, CHUNK * 128), F32),
        name="rel_bias",
    )(tab)
    tail = tail.reshape(HEADS, CHUNK, 128)
    bias = jnp.concatenate([jnp.zeros((HEADS, CHUNK, BIAS_KEYS - 128), F32), tail], axis=-1)
    return bias.reshape(HEADS * CHUNK, BIAS_KEYS)


def _in_proj_kernel(x_ref, g_ref, w_hbm, mu_ref, q_ref, k_ref, v_ref, rw_ref, gate_ref,
                    carry_ref, w_ref, stage_ref, sem, *, tiles_per_seq, layer):
    i = pl.program_id(0)

    @pl.when(i == 0)
    def _():
        carry_ref[...] = jnp.zeros(carry_ref.shape, F32)

        def chunk_copy(c):
            return pltpu.make_async_copy(w_hbm.at[layer, :, pl.ds(c * W_IN_CHUNK, W_IN_CHUNK)],
                                         stage_ref.at[c % 2], sem.at[c % 2])

        n_chunks = D_IN // W_IN_CHUNK
        chunk_copy(0).start()
        for c in range(n_chunks):
            if c + 1 < n_chunks:
                chunk_copy(c + 1).start()
            chunk_copy(c).wait()
            w_ref[:, c * W_IN_CHUNK:(c + 1) * W_IN_CHUNK] = stage_ref[c % 2].astype(BF16)

    x = x_ref[...]
    h = x * lax.rsqrt(jnp.mean(x * x, axis=-1, keepdims=True) + RMS_EPS) * g_ref[...]
    hb = h.astype(BF16)
    q_ref[...] = _dot(hb, w_ref[:, 0:WIDTH]).astype(BF16)
    k_ref[...] = _dot(hb, w_ref[:, WIDTH:2 * WIDTH]).astype(BF16)
    v_ref[...] = _dot(hb, w_ref[:, 2 * WIDTH:ATT_PROJ]).astype(BF16)
    rw = _dot(hb, w_ref[:, ATT_PROJ:ATT_PROJ + RWKV_PROJ])
    tm = rw.shape[0]
    first_prev = jnp.where(i % tiles_per_seq == 0, 0.0, carry_ref[0:1, :])
    rolled = pltpu.roll(rw, 1, axis=0)
    row = lax.broadcasted_iota(jnp.int32, rw.shape, 0)
    prev = jnp.where(row == 0, first_prev, rolled)
    carry_ref[0:1, :] = rw[tm - 1:tm, :]
    rw_ref[...] = rw + (prev - rw) * mu_ref[...]
    gate_ref[...] = _sigmoid(_dot(hb, w_ref[:, ATT_PROJ + RWKV_PROJ:D_IN])).astype(BF16)


def _in_proj(x2, ln_g, w_in, layer, mu, seq):
    n = x2.shape[0]
    tm = 1024
    assert seq % tm == 0, "a projection tile must not straddle two sequences (token shift carry)"
    row = lambda i: (i, 0)
    const = lambda i: (0, 0)
    return pl.pallas_call(
        functools.partial(_in_proj_kernel, tiles_per_seq=seq // tm, layer=layer),
        grid=(n // tm,),
        in_specs=[
            pl.BlockSpec((tm, D_MODEL), row),
            pl.BlockSpec((1, D_MODEL), const),
            pl.BlockSpec(memory_space=pl.ANY),
            pl.BlockSpec((1, RWKV_PROJ), const),
        ],
        out_specs=[
            pl.BlockSpec((tm, WIDTH), row),
            pl.BlockSpec((tm, WIDTH), row),
            pl.BlockSpec((tm, WIDTH), row),
            pl.BlockSpec((tm, RWKV_PROJ), row),
            pl.BlockSpec((tm, 2 * D_MODEL), row),
        ],
        out_shape=[
            jax.ShapeDtypeStruct((n, WIDTH), BF16),
            jax.ShapeDtypeStruct((n, WIDTH), BF16),
            jax.ShapeDtypeStruct((n, WIDTH), BF16),
            jax.ShapeDtypeStruct((n, RWKV_PROJ), F32),
            jax.ShapeDtypeStruct((n, 2 * D_MODEL), BF16),
        ],
        scratch_shapes=[pltpu.VMEM((8, RWKV_PROJ), F32),
                        pltpu.VMEM((D_MODEL, D_IN), BF16),
                        pltpu.VMEM((2, D_MODEL, W_IN_CHUNK), F32),
                        pltpu.SemaphoreType.DMA((2,))],
        compiler_params=pltpu.CompilerParams(
            dimension_semantics=("arbitrary",), vmem_limit_bytes=VMEM_LIMIT),
        name="in_proj",
    )(x2, ln_g, w_in, mu)


def _band_attn_kernel(q_ref, k_ref, v_ref, bias_ref, o_ref, kpad_ref, vpad_ref):
    seq = k_ref.shape[1]
    pad = LEFT_CHUNKS * CHUNK
    kpad_ref[0:pad, :] = jnp.zeros((pad, WIDTH), BF16)
    vpad_ref[0:pad, :] = jnp.zeros((pad, WIDTH), BF16)
    kpad_ref[pad:pad + seq, :] = k_ref[0]
    vpad_ref[pad:pad + seq, :] = v_ref[0]

    gw = ATT_GROUP * HEAD_DIM
    rows = ATT_GROUP * CHUNK
    r_head = lax.broadcasted_iota(jnp.int32, (rows, gw), 0) // CHUNK
    l_head = lax.broadcasted_iota(jnp.int32, (rows, gw), 1) // HEAD_DIM
    own = r_head == l_head
    kpos_lo = lax.broadcasted_iota(jnp.int32, (rows, BAND - BIAS_KEYS), 1)
    kpos_hi = lax.broadcasted_iota(jnp.int32, (rows, BIAS_KEYS), 1) + (BAND - BIAS_KEYS)
    neg = jnp.finfo(F32).min
    groups = range(HEADS // ATT_GROUP)
    lanes = [slice(g * gw, (g + 1) * gw) for g in groups]

    def chunk_pair(i, carry, masked):
        units = [(j, g) for j in range(ATT_CHUNKS) for g in groups]
        ids = range(len(units))
        starts = [pl.multiple_of((i * ATT_CHUNKS + j) * CHUNK, CHUNK) for j in range(ATT_CHUNKS)]
        kb = [kpad_ref[pl.ds(st, BAND), :] for st in starts]
        vb = [vpad_ref[pl.ds(st, BAND), :] for st in starts]
        q = [q_ref[0, pl.ds(st, CHUNK), :] * (HEAD_DIM ** -0.5) for st in starts]
        qrows = [jnp.where(own, jnp.concatenate([q[j][:, lanes[g]]] * ATT_GROUP, axis=0),
                           jnp.zeros((), BF16)) for j, g in units]
        s = [_dot_nt(qrows[u], kb[j][:, lanes[g]]) for u, (j, g) in enumerate(units)]
        s_lo = [s[u][:, 0:BAND - BIAS_KEYS] for u in ids]
        s_hi = [s[u][:, BAND - BIAS_KEYS:BAND] + bias_ref[g * rows:(g + 1) * rows, :]
                for u, (j, g) in enumerate(units)]
        if masked:
            first = [(LEFT_CHUNKS - (i * ATT_CHUNKS + j)) * CHUNK for j in range(ATT_CHUNKS)]
            s_lo = [jnp.where(kpos_lo >= first[j], s_lo[u], neg) for u, (j, g) in enumerate(units)]
            s_hi = [jnp.where(kpos_hi >= first[j], s_hi[u], neg) for u, (j, g) in enumerate(units)]
        m = [jnp.maximum(jnp.max(s_lo[u], axis=-1, keepdims=True),
                         jnp.max(s_hi[u], axis=-1, keepdims=True)) for u in ids]
        p_lo = [jnp.exp(s_lo[u] - m[u]) for u in ids]
        p_hi = [jnp.exp(s_hi[u] - m[u]) for u in ids]
        denom = [jnp.sum(p_lo[u], axis=-1, keepdims=True) + jnp.sum(p_hi[u], axis=-1, keepdims=True)
                 for u in ids]
        o_all = [(_dot(p_lo[u].astype(BF16), vb[j][0:BAND - BIAS_KEYS, lanes[g]])
                  + _dot(p_hi[u].astype(BF16), vb[j][BAND - BIAS_KEYS:BAND, lanes[g]])) / denom[u]
                 for u, (j, g) in enumerate(units)]
        for u, (j, g) in enumerate(units):
            o_own = jnp.where(own, o_all[u], 0.0)
            o = o_own[0:CHUNK]
            for h in range(1, ATT_GROUP):
                o = o + o_own[h * CHUNK:(h + 1) * CHUNK]
            o_ref[0, pl.ds(starts[j], CHUNK), lanes[g]] = o.astype(BF16)
        return carry

    n_trips = seq // (CHUNK * ATT_CHUNKS)
    n_masked = min(LEFT_CHUNKS // ATT_CHUNKS, n_trips)
    lax.fori_loop(0, n_masked, functools.partial(chunk_pair, masked=True), 0)
    lax.fori_loop(n_masked, n_trips, functools.partial(chunk_pair, masked=False), 0)


def _band_attn(q, k, v, bias):
    b, seq, _ = q.shape
    whole = pl.BlockSpec((1, seq, WIDTH), lambda i: (i, 0, 0))
    return pl.pallas_call(
        _band_attn_kernel,
        grid=(b,),
        in_specs=[whole, whole, whole, pl.BlockSpec((HEADS * CHUNK, BIAS_KEYS), lambda i: (0, 0))],
        out_specs=whole,
        out_shape=jax.ShapeDtypeStruct((b, seq, WIDTH), BF16),
        scratch_shapes=[pltpu.VMEM((seq + LEFT_CHUNKS * CHUNK, WIDTH), BF16),
                        pltpu.VMEM((seq + LEFT_CHUNKS * CHUNK, WIDTH), BF16)],
        compiler_params=pltpu.CompilerParams(
            dimension_semantics=("arbitrary",), vmem_limit_bytes=VMEM_LIMIT),
        name="band_attn",
    )(q, k, v, bias)


def _rwkv_kernel(rw_ref, w2a2_ref, g2_ref, w0_ref, a0_ref, kk_ref, ka_ref, rk_ref,
                 gng_ref, gnb_ref, y_ref, state_ref):
    c = pl.program_id(1)
    t = CHUNK
    nb = rw_ref.shape[0]

    @pl.when(c == 0)
    def _():
        state_ref[...] = jnp.zeros(state_ref.shape, F32)

    lane = lax.broadcasted_iota(jnp.int32, (t, 2 * HEAD_DIM), 1)
    low = lane < HEAD_DIM
    r2 = lax.broadcasted_iota(jnp.int32, (2 * t, 2 * HEAD_DIM), 0)
    c2 = lax.broadcasted_iota(jnp.int32, (2 * t, 2 * HEAD_DIM), 1)
    own = (r2 < t) == (c2 < HEAD_DIM)
    strict = (r2 & (t - 1)) > (c2 & (t - 1))
    incl = (r2 & (t - 1)) >= (c2 & (t - 1))
    eye = jnp.where(r2 == c2, 1.0, 0.0).astype(F32)

    def stack2(xp):
        return jnp.concatenate([jnp.where(low, xp, 0.0), jnp.where(low, 0.0, xp)], axis=0)

    def wide(seqs, start):
        rows = len(seqs) * t
        rw = rw_ref[seqs[0]:seqs[-1] + 1, pl.ds(start, t), :].reshape(rows, RWKV_PROJ)
        r = rw[:, 0:WIDTH]
        k = rw[:, WIDTH:2 * WIDTH]
        v = rw[:, 2 * WIDTH:3 * WIDTH]
        lora = rw[:, 3 * WIDTH:3 * WIDTH + DECAY_LORA + AAA_LORA]
        g_lo = rw[:, 3 * WIDTH + DECAY_LORA + AAA_LORA:RWKV_PROJ]
        lane128 = lax.broadcasted_iota(jnp.int32, lora.shape, 1)
        lora = jnp.where(lane128 < DECAY_LORA, jnp.tanh(lora), lora)
        wa = _mm(lora, w2a2_ref[...])
        log_decay = -math.exp(-0.5) * _sigmoid(w0_ref[...] + wa[:, 0:WIDTH])
        lr = _sigmoid(a0_ref[...] + wa[:, WIDTH:2 * WIDTH])
        gate = _mm(_sigmoid(g_lo), g2_ref[...])
        kk_raw = k * kk_ref[...]
        k_mod = k * (1.0 + (lr - 1.0) * ka_ref[...])
        row = lax.broadcasted_iota(jnp.int32, (rows, rows), 0)
        col = lax.broadcasted_iota(jnp.int32, (rows, rows), 1)
        tri = jnp.where((row >= col) & ((row // t) == (col // t)), 1.0, 0.0).astype(BF16)
        ld1 = log_decay.astype(BF16)
        rem = log_decay - ld1.astype(F32)
        ld2 = rem.astype(BF16)
        ld3 = (rem - ld2.astype(F32)).astype(BF16)
        parts = _dot(tri, jnp.concatenate([ld1, ld2, ld3], axis=1))
        logp = parts[:, 0:WIDTH] + parts[:, WIDTH:2 * WIDTH] + parts[:, 2 * WIDTH:3 * WIDTH]
        p_in = jnp.exp(logp)
        p_ex = jnp.exp(logp - log_decay)
        p_inv = jnp.exp(-logp)
        kk = kk_raw / jnp.maximum(jnp.sqrt(_head_sums(kk_raw * kk_raw)), 1e-12)
        return dict(a_hat=-kk * p_ex, r_hat=r * p_in, b_hat=kk * lr * p_inv, k_hat=k_mod * p_inv,
                    v=v, p_in=p_in, gate=gate, bonus=_head_sums(r * k_mod * rk_ref[...]) * v)

    def front(seqs, w):
        chains = [(j, b, p) for j, b in enumerate(seqs) for p in range(HEADS // 2)]
        ids = range(len(chains))
        rs = [slice(j * t, (j + 1) * t) for j, _, _ in chains]
        ls = [slice(2 * HEAD_DIM * p, 2 * HEAD_DIM * (p + 1)) for _, _, p in chains]
        p_end = [w["p_in"][(j + 1) * t - 1:(j + 1) * t, ls[i]] for i, (j, _, _) in enumerate(chains)]
        ar = [jnp.concatenate([stack2(w["a_hat"][rs[i], ls[i]]), stack2(w["r_hat"][rs[i], ls[i]])],
                              axis=0).astype(BF16) for i in ids]
        bk2 = [jnp.concatenate([stack2(w["b_hat"][rs[i], ls[i]]), stack2(w["k_hat"][rs[i], ls[i]])], axis=0)
               for i in ids]
        bk = [bk2[i].astype(BF16) for i in ids]
        btkt = [(bk2[i] * p_end[i]).astype(BF16) for i in ids]
        v2 = [stack2(w["v"][rs[i], ls[i]]).astype(BF16) for i in ids]
        g = [_dot_nt(ar[i], bk[i]) for i in ids]
        st = [state_ref[b, p] for _, b, p in chains]
        ars = [_dot_nt(ar[i], st[i].astype(BF16)) for i in ids]
        l_ab = [jnp.where(strict, g[i][0:2 * t, 0:2 * t], 0.0) for i in ids]
        lm = [jnp.concatenate([jnp.where(strict, g[i][0:2 * t, 2 * t:4 * t], 0.0),
                               jnp.where(incl, g[i][2 * t:4 * t, 2 * t:4 * t], 0.0)], axis=0)
              for i in ids]
        m_rb = [jnp.where(incl, g[i][2 * t:4 * t, 0:2 * t], 0.0).astype(BF16) for i in ids]
        lv = [_mm(lm[i], v2[i]) for i in ids]
        return dict(chains=chains, rs=rs, ls=ls, p_end=p_end, btkt=btkt, v2=v2, st=st, ars=ars,
                    l_ab=l_ab, m_rb=m_rb, lv=lv)

    def inverse(f):
        w_inv = [eye + l for l in f["l_ab"]]
        l_pow = f["l_ab"]
        for _ in range(int(math.log2(t)) - 1):
            l_pow = [_mm(l, l) for l in l_pow]
            w_inv = [w + _mm(w, l) for w, l in zip(w_inv, l_pow)]
        return w_inv

    def back(w, f, w_inv, start):
        chains, rs, ls = f["chains"], f["rs"], f["ls"]
        ids = range(len(chains))
        z = [_mm(w_inv[i], f["ars"][i][0:2 * t] + f["lv"][i][0:2 * t]) for i in ids]
        y = [f["ars"][i][2 * t:4 * t] + f["lv"][i][2 * t:4 * t] + _mm(f["m_rb"][i], z[i]) for i in ids]
        for i, (_, b, p) in enumerate(chains):
            zv = jnp.concatenate([z[i].astype(BF16), f["v2"][i]], axis=0)
            state_ref[b, p] = f["st"][i] * f["p_end"][i] + _dot_tn(zv, f["btkt"][i])
        for i, (_, b, p) in enumerate(chains):
            mean = jnp.sum(y[i], axis=-1, keepdims=True) * (1.0 / HEAD_DIM)
            dev = jnp.where(own, y[i] - mean, 0.0)
            var = jnp.sum(dev * dev, axis=-1, keepdims=True) * (1.0 / HEAD_DIM)
            yn = dev * lax.rsqrt(var + GN_EPS)
            yn = yn[0:t] + yn[t:2 * t]
            out = ((yn * gng_ref[:, ls[i]] + gnb_ref[:, ls[i]] + w["bonus"][rs[i], ls[i]])
                   * w["gate"][rs[i], ls[i]])
            y_ref[b, pl.ds(start, t), ls[i]] = out.astype(BF16)

    seqs = list(range(nb))

    def chunk(j, carry):
        start = pl.multiple_of(j * t, t)
        w = wide(seqs, start)
        f = front(seqs, w)
        back(w, f, inverse(f), start)
        return carry

    lax.fori_loop(0, rw_ref.shape[1] // t, chunk, 0)


def _rwkv(rw, w2a2, g2, w0, a0, k_k, k_a, r_k, gn_g, gn_b):
    b, seq, _ = rw.shape
    nc = seq // CHUNK
    nb = RWKV_SEQS
    const = lambda i, c: (0, 0)
    vec = pl.BlockSpec((1, WIDTH), const)
    return pl.pallas_call(
        _rwkv_kernel,
        grid=(b // nb, nc // RWKV_CHUNKS),
        in_specs=[
            pl.BlockSpec((nb, CHUNK * RWKV_CHUNKS, RWKV_PROJ), lambda i, c: (i, c, 0)),
            pl.BlockSpec((DECAY_LORA + AAA_LORA, 2 * WIDTH), const),
            pl.BlockSpec((GATE_LORA, WIDTH), const),
            vec, vec, vec, vec, vec, vec, vec,
        ],
        out_specs=pl.BlockSpec((nb, CHUNK * RWKV_CHUNKS, WIDTH), lambda i, c: (i, c, 0)),
        out_shape=jax.ShapeDtypeStruct((b, seq, WIDTH), BF16),
        scratch_shapes=[pltpu.VMEM((nb, HEADS // 2, 2 * HEAD_DIM, 2 * HEAD_DIM), F32)],
        compiler_params=pltpu.CompilerParams(
            dimension_semantics=("arbitrary", "arbitrary"), vmem_limit_bytes=VMEM_LIMIT),
        name="rwkv7",
    )(rw, w2a2, g2, w0, a0, k_k, k_a, r_k, gn_g, gn_b)


def _merge_kernel(x_ref, att_ref, rwkv_ref, gate_ref, wa32_ref, wb32_ref, wo32_ref, g_ref,
                  wr_ref, br_ref, x1_ref, h2_ref, comb_ref, cnt_ref, wa_ref, wb_ref, wo_ref):
    @pl.when(pl.program_id(0) == 0)
    def _():
        wa_ref[...] = wa32_ref[...].astype(BF16)
        wb_ref[...] = wb32_ref[...].astype(BF16)
        wo_ref[...] = wo32_ref[...].astype(BF16)

    ga = gate_ref[:, 0:D_MODEL].astype(F32)
    gb = gate_ref[:, D_MODEL:2 * D_MODEL].astype(F32)
    merged = ga * _dot(att_ref[...], wa_ref[...]) + gb * _dot(rwkv_ref[...], wb_ref[...])
    x1 = x_ref[...] + _dot(merged.astype(BF16), wo_ref[...])
    x1_ref[...] = x1
    h2 = x1 * lax.rsqrt(jnp.mean(x1 * x1, axis=-1, keepdims=True) + RMS_EPS) * g_ref[...]
    h2_hi = h2.astype(BF16)
    h2_ref[...] = _pack_bf16(h2)
    h2_lo = (h2 - h2_hi.astype(F32)).astype(BF16)
    hw = _dot(h2_hi, wr_ref[...])
    lw = _dot(h2_lo, wr_ref[:, 0:ROUTER_LANES])
    logits = hw[:, 0:ROUTER_LANES] + (hw[:, ROUTER_LANES:2 * ROUTER_LANES] + lw) + br_ref[...]
    rec, counts = _route(logits)
    comb_ref[...] = rec
    cnt_ref[0] = jnp.broadcast_to(counts, (8, ROUTER_LANES)).astype(jnp.int32)


def _route(logits):
    lane_i = lax.broadcasted_iota(jnp.int32, logits.shape, 1)
    lane = lane_i.astype(F32)
    lane_group = (lane_i // EXPERTS_PER_GROUP).astype(F32)
    neg = jnp.finfo(F32).min
    big = float(ROUTER_LANES)

    def first_argmax(vals, mask):
        vm = jnp.where(mask, vals, neg)
        mx = jnp.max(vm, axis=-1, keepdims=True)
        idx = jnp.min(jnp.where(vm == mx, jnp.where(mask, lane, big), big), axis=-1, keepdims=True)
        return mx, idx

    is_group = (lane_i >= GROUP_LANE0) & (lane_i < GROUP_LANE0 + N_GROUPS)
    g_max, g_lane = first_argmax(logits, is_group)
    g_prob = 1.0 / jnp.sum(jnp.where(is_group, jnp.exp(logits - g_max), 0.0),
                           axis=-1, keepdims=True)
    g_idx = g_lane - GROUP_LANE0
    in_group = lane_group == g_idx
    e1, i1 = first_argmax(logits, in_group)
    e2, i2 = first_argmax(logits, in_group & (lane != i1))
    w2 = jnp.exp(e2 - e1)
    p1 = 1.0 / (1.0 + w2)
    p2 = w2 / (1.0 + w2)
    rec = jnp.where(lane_i == EXPERT1_LANE, i1, jnp.where(lane_i == EXPERT2_LANE, i2, 0.0))
    rec = jnp.where(lane_i == WEIGHT1_LANE, p1 * g_prob, jnp.where(lane_i == WEIGHT2_LANE, p2 * g_prob, rec))
    counts = jnp.sum(jnp.where(lane == i1, 1.0, 0.0) + jnp.where(lane == i2, 1.0, 0.0),
                     axis=0, keepdims=True)
    return rec, counts


def _merge(x2, att, rwkv, gates, wa, wb, wo, ln_g, wr, br):
    n = x2.shape[0]
    tm = MERGE_TILE
    row = lambda i: (i, 0)
    const = lambda i: (0, 0)
    return pl.pallas_call(
        _merge_kernel,
        grid=(n // tm,),
        in_specs=[
            pl.BlockSpec((tm, D_MODEL), row),
            pl.BlockSpec((tm, WIDTH), row),
            pl.BlockSpec((tm, WIDTH), row),
            pl.BlockSpec((tm, 2 * D_MODEL), row),
            pl.BlockSpec((WIDTH, D_MODEL), const),
            pl.BlockSpec((WIDTH, D_MODEL), const),
            pl.BlockSpec((D_MODEL, D_MODEL), const),
            pl.BlockSpec((1, D_MODEL), const),
            pl.BlockSpec((D_MODEL, 2 * ROUTER_LANES), const),
            pl.BlockSpec((1, ROUTER_LANES), const),
        ],
        out_specs=[
            pl.BlockSpec((tm, D_MODEL), row),
            pl.BlockSpec((tm, PACKED), row),
            pl.BlockSpec((tm, ROUTER_LANES), row),
            pl.BlockSpec((1, 8, ROUTER_LANES), lambda i: (i, 0, 0)),
        ],
        out_shape=[
            jax.ShapeDtypeStruct((n, D_MODEL), F32),
            jax.ShapeDtypeStruct((n, PACKED), jnp.uint32),
            jax.ShapeDtypeStruct((n, ROUTER_LANES), F32),
            jax.ShapeDtypeStruct((n // tm, 8, ROUTER_LANES), jnp.int32),
        ],
        scratch_shapes=[pltpu.VMEM((WIDTH, D_MODEL), BF16), pltpu.VMEM((WIDTH, D_MODEL), BF16),
                        pltpu.VMEM((D_MODEL, D_MODEL), BF16)],
        compiler_params=pltpu.CompilerParams(
            dimension_semantics=("arbitrary",), vmem_limit_bytes=VMEM_LIMIT),
        name="merge",
    )(x2, att, rwkv, gates, wa, wb, wo, ln_g, wr, br)


def _moe_plan_kernel(cnt_ref, base_ref, blk_ref):
    nt = cnt_ref.shape[0]
    cnt = cnt_ref[...].astype(F32)
    lane = lax.broadcasted_iota(jnp.int32, (8, ROUTER_LANES), 1)
    total = jnp.broadcast_to(jnp.sum(cnt, axis=0, keepdims=True), (8, ROUTER_LANES))
    padded = jnp.floor((total + (MOE_BLOCK - 1)) * (1.0 / MOE_BLOCK)) * MOE_BLOCK
    r = lax.broadcasted_iota(jnp.int32, (ROUTER_LANES, ROUTER_LANES), 0)
    c = lax.broadcasted_iota(jnp.int32, (ROUTER_LANES, ROUTER_LANES), 1)
    seg_start = _dot_hi(padded, jnp.where(r < c, 1.0, 0.0).astype(F32))
    tr = lax.broadcasted_iota(jnp.int32, (nt, nt), 0)
    tc = lax.broadcasted_iota(jnp.int32, (nt, nt), 1)
    tile_off = _dot_hi(jnp.where(tc < tr, 1.0, 0.0).astype(F32), cnt)
    base_ref[...] = seg_start[0:1, :] + tile_off
    seg_end = (seg_start + padded)[0:1, :]
    rows_total = jnp.sum(jnp.where(lane[0:1, :] < N_EXPERTS, padded[0:1, :], 0.0), axis=-1, keepdims=True)
    nblk = blk_ref.shape[0]
    blk_row = lax.broadcasted_iota(jnp.int32, (nblk, ROUTER_LANES), 0).astype(F32) * MOE_BLOCK
    blk_lane = lax.broadcasted_iota(jnp.int32, (nblk, ROUTER_LANES), 1)
    done = jnp.where((seg_end <= blk_row) & (blk_lane < N_EXPERTS), 1.0, 0.0)
    expert = jnp.minimum(jnp.sum(done, axis=-1, keepdims=True), N_EXPERTS - 1.0)
    valid = jnp.where(blk_row < rows_total, 1.0, 0.0)
    blk_idx = lax.broadcasted_iota(jnp.int32, (nblk, ROUTER_LANES), 0).astype(F32)
    source = jnp.minimum(blk_idx, rows_total * (1.0 / MOE_BLOCK) - 1.0)
    has_rows = jnp.where((padded > 0.0) & (lane < N_EXPERTS), 1.0, 0.0)
    before = _dot_hi(has_rows, jnp.where(r < c, 1.0, 0.0).astype(F32))[0:1, :]
    lane_f = blk_lane.astype(F32)
    ordinal = jnp.sum(jnp.where(lane_f == expert, before, 0.0), axis=-1, keepdims=True)
    later = (lane_f > expert) & (has_rows[0:1, :] > 0.0)
    nxt = jnp.min(jnp.where(later, lane_f, float(N_EXPERTS)), axis=-1, keepdims=True)
    table = jnp.where(blk_lane == 0, expert, jnp.where(blk_lane == 1, valid, jnp.where(blk_lane == 2, source,
                      jnp.where(blk_lane == 3, nxt, jnp.where(blk_lane == 4, ordinal, 0.0)))))
    blk_ref[...] = table.astype(jnp.int32)


def _moe_plan(cnt, n_blocks):
    nt = cnt.shape[0]
    return pl.pallas_call(
        _moe_plan_kernel,
        out_shape=[jax.ShapeDtypeStruct((nt, ROUTER_LANES), F32),
                   jax.ShapeDtypeStruct((n_blocks, ROUTER_LANES), jnp.int32)],
        name="moe_plan",
    )(cnt)


def _moe_pos_kernel(comb_ref, base_ref, pos_ref):
    t = comb_ref.shape[0]
    comb = comb_ref[...]
    lane_i = lax.broadcasted_iota(jnp.int32, (t, ROUTER_LANES), 1)
    lane = lane_i.astype(F32)
    pick1 = jnp.where(lane == comb[:, EXPERT1_LANE:EXPERT1_LANE + 1], 1.0, 0.0)
    pick2 = jnp.where(lane == comb[:, EXPERT2_LANE:EXPERT2_LANE + 1], 1.0, 0.0)
    rows = lax.broadcasted_iota(jnp.int32, (t, t), 0)
    cols = lax.broadcasted_iota(jnp.int32, (t, t), 1)
    earlier = jnp.where(cols < rows, 1.0, 0.0).astype(BF16)
    before1 = _dot(earlier, pick1.astype(BF16))
    before2 = _dot(earlier, pick2.astype(BF16))
    base = base_ref[0]
    firsts = jnp.sum(pick1, axis=0, keepdims=True)
    pos1 = jnp.sum(pick1 * (base + before1), axis=-1, keepdims=True)
    pos2 = jnp.sum(pick2 * (base + firsts + before2), axis=-1, keepdims=True)
    both = jnp.where(lane_i == 0, pos1, jnp.where(lane_i == 1, pos2, 0.0))
    pos_ref[...] = jnp.transpose(both)[0:8, :].astype(jnp.int32)


def _moe_pos(comb, base):
    n = comb.shape[0]
    t = MERGE_TILE
    return pl.pallas_call(
        _moe_pos_kernel,
        grid=(n // t,),
        in_specs=[pl.BlockSpec((t, ROUTER_LANES), lambda i: (i, 0)),
                  pl.BlockSpec((1, 1, ROUTER_LANES), lambda i: (i, 0, 0))],
        out_specs=pl.BlockSpec((8, t), lambda i: (0, i)),
        out_shape=jax.ShapeDtypeStruct((8, n), jnp.int32),
        compiler_params=pltpu.CompilerParams(dimension_semantics=("arbitrary",)),
        name="moe_pos",
    )(comb, base.reshape(n // t, 1, ROUTER_LANES))


def _sc_mesh():
    return plsc.VectorSubcoreMesh(core_axis_name="core", subcore_axis_name="subcore")


def _sc_scratch():
    return [pltpu.VMEM((8, SC_WINDOW), jnp.int32), pltpu.VMEM((SC_WINDOW, PACKED), jnp.uint32)]


def _sc_scatter_rows(h, pos, n_rows):
    per_worker = h.shape[0] // (SC_WINDOW * SC_WORKERS)

    @pl.kernel(out_type=jax.ShapeDtypeStruct((n_rows, PACKED), jnp.uint32), mesh=_sc_mesh(),
               scratch_types=_sc_scratch())
    def scatter(h_hbm, pos_hbm, out_hbm, idx, buf):
        worker = lax.axis_index("core") * SC_SUBCORES + lax.axis_index("subcore")

        @pl.loop(0, per_worker)
        def _(b):
            start = (worker * per_worker + b) * SC_WINDOW
            pltpu.sync_copy(pos_hbm.at[:, pl.ds(start, SC_WINDOW)], idx)
            pltpu.sync_copy(h_hbm.at[pl.ds(start, SC_WINDOW)], buf)
            pltpu.sync_copy(buf, out_hbm.at[idx.at[0]])
            pltpu.sync_copy(buf, out_hbm.at[idx.at[1]])

    return scatter(h, pos)


def _sc_gather_rows(y, pos):
    n = pos.shape[1]
    per_worker = n // (SC_WINDOW * SC_WORKERS)
    out = jax.ShapeDtypeStruct((n, PACKED), jnp.uint32)

    @pl.kernel(out_type=(out, out), mesh=_sc_mesh(), scratch_types=_sc_scratch())
    def gather(y_hbm, pos_hbm, o1_hbm, o2_hbm, idx, buf):
        worker = lax.axis_index("core") * SC_SUBCORES + lax.axis_index("subcore")

        @pl.loop(0, per_worker)
        def _(b):
            start = (worker * per_worker + b) * SC_WINDOW
            pltpu.sync_copy(pos_hbm.at[:, pl.ds(start, SC_WINDOW)], idx)
            pltpu.sync_copy(y_hbm.at[idx.at[0]], buf)
            pltpu.sync_copy(buf, o1_hbm.at[pl.ds(start, SC_WINDOW)])
            pltpu.sync_copy(y_hbm.at[idx.at[1]], buf)
            pltpu.sync_copy(buf, o2_hbm.at[pl.ds(start, SC_WINDOW)])

    return gather(y, pos)


def _expert_mlp_kernel(expert_ref, valid_ref, source_ref, next_ref, ordinal_ref, hs_hbm,
                       wg_hbm, wu_hbm, wd_hbm, y_ref, wg32_ref, wu32_ref, wd32_ref,
                       wg_ref, wu_ref, wd_ref, sem, rows_ref, rows_sem):
    b = pl.program_id(0)
    n_blocks = pl.num_programs(0)

    def rows_copy(blk):
        slot = blk % ROW_RING
        return pltpu.make_async_copy(hs_hbm.at[pl.ds(pl.multiple_of(blk * MOE_BLOCK, MOE_BLOCK), MOE_BLOCK)],
                                     rows_ref.at[slot], rows_sem.at[slot])

    def weight_copies(e, slot):
        return [pltpu.make_async_copy(src.at[e], dst.at[slot], sem.at[slot])
                for src, dst in ((wg_hbm, wg32_ref), (wu_hbm, wu32_ref), (wd_hbm, wd32_ref))]

    @pl.when(valid_ref[b] != 0)
    def _():
        @pl.when(b == 0)
        def _():
            rows_copy(0).start()
            for ahead in range(1, ROW_RING - 1):
                @pl.when(valid_ref[jnp.minimum(ahead, n_blocks - 1)] != 0)
                def _():
                    rows_copy(ahead).start()

        far = b + ROW_RING - 1

        @pl.when((far < n_blocks) & (valid_ref[jnp.minimum(far, n_blocks - 1)] != 0))
        def _():
            rows_copy(far).start()

        rows_copy(b).wait()

        @pl.when((b == 0) | (expert_ref[b] != expert_ref[jnp.maximum(b - 1, 0)]))
        def _():
            slot = ordinal_ref[b] % 2

            @pl.when(b == 0)
            def _():
                for cp in weight_copies(expert_ref[0], slot):
                    cp.start()

            for cp in weight_copies(expert_ref[b], slot):
                cp.wait()

            @pl.when(next_ref[b] < N_EXPERTS)
            def _():
                for cp in weight_copies(next_ref[b], 1 - slot):
                    cp.start()

            wg_ref[...] = wg32_ref[slot].astype(BF16)
            wu_ref[...] = wu32_ref[slot].astype(BF16)
            wd_ref[...] = wd32_ref[slot].astype(BF16)

        half = D_MODEL // 2
        h_lo, h_hi = _unpack_bf16(rows_ref[b % ROW_RING])
        h_lo = h_lo.astype(BF16)
        h_hi = h_hi.astype(BF16)
        hg = _dot(h_lo, wg_ref[0:half, :]) + _dot(h_hi, wg_ref[half:D_MODEL, :])
        hu = _dot(h_lo, wu_ref[0:half, :]) + _dot(h_hi, wu_ref[half:D_MODEL, :])
        act = (hg * _sigmoid(hg) * hu).astype(BF16)
        y_ref[...] = _pack_bf16(_dot(act, wd_ref[...]))


def _expert_mlp(blk, hs, wg, wu, wd):
    rows = lambda b, expert, valid, source, nxt, ordinal: (source[b], 0)
    grid_spec = pltpu.PrefetchScalarGridSpec(
        num_scalar_prefetch=5,
        grid=(hs.shape[0] // MOE_BLOCK,),
        in_specs=[pl.BlockSpec(memory_space=pl.ANY)] * 4,
        out_specs=pl.BlockSpec((MOE_BLOCK, PACKED), rows),
        scratch_shapes=[pltpu.VMEM((2, D_MODEL, D_EXPERT), F32), pltpu.VMEM((2, D_MODEL, D_EXPERT), F32),
                        pltpu.VMEM((2, D_EXPERT, D_MODEL), F32),
                        pltpu.VMEM((D_MODEL, D_EXPERT), BF16), pltpu.VMEM((D_MODEL, D_EXPERT), BF16),
                        pltpu.VMEM((D_EXPERT, D_MODEL), BF16),
                        pltpu.SemaphoreType.DMA((2,)),
                        pltpu.VMEM((ROW_RING, MOE_BLOCK, PACKED), jnp.uint32),
                        pltpu.SemaphoreType.DMA((ROW_RING,))],
    )
    return pl.pallas_call(
        _expert_mlp_kernel,
        grid_spec=grid_spec,
        out_shape=jax.ShapeDtypeStruct((hs.shape[0], PACKED), jnp.uint32),
        compiler_params=pltpu.CompilerParams(
            dimension_semantics=("arbitrary",), vmem_limit_bytes=VMEM_LIMIT),
        name="expert_mlp",
    )(blk[:, 0], blk[:, 1], blk[:, 2], blk[:, 3], blk[:, 4], hs, wg, wu, wd)


def _moe_out_kernel(x1_ref, y1_ref, y2_ref, comb_ref, lnf_ref, out_ref, *, final_norm):
    w1 = comb_ref[:, WEIGHT1_LANE:WEIGHT1_LANE + 1]
    w2 = comb_ref[:, WEIGHT2_LANE:WEIGHT2_LANE + 1]
    a_lo, a_hi = _unpack_bf16(y1_ref[...])
    b_lo, b_hi = _unpack_bf16(y2_ref[...])
    moe = jnp.concatenate([w1 * a_lo + w2 * b_lo, w1 * a_hi + w2 * b_hi], axis=1)
    y = x1_ref[...] + moe
    if final_norm:
        y = y * lax.rsqrt(jnp.mean(y * y, axis=-1, keepdims=True) + RMS_EPS) * lnf_ref[...]
    out_ref[...] = y


def _moe_out(x1, y1, y2, comb, lnf, final_norm):
    n = x1.shape[0]
    t = FINAL_TILE
    row = lambda i: (i, 0)
    return pl.pallas_call(
        functools.partial(_moe_out_kernel, final_norm=final_norm),
        grid=(n // t,),
        in_specs=[pl.BlockSpec((t, D_MODEL), row), pl.BlockSpec((t, PACKED), row),
                  pl.BlockSpec((t, PACKED), row), pl.BlockSpec((t, ROUTER_LANES), row),
                  pl.BlockSpec((1, D_MODEL), lambda i: (0, 0))],
        out_specs=pl.BlockSpec((t, D_MODEL), row),
        out_shape=jax.ShapeDtypeStruct((n, D_MODEL), F32),
        compiler_params=pltpu.CompilerParams(
            dimension_semantics=("arbitrary",), vmem_limit_bytes=VMEM_LIMIT),
        name="moe_out",
    )(x1, y1, y2, comb, lnf)


def _moe(counts, x1, h2p, comb, wg, wu, wd, lnf, final_norm):
    n = x1.shape[0]
    n_blocks = (2 * n) // MOE_BLOCK + N_EXPERTS
    base, blk = _moe_plan(counts[:, 0, :], n_blocks)
    pos = _moe_pos(comb, base)
    hs = _sc_scatter_rows(h2p, pos, n_blocks * MOE_BLOCK)
    ys = _expert_mlp(blk, hs, wg, wu, wd)
    y1, y2 = _sc_gather_rows(ys, pos)
    return _moe_out(x1, y1, y2, comb, lnf, final_norm)


def kernel(x, ln_mix_g, w_in, att_rel_bias, rwkv_mu, rwkv_w0, rwkv_w2, rwkv_a0, rwkv_a2, rwkv_g2,
           rwkv_k_k, rwkv_k_a, rwkv_r_k, rwkv_gn_g, rwkv_gn_b, w_branch_att, w_branch_rwkv, w_out,
           ln_ffn_g, router_group_w, router_group_b, router_expert_w, router_expert_b,
           expert_w_gate, expert_w_up, expert_w_down, ln_final_g):
    bsz, seq, d = x.shape
    depth = w_in.shape[0]
    n = bsz * seq
    x2 = x.reshape(n, d)
    for l in range(depth):
        q, k, v, rw, gates = _in_proj(x2, ln_mix_g[l][None, :], w_in, l, rwkv_mu[l][None, :], seq)
        bias = _rel_bias(att_rel_bias[l])
        att = _band_attn(q.reshape(bsz, seq, WIDTH), k.reshape(bsz, seq, WIDTH),
                         v.reshape(bsz, seq, WIDTH), bias)
        zeros = jnp.zeros((DECAY_LORA, WIDTH), F32)
        w2a2 = jnp.concatenate(
            [jnp.concatenate([rwkv_w2[l], zeros], axis=1),
             jnp.concatenate([zeros, rwkv_a2[l]], axis=1)], axis=0)
        rwkv = _rwkv(rw.reshape(bsz, seq, RWKV_PROJ), w2a2.astype(BF16), rwkv_g2[l].astype(BF16),
                     rwkv_w0[l][None, :], rwkv_a0[l][None, :], rwkv_k_k[l][None, :],
                     rwkv_k_a[l][None, :], rwkv_r_k[l].reshape(1, WIDTH),
                     rwkv_gn_g[l][None, :], rwkv_gn_b[l][None, :])
        wr = jnp.concatenate([router_expert_w[l], router_group_w[l]], axis=1)
        wr = jnp.pad(wr, ((0, 0), (0, ROUTER_LANES - wr.shape[1])))
        wr_hi = wr.astype(BF16)
        wr = jnp.concatenate([wr_hi, (wr - wr_hi.astype(F32)).astype(BF16)], axis=1)
        br = jnp.concatenate([router_expert_b[l], router_group_b[l]])
        br = jnp.pad(br, (0, ROUTER_LANES - br.shape[0]))[None, :]
        x1, h2, comb, counts = _merge(x2, att.reshape(n, WIDTH), rwkv.reshape(n, WIDTH), gates,
                                      w_branch_att[l], w_branch_rwkv[l], w_out[l],
                                      ln_ffn_g[l][None, :], wr, br)
        x2 = _moe(counts, x1, h2, comb, expert_w_gate[l], expert_w_up[l], expert_w_down[l],
                  ln_final_g[None, :], final_norm=(l == depth - 1))
    return x2.reshape(bsz, seq, d)
```

```python
import functools
import math

import jax
import jax.numpy as jnp
from jax import lax
from jax.experimental import pallas as pl
from jax.experimental.pallas import tpu as pltpu
from jax.experimental.pallas import tpu_sc as plsc

F32 = jnp.float32
BF16 = jnp.bfloat16
HIGHEST = lax.Precision.HIGHEST

D_MODEL = 1024
CHUNK = 64
HEADS = 8
HEAD_DIM = 64
WIDTH = HEADS * HEAD_DIM
LEFT_CHUNKS = 8
BAND = (LEFT_CHUNKS + 1) * CHUNK
REL_CLIP = 64
N_REL = 2 * REL_CLIP + 1
DECAY_LORA = 64
AAA_LORA = 64
GATE_LORA = 128
GN_EPS = 64e-5
RMS_EPS = 1e-6
ATT_PROJ = 3 * WIDTH
RWKV_PROJ = 3 * WIDTH + DECAY_LORA + AAA_LORA + GATE_LORA
D_IN = ATT_PROJ + RWKV_PROJ + 2 * D_MODEL
W_IN_CHUNK = 256
N_GROUPS = 4
EXPERTS_PER_GROUP = 8
N_EXPERTS = N_GROUPS * EXPERTS_PER_GROUP
D_EXPERT = 256
RWKV_SEQS = 4
RWKV_CHUNKS = 4
BIAS_KEYS = 192
ATT_CHUNKS = 8
ATT_GROUP = 4
ROUTER_LANES = 128
GROUP_LANE0 = N_EXPERTS
EXPERT1_LANE, EXPERT2_LANE, WEIGHT1_LANE, WEIGHT2_LANE = 126, 125, 124, 123
MERGE_TILE = 512
MOE_BLOCK = 1024
PACKED = D_MODEL // 2
SC_WINDOW = 128
SC_CORES, SC_SUBCORES = 2, 16
SC_WORKERS = SC_CORES * SC_SUBCORES
FINAL_TILE = 1024
ROW_RING = 3

V7X_VMEM_BYTES = 64 * 1024 * 1024
VMEM_LIMIT = V7X_VMEM_BYTES - 12 * 1024 * 1024


def _dot(a, b):
    return jnp.dot(a, b, preferred_element_type=F32)


def _dot_hi(a, b):
    return jnp.dot(a, b, preferred_element_type=F32, precision=HIGHEST)


def _dot_nt(a, b, precision=None):
    return lax.dot_general(a, b, (((1,), (1,)), ((), ())),
                           preferred_element_type=F32, precision=precision)


def _dot_tn(a, b, precision=None):
    return lax.dot_general(a, b, (((0,), (0,)), ((), ())),
                           preferred_element_type=F32, precision=precision)


def _sigmoid(x):
    return 1.0 / (1.0 + jnp.exp(-x))


def _pack_bf16(x):
    w = x.shape[1] // 2
    hi = pltpu.bitcast(x[:, :w].astype(BF16).astype(F32), jnp.uint32)
    lo = pltpu.bitcast(x[:, w:].astype(BF16).astype(F32), jnp.uint32)
    return hi | lax.shift_right_logical(lo, jnp.uint32(16))


def _unpack_bf16(p):
    hi = pltpu.bitcast(p & jnp.uint32(0xFFFF0000), F32)
    lo = pltpu.bitcast(lax.shift_left(p, jnp.uint32(16)), F32)
    return hi, lo


def _mm(a, b):
    return jnp.dot(a.astype(BF16), b.astype(BF16), preferred_element_type=F32)


def _head_sums(x):
    outs = []
    lane = lax.broadcasted_iota(jnp.int32, (x.shape[0], 2 * HEAD_DIM), 1)
    low = lane < HEAD_DIM
    for p in range(HEADS // 2):
        xp = x[:, 2 * HEAD_DIM * p:2 * HEAD_DIM * (p + 1)]
        s_lo = jnp.sum(jnp.where(low, xp, 0.0), axis=-1, keepdims=True)
        s_hi = jnp.sum(jnp.where(low, 0.0, xp), axis=-1, keepdims=True)
        outs.append(jnp.where(low, s_lo, s_hi))
    return jnp.concatenate(outs, axis=-1)


def _rel_bias_kernel(tab_ref, out_ref):
    rows = tab_ref.shape[1]
    n = lax.broadcasted_iota(jnp.int32, (rows, CHUNK * 128), 1)
    r = lax.broadcasted_iota(jnp.int32, (rows, CHUNK * 128), 0)
    q = n >> 7
    kk = n & 127
    idx = jnp.clip(CHUNK + q - kk, -REL_CLIP, REL_CLIP) + REL_CLIP
    pick = jnp.where(r == idx, 1.0, 0.0) - jnp.where(r == N_REL - 1, 1.0, 0.0)
    out_ref[...] = _dot_hi(tab_ref[...], pick.astype(F32))


def _rel_bias(rel_table):
    rows = 136
    tab = jnp.pad(rel_table.astype(F32), ((0, 0), (0, rows - N_REL)))
    tail = pl.pallas_call(
        _rel_bias_kernel,
        out_shape=jax.ShapeDtypeStruct((HEADS, CHUNK * 128), F32),
        name="rel_bias",
    )(tab)
    tail = tail.reshape(HEADS, CHUNK, 128)
    bias = jnp.concatenate([jnp.zeros((HEADS, CHUNK, BIAS_KEYS - 128), F32), tail], axis=-1)
    return bias.reshape(HEADS * CHUNK, BIAS_KEYS)


def _in_proj_kernel(x_ref, g_ref, w_hbm, mu_ref, q_ref, k_ref, v_ref, rw_ref, gate_ref,
                    carry_ref, w_ref, stage_ref, sem, *, tiles_per_seq, layer):
    i = pl.program_id(0)

    def chunk_copy(c):
        return pltpu.make_async_copy(w_hbm.at[layer, :, pl.ds(c * W_IN_CHUNK, W_IN_CHUNK)],
                                     stage_ref.at[c % 2], sem.at[c % 2])

    n_chunks = D_IN // W_IN_CHUNK

    def body(first):
        def load(upto):
            if not first:
                return
            for c in range(loaded[0], upto // W_IN_CHUNK):
                if c + 1 < n_chunks:
                    chunk_copy(c + 1).start()
                chunk_copy(c).wait()
                w_ref[:, c * W_IN_CHUNK:(c + 1) * W_IN_CHUNK] = stage_ref[c % 2].astype(BF16)
            loaded[0] = upto // W_IN_CHUNK

        loaded = [0]
        if first:
            chunk_copy(0).start()
        x = x_ref[...]
        h = x * lax.rsqrt(jnp.mean(x * x, axis=-1, keepdims=True) + RMS_EPS) * g_ref[...]
        hb = h.astype(BF16)
        load(WIDTH)
        q_ref[...] = _dot(hb, w_ref[:, 0:WIDTH]).astype(BF16)
        load(2 * WIDTH)
        k_ref[...] = _dot(hb, w_ref[:, WIDTH:2 * WIDTH]).astype(BF16)
        load(ATT_PROJ)
        v_ref[...] = _dot(hb, w_ref[:, 2 * WIDTH:ATT_PROJ]).astype(BF16)
        load(ATT_PROJ + RWKV_PROJ)
        rw = _dot(hb, w_ref[:, ATT_PROJ:ATT_PROJ + RWKV_PROJ])
        tm = rw.shape[0]
        if first:
            first_prev = jnp.zeros((1, RWKV_PROJ), F32)
        else:
            first_prev = jnp.where(i % tiles_per_seq == 0, 0.0, carry_ref[0:1, :])
        rolled = pltpu.roll(rw, 1, axis=0)
        row = lax.broadcasted_iota(jnp.int32, rw.shape, 0)
        prev = jnp.where(row == 0, first_prev, rolled)
        carry_ref[0:1, :] = rw[tm - 1:tm, :]
        rw_ref[...] = rw + (prev - rw) * mu_ref[...]
        load(D_IN)
        gate_ref[...] = _sigmoid(_dot(hb, w_ref[:, ATT_PROJ + RWKV_PROJ:D_IN])).astype(BF16)

    @pl.when(i == 0)
    def _():
        body(True)

    @pl.when(i != 0)
    def _():
        body(False)


def _in_proj(x2, ln_g, w_in, layer, mu, seq):
    n = x2.shape[0]
    tm = 1024
    assert seq % tm == 0, "a projection tile must not straddle two sequences (token shift carry)"
    row = lambda i: (i, 0)
    const = lambda i: (0, 0)
    return pl.pallas_call(
        functools.partial(_in_proj_kernel, tiles_per_seq=seq // tm, layer=layer),
        grid=(n // tm,),
        in_specs=[
            pl.BlockSpec((tm, D_MODEL), row),
            pl.BlockSpec((1, D_MODEL), const),
            pl.BlockSpec(memory_space=pl.ANY),
            pl.BlockSpec((1, RWKV_PROJ), const),
        ],
        out_specs=[
            pl.BlockSpec((tm, WIDTH), row),
            pl.BlockSpec((tm, WIDTH), row),
            pl.BlockSpec((tm, WIDTH), row),
            pl.BlockSpec((tm, RWKV_PROJ), row),
            pl.BlockSpec((tm, 2 * D_MODEL), row),
        ],
        out_shape=[
            jax.ShapeDtypeStruct((n, WIDTH), BF16),
            jax.ShapeDtypeStruct((n, WIDTH), BF16),
            jax.ShapeDtypeStruct((n, WIDTH), BF16),
            jax.ShapeDtypeStruct((n, RWKV_PROJ), F32),
            jax.ShapeDtypeStruct((n, 2 * D_MODEL), BF16),
        ],
        scratch_shapes=[pltpu.VMEM((8, RWKV_PROJ), F32),
                        pltpu.VMEM((D_MODEL, D_IN), BF16),
                        pltpu.VMEM((2, D_MODEL, W_IN_CHUNK), F32),
                        pltpu.SemaphoreType.DMA((2,))],
        compiler_params=pltpu.CompilerParams(
            dimension_semantics=("arbitrary",), vmem_limit_bytes=VMEM_LIMIT),
        name="in_proj",
    )(x2, ln_g, w_in, mu)


def _band_attn_kernel(q_ref, k_ref, v_ref, bias_ref, o_ref, kpad_ref, vpad_ref):
    seq = k_ref.shape[1]
    pad = LEFT_CHUNKS * CHUNK
    kpad_ref[0:pad, :] = jnp.zeros((pad, WIDTH), BF16)
    vpad_ref[0:pad, :] = jnp.zeros((pad, WIDTH), BF16)
    kpad_ref[pad:pad + seq, :] = k_ref[0]
    vpad_ref[pad:pad + seq, :] = v_ref[0]

    gw = ATT_GROUP * HEAD_DIM
    rows = ATT_GROUP * CHUNK
    r_head = lax.broadcasted_iota(jnp.int32, (rows, gw), 0) // CHUNK
    l_head = lax.broadcasted_iota(jnp.int32, (rows, gw), 1) // HEAD_DIM
    own = r_head == l_head
    kpos_lo = lax.broadcasted_iota(jnp.int32, (rows, BAND - BIAS_KEYS), 1)
    kpos_hi = lax.broadcasted_iota(jnp.int32, (rows, BIAS_KEYS), 1) + (BAND - BIAS_KEYS)
    neg = jnp.finfo(F32).min
    groups = range(HEADS // ATT_GROUP)
    lanes = [slice(g * gw, (g + 1) * gw) for g in groups]

    def chunk_pair(i, carry, masked):
        units = [(j, g) for j in range(ATT_CHUNKS) for g in groups]
        ids = range(len(units))
        starts = [pl.multiple_of((i * ATT_CHUNKS + j) * CHUNK, CHUNK) for j in range(ATT_CHUNKS)]
        kb = [kpad_ref[pl.ds(st, BAND), :] for st in starts]
        vb = [vpad_ref[pl.ds(st, BAND), :] for st in starts]
        q = [q_ref[0, pl.ds(st, CHUNK), :] * (HEAD_DIM ** -0.5) for st in starts]
        qrows = [jnp.where(own, jnp.concatenate([q[j][:, lanes[g]]] * ATT_GROUP, axis=0),
                           jnp.zeros((), BF16)) for j, g in units]
        s = [_dot_nt(qrows[u], kb[j][:, lanes[g]]) for u, (j, g) in enumerate(units)]
        s_lo = [s[u][:, 0:BAND - BIAS_KEYS] for u in ids]
        s_hi = [s[u][:, BAND - BIAS_KEYS:BAND] + bias_ref[g * rows:(g + 1) * rows, :]
                for u, (j, g) in enumerate(units)]
        if masked:
            first = [(LEFT_CHUNKS - (i * ATT_CHUNKS + j)) * CHUNK for j in range(ATT_CHUNKS)]
            s_lo = [jnp.where(kpos_lo >= first[j], s_lo[u], neg) for u, (j, g) in enumerate(units)]
            s_hi = [jnp.where(kpos_hi >= first[j], s_hi[u], neg) for u, (j, g) in enumerate(units)]
        m = [jnp.maximum(jnp.max(s_lo[u], axis=-1, keepdims=True),
                         jnp.max(s_hi[u], axis=-1, keepdims=True)) for u in ids]
        p_lo = [jnp.exp(s_lo[u] - m[u]) for u in ids]
        p_hi = [jnp.exp(s_hi[u] - m[u]) for u in ids]
        denom = [jnp.sum(p_lo[u], axis=-1, keepdims=True) + jnp.sum(p_hi[u], axis=-1, keepdims=True)
                 for u in ids]
        o_all = [(_dot(p_lo[u].astype(BF16), vb[j][0:BAND - BIAS_KEYS, lanes[g]])
                  + _dot(p_hi[u].astype(BF16), vb[j][BAND - BIAS_KEYS:BAND, lanes[g]])) / denom[u]
                 for u, (j, g) in enumerate(units)]
        for u, (j, g) in enumerate(units):
            o_own = jnp.where(own, o_all[u], 0.0)
            o = o_own[0:CHUNK]
            for h in range(1, ATT_GROUP):
                o = o + o_own[h * CHUNK:(h + 1) * CHUNK]
            o_ref[0, pl.ds(starts[j], CHUNK), lanes[g]] = o.astype(BF16)
        return carry

    n_trips = seq // (CHUNK * ATT_CHUNKS)
    n_masked = min(LEFT_CHUNKS // ATT_CHUNKS, n_trips)
    lax.fori_loop(0, n_masked, functools.partial(chunk_pair, masked=True), 0)
    lax.fori_loop(n_masked, n_trips, functools.partial(chunk_pair, masked=False), 0)


def _band_attn(q, k, v, bias):
    b, seq, _ = q.shape
    whole = pl.BlockSpec((1, seq, WIDTH), lambda i: (i, 0, 0))
    return pl.pallas_call(
        _band_attn_kernel,
        grid=(b,),
        in_specs=[whole, whole, whole, pl.BlockSpec((HEADS * CHUNK, BIAS_KEYS), lambda i: (0, 0))],
        out_specs=whole,
        out_shape=jax.ShapeDtypeStruct((b, seq, WIDTH), BF16),
        scratch_shapes=[pltpu.VMEM((seq + LEFT_CHUNKS * CHUNK, WIDTH), BF16),
                        pltpu.VMEM((seq + LEFT_CHUNKS * CHUNK, WIDTH), BF16)],
        compiler_params=pltpu.CompilerParams(
            dimension_semantics=("arbitrary",), vmem_limit_bytes=VMEM_LIMIT),
        name="band_attn",
    )(q, k, v, bias)


def _rwkv_kernel(rw_ref, w2a2_ref, g2_ref, w0_ref, a0_ref, kk_ref, ka_ref, rk_ref,
                 gng_ref, gnb_ref, y_ref, state_ref):
    c = pl.program_id(1)
    t = CHUNK
    nb = rw_ref.shape[0]

    @pl.when(c == 0)
    def _():
        state_ref[...] = jnp.zeros(state_ref.shape, F32)

    lane = lax.broadcasted_iota(jnp.int32, (t, 2 * HEAD_DIM), 1)
    low = lane < HEAD_DIM
    r2 = lax.broadcasted_iota(jnp.int32, (2 * t, 2 * HEAD_DIM), 0)
    c2 = lax.broadcasted_iota(jnp.int32, (2 * t, 2 * HEAD_DIM), 1)
    own = (r2 < t) == (c2 < HEAD_DIM)
    strict = (r2 & (t - 1)) > (c2 & (t - 1))
    incl = (r2 & (t - 1)) >= (c2 & (t - 1))
    eye = jnp.where(r2 == c2, 1.0, 0.0).astype(F32)

    def stack2(xp):
        return jnp.concatenate([jnp.where(low, xp, 0.0), jnp.where(low, 0.0, xp)], axis=0)

    def wide(seqs, start):
        rows = len(seqs) * t
        rw = rw_ref[seqs[0]:seqs[-1] + 1, pl.ds(start, t), :].reshape(rows, RWKV_PROJ)
        r = rw[:, 0:WIDTH]
        k = rw[:, WIDTH:2 * WIDTH]
        v = rw[:, 2 * WIDTH:3 * WIDTH]
        lora = rw[:, 3 * WIDTH:3 * WIDTH + DECAY_LORA + AAA_LORA]
        g_lo = rw[:, 3 * WIDTH + DECAY_LORA + AAA_LORA:RWKV_PROJ]
        lane128 = lax.broadcasted_iota(jnp.int32, lora.shape, 1)
        lora = jnp.where(lane128 < DECAY_LORA, jnp.tanh(lora), lora)
        wa = _mm(lora, w2a2_ref[...])
        log_decay = -math.exp(-0.5) * _sigmoid(w0_ref[...] + wa[:, 0:WIDTH])
        lr = _sigmoid(a0_ref[...] + wa[:, WIDTH:2 * WIDTH])
        gate = _mm(_sigmoid(g_lo), g2_ref[...])
        kk_raw = k * kk_ref[...]
        k_mod = k * (1.0 + (lr - 1.0) * ka_ref[...])
        row = lax.broadcasted_iota(jnp.int32, (rows, rows), 0)
        col = lax.broadcasted_iota(jnp.int32, (rows, rows), 1)
        tri = jnp.where((row >= col) & ((row // t) == (col // t)), 1.0, 0.0).astype(BF16)
        ld1 = log_decay.astype(BF16)
        rem = log_decay - ld1.astype(F32)
        ld2 = rem.astype(BF16)
        ld3 = (rem - ld2.astype(F32)).astype(BF16)
        parts = _dot(tri, jnp.concatenate([ld1, ld2, ld3], axis=1))
        logp = parts[:, 0:WIDTH] + parts[:, WIDTH:2 * WIDTH] + parts[:, 2 * WIDTH:3 * WIDTH]
        p_in = jnp.exp(logp)
        p_ex = jnp.exp(logp - log_decay)
        p_inv = jnp.exp(-logp)
        kk = kk_raw / jnp.maximum(jnp.sqrt(_head_sums(kk_raw * kk_raw)), 1e-12)
        return dict(a_hat=-kk * p_ex, r_hat=r * p_in, b_hat=kk * lr * p_inv, k_hat=k_mod * p_inv,
                    v=v, p_in=p_in, gate=gate, bonus=_head_sums(r * k_mod * rk_ref[...]) * v)

    def front(seqs, w):
        chains = [(j, b, p) for j, b in enumerate(seqs) for p in range(HEADS // 2)]
        ids = range(len(chains))
        rs = [slice(j * t, (j + 1) * t) for j, _, _ in chains]
        ls = [slice(2 * HEAD_DIM * p, 2 * HEAD_DIM * (p + 1)) for _, _, p in chains]
        p_end = [w["p_in"][(j + 1) * t - 1:(j + 1) * t, ls[i]] for i, (j, _, _) in enumerate(chains)]
        ar = [jnp.concatenate([stack2(w["a_hat"][rs[i], ls[i]]), stack2(w["r_hat"][rs[i], ls[i]])],
                              axis=0).astype(BF16) for i in ids]
        bk2 = [jnp.concatenate([stack2(w["b_hat"][rs[i], ls[i]]), stack2(w["k_hat"][rs[i], ls[i]])], axis=0)
               for i in ids]
        bk = [bk2[i].astype(BF16) for i in ids]
        btkt = [(bk2[i] * p_end[i]).astype(BF16) for i in ids]
        v2 = [stack2(w["v"][rs[i], ls[i]]).astype(BF16) for i in ids]
        g = [_dot_nt(ar[i], bk[i]) for i in ids]
        st = [state_ref[b, p] for _, b, p in chains]
        ars = [_dot_nt(ar[i], st[i].astype(BF16)) for i in ids]
        l_ab = [jnp.where(strict, g[i][0:2 * t, 0:2 * t], 0.0) for i in ids]
        lm = [jnp.concatenate([jnp.where(strict, g[i][0:2 * t, 2 * t:4 * t], 0.0),
                               jnp.where(incl, g[i][2 * t:4 * t, 2 * t:4 * t], 0.0)], axis=0)
              for i in ids]
        m_rb = [jnp.where(incl, g[i][2 * t:4 * t, 0:2 * t], 0.0).astype(BF16) for i in ids]
        lv = [_mm(lm[i], v2[i]) for i in ids]
        return dict(chains=chains, rs=rs, ls=ls, p_end=p_end, btkt=btkt, v2=v2, st=st, ars=ars,
                    l_ab=l_ab, m_rb=m_rb, lv=lv)

    def inverse(f):
        w_inv = [eye + l for l in f["l_ab"]]
        l_pow = f["l_ab"]
        for _ in range(int(math.log2(t)) - 1):
            l_pow = [_mm(l, l) for l in l_pow]
            w_inv = [w + _mm(w, l) for w, l in zip(w_inv, l_pow)]
        return w_inv

    def back(w, f, w_inv, start):
        chains, rs, ls = f["chains"], f["rs"], f["ls"]
        ids = range(len(chains))
        z = [_mm(w_inv[i], f["ars"][i][0:2 * t] + f["lv"][i][0:2 * t]) for i in ids]
        y = [f["ars"][i][2 * t:4 * t] + f["lv"][i][2 * t:4 * t] + _mm(f["m_rb"][i], z[i]) for i in ids]
        for i, (_, b, p) in enumerate(chains):
            zv = jnp.concatenate([z[i].astype(BF16), f["v2"][i]], axis=0)
            state_ref[b, p] = f["st"][i] * f["p_end"][i] + _dot_tn(zv, f["btkt"][i])
        for i, (_, b, p) in enumerate(chains):
            mean = jnp.sum(y[i], axis=-1, keepdims=True) * (1.0 / HEAD_DIM)
            dev = jnp.where(own, y[i] - mean, 0.0)
            var = jnp.sum(dev * dev, axis=-1, keepdims=True) * (1.0 / HEAD_DIM)
            yn = dev * lax.rsqrt(var + GN_EPS)
            yn = yn[0:t] + yn[t:2 * t]
            out = ((yn * gng_ref[:, ls[i]] + gnb_ref[:, ls[i]] + w["bonus"][rs[i], ls[i]])
                   * w["gate"][rs[i], ls[i]])
            y_ref[b, pl.ds(start, t), ls[i]] = out.astype(BF16)

    seqs = list(range(nb))

    def chunk(j, carry):
        start = pl.multiple_of(j * t, t)
        w = wide(seqs, start)
        f = front(seqs, w)
        back(w, f, inverse(f), start)
        return carry

    lax.fori_loop(0, rw_ref.shape[1] // t, chunk, 0)


def _rwkv(rw, w2a2, g2, w0, a0, k_k, k_a, r_k, gn_g, gn_b):
    b, seq, _ = rw.shape
    nc = seq // CHUNK
    nb = RWKV_SEQS
    const = lambda i, c: (0, 0)
    vec = pl.BlockSpec((1, WIDTH), const)
    return pl.pallas_call(
        _rwkv_kernel,
        grid=(b // nb, nc // RWKV_CHUNKS),
        in_specs=[
            pl.BlockSpec((nb, CHUNK * RWKV_CHUNKS, RWKV_PROJ), lambda i, c: (i, c, 0)),
            pl.BlockSpec((DECAY_LORA + AAA_LORA, 2 * WIDTH), const),
            pl.BlockSpec((GATE_LORA, WIDTH), const),
            vec, vec, vec, vec, vec, vec, vec,
        ],
        out_specs=pl.BlockSpec((nb, CHUNK * RWKV_CHUNKS, WIDTH), lambda i, c: (i, c, 0)),
        out_shape=jax.ShapeDtypeStruct((b, seq, WIDTH), BF16),
        scratch_shapes=[pltpu.VMEM((nb, HEADS // 2, 2 * HEAD_DIM, 2 * HEAD_DIM), F32)],
        compiler_params=pltpu.CompilerParams(
            dimension_semantics=("arbitrary", "arbitrary"), vmem_limit_bytes=VMEM_LIMIT),
        name="rwkv7",
    )(rw, w2a2, g2, w0, a0, k_k, k_a, r_k, gn_g, gn_b)


def _merge_kernel(x_ref, att_ref, rwkv_ref, gate_ref, wa32_ref, wb32_ref, wo32_ref, g_ref,
                  wr_ref, br_ref, x1_ref, h2_ref, comb_ref, cnt_ref, wa_ref, wb_ref, wo_ref):
    @pl.when(pl.program_id(0) == 0)
    def _():
        wa_ref[...] = wa32_ref[...].astype(BF16)
        wb_ref[...] = wb32_ref[...].astype(BF16)
        wo_ref[...] = wo32_ref[...].astype(BF16)

    ga = gate_ref[:, 0:D_MODEL].astype(F32)
    gb = gate_ref[:, D_MODEL:2 * D_MODEL].astype(F32)
    merged = ga * _dot(att_ref[...], wa_ref[...]) + gb * _dot(rwkv_ref[...], wb_ref[...])
    x1 = x_ref[...] + _dot(merged.astype(BF16), wo_ref[...])
    x1_ref[...] = x1
    h2 = x1 * lax.rsqrt(jnp.mean(x1 * x1, axis=-1, keepdims=True) + RMS_EPS) * g_ref[...]
    h2_hi = h2.astype(BF16)
    h2_ref[...] = _pack_bf16(h2)
    h2_lo = (h2 - h2_hi.astype(F32)).astype(BF16)
    hw = _dot(h2_hi, wr_ref[...])
    lw = _dot(h2_lo, wr_ref[:, 0:ROUTER_LANES])
    logits = hw[:, 0:ROUTER_LANES] + (hw[:, ROUTER_LANES:2 * ROUTER_LANES] + lw) + br_ref[...]
    rec, counts = _route(logits)
    comb_ref[...] = rec
    cnt_ref[0] = jnp.broadcast_to(counts, (8, ROUTER_LANES)).astype(jnp.int32)


def _route(logits):
    lane_i = lax.broadcasted_iota(jnp.int32, logits.shape, 1)
    lane = lane_i.astype(F32)
    lane_group = (lane_i // EXPERTS_PER_GROUP).astype(F32)
    neg = jnp.finfo(F32).min
    big = float(ROUTER_LANES)

    def first_argmax(vals, mask):
        vm = jnp.where(mask, vals, neg)
        mx = jnp.max(vm, axis=-1, keepdims=True)
        idx = jnp.min(jnp.where(vm == mx, jnp.where(mask, lane, big), big), axis=-1, keepdims=True)
        return mx, idx

    is_group = (lane_i >= GROUP_LANE0) & (lane_i < GROUP_LANE0 + N_GROUPS)
    g_max, g_lane = first_argmax(logits, is_group)
    g_prob = 1.0 / jnp.sum(jnp.where(is_group, jnp.exp(logits - g_max), 0.0),
                           axis=-1, keepdims=True)
    g_idx = g_lane - GROUP_LANE0
    in_group = lane_group == g_idx
    e1, i1 = first_argmax(logits, in_group)
    e2, i2 = first_argmax(logits, in_group & (lane != i1))
    w2 = jnp.exp(e2 - e1)
    p1 = 1.0 / (1.0 + w2)
    p2 = w2 / (1.0 + w2)
    rec = jnp.where(lane_i == EXPERT1_LANE, i1, jnp.where(lane_i == EXPERT2_LANE, i2, 0.0))
    rec = jnp.where(lane_i == WEIGHT1_LANE, p1 * g_prob, jnp.where(lane_i == WEIGHT2_LANE, p2 * g_prob, rec))
    counts = jnp.sum(jnp.where(lane == i1, 1.0, 0.0) + jnp.where(lane == i2, 1.0, 0.0),
                     axis=0, keepdims=True)
    return rec, counts


def _merge(x2, att, rwkv, gates, wa, wb, wo, ln_g, wr, br):
    n = x2.shape[0]
    tm = MERGE_TILE
    row = lambda i: (i, 0)
    const = lambda i: (0, 0)
    return pl.pallas_call(
        _merge_kernel,
        grid=(n // tm,),
        in_specs=[
            pl.BlockSpec((tm, D_MODEL), row),
            pl.BlockSpec((tm, WIDTH), row),
            pl.BlockSpec((tm, WIDTH), row),
            pl.BlockSpec((tm, 2 * D_MODEL), row),
            pl.BlockSpec((WIDTH, D_MODEL), const),
            pl.BlockSpec((WIDTH, D_MODEL), const),
            pl.BlockSpec((D_MODEL, D_MODEL), const),
            pl.BlockSpec((1, D_MODEL), const),
            pl.BlockSpec((D_MODEL, 2 * ROUTER_LANES), const),
            pl.BlockSpec((1, ROUTER_LANES), const),
        ],
        out_specs=[
            pl.BlockSpec((tm, D_MODEL), row),
            pl.BlockSpec((tm, PACKED), row),
            pl.BlockSpec((tm, ROUTER_LANES), row),
            pl.BlockSpec((1, 8, ROUTER_LANES), lambda i: (i, 0, 0)),
        ],
        out_shape=[
            jax.ShapeDtypeStruct((n, D_MODEL), F32),
            jax.ShapeDtypeStruct((n, PACKED), jnp.uint32),
            jax.ShapeDtypeStruct((n, ROUTER_LANES), F32),
            jax.ShapeDtypeStruct((n // tm, 8, ROUTER_LANES), jnp.int32),
        ],
        scratch_shapes=[pltpu.VMEM((WIDTH, D_MODEL), BF16), pltpu.VMEM((WIDTH, D_MODEL), BF16),
                        pltpu.VMEM((D_MODEL, D_MODEL), BF16)],
        compiler_params=pltpu.CompilerParams(
            dimension_semantics=("arbitrary",), vmem_limit_bytes=VMEM_LIMIT),
        name="merge",
    )(x2, att, rwkv, gates, wa, wb, wo, ln_g, wr, br)


def _moe_plan_kernel(cnt_ref, base_ref, blk_ref):
    nt = cnt_ref.shape[0]
    cnt = cnt_ref[...].astype(F32)
    lane = lax.broadcasted_iota(jnp.int32, (8, ROUTER_LANES), 1)
    total = jnp.broadcast_to(jnp.sum(cnt, axis=0, keepdims=True), (8, ROUTER_LANES))
    padded = jnp.floor((total + (MOE_BLOCK - 1)) * (1.0 / MOE_BLOCK)) * MOE_BLOCK
    r = lax.broadcasted_iota(jnp.int32, (ROUTER_LANES, ROUTER_LANES), 0)
    c = lax.broadcasted_iota(jnp.int32, (ROUTER_LANES, ROUTER_LANES), 1)
    seg_start = _dot_hi(padded, jnp.where(r < c, 1.0, 0.0).astype(F32))
    tr = lax.broadcasted_iota(jnp.int32, (nt, nt), 0)
    tc = lax.broadcasted_iota(jnp.int32, (nt, nt), 1)
    tile_off = _dot_hi(jnp.where(tc < tr, 1.0, 0.0).astype(F32), cnt)
    base_ref[...] = seg_start[0:1, :] + tile_off
    seg_end = (seg_start + padded)[0:1, :]
    rows_total = jnp.sum(jnp.where(lane[0:1, :] < N_EXPERTS, padded[0:1, :], 0.0), axis=-1, keepdims=True)
    nblk = blk_ref.shape[0]
    blk_row = lax.broadcasted_iota(jnp.int32, (nblk, ROUTER_LANES), 0).astype(F32) * MOE_BLOCK
    blk_lane = lax.broadcasted_iota(jnp.int32, (nblk, ROUTER_LANES), 1)
    done = jnp.where((seg_end <= blk_row) & (blk_lane < N_EXPERTS), 1.0, 0.0)
    expert = jnp.minimum(jnp.sum(done, axis=-1, keepdims=True), N_EXPERTS - 1.0)
    valid = jnp.where(blk_row < rows_total, 1.0, 0.0)
    blk_idx = lax.broadcasted_iota(jnp.int32, (nblk, ROUTER_LANES), 0).astype(F32)
    source = jnp.minimum(blk_idx, rows_total * (1.0 / MOE_BLOCK) - 1.0)
    has_rows = jnp.where((padded > 0.0) & (lane < N_EXPERTS), 1.0, 0.0)
    before = _dot_hi(has_rows, jnp.where(r < c, 1.0, 0.0).astype(F32))[0:1, :]
    lane_f = blk_lane.astype(F32)
    ordinal = jnp.sum(jnp.where(lane_f == expert, before, 0.0), axis=-1, keepdims=True)
    later = (lane_f > expert) & (has_rows[0:1, :] > 0.0)
    nxt = jnp.min(jnp.where(later, lane_f, float(N_EXPERTS)), axis=-1, keepdims=True)
    table = jnp.where(blk_lane == 0, expert, jnp.where(blk_lane == 1, valid, jnp.where(blk_lane == 2, source,
                      jnp.where(blk_lane == 3, nxt, jnp.where(blk_lane == 4, ordinal, 0.0)))))
    blk_ref[...] = table.astype(jnp.int32)


def _moe_plan(cnt, n_blocks):
    nt = cnt.shape[0]
    return pl.pallas_call(
        _moe_plan_kernel,
        out_shape=[jax.ShapeDtypeStruct((nt, ROUTER_LANES), F32),
                   jax.ShapeDtypeStruct((n_blocks, ROUTER_LANES), jnp.int32)],
        name="moe_plan",
    )(cnt)


def _moe_pos_kernel(comb_ref, base_ref, pos_ref):
    t = comb_ref.shape[0]
    comb = comb_ref[...]
    lane_i = lax.broadcasted_iota(jnp.int32, (t, ROUTER_LANES), 1)
    lane = lane_i.astype(F32)
    pick1 = jnp.where(lane == comb[:, EXPERT1_LANE:EXPERT1_LANE + 1], 1.0, 0.0)
    pick2 = jnp.where(lane == comb[:, EXPERT2_LANE:EXPERT2_LANE + 1], 1.0, 0.0)
    rows = lax.broadcasted_iota(jnp.int32, (t, t), 0)
    cols = lax.broadcasted_iota(jnp.int32, (t, t), 1)
    earlier = jnp.where(cols < rows, 1.0, 0.0).astype(BF16)
    before1 = _dot(earlier, pick1.astype(BF16))
    before2 = _dot(earlier, pick2.astype(BF16))
    base = base_ref[0]
    firsts = jnp.sum(pick1, axis=0, keepdims=True)
    pos1 = jnp.sum(pick1 * (base + before1), axis=-1, keepdims=True)
    pos2 = jnp.sum(pick2 * (base + firsts + before2), axis=-1, keepdims=True)
    both = jnp.where(lane_i == 0, pos1, jnp.where(lane_i == 1, pos2, 0.0))
    pos_ref[...] = jnp.transpose(both)[0:8, :].astype(jnp.int32)


def _moe_pos(comb, base):
    n = comb.shape[0]
    t = MERGE_TILE
    return pl.pallas_call(
        _moe_pos_kernel,
        grid=(n // t,),
        in_specs=[pl.BlockSpec((t, ROUTER_LANES), lambda i: (i, 0)),
                  pl.BlockSpec((1, 1, ROUTER_LANES), lambda i: (i, 0, 0))],
        out_specs=pl.BlockSpec((8, t), lambda i: (0, i)),
        out_shape=jax.ShapeDtypeStruct((8, n), jnp.int32),
        compiler_params=pltpu.CompilerParams(dimension_semantics=("arbitrary",)),
        name="moe_pos",
    )(comb, base.reshape(n // t, 1, ROUTER_LANES))


def _sc_mesh():
    return plsc.VectorSubcoreMesh(core_axis_name="core", subcore_axis_name="subcore")


def _sc_scratch():
    return [pltpu.VMEM((8, SC_WINDOW), jnp.int32), pltpu.VMEM((SC_WINDOW, PACKED), jnp.uint32)]


def _sc_scatter_rows(h, pos, n_rows):
    per_worker = h.shape[0] // (SC_WINDOW * SC_WORKERS)

    @pl.kernel(out_type=jax.ShapeDtypeStruct((n_rows, PACKED), jnp.uint32), mesh=_sc_mesh(),
               scratch_types=_sc_scratch())
    def scatter(h_hbm, pos_hbm, out_hbm, idx, buf):
        worker = lax.axis_index("core") * SC_SUBCORES + lax.axis_index("subcore")

        @pl.loop(0, per_worker)
        def _(b):
            start = (worker * per_worker + b) * SC_WINDOW
            pltpu.sync_copy(pos_hbm.at[:, pl.ds(start, SC_WINDOW)], idx)
            pltpu.sync_copy(h_hbm.at[pl.ds(start, SC_WINDOW)], buf)
            pltpu.sync_copy(buf, out_hbm.at[idx.at[0]])
            pltpu.sync_copy(buf, out_hbm.at[idx.at[1]])

    return scatter(h, pos)


def _sc_gather_rows(y, pos):
    n = pos.shape[1]
    per_worker = n // (SC_WINDOW * SC_WORKERS)
    out = jax.ShapeDtypeStruct((n, PACKED), jnp.uint32)

    @pl.kernel(out_type=(out, out), mesh=_sc_mesh(), scratch_types=_sc_scratch())
    def gather(y_hbm, pos_hbm, o1_hbm, o2_hbm, idx, buf):
        worker = lax.axis_index("core") * SC_SUBCORES + lax.axis_index("subcore")

        @pl.loop(0, per_worker)
        def _(b):
            start = (worker * per_worker + b) * SC_WINDOW
            pltpu.sync_copy(pos_hbm.at[:, pl.ds(start, SC_WINDOW)], idx)
            pltpu.sync_copy(y_hbm.at[idx.at[0]], buf)
            pltpu.sync_copy(buf, o1_hbm.at[pl.ds(start, SC_WINDOW)])
            pltpu.sync_copy(y_hbm.at[idx.at[1]], buf)
            pltpu.sync_copy(buf, o2_hbm.at[pl.ds(start, SC_WINDOW)])

    return gather(y, pos)


def _expert_mlp_kernel(expert_ref, valid_ref, source_ref, next_ref, ordinal_ref, hs_hbm,
                       wg_hbm, wu_hbm, wd_hbm, y_ref, wg32_ref, wu32_ref, wd32_ref,
                       wg_ref, wu_ref, wd_ref, sem, rows_ref, rows_sem):
    b = pl.program_id(0)
    n_blocks = pl.num_programs(0)

    def rows_copy(blk):
        slot = blk % ROW_RING
        return pltpu.make_async_copy(hs_hbm.at[pl.ds(pl.multiple_of(blk * MOE_BLOCK, MOE_BLOCK), MOE_BLOCK)],
                                     rows_ref.at[slot], rows_sem.at[slot])

    def weight_copies(e, slot):
        return [pltpu.make_async_copy(src.at[e], dst.at[slot], sem.at[slot])
                for src, dst in ((wg_hbm, wg32_ref), (wu_hbm, wu32_ref), (wd_hbm, wd32_ref))]

    @pl.when(valid_ref[b] != 0)
    def _():
        @pl.when(b == 0)
        def _():
            rows_copy(0).start()
            for ahead in range(1, ROW_RING - 1):
                @pl.when(valid_ref[jnp.minimum(ahead, n_blocks - 1)] != 0)
                def _():
                    rows_copy(ahead).start()

        far = b + ROW_RING - 1

        @pl.when((far < n_blocks) & (valid_ref[jnp.minimum(far, n_blocks - 1)] != 0))
        def _():
            rows_copy(far).start()

        rows_copy(b).wait()

        @pl.when((b == 0) | (expert_ref[b] != expert_ref[jnp.maximum(b - 1, 0)]))
        def _():
            slot = ordinal_ref[b] % 2

            @pl.when(b == 0)
            def _():
                for cp in weight_copies(expert_ref[0], slot):
                    cp.start()

            for cp in weight_copies(expert_ref[b], slot):
                cp.wait()

            @pl.when(next_ref[b] < N_EXPERTS)
            def _():
                for cp in weight_copies(next_ref[b], 1 - slot):
                    cp.start()

            wg_ref[...] = wg32_ref[slot].astype(BF16)
            wu_ref[...] = wu32_ref[slot].astype(BF16)
            wd_ref[...] = wd32_ref[slot].astype(BF16)

        half = D_MODEL // 2
        h_lo, h_hi = _unpack_bf16(rows_ref[b % ROW_RING])
        h_lo = h_lo.astype(BF16)
        h_hi = h_hi.astype(BF16)
        hg = _dot(h_lo, wg_ref[0:half, :]) + _dot(h_hi, wg_ref[half:D_MODEL, :])
        hu = _dot(h_lo, wu_ref[0:half, :]) + _dot(h_hi, wu_ref[half:D_MODEL, :])
        act = (hg * _sigmoid(hg) * hu).astype(BF16)
        y_ref[...] = _pack_bf16(_dot(act, wd_ref[...]))


def _expert_mlp(blk, hs, wg, wu, wd):
    rows = lambda b, expert, valid, source, nxt, ordinal: (source[b], 0)
    grid_spec = pltpu.PrefetchScalarGridSpec(
        num_scalar_prefetch=5,
        grid=(hs.shape[0] // MOE_BLOCK,),
        in_specs=[pl.BlockSpec(memory_space=pl.ANY)] * 4,
        out_specs=pl.BlockSpec((MOE_BLOCK, PACKED), rows),
        scratch_shapes=[pltpu.VMEM((2, D_MODEL, D_EXPERT), F32), pltpu.VMEM((2, D_MODEL, D_EXPERT), F32),
                        pltpu.VMEM((2, D_EXPERT, D_MODEL), F32),
                        pltpu.VMEM((D_MODEL, D_EXPERT), BF16), pltpu.VMEM((D_MODEL, D_EXPERT), BF16),
                        pltpu.VMEM((D_EXPERT, D_MODEL), BF16),
                        pltpu.SemaphoreType.DMA((2,)),
                        pltpu.VMEM((ROW_RING, MOE_BLOCK, PACKED), jnp.uint32),
                        pltpu.SemaphoreType.DMA((ROW_RING,))],
    )
    return pl.pallas_call(
        _expert_mlp_kernel,
        grid_spec=grid_spec,
        out_shape=jax.ShapeDtypeStruct((hs.shape[0], PACKED), jnp.uint32),
        compiler_params=pltpu.CompilerParams(
            dimension_semantics=("arbitrary",), vmem_limit_bytes=VMEM_LIMIT),
        name="expert_mlp",
    )(blk[:, 0], blk[:, 1], blk[:, 2], blk[:, 3], blk[:, 4], hs, wg, wu, wd)


def _moe_out_kernel(x1_ref, y1_ref, y2_ref, comb_ref, lnf_ref, out_ref, *, final_norm):
    w1 = comb_ref[:, WEIGHT1_LANE:WEIGHT1_LANE + 1]
    w2 = comb_ref[:, WEIGHT2_LANE:WEIGHT2_LANE + 1]
    a_lo, a_hi = _unpack_bf16(y1_ref[...])
    b_lo, b_hi = _unpack_bf16(y2_ref[...])
    moe = jnp.concatenate([w1 * a_lo + w2 * b_lo, w1 * a_hi + w2 * b_hi], axis=1)
    y = x1_ref[...] + moe
    if final_norm:
        y = y * lax.rsqrt(jnp.mean(y * y, axis=-1, keepdims=True) + RMS_EPS) * lnf_ref[...]
    out_ref[...] = y


def _moe_out(x1, y1, y2, comb, lnf, final_norm):
    n = x1.shape[0]
    t = FINAL_TILE
    row = lambda i: (i, 0)
    return pl.pallas_call(
        functools.partial(_moe_out_kernel, final_norm=final_norm),
        grid=(n // t,),
        in_specs=[pl.BlockSpec((t, D_MODEL), row), pl.BlockSpec((t, PACKED), row),
                  pl.BlockSpec((t, PACKED), row), pl.BlockSpec((t, ROUTER_LANES), row),
                  pl.BlockSpec((1, D_MODEL), lambda i: (0, 0))],
        out_specs=pl.BlockSpec((t, D_MODEL), row),
        out_shape=jax.ShapeDtypeStruct((n, D_MODEL), F32),
        compiler_params=pltpu.CompilerParams(
            dimension_semantics=("arbitrary",), vmem_limit_bytes=VMEM_LIMIT),
        name="moe_out",
    )(x1, y1, y2, comb, lnf)


def _moe(counts, x1, h2p, comb, wg, wu, wd, lnf, final_norm):
    n = x1.shape[0]
    n_blocks = (2 * n) // MOE_BLOCK + N_EXPERTS
    base, blk = _moe_plan(counts[:, 0, :], n_blocks)
    pos = _moe_pos(comb, base)
    hs = _sc_scatter_rows(h2p, pos, n_blocks * MOE_BLOCK)
    ys = _expert_mlp(blk, hs, wg, wu, wd)
    y1, y2 = _sc_gather_rows(ys, pos)
    return _moe_out(x1, y1, y2, comb, lnf, final_norm)


def kernel(x, ln_mix_g, w_in, att_rel_bias, rwkv_mu, rwkv_w0, rwkv_w2, rwkv_a0, rwkv_a2, rwkv_g2,
           rwkv_k_k, rwkv_k_a, rwkv_r_k, rwkv_gn_g, rwkv_gn_b, w_branch_att, w_branch_rwkv, w_out,
           ln_ffn_g, router_group_w, router_group_b, router_expert_w, router_expert_b,
           expert_w_gate, expert_w_up, expert_w_down, ln_final_g):
    bsz, seq, d = x.shape
    depth = w_in.shape[0]
    n = bsz * seq
    x2 = x.reshape(n, d)
    for l in range(depth):
        q, k, v, rw, gates = _in_proj(x2, ln_mix_g[l][None, :], w_in, l, rwkv_mu[l][None, :], seq)
        bias = _rel_bias(att_rel_bias[l])
        att = _band_attn(q.reshape(bsz, seq, WIDTH), k.reshape(bsz, seq, WIDTH),
                         v.reshape(bsz, seq, WIDTH), bias)
        zeros = jnp.zeros((DECAY_LORA, WIDTH), F32)
        w2a2 = jnp.concatenate(
            [jnp.concatenate([rwkv_w2[l], zeros], axis=1),
             jnp.concatenate([zeros, rwkv_a2[l]], axis=1)], axis=0)
        rwkv = _rwkv(rw.reshape(bsz, seq, RWKV_PROJ), w2a2.astype(BF16), rwkv_g2[l].astype(BF16),
                     rwkv_w0[l][None, :], rwkv_a0[l][None, :], rwkv_k_k[l][None, :],
                     rwkv_k_a[l][None, :], rwkv_r_k[l].reshape(1, WIDTH),
                     rwkv_gn_g[l][None, :], rwkv_gn_b[l][None, :])
        wr = jnp.concatenate([router_expert_w[l], router_group_w[l]], axis=1)
        wr = jnp.pad(wr, ((0, 0), (0, ROUTER_LANES - wr.shape[1])))
        wr_hi = wr.astype(BF16)
        wr = jnp.concatenate([wr_hi, (wr - wr_hi.astype(F32)).astype(BF16)], axis=1)
        br = jnp.concatenate([router_expert_b[l], router_group_b[l]])
        br = jnp.pad(br, (0, ROUTER_LANES - br.shape[0]))[None, :]
        x1, h2, comb, counts = _merge(x2, att.reshape(n, WIDTH), rwkv.reshape(n, WIDTH), gates,
                                      w_branch_att[l], w_branch_rwkv[l], w_out[l],
                                      ln_ffn_g[l][None, :], wr, br)
        x2 = _moe(counts, x1, h2, comb, expert_w_gate[l], expert_w_up[l], expert_w_down[l],
                  ln_final_g[None, :], final_norm=(l == depth - 1))
    return x2.reshape(bsz, seq, d)
```
